```python
import math
import numpy as np
import jax
import jax.numpy as jnp
from jax import lax

D_MODEL = 1024
BATCH = 8
SEQ = 2048
DEPTH = 2
DEC_BATCH = 32
DEC_SEQ = 8
PAST_LEN = 8192
PAGE_SIZE = 128

NSA_HEADS = 8
NSA_KV_HEADS = 2
NSA_HPG = NSA_HEADS // NSA_KV_HEADS
HEAD_DIM = 64
CMP_BLOCK = 32
CMP_STRIDE = 16
CMP_HIDDEN = 64
SEL_BLOCK = 64
SEL_TOPK = 16
FORCE_BONUS = 100.0
WINDOW = 512
POOL_WINDOWS = (2, 4, 8, 16)
POOL_GROUP_W = 128
POOL_W = len(POOL_WINDOWS) * POOL_GROUP_W
POOL_MEM = max(POOL_WINDOWS) - 1
GM_GROUPS = 4
GM_GROUP_W = 128
GM_W = GM_GROUPS * GM_GROUP_W
GM_CHUNK = 128
DIFF_HEADS = 8
DIFF_QK = 32
DIFF_V = 2 * DIFF_QK
N_BRANCH = 4
BRANCH_W = 512
D_FF = 2816
N_EXPERTS = 8
TOP_K = 2
D_FF_EXPERT = 3584
PLE_DIM = 256
Q_BLOCK = 128
SEL_Q_BLOCK = 64
EPS = 1e-6
NEG = -1e30

IN_SPLITS = (
    ('nsa_q', NSA_HEADS * HEAD_DIM),
    ('nsa_kv', 6 * NSA_KV_HEADS * HEAD_DIM),
    ('nsa_gate', 3 * NSA_HEADS),
    ('pool', POOL_W),
    ('gm_uv', 2 * GM_W),
    ('diff_q', DIFF_HEADS * 2 * DIFF_QK),
    ('diff_kv', 2 * DIFF_HEADS * DIFF_V),
)
MIX_IN = sum(w for _, w in IN_SPLITS)

kernel_name = 'nsa_pool_gmlp_diffattn_gated_hybrid_step'


def split_in(z):
    parts, o = {}, 0
    for name, w in IN_SPLITS:
        parts[name] = z[..., o:o + w]
        o += w
    return parts


def rmsnorm(x, g):
    xf = x.astype(jnp.float32)
    y = xf * lax.rsqrt(jnp.mean(xf * xf, axis=-1, keepdims=True) + EPS)
    return (y * g.astype(jnp.float32)).astype(x.dtype)


def layernorm(x, g, b):
    xf = x.astype(jnp.float32)
    xc = xf - jnp.mean(xf, axis=-1, keepdims=True)
    y = xc * lax.rsqrt(jnp.mean(xc * xc, axis=-1, keepdims=True) + EPS)
    return (y * g.astype(jnp.float32) + b.astype(jnp.float32)).astype(x.dtype)


def blockwise(fn, block, args, axes, out_axis):
    sq = args[0].shape[axes[0]]
    if sq <= block or sq % block:
        return fn(*args)
    n = sq // block
    split = tuple(jnp.moveaxis(a.reshape(a.shape[:ax] + (n, block) + a.shape[ax + 1:]), ax, 0)
                  for a, ax in zip(args, axes))
    out = lax.map(lambda t: fn(*t), split)
    out = jnp.moveaxis(out, 0, out_axis)
    return out.reshape(out.shape[:out_axis] + (sq,) + out.shape[out_axis + 2:])


def gqa_dense(q, k, v, mask):
    s = jnp.einsum('bqgjd,bkgd->bgjqk', q, k).astype(jnp.float32) * HEAD_DIM ** -0.5
    p = jax.nn.softmax(jnp.where(mask, s, NEG), axis=-1)
    return jnp.einsum('bgjqk,bkgd->bqgjd', p.astype(v.dtype), v)


def compress(rows, pe, w1, w2):
    b, l = rows.shape[:2]
    n_ch = l // CMP_STRIDE
    ch = rows[:, :n_ch * CMP_STRIDE].reshape(b, n_ch, CMP_STRIDE, NSA_KV_HEADS, HEAD_DIM)
    pre_lo = jnp.einsum('bcsgd,gsdh->bcgh', ch, w1[:, :CMP_STRIDE])
    pre_hi = jnp.einsum('bcsgd,gsdh->bcgh', ch, w1[:, CMP_STRIDE:])
    bias = jnp.einsum('sgd,gsdh->gh', pe, w1)
    hid = jax.nn.gelu(pre_lo[:, :-1] + pre_hi[:, 1:] + bias)
    return jnp.einsum('bcgh,ghd->bcgd', hid, w2)


def cmp_to_sel(n_cmp, n_sel):
    r = SEL_BLOCK // CMP_STRIDE
    k = np.arange(n_cmp)[:, None] - r * np.arange(n_sel)[None, :]
    m = sum(((k + n >= 0) & (k + n < r)).astype(np.float32) for n in range(CMP_BLOCK // CMP_STRIDE))
    return jnp.asarray(m, dtype=jnp.float32)


def nsa_cmp_sel(q, qpos, kv_rows, pe, w1, w2):
    b, l = kv_rows.shape[:2]
    scale = HEAD_DIM ** -0.5
    kc = compress(kv_rows[:, :, 0], pe[0], w1[0], w2[0])
    vc = compress(kv_rows[:, :, 1], pe[1], w1[1], w2[1])
    n_cmp = kc.shape[1]
    cend = jnp.arange(n_cmp) * CMP_STRIDE + CMP_BLOCK - 1
    cmask = cend[None, :] <= qpos[:, None]
    s = jnp.einsum('bqgjd,bcgd->bgjqc', q, kc).astype(jnp.float32) * scale
    p = jax.nn.softmax(jnp.where(cmask, s, NEG), axis=-1) * cmask
    o_cmp = jnp.einsum('bgjqc,bcgd->bqgjd', p.astype(vc.dtype), vc)
    n_sel = -(-l // SEL_BLOCK)
    imp = jnp.einsum('bgjqc,cn->bgqn', p, cmp_to_sel(n_cmp, n_sel))
    cur = qpos // SEL_BLOCK
    blk = jnp.arange(n_sel)
    valid = blk[None, :] <= cur[:, None]
    forced = (blk[None, :] == 0) | (blk[None, :] == cur[:, None]) | (blk[None, :] == cur[:, None] - 1)
    score = jnp.where(valid, imp + FORCE_BONUS * forced.astype(jnp.float32), -1.0)
    _, idx = lax.top_k(score, min(SEL_TOPK, n_sel))
    ok = idx * SEL_BLOCK <= qpos[None, None, :, None]
    pad = n_sel * SEL_BLOCK - l
    def blocks(r):
        r = jnp.pad(r, ((0, 0), (0, pad), (0, 0), (0, 0)))
        return r.reshape(b, n_sel, SEL_BLOCK, NSA_KV_HEADS, HEAD_DIM).transpose(0, 3, 1, 2, 4)
    kb = blocks(kv_rows[:, :, 2])
    vb = blocks(kv_rows[:, :, 3])
    bi = jnp.arange(b)[:, None, None, None]
    gi = jnp.arange(NSA_KV_HEADS)[None, :, None, None]

    def sel_block(q_b, idx_b, ok_b, qpos_b):
        kg = kb[bi, gi, idx_b]
        vg = vb[bi, gi, idx_b]
        sb = jnp.einsum('bqgjd,bgqktd->bgjqkt', q_b, kg).astype(jnp.float32) * scale
        tpos = idx_b[..., None] * SEL_BLOCK + jnp.arange(SEL_BLOCK)
        m = (ok_b[..., None] & (tpos <= qpos_b[None, None, :, None, None]))[:, :, None]
        sb = jnp.where(m, sb, NEG)
        sh = sb.shape
        pb = jax.nn.softmax(sb.reshape(sh[:-2] + (-1,)), axis=-1).reshape(sh)
        return jnp.einsum('bgjqkt,bgqktd->bqgjd', pb.astype(vg.dtype), vg)

    o_sel = blockwise(sel_block, SEL_Q_BLOCK, (q, idx, ok, qpos), (1, 2, 2, 0), 1)
    return o_cmp, o_sel


def window_prompt(q, kw, vw):
    b, s = q.shape[:2]
    nb = s // Q_BLOCK
    nband = WINDOW // Q_BLOCK + 1
    def band(r):
        r = jnp.pad(r, ((0, 0), (WINDOW, 0), (0, 0), (0, 0))).reshape(b, nb + nband - 1, Q_BLOCK, NSA_KV_HEADS, HEAD_DIM)
        return jnp.stack([r[:, o:o + nb] for o in range(nband)], axis=2).reshape(b, nb, nband * Q_BLOCK, NSA_KV_HEADS, HEAD_DIM)
    kband, vband = band(kw), band(vw)
    qb = q.reshape(b, nb, Q_BLOCK, NSA_KV_HEADS, NSA_HPG, HEAD_DIM)
    qpos = jnp.arange(nb)[:, None] * Q_BLOCK + jnp.arange(Q_BLOCK)
    kpos = jnp.arange(nb)[:, None] * Q_BLOCK - WINDOW + jnp.arange(nband * Q_BLOCK)
    m = ((kpos[:, None, :] >= 0) & (kpos[:, None, :] <= qpos[:, :, None])
         & (kpos[:, None, :] > qpos[:, :, None] - WINDOW))
    sc = jnp.einsum('bnqgjd,bnkgd->bngjqk', qb, kband).astype(jnp.float32) * HEAD_DIM ** -0.5
    p = jax.nn.softmax(jnp.where(m[None, :, None, None], sc, NEG), axis=-1)
    o = jnp.einsum('bngjqk,bnkgd->bnqgjd', p.astype(vband.dtype), vband)
    return o.reshape(b, s, NSA_KV_HEADS, NSA_HPG, HEAD_DIM)


def pool_mix(xp, prev, start, pw, scale):
    sq = xp.shape[1]
    xcat = jnp.concatenate([prev, xp], axis=1)
    cs = jnp.pad(jnp.cumsum(xcat.astype(jnp.float32), axis=1), ((0, 0), (1, 0), (0, 0)))
    pos = start + jnp.arange(sq)
    outs = []
    for g, w in enumerate(POOL_WINDOWS):
        c0, c1 = g * POOL_GROUP_W, (g + 1) * POOL_GROUP_W
        tot = cs[:, POOL_MEM + 1:POOL_MEM + 1 + sq, c0:c1] - cs[:, POOL_MEM + 1 - w:POOL_MEM + 1 - w + sq, c0:c1]
        cnt = jnp.minimum(w, pos + 1).astype(jnp.float32)[None, :, None]
        mix = tot / cnt - xp[..., c0:c1].astype(jnp.float32)
        outs.append(jnp.einsum('bsc,cd->bsd', mix.astype(xp.dtype), pw[g]))
    return jnp.concatenate(outs, axis=-1) * scale, xcat[:, -POOL_MEM:]


def gmlp_mix(uv, ng, nb_, ws, bs):
    b, sq = uv.shape[:2]
    z = jax.nn.gelu(uv)
    u = z[..., :GM_W]
    v = layernorm(z[..., GM_W:], ng, nb_)
    c = min(sq, GM_CHUNK)
    n = sq // c
    w = jnp.tril(ws[:, :c, :c])
    vb = v.reshape(b, n, c, GM_GROUPS, GM_GROUP_W)
    mixed = jnp.einsum('gts,bnsgc->bntgc', w, vb) + bs[:, :c].T[None, None, :, :, None]
    return u * mixed.reshape(b, sq, GM_W), v


def diff_attend(q, k, v, qpos, kpos, lam):
    scale = DIFF_QK ** -0.5
    def blk(q_b, qpos_b):
        s = jnp.einsum('bqhmd,bkhmd->bhmqk', q_b, k).astype(jnp.float32) * scale
        m = kpos[None, :] <= qpos_b[:, None]
        p = jax.nn.softmax(jnp.where(m, s, NEG), axis=-1)
        a = p[:, :, 0] - lam * p[:, :, 1]
        return jnp.einsum('bhqk,bkhd->bqhd', a.astype(v.dtype), v)
    return blockwise(blk, Q_BLOCK, (q, qpos), (1, 0), 1)


def finish_mixer(h, z, o_cmp, o_sel, o_win, o_pool, o_gm, o_diff, lp, lam_init):
    b, s = h.shape[:2]
    g = jax.nn.sigmoid(z['nsa_gate']).reshape(b, s, NSA_KV_HEADS, NSA_HPG, 3)
    o_nsa = (g[..., 0:1] * o_cmp + g[..., 1:2] * o_sel + g[..., 2:3] * o_win).reshape(b, s, BRANCH_W)
    o_diff = (rmsnorm(o_diff, lp['diff_norm_g']) * (1.0 - lam_init)).reshape(b, s, BRANCH_W)
    ob = jnp.stack([o_nsa, o_pool, o_gm, o_diff], axis=2)
    yb = jnp.einsum('bsnc,ncd->bsnd', ob, lp['w_branch'])
    gates = jax.nn.sigmoid(h @ lp['w_gate'] + lp['b_gate']).reshape(b, s, N_BRANCH, D_MODEL)
    return jnp.sum(gates * yb, axis=2) @ lp['w_o']


def mixer_prompt(h, lp, lam, lam_init):
    b, s = h.shape[:2]
    z = split_in(h @ lp['w_in'])
    pos = jnp.arange(s)
    q = z['nsa_q'].reshape(b, s, NSA_KV_HEADS, NSA_HPG, HEAD_DIM)
    kv = z['nsa_kv'].reshape(b, s, 6, NSA_KV_HEADS, HEAD_DIM)
    o_cmp, o_sel = nsa_cmp_sel(q, pos, kv[:, :, :4], lp['cmp_pe'], lp['cmp_w1'], lp['cmp_w2'])
    o_win = window_prompt(q, kv[:, :, 4], kv[:, :, 5])
    prev = jnp.zeros((b, POOL_MEM, POOL_W), h.dtype)
    o_pool, pool_state = pool_mix(z['pool'], prev, 0, lp['pool_w'], lp['pool_scale'])
    o_gm, _ = gmlp_mix(z['gm_uv'], lp['gm_ng'], lp['gm_nb'], lp['gm_ws'], lp['gm_bs'])
    dq = z['diff_q'].reshape(b, s, DIFF_HEADS, 2, DIFF_QK)
    dkv = z['diff_kv'].reshape(b, s, 2, DIFF_HEADS, DIFF_V)
    o_diff = diff_attend(dq, dkv[:, :, 0].reshape(b, s, DIFF_HEADS, 2, DIFF_QK), dkv[:, :, 1], pos, pos, lam)
    out = finish_mixer(h, z, o_cmp, o_sel, o_win, o_pool, o_gm, o_diff, lp, lam_init)
    win_state = kv[:, s - min(WINDOW, s):, 4:]
    return out, kv[:, :, :4], win_state, pool_state, dkv


def mixer_sample(h, cache_nsa_l, cache_diff_l, win_l, pool_l, page_table, lp, lam, lam_init):
    b, sq = h.shape[:2]
    past = page_table.shape[1] * cache_nsa_l.shape[1]
    pos = past + jnp.arange(sq)
    z = split_in(h @ lp['w_in'])
    q = z['nsa_q'].reshape(b, sq, NSA_KV_HEADS, NSA_HPG, HEAD_DIM)
    kv = z['nsa_kv'].reshape(b, sq, 6, NSA_KV_HEADS, HEAD_DIM)
    kv_past = cache_nsa_l[page_table].reshape(b, past, 4, NSA_KV_HEADS, HEAD_DIM)
    kv_all = jnp.concatenate([kv_past, kv[:, :, :4]], axis=1)
    o_cmp, o_sel = nsa_cmp_sel(q, pos, kv_all, lp['cmp_pe'], lp['cmp_w1'], lp['cmp_w2'])
    wbuf = win_l.shape[1]
    win_all = jnp.concatenate([win_l, kv[:, :, 4:]], axis=1)
    kpos_w = past - wbuf + jnp.arange(wbuf + sq)
    mw = (kpos_w[None, :] <= pos[:, None]) & (kpos_w[None, :] > pos[:, None] - WINDOW)
    o_win = gqa_dense(q, win_all[:, :, 0], win_all[:, :, 1], mw)
    o_pool, pool_state = pool_mix(z['pool'], pool_l, past, lp['pool_w'], lp['pool_scale'])
    o_gm, gm_v = gmlp_mix(z['gm_uv'], lp['gm_ng'], lp['gm_nb'], lp['gm_ws'], lp['gm_bs'])
    dq = z['diff_q'].reshape(b, sq, DIFF_HEADS, 2, DIFF_QK)
    dkv = z['diff_kv'].reshape(b, sq, 2, DIFF_HEADS, DIFF_V)
    dkv_all = jnp.concatenate([cache_diff_l[page_table].reshape(b, past, 2, DIFF_HEADS, DIFF_V), dkv], axis=1)
    kpos = jnp.arange(past + sq)
    o_diff = diff_attend(dq, dkv_all[:, :, 0].reshape(b, past + sq, DIFF_HEADS, 2, DIFF_QK), dkv_all[:, :, 1], pos, kpos, lam)
    out = finish_mixer(h, z, o_cmp, o_sel, o_win, o_pool, o_gm, o_diff, lp, lam_init)
    return out, kv[:, :, :4], win_all[:, sq:], pool_state, gm_v, dkv


def swiglu(x, wg, wu, wd):
    return (jax.nn.silu(x @ wg) * (x @ wu)) @ wd


def moe_swiglu(x, router, router_b, wg, wu, wd):
    logits = (x @ router).astype(jnp.float32) + router_b.astype(jnp.float32)
    top_v, top_i = lax.top_k(logits, TOP_K)
    top_w = jax.nn.softmax(top_v, axis=-1)
    comb = jnp.sum(jax.nn.one_hot(top_i, N_EXPERTS, dtype=jnp.float32) * top_w[..., None], axis=-2)
    out = jnp.zeros_like(x)
    for e in range(N_EXPERTS):
        out = out + comb[..., e:e + 1].astype(x.dtype) * swiglu(x, wg[e], wu[e], wd[e])
    return out


def setup_inputs(seed: int = 0) -> dict:
    key = jax.random.key(seed)
    kit = iter(list(jax.random.split(key, 64)))
    def nrm(shape, scale=1.0):
        return jax.random.normal(next(kit), shape, jnp.float32) * scale
    def gain(shape):
        return 1.0 + nrm(shape, 0.1)
    n_pages = PAST_LEN // PAGE_SIZE
    n_used = DEC_BATCH * n_pages
    n_phys = n_used + (n_used + 3) // 4
    wbuf = min(WINDOW, PAST_LEN)
    n_dense = (DEPTH + 1) // 2
    n_moe = DEPTH // 2
    inp = {}
    inp['x_prompt'] = nrm((BATCH, SEQ, D_MODEL))
    inp['x_sample'] = nrm((DEC_BATCH, DEC_SEQ, D_MODEL))
    inp['cache_nsa'] = nrm((DEPTH, n_phys, PAGE_SIZE, 4, NSA_KV_HEADS, HEAD_DIM))
    inp['cache_diff'] = nrm((DEPTH, n_phys, PAGE_SIZE, 2, DIFF_HEADS, DIFF_V))
    inp['state_nsa_win'] = nrm((DEPTH, DEC_BATCH, wbuf, 2, NSA_KV_HEADS, HEAD_DIM))
    inp['state_pool'] = nrm((DEPTH, DEC_BATCH, POOL_MEM, POOL_W))
    inp['page_table'] = jax.random.permutation(next(kit), n_phys)[:n_used].reshape(DEC_BATCH, n_pages).astype(jnp.int32)
    inp['p_prompt'] = nrm((DEPTH, BATCH, SEQ, PLE_DIM))
    inp['p_sample'] = nrm((DEPTH, DEC_BATCH, DEC_SEQ, PLE_DIM))
    inp['norm_mix_g'] = gain((DEPTH, D_MODEL))
    inp['w_in'] = nrm((DEPTH, D_MODEL, MIX_IN), D_MODEL ** -0.5)
    inp['nsa_cmp_pe'] = nrm((DEPTH, 2, CMP_BLOCK, NSA_KV_HEADS, HEAD_DIM), 0.5)
    inp['nsa_cmp_w1'] = nrm((DEPTH, 2, NSA_KV_HEADS, CMP_BLOCK, HEAD_DIM, CMP_HIDDEN), (CMP_BLOCK * HEAD_DIM) ** -0.5)
    inp['nsa_cmp_w2'] = nrm((DEPTH, 2, NSA_KV_HEADS, CMP_HIDDEN, HEAD_DIM), CMP_HIDDEN ** -0.5)
    inp['pool_w'] = nrm((DEPTH, len(POOL_WINDOWS), POOL_GROUP_W, POOL_GROUP_W), POOL_GROUP_W ** -0.5)
    inp['pool_scale'] = gain((DEPTH, POOL_W))
    inp['gm_norm_g'] = gain((DEPTH, GM_W))
    inp['gm_norm_b'] = nrm((DEPTH, GM_W), 0.1)
    inp['gm_ws'] = nrm((DEPTH, GM_GROUPS, GM_CHUNK, GM_CHUNK), GM_CHUNK ** -0.5)
    inp['gm_bs'] = gain((DEPTH, GM_GROUPS, GM_CHUNK))
    inp['diff_lq1'] = nrm((DEPTH, DIFF_QK), 0.1)
    inp['diff_lk1'] = nrm((DEPTH, DIFF_QK), 0.1)
    inp['diff_lq2'] = nrm((DEPTH, DIFF_QK), 0.1)
    inp['diff_lk2'] = nrm((DEPTH, DIFF_QK), 0.1)
    inp['diff_norm_g'] = gain((DEPTH, DIFF_V))
    inp['w_branch'] = nrm((DEPTH, N_BRANCH, BRANCH_W, D_MODEL), BRANCH_W ** -0.5)
    inp['w_gate'] = nrm((DEPTH, D_MODEL, N_BRANCH * D_MODEL), D_MODEL ** -0.5)
    inp['b_gate'] = nrm((DEPTH, N_BRANCH * D_MODEL), 0.1)
    inp['w_o'] = nrm((DEPTH, D_MODEL, D_MODEL), D_MODEL ** -0.5)
    inp['norm_ffn_g'] = gain((DEPTH, D_MODEL))
    inp['ffn_w_gate'] = nrm((n_dense, D_MODEL, D_FF), D_MODEL ** -0.5)
    inp['ffn_w_up'] = nrm((n_dense, D_MODEL, D_FF), D_MODEL ** -0.5)
    inp['ffn_w_down'] = nrm((n_dense, D_FF, D_MODEL), D_FF ** -0.5)
    inp['moe_router'] = nrm((n_moe, D_MODEL, N_EXPERTS), D_MODEL ** -0.5)
    inp['moe_router_b'] = nrm((n_moe, N_EXPERTS), 0.01)
    inp['moe_w_gate'] = nrm((n_moe, N_EXPERTS, D_MODEL, D_FF_EXPERT), D_MODEL ** -0.5)
    inp['moe_w_up'] = nrm((n_moe, N_EXPERTS, D_MODEL, D_FF_EXPERT), D_MODEL ** -0.5)
    inp['moe_w_down'] = nrm((n_moe, N_EXPERTS, D_FF_EXPERT, D_MODEL), D_FF_EXPERT ** -0.5)
    inp['ple_norm_g'] = gain((DEPTH, D_MODEL))
    inp['ple_w_gate'] = nrm((DEPTH, D_MODEL, D_MODEL), D_MODEL ** -0.5)
    inp['ple_w_proj'] = nrm((DEPTH, PLE_DIM, D_MODEL), PLE_DIM ** -0.5)
    inp['final_norm_g'] = gain((D_MODEL,))
    return inp


def reference(x_prompt, x_sample, cache_nsa, cache_diff, state_nsa_win, state_pool, page_table, p_prompt, p_sample,
              norm_mix_g, w_in, nsa_cmp_pe, nsa_cmp_w1, nsa_cmp_w2, pool_w, pool_scale, gm_norm_g, gm_norm_b,
              gm_ws, gm_bs, diff_lq1, diff_lk1, diff_lq2, diff_lk2, diff_norm_g, w_branch, w_gate, b_gate, w_o,
              norm_ffn_g, ffn_w_gate, ffn_w_up, ffn_w_down, moe_router, moe_router_b, moe_w_gate, moe_w_up,
              moe_w_down, ple_norm_g, ple_w_gate, ple_w_proj, final_norm_g):
    xp, xs = x_prompt, x_sample
    nsa_p, nsa_s, win_p, win_s, pool_p, pool_s, gmv_s, diff_p, diff_s = ([] for _ in range(9))
    for l in range(DEPTH):
        lp = dict(w_in=w_in[l], cmp_pe=nsa_cmp_pe[l], cmp_w1=nsa_cmp_w1[l], cmp_w2=nsa_cmp_w2[l],
                  pool_w=pool_w[l], pool_scale=pool_scale[l], gm_ng=gm_norm_g[l], gm_nb=gm_norm_b[l],
                  gm_ws=gm_ws[l], gm_bs=gm_bs[l], diff_norm_g=diff_norm_g[l], w_branch=w_branch[l],
                  w_gate=w_gate[l], b_gate=b_gate[l], w_o=w_o[l])
        lam_init = 0.8 - 0.6 * math.exp(-0.3 * l)
        lam = (jnp.exp(jnp.sum(diff_lq1[l] * diff_lk1[l]).astype(jnp.float32))
               - jnp.exp(jnp.sum(diff_lq2[l] * diff_lk2[l]).astype(jnp.float32)) + lam_init)
        mp, r_nsa, r_win, r_pool, r_diff = mixer_prompt(rmsnorm(xp, norm_mix_g[l]), lp, lam, lam_init)
        ms, s_nsa, s_win, s_pool, s_gmv, s_diff = mixer_sample(
            rmsnorm(xs, norm_mix_g[l]), cache_nsa[l], cache_diff[l], state_nsa_win[l], state_pool[l],
            page_table, lp, lam, lam_init)
        xp = xp + mp
        xs = xs + ms
        nsa_p.append(r_nsa); win_p.append(r_win); pool_p.append(r_pool); diff_p.append(r_diff)
        nsa_s.append(s_nsa); win_s.append(s_win); pool_s.append(s_pool); gmv_s.append(s_gmv); diff_s.append(s_diff)
        if l % 2 == 0:
            i = l // 2
            xp = xp + swiglu(rmsnorm(xp, norm_ffn_g[l]), ffn_w_gate[i], ffn_w_up[i], ffn_w_down[i])
            xs = xs + swiglu(rmsnorm(xs, norm_ffn_g[l]), ffn_w_gate[i], ffn_w_up[i], ffn_w_down[i])
        else:
            i = l // 2
            xp = xp + moe_swiglu(rmsnorm(xp, norm_ffn_g[l]), moe_router[i], moe_router_b[i], moe_w_gate[i], moe_w_up[i], moe_w_down[i])
            xs = xs + moe_swiglu(rmsnorm(xs, norm_ffn_g[l]), moe_router[i], moe_router_b[i], moe_w_gate[i], moe_w_up[i], moe_w_down[i])
        xp = xp + jax.nn.sigmoid(rmsnorm(xp, ple_norm_g[l]) @ ple_w_gate[l]) * (p_prompt[l] @ ple_w_proj[l])
        xs = xs + jax.nn.sigmoid(rmsnorm(xs, ple_norm_g[l]) @ ple_w_gate[l]) * (p_sample[l] @ ple_w_proj[l])
    y_prompt = rmsnorm(xp, final_norm_g)
    y_sample = rmsnorm(xs, final_norm_g)
    return (y_prompt, y_sample, jnp.stack(nsa_p), jnp.stack(nsa_s), jnp.stack(win_p), jnp.stack(win_s),
            jnp.stack(pool_p), jnp.stack(pool_s), jnp.stack(gmv_s), jnp.stack(diff_p), jnp.stack(diff_s))
```

```python
import functools
import math

import numpy as np
import jax
import jax.numpy as jnp
from jax import lax
from jax.experimental import pallas as pl
from jax.experimental.pallas import tpu as pltpu

F32 = jnp.float32
BF16 = jnp.bfloat16

D_MODEL = 1024
PAGE = 128
NSA_HEADS = 8
NSA_G = 2
NSA_J = 4
DH = 64
CMP_BLOCK = 32
CMP_STRIDE = 16
SEL_BLOCK = 64
SEL_TOPK = 16
FORCE_BONUS = 100.0
WINDOW = 512
POOL_WINDOWS = (2, 4, 8, 16)
POOL_MEM = 15
GM_W = 512
GM_CHUNK = 128
DIFF_HEADS = 8
DIFF_QK = 32
N_BRANCH = 4
BRANCH_W = 512
N_EXPERTS = 8
EPS = 1e-6
NEG = -1e30
LANE = 128
VMEM_LIMIT = 56 * 1024 * 1024

OFF_Q, OFF_KV, OFF_GATE, OFF_POOL, OFF_GM, OFF_DQ, OFF_DKV, OFF_END = 0, 512, 1280, 1304, 1816, 2840, 3352, 4376


def _tile(n, pref):
    t = min(n, pref)
    while n % t:
        t //= 2
    return t


def _cp(sem, vmem=VMEM_LIMIT):
    return pltpu.CompilerParams(dimension_semantics=sem, vmem_limit_bytes=vmem)


def _gelu(x):
    return 0.5 * x * (1.0 + jnp.tanh(0.7978845608028654 * (x + 0.044715 * (x * x * x))))


def _rms(x, g):
    return x * lax.rsqrt(jnp.mean(x * x, axis=-1, keepdims=True) + EPS) * g


def _dot(a, b):
    return jnp.dot(a, b, preferred_element_type=F32)


def _dot_nt(a, b):
    return lax.dot_general(a, b, (((1,), (1,)), ((), ())), preferred_element_type=F32)


def _rmsnorm_body(x_ref, g_ref, o_ref):
    o_ref[...] = _rms(x_ref[...], g_ref[...]).astype(o_ref.dtype)


def rmsnorm(x, g, out_dtype):
    t, d = x.shape
    tm = _tile(t, 512)
    return pl.pallas_call(
        _rmsnorm_body, grid=(t // tm,),
        in_specs=[pl.BlockSpec((tm, d), lambda i: (i, 0)), pl.BlockSpec((1, d), lambda i: (0, 0))],
        out_specs=pl.BlockSpec((tm, d), lambda i: (i, 0)),
        out_shape=jax.ShapeDtypeStruct((t, d), out_dtype),
        compiler_params=_cp(("parallel",)),
    )(x, g.reshape(1, d))


def _mm_body(x_ref, w_ref, o_ref, *, scale, act):
    y = _dot(x_ref[...].astype(BF16), w_ref[...])
    if scale is not None:
        y = y * scale
    if act == "sigmoid":
        y = jax.nn.sigmoid(y)
    o_ref[...] = y.astype(o_ref.dtype)


def mm(x, w, *, scale=None, act=None, out_dtype=F32):
    t, k = x.shape
    n = w.shape[1]
    tm = _tile(t, 512)
    tn = _tile(n, 512)
    return pl.pallas_call(
        functools.partial(_mm_body, scale=scale, act=act), grid=(t // tm, n // tn),
        in_specs=[pl.BlockSpec((tm, k), lambda i, j: (i, 0)), pl.BlockSpec((k, tn), lambda i, j: (0, j))],
        out_specs=pl.BlockSpec((tm, tn), lambda i, j: (i, j)),
        out_shape=jax.ShapeDtypeStruct((t, n), out_dtype),
        compiler_params=_cp(("parallel", "parallel")),
    )(x, w.astype(BF16))


def _compress_body(xk_ref, xv_ref, wc_ref, bias_ref, w2_ref, o_ref, *, nch):
    acc = jnp.zeros((nch, 512), F32)
    for s in range(CMP_STRIDE):
        xs = jnp.concatenate([xk_ref[pl.ds(s, nch, stride=CMP_STRIDE), :],
                              xv_ref[pl.ds(s, nch, stride=CMP_STRIDE), :]], axis=1).astype(BF16)
        acc = acc + _dot(xs, wc_ref[s])
    hi_next = pltpu.roll(acc[:, 256:], nch - 1, 0)
    hid = _gelu(acc[:, :256] + hi_next + bias_ref[...])
    o_ref[...] = _dot(hid.astype(BF16), w2_ref[...]).astype(o_ref.dtype)


def compress(rows, wc, bias, w2, nch):
    b = rows.shape[0]
    return pl.pallas_call(
        functools.partial(_compress_body, nch=nch), grid=(b,),
        in_specs=[pl.BlockSpec((None, nch * CMP_STRIDE, LANE), lambda i: (i, 0, 0)),
                  pl.BlockSpec((None, nch * CMP_STRIDE, LANE), lambda i: (i, 0, 1)),
                  pl.BlockSpec((CMP_STRIDE, 256, 512), lambda i: (0, 0, 0)),
                  pl.BlockSpec((1, 256), lambda i: (0, 0)),
                  pl.BlockSpec((256, 256), lambda i: (0, 0))],
        out_specs=pl.BlockSpec((None, nch, 256), lambda i: (i, 0, 0)),
        out_shape=jax.ShapeDtypeStruct((b, nch, 256), BF16),
        compiler_params=_cp(("parallel",)),
    )(rows, rows, wc, bias, w2)


def _stack_heads(q, tq):
    lane = lax.broadcasted_iota(jnp.int32, (tq, LANE), 1)
    parts = []
    for g in range(NSA_G):
        keep = (lane < DH) if g == 0 else (lane >= DH)
        for j in range(NSA_J):
            parts.append(jnp.where(keep, q[:, j * LANE:(j + 1) * LANE], 0.0))
    return jnp.concatenate(parts, axis=0)


def _unstack_heads(o, tq):
    lane = lax.broadcasted_iota(jnp.int32, (tq, LANE), 1)
    outs = []
    for j in range(NSA_J):
        outs.append(jnp.where(lane < DH, o[j * tq:(j + 1) * tq], o[(NSA_J + j) * tq:(NSA_J + j + 1) * tq]))
    return jnp.concatenate(outs, axis=1)


def _cmp_body(q_ref, kv_ref, msel_ref, o_ref, sel_ref, *, tq, nch, n_cmp, n_sel, nsp, qpos0):
    i = pl.program_id(1)
    r = 8 * tq
    qs = _stack_heads(q_ref[...], tq).astype(BF16)
    kc = kv_ref[:, 0:LANE]
    vc = kv_ref[:, LANE:2 * LANE]
    s = _dot_nt(qs, kc)
    row = lax.broadcasted_iota(jnp.int32, (r, nch), 0)
    col = lax.broadcasted_iota(jnp.int32, (r, nch), 1)
    qpos = qpos0 + i * tq + (row & (tq - 1))
    cmask = (col * CMP_STRIDE + (CMP_BLOCK - 1) <= qpos) & (col < n_cmp)
    s = jnp.where(cmask, s, NEG)
    m = jnp.max(s, axis=1, keepdims=True)
    e = jnp.where(cmask, jnp.exp(s - m), 0.0)
    l = jnp.sum(e, axis=1, keepdims=True)
    p = (e * (1.0 / jnp.where(l > 0.0, l, 1.0))).astype(BF16)
    o_ref[...] = _unstack_heads(_dot(p, vc), tq)
    imp_all = _dot(p, msel_ref[...])
    blk = lax.broadcasted_iota(jnp.int32, (tq, nsp), 1)
    qp = qpos0 + i * tq + lax.broadcasted_iota(jnp.int32, (tq, nsp), 0)
    cur = qp // SEL_BLOCK
    valid = (blk <= cur) & (blk < n_sel)
    forced = (blk == 0) | (blk == cur) | (blk == cur - 1)
    for g in range(NSA_G):
        imp = imp_all[(g * NSA_J) * tq:(g * NSA_J + 1) * tq]
        for j in range(1, NSA_J):
            imp = imp + imp_all[(g * NSA_J + j) * tq:(g * NSA_J + j + 1) * tq]
        score = jnp.where(valid, imp + FORCE_BONUS * forced.astype(F32), -1.0)
        score = jnp.where(blk < n_sel, score, -2.0)
        rank = jnp.zeros((tq, nsp), F32)
        for mth in range(n_sel):
            cm = score[:, mth:mth + 1]
            beats = (cm > score) | ((cm == score) & (blk > mth))
            rank = rank + beats.astype(F32)
        chosen = (rank < float(min(SEL_TOPK, n_sel))) & valid
        sel_ref[:, g * nsp:(g + 1) * nsp] = chosen.astype(F32)


def cmp_attend(q, kcvc, msel, *, tq, n_cmp, n_sel, qpos0):
    b, sq, _ = q.shape
    nch = kcvc.shape[1]
    nsp = msel.shape[1]
    return pl.pallas_call(
        functools.partial(_cmp_body, tq=tq, nch=nch, n_cmp=n_cmp, n_sel=n_sel, nsp=nsp, qpos0=qpos0),
        grid=(b, sq // tq),
        in_specs=[pl.BlockSpec((None, tq, 512), lambda bi, i: (bi, i, 0)),
                  pl.BlockSpec((None, nch, 256), lambda bi, i: (bi, 0, 0)),
                  pl.BlockSpec((nch, nsp), lambda bi, i: (0, 0))],
        out_specs=[pl.BlockSpec((None, tq, 512), lambda bi, i: (bi, i, 0)),
                   pl.BlockSpec((None, tq, 2 * nsp), lambda bi, i: (bi, i, 0))],
        out_shape=[jax.ShapeDtypeStruct((b, sq, 512), F32), jax.ShapeDtypeStruct((b, sq, 2 * nsp), F32)],
        compiler_params=_cp(("parallel", "parallel")),
    )(q, kcvc, msel)


def _nsa_flash_body(*refs, mode, tq, tk, nk, qpos0, kpos0, kt_fn, nsp):
    if mode == "sel":
        q_ref, k_ref, v_ref, sel_ref, e_ref, o_ref, qs_ref, m_ref, l_ref, acc_ref = refs
    else:
        q_ref, k_ref, v_ref, o_ref, qs_ref, m_ref, l_ref, acc_ref = refs
    i = pl.program_id(1)
    j = pl.program_id(2)
    r = 8 * tq

    @pl.when(j == 0)
    def _init():
        qs_ref[...] = _stack_heads(q_ref[...], tq).astype(BF16)
        m_ref[...] = jnp.full((r, 1), NEG, F32)
        l_ref[...] = jnp.zeros((r, 1), F32)
        acc_ref[...] = jnp.zeros((r, LANE), F32)

    kt, valid = kt_fn(i, j)

    @pl.when(valid)
    def _step():
        k = k_ref[...].astype(BF16)
        v = v_ref[...].astype(BF16)
        s = _dot_nt(qs_ref[...], k)
        row = lax.broadcasted_iota(jnp.int32, (r, tk), 0)
        qpos = qpos0 + i * tq + (row & (tq - 1))
        kpos = kpos0 + kt * tk + lax.broadcasted_iota(jnp.int32, (r, tk), 1)
        mask = kpos <= qpos
        if mode == "win":
            mask = mask & (kpos > qpos - WINDOW)
        else:
            sel = sel_ref[...].astype(BF16)
            e = e_ref[...]
            m0 = _dot(sel[:, :nsp], e)
            m1 = _dot(sel[:, nsp:], e)
            selm = jnp.concatenate([m0] * NSA_J + [m1] * NSA_J, axis=0)
            mask = mask & (selm > 0.5)
        s = jnp.where(mask, s, NEG)
        m_old = m_ref[...]
        m_new = jnp.maximum(m_old, jnp.max(s, axis=1, keepdims=True))
        alpha = jnp.exp(m_old - m_new)
        p = jnp.where(mask, jnp.exp(s - m_new), 0.0)
        l_ref[...] = alpha * l_ref[...] + jnp.sum(p, axis=1, keepdims=True)
        acc_ref[...] = alpha * acc_ref[...] + _dot(p.astype(BF16), v)
        m_ref[...] = m_new

    @pl.when(j == nk - 1)
    def _fin():
        l = l_ref[...]
        o = acc_ref[...] * (1.0 / jnp.where(l > 0.0, l, 1.0))
        o_ref[...] = _unstack_heads(o, tq)


def nsa_flash(q, kv, kcol, vcol, *, mode, tq, tk, nk, qpos0, kpos0, kt_fn, sel=None, emat=None):
    b, sq, _ = q.shape
    nsp = 0 if sel is None else sel.shape[2] // 2

    def kmap(col):
        return lambda bi, i, j: (bi, kt_fn(i, j)[0], col)

    in_specs = [pl.BlockSpec((None, tq, 512), lambda bi, i, j: (bi, i, 0)),
                pl.BlockSpec((None, tk, LANE), kmap(kcol)),
                pl.BlockSpec((None, tk, LANE), kmap(vcol))]
    args = [q, kv, kv]
    if mode == "sel":
        in_specs += [pl.BlockSpec((None, tq, 2 * nsp), lambda bi, i, j: (bi, i, 0)),
                     pl.BlockSpec((nsp, tk), lambda bi, i, j: (0, kt_fn(i, j)[0]))]
        args += [sel, emat]
    r = 8 * tq
    return pl.pallas_call(
        functools.partial(_nsa_flash_body, mode=mode, tq=tq, tk=tk, nk=nk, qpos0=qpos0, kpos0=kpos0,
                          kt_fn=kt_fn, nsp=nsp),
        grid=(b, sq // tq, nk),
        in_specs=in_specs,
        out_specs=pl.BlockSpec((None, tq, 512), lambda bi, i, j: (bi, i, 0)),
        out_shape=jax.ShapeDtypeStruct((b, sq, 512), F32),
        scratch_shapes=[pltpu.VMEM((r, LANE), BF16), pltpu.VMEM((r, 1), F32), pltpu.VMEM((r, 1), F32),
                        pltpu.VMEM((r, LANE), F32)],
        compiler_params=_cp(("parallel", "parallel", "arbitrary")),
    )(*args)


def _diff_body(lam_ref, q_ref, k_ref, v_ref, g_ref, o_ref, qs_ref, m_ref, l_ref, acc_ref, *,
               tq, tk, nk, qpos0, lam_init, kt_fn):
    i = pl.program_id(1)
    j = pl.program_id(2)
    r = 4 * tq
    npair = DIFF_HEADS // 2

    @pl.when(j == 0)
    def _init():
        lane = lax.broadcasted_iota(jnp.int32, (tq, LANE), 1)
        for hp in range(npair):
            q = q_ref[:, hp * LANE:(hp + 1) * LANE]
            parts = []
            for h in range(2):
                for mth in range(2):
                    lo = h * 2 * DIFF_QK + mth * DIFF_QK
                    parts.append(jnp.where((lane >= lo) & (lane < lo + DIFF_QK), q, 0.0))
            qs_ref[hp] = jnp.concatenate(parts, axis=0).astype(BF16)
        m_ref[...] = jnp.full((npair, r, 1), NEG, F32)
        l_ref[...] = jnp.zeros((npair, r, 1), F32)
        acc_ref[...] = jnp.zeros((npair, r, LANE), F32)

    kt, valid = kt_fn(i, j)

    @pl.when(valid)
    def _step():
        row = lax.broadcasted_iota(jnp.int32, (r, tk), 0)
        qpos = qpos0 + i * tq + (row & (tq - 1))
        kpos = kt * tk + lax.broadcasted_iota(jnp.int32, (r, tk), 1)
        mask = kpos <= qpos
        for hp in range(npair):
            k = k_ref[:, hp * LANE:(hp + 1) * LANE].astype(BF16)
            v = v_ref[:, hp * LANE:(hp + 1) * LANE].astype(BF16)
            s = jnp.where(mask, _dot_nt(qs_ref[hp], k), NEG)
            m_old = m_ref[hp]
            m_new = jnp.maximum(m_old, jnp.max(s, axis=1, keepdims=True))
            alpha = jnp.exp(m_old - m_new)
            p = jnp.where(mask, jnp.exp(s - m_new), 0.0)
            l_ref[hp] = alpha * l_ref[hp] + jnp.sum(p, axis=1, keepdims=True)
            acc_ref[hp] = alpha * acc_ref[hp] + _dot(p.astype(BF16), v)
            m_ref[hp] = m_new

    @pl.when(j == nk - 1)
    def _fin():
        lp = lam_ref[...]
        lam = (jnp.exp(jnp.sum(lp[0:1] * lp[1:2], axis=1, keepdims=True))
               - jnp.exp(jnp.sum(lp[2:3] * lp[3:4], axis=1, keepdims=True)) + lam_init)
        lane = lax.broadcasted_iota(jnp.int32, (tq, LANE), 1)
        lo_half = lane < 2 * DIFF_QK
        for hp in range(npair):
            l = l_ref[hp]
            a = acc_ref[hp] * (1.0 / jnp.where(l > 0.0, l, 1.0))
            o0 = a[0:tq] - lam * a[tq:2 * tq]
            o1 = a[2 * tq:3 * tq] - lam * a[3 * tq:4 * tq]
            o = jnp.where(lo_half, o0, o1)
            sq = o * o
            ms0 = jnp.sum(jnp.where(lo_half, sq, 0.0), axis=1, keepdims=True)
            ms1 = jnp.sum(jnp.where(lo_half, 0.0, sq), axis=1, keepdims=True)
            ms = jnp.where(lo_half, ms0, ms1) * (1.0 / (2 * DIFF_QK))
            y = o * lax.rsqrt(ms + EPS) * g_ref[...]
            o_ref[:, hp * LANE:(hp + 1) * LANE] = y * (1.0 - lam_init)


def diff_attend(lam_par, q, kv, gnorm, *, tq, tk, nk, qpos0, lam_init, kt_fn):
    b, sq, _ = q.shape
    r = 4 * tq
    npair = DIFF_HEADS // 2
    return pl.pallas_call(
        functools.partial(_diff_body, tq=tq, tk=tk, nk=nk, qpos0=qpos0, lam_init=lam_init, kt_fn=kt_fn),
        grid=(b, sq // tq, nk),
        in_specs=[pl.BlockSpec((4, DIFF_QK), lambda bi, i, j: (0, 0)),
                  pl.BlockSpec((None, tq, 512), lambda bi, i, j: (bi, i, 0)),
                  pl.BlockSpec((None, tk, 512), lambda bi, i, j: (bi, kt_fn(i, j)[0], 0)),
                  pl.BlockSpec((None, tk, 512), lambda bi, i, j: (bi, kt_fn(i, j)[0], 1)),
                  pl.BlockSpec((1, LANE), lambda bi, i, j: (0, 0))],
        out_specs=pl.BlockSpec((None, tq, 512), lambda bi, i, j: (bi, i, 0)),
        out_shape=jax.ShapeDtypeStruct((b, sq, 512), F32),
        scratch_shapes=[pltpu.VMEM((npair, r, LANE), BF16), pltpu.VMEM((npair, r, 1), F32),
                        pltpu.VMEM((npair, r, 1), F32), pltpu.VMEM((npair, r, LANE), F32)],
        compiler_params=_cp(("parallel", "parallel", "arbitrary")),
    )(lam_par, q, kv, kv, gnorm)


def _pool_body(x_ref, pw_ref, sc_ref, o_ref, *, sq, start):
    pos = start + lax.broadcasted_iota(jnp.int32, (sq, LANE), 0)
    for g, w in enumerate(POOL_WINDOWS):
        c0 = g * LANE
        x = x_ref[pl.ds(16, sq), c0:c0 + LANE]
        tot = x
        for back in range(1, w):
            tot = tot + x_ref[pl.ds(16 - back, sq), c0:c0 + LANE]
        cnt = jnp.minimum(w, pos + 1).astype(F32)
        mix = tot / cnt - x
        y = _dot(mix.astype(BF16), pw_ref[g])
        o_ref[:, c0:c0 + LANE] = y * sc_ref[:, c0:c0 + LANE]


def pool_mix(xcat, pw, scale, start):
    b, rows, _ = xcat.shape
    sq = rows - 16
    return pl.pallas_call(
        functools.partial(_pool_body, sq=sq, start=start), grid=(b,),
        in_specs=[pl.BlockSpec((None, rows, 512), lambda i: (i, 0, 0)),
                  pl.BlockSpec((4, LANE, LANE), lambda i: (0, 0, 0)),
                  pl.BlockSpec((1, 512), lambda i: (0, 0))],
        out_specs=pl.BlockSpec((None, sq, 512), lambda i: (i, 0, 0)),
        out_shape=jax.ShapeDtypeStruct((b, sq, 512), F32),
        compiler_params=_cp(("parallel",)),
    )(xcat, pw.astype(BF16), scale.reshape(1, 512))


def _gmlp_body(z_ref, ng_ref, nb_ref, ws_ref, bs_ref, o_ref, v_ref, *, tg):
    z = _gelu(z_ref[...])
    u = z[:, :GM_W]
    vr = z[:, GM_W:]
    xc = vr - jnp.mean(vr, axis=-1, keepdims=True)
    v = xc * lax.rsqrt(jnp.mean(xc * xc, axis=-1, keepdims=True) + EPS) * ng_ref[...] + nb_ref[...]
    v_ref[...] = v
    vb = v.astype(BF16)
    for c in range(tg // GM_CHUNK):
        r0 = c * GM_CHUNK
        for g in range(4):
            c0 = g * LANE
            mixed = _dot(ws_ref[g], vb[r0:r0 + GM_CHUNK, c0:c0 + LANE]) + bs_ref[:, c0:c0 + LANE]
            o_ref[r0:r0 + GM_CHUNK, c0:c0 + LANE] = u[r0:r0 + GM_CHUNK, c0:c0 + LANE] * mixed


def gmlp_mix(z, ng, nb, ws_tril, bs_exp):
    b, s, _ = z.shape
    tg = _tile(s, 512)
    return pl.pallas_call(
        functools.partial(_gmlp_body, tg=tg), grid=(b, s // tg),
        in_specs=[pl.BlockSpec((None, tg, 1024), lambda bi, i: (bi, i, 0)),
                  pl.BlockSpec((1, 512), lambda bi, i: (0, 0)),
                  pl.BlockSpec((1, 512), lambda bi, i: (0, 0)),
                  pl.BlockSpec((4, GM_CHUNK, GM_CHUNK), lambda bi, i: (0, 0, 0)),
                  pl.BlockSpec((GM_CHUNK, 512), lambda bi, i: (0, 0))],
        out_specs=[pl.BlockSpec((None, tg, 512), lambda bi, i: (bi, i, 0)),
                   pl.BlockSpec((None, tg, 512), lambda bi, i: (bi, i, 0))],
        out_shape=[jax.ShapeDtypeStruct((b, s, 512), F32), jax.ShapeDtypeStruct((b, s, 512), F32)],
        compiler_params=_cp(("parallel", "parallel")),
    )(z, ng.reshape(1, 512), nb.reshape(1, 512), ws_tril, bs_exp)


def _finish_body(x_ref, hn_ref, g3_ref, ocmp_ref, osel_ref, owin_ref, opool_ref, ogm_ref, odiff_ref,
                 wgate_ref, bgate_ref, wbr_ref, wo_ref, o_ref):
    hn = hn_ref[...]
    g3 = g3_ref[...]
    onsa = g3[:, 0:512] * ocmp_ref[...] + g3[:, 512:1024] * osel_ref[...] + g3[:, 1024:1536] * owin_ref[...]
    branches = (onsa, opool_ref[...], ogm_ref[...], odiff_ref[...])
    acc = jnp.zeros(o_ref.shape, F32)
    for n in range(N_BRANCH):
        c0 = n * D_MODEL
        gate = jax.nn.sigmoid(_dot(hn, wgate_ref[:, c0:c0 + D_MODEL]) + bgate_ref[:, c0:c0 + D_MODEL])
        acc = acc + gate * _dot(branches[n].astype(BF16), wbr_ref[n])
    o_ref[...] = x_ref[...] + _dot(acc.astype(BF16), wo_ref[...])


def finish_mixer(x, hn, g3, ocmp, osel, owin, opool, ogm, odiff, wgate, bgate, wbr, wo):
    t = x.shape[0]
    tm = _tile(t, 256)
    row = lambda w: pl.BlockSpec((tm, w), lambda i: (i, 0))
    const = lambda shape: pl.BlockSpec(shape, lambda i: (0,) * len(shape), pipeline_mode=pl.Buffered(1))
    return pl.pallas_call(
        _finish_body, grid=(t // tm,),
        in_specs=[row(1024), row(1024), row(1536), row(512), row(512), row(512), row(512), row(512), row(512),
                  const((D_MODEL, N_BRANCH * D_MODEL)), const((1, N_BRANCH * D_MODEL)),
                  const((N_BRANCH, BRANCH_W, D_MODEL)), const((D_MODEL, D_MODEL))],
        out_specs=row(1024),
        out_shape=jax.ShapeDtypeStruct((t, D_MODEL), F32),
        compiler_params=_cp(("parallel",)),
    )(x, hn, g3, ocmp, osel, owin, opool, ogm, odiff, wgate, bgate, wbr, wo)


def _ffn_body(x_ref, g_ref, wg_ref, wu_ref, wd_ref, o_ref, hn_ref, acc_ref, *, nf):
    j = pl.program_id(1)

    @pl.when(j == 0)
    def _init():
        hn_ref[...] = _rms(x_ref[...], g_ref[...]).astype(BF16)
        acc_ref[...] = jnp.zeros(acc_ref.shape, F32)

    h = hn_ref[...]
    a = _dot(h, wg_ref[...])
    act = a * jax.nn.sigmoid(a) * _dot(h, wu_ref[...])
    acc_ref[...] += _dot(act.astype(BF16), wd_ref[...])

    @pl.when(j == nf - 1)
    def _fin():
        o_ref[...] = x_ref[...] + acc_ref[...]


def ffn_swiglu(x, g, wg, wu, wd):
    t, d = x.shape
    f = wg.shape[1]
    tm = _tile(t, 512)
    tf = 256
    nf = f // tf
    return pl.pallas_call(
        functools.partial(_ffn_body, nf=nf), grid=(t // tm, nf),
        in_specs=[pl.BlockSpec((tm, d), lambda i, j: (i, 0)), pl.BlockSpec((1, d), lambda i, j: (0, 0)),
                  pl.BlockSpec((d, tf), lambda i, j: (0, j)), pl.BlockSpec((d, tf), lambda i, j: (0, j)),
                  pl.BlockSpec((tf, d), lambda i, j: (j, 0))],
        out_specs=pl.BlockSpec((tm, d), lambda i, j: (i, 0)),
        out_shape=jax.ShapeDtypeStruct((t, d), F32),
        scratch_shapes=[pltpu.VMEM((tm, d), BF16), pltpu.VMEM((tm, d), F32)],
        compiler_params=_cp(("parallel", "arbitrary")),
    )(x, g.reshape(1, d), wg.astype(BF16), wu.astype(BF16), wd.astype(BF16))


def _moe_body(x_ref, g_ref, r_ref, rb_ref, wg_ref, wu_ref, wd_ref, o_ref, hn_ref, comb_ref, acc_ref, *, nf):
    e = pl.program_id(1)
    f = pl.program_id(2)
    tm = x_ref.shape[0]
    lane = lax.broadcasted_iota(jnp.int32, (tm, LANE), 1)

    @pl.when((e == 0) & (f == 0))
    def _init():
        hn = _rms(x_ref[...], g_ref[...]).astype(BF16)
        hn_ref[...] = hn
        lg = _dot(hn, r_ref[...]) + rb_ref[...]
        m1 = jnp.max(lg, axis=1, keepdims=True)
        i1 = jnp.min(jnp.where(lg == m1, lane, LANE), axis=1, keepdims=True)
        lg2 = jnp.where(lane == i1, -3e38, lg)
        m2 = jnp.max(lg2, axis=1, keepdims=True)
        i2 = jnp.min(jnp.where(lg2 == m2, lane, LANE), axis=1, keepdims=True)
        e2 = jnp.exp(m2 - m1)
        w1 = 1.0 / (1.0 + e2)
        comb_ref[...] = jnp.where(lane == i1, w1, 0.0) + jnp.where(lane == i2, e2 * w1, 0.0)
        acc_ref[...] = jnp.zeros(acc_ref.shape, F32)

    ce = jnp.sum(jnp.where(lane == e, comb_ref[...], 0.0), axis=1, keepdims=True)
    h = hn_ref[...]
    a = _dot(h, wg_ref[...])
    act = a * jax.nn.sigmoid(a) * _dot(h, wu_ref[...])
    acc_ref[...] += ce * _dot(act.astype(BF16), wd_ref[...])

    @pl.when((e == N_EXPERTS - 1) & (f == nf - 1))
    def _fin():
        o_ref[...] = x_ref[...] + acc_ref[...]


def moe_swiglu(x, g, router, router_b, wg, wu, wd):
    t, d = x.shape
    f = wg.shape[2]
    tm = _tile(t, 512)
    tf = 512
    nf = f // tf
    rpad = jnp.zeros((d, LANE), F32).at[:, :N_EXPERTS].set(router).astype(BF16)
    rbpad = jnp.full((1, LANE), NEG, F32).at[0, :N_EXPERTS].set(router_b)
    return pl.pallas_call(
        functools.partial(_moe_body, nf=nf), grid=(t // tm, N_EXPERTS, nf),
        in_specs=[pl.BlockSpec((tm, d), lambda i, e, j: (i, 0)), pl.BlockSpec((1, d), lambda i, e, j: (0, 0)),
                  pl.BlockSpec((d, LANE), lambda i, e, j: (0, 0)), pl.BlockSpec((1, LANE), lambda i, e, j: (0, 0)),
                  pl.BlockSpec((None, d, tf), lambda i, e, j: (e, 0, j)),
                  pl.BlockSpec((None, d, tf), lambda i, e, j: (e, 0, j)),
                  pl.BlockSpec((None, tf, d), lambda i, e, j: (e, j, 0))],
        out_specs=pl.BlockSpec((tm, d), lambda i, e, j: (i, 0)),
        out_shape=jax.ShapeDtypeStruct((t, d), F32),
        scratch_shapes=[pltpu.VMEM((tm, d), BF16), pltpu.VMEM((tm, LANE), F32), pltpu.VMEM((tm, d), F32)],
        compiler_params=_cp(("parallel", "arbitrary", "arbitrary")),
    )(x, g.reshape(1, d), rpad, rbpad, wg.astype(BF16), wu.astype(BF16), wd.astype(BF16))


def _ple_body(x_ref, p_ref, g_ref, wg_ref, wp_ref, fg_ref, o_ref, *, final):
    x = x_ref[...]
    hn = _rms(x, g_ref[...]).astype(BF16)
    gate = jax.nn.sigmoid(_dot(hn, wg_ref[...]))
    y = x + gate * _dot(p_ref[...].astype(BF16), wp_ref[...])
    if final:
        y = _rms(y, fg_ref[...])
    o_ref[...] = y


def ple(x, p, g, wg, wp, fg, final):
    t, d = x.shape
    pd = p.shape[1]
    tm = _tile(t, 512)
    return pl.pallas_call(
        functools.partial(_ple_body, final=final), grid=(t // tm,),
        in_specs=[pl.BlockSpec((tm, d), lambda i: (i, 0)), pl.BlockSpec((tm, pd), lambda i: (i, 0)),
                  pl.BlockSpec((1, d), lambda i: (0, 0)), pl.BlockSpec((d, d), lambda i: (0, 0)),
                  pl.BlockSpec((pd, d), lambda i: (0, 0)), pl.BlockSpec((1, d), lambda i: (0, 0))],
        out_specs=pl.BlockSpec((tm, d), lambda i: (i, 0)),
        out_shape=jax.ShapeDtypeStruct((t, d), F32),
        compiler_params=_cp(("parallel",)),
    )(x, p, g.reshape(1, d), wg.astype(BF16), wp.astype(BF16), fg.reshape(1, d))


def _gather_body(pt_ref, cache_ref, new_ref, zero_ref, out_ref, sem, *, layer, n_pages, sq, pad):
    b = pl.program_id(0)
    past = n_pages * PAGE
    copies = []
    for p in range(n_pages):
        copies.append(pltpu.make_async_copy(cache_ref.at[layer, pt_ref[b, p]],
                                            out_ref.at[b, pl.ds(p * PAGE, PAGE)], sem))
    copies.append(pltpu.make_async_copy(new_ref.at[b], out_ref.at[b, pl.ds(past, sq)], sem))
    copies.append(pltpu.make_async_copy(zero_ref, out_ref.at[b, pl.ds(past + sq, pad - sq)], sem))
    for c in copies:
        c.start()
    for c in copies:
        c.wait()


def gather_pages(page_table, cache, new_rows, layer, pad):
    b, n_pages = page_table.shape
    sq, w = new_rows.shape[1:]
    past = n_pages * PAGE
    zeros = jnp.zeros((pad - sq, w), F32)
    any_spec = pl.BlockSpec(memory_space=pl.ANY)
    return pl.pallas_call(
        functools.partial(_gather_body, layer=layer, n_pages=n_pages, sq=sq, pad=pad),
        grid_spec=pltpu.PrefetchScalarGridSpec(
            num_scalar_prefetch=1, grid=(b,), in_specs=[any_spec, any_spec, any_spec], out_specs=any_spec,
            scratch_shapes=[pltpu.SemaphoreType.DMA(())]),
        out_shape=jax.ShapeDtypeStruct((b, past + pad, w), F32),
        compiler_params=_cp(("arbitrary",)),
    )(page_table, cache, new_rows, zeros)


def _q_perm():
    idx = np.zeros(512, np.int32)
    for j in range(NSA_J):
        for g in range(NSA_G):
            for d in range(DH):
                idx[j * LANE + g * DH + d] = (g * NSA_J + j) * DH + d
    return idx


def _gate3_perm():
    idx = np.zeros(3 * 512, np.int32)
    for c in range(3):
        for j in range(NSA_J):
            for g in range(NSA_G):
                idx[c * 512 + j * LANE + g * DH:c * 512 + j * LANE + (g + 1) * DH] = (g * NSA_J + j) * 3 + c
    return idx


def _cmp_to_sel(n_cmp, n_sel, nch, nsp):
    r = SEL_BLOCK // CMP_STRIDE
    k = np.arange(n_cmp)[:, None] - r * np.arange(n_sel)[None, :]
    m = sum(((k + n >= 0) & (k + n < r)).astype(np.float32) for n in range(CMP_BLOCK // CMP_STRIDE))
    out = np.zeros((nch, nsp), np.float32)
    out[:n_cmp, :n_sel] = m
    return out


def _block_expand(nsp, lk):
    return (np.arange(lk)[None, :] // SEL_BLOCK == np.arange(nsp)[:, None]).astype(np.float32)


def _cmp_weights(pe, w1, w2):
    eye = jnp.eye(2, dtype=F32)
    w6 = w1.reshape(2, NSA_G, 2, CMP_STRIDE, DH, DH)
    wc = jnp.einsum("kgxsdh,kK,gG->skgdxKGh", w6, eye, eye).reshape(CMP_STRIDE, 256, 512).astype(BF16)
    w2b = jnp.einsum("kghd,kK,gG->kghKGd", w2, eye, eye).reshape(256, 256).astype(BF16)
    pe_rows = jnp.transpose(pe, (0, 2, 1, 3)).reshape(4, CMP_BLOCK * DH)
    xb = jnp.einsum("rc,rR->rRc", pe_rows, jnp.eye(4, dtype=F32)).reshape(4, 4 * CMP_BLOCK * DH)
    xb = jnp.zeros((16, 4 * CMP_BLOCK * DH), F32).at[:4].set(xb)
    wb = jnp.zeros((4 * CMP_BLOCK * DH, LANE), F32).at[:, :DH].set(w1.reshape(4 * CMP_BLOCK * DH, DH))
    bias = mm(xb, wb)[:4, :DH].reshape(1, 256)
    return wc, bias, w2b


def _mixer(x, lw, lam_init, *, b, sq, sample=None):
    t = b * sq
    hn = rmsnorm(x, lw["norm_mix_g"], BF16)
    w_in = lw["w_in"]
    qperm = _q_perm()
    q = mm(hn, w_in[:, OFF_Q:OFF_KV][:, qperm], scale=DH ** -0.5).reshape(b, sq, 512)
    rows4 = mm(hn, w_in[:, OFF_KV:OFF_KV + 512]).reshape(b, sq, 512)
    winkv = mm(hn, w_in[:, OFF_KV + 512:OFF_GATE]).reshape(b, sq, 256)
    g3 = mm(hn, w_in[:, OFF_GATE:OFF_POOL][:, _gate3_perm()], act="sigmoid")
    zpool = mm(hn, w_in[:, OFF_POOL:OFF_GM]).reshape(b, sq, 512)
    zgm = mm(hn, w_in[:, OFF_GM:OFF_DQ]).reshape(b, sq, 1024)
    dq = mm(hn, w_in[:, OFF_DQ:OFF_DKV], scale=DIFF_QK ** -0.5).reshape(b, sq, 512)
    dkv = mm(hn, w_in[:, OFF_DKV:OFF_END]).reshape(b, sq, 1024)

    wc, cbias, w2b = _cmp_weights(lw["cmp_pe"], lw["cmp_w1"], lw["cmp_w2"])
    lam_par = jnp.stack([lw["diff_lq1"], lw["diff_lk1"], lw["diff_lq2"], lw["diff_lk2"]])
    gnorm = jnp.tile(lw["diff_norm_g"], 2).reshape(1, LANE)

    if sample is None:
        past, l_tot = 0, sq
        kv_all, dkv_all = rows4, dkv
        nch = sq // CMP_STRIDE
        tq = _tile(sq, 128)
        tk_sel = _tile(sq, 256)
        nk_sel = sq // tk_sel
        tw = _tile(sq, 256)
        nband = WINDOW // tw + 1
        win_all, kpos0_w, nk_w = winkv, 0, nband
        win_kt = lambda i, j: (jnp.maximum(i - (nband - 1) + j, 0), i - (nband - 1) + j >= 0)
        tq_w = tw
        tq_d = _tile(sq, 256)
        tk_d = tq_d
        nk_d = sq // tk_d
        xcat = jnp.concatenate([jnp.zeros((b, 16, 512), F32), zpool], axis=1)
        zgm_in = zgm
    else:
        layer = sample["layer"]
        pt = sample["page_table"]
        past = pt.shape[1] * PAGE
        l_tot = past + sq
        pad = 512
        kv_all = gather_pages(pt, sample["cache_nsa"], rows4, layer, pad)
        dkv_all = gather_pages(pt, sample["cache_diff"], dkv, layer, pad)
        nch = past // CMP_STRIDE
        tq = sq
        tk_sel = 512
        nk_sel = (past + pad) // tk_sel
        wbuf = sample["win"].shape[1]
        tw = 128
        wrows = -(-(wbuf + sq) // tw) * tw
        win_all = jnp.concatenate([sample["win"], winkv, jnp.zeros((b, wrows - wbuf - sq, 256), F32)], axis=1)
        kpos0_w, nk_w = past - wbuf, wrows // tw
        win_kt = lambda i, j: (j, j >= 0)
        tq_w = sq
        tq_d = sq
        tk_d = 512
        nk_d = (past + pad) // tk_d
        xcat = jnp.concatenate([jnp.zeros((b, 1, 512), F32), sample["pool"], zpool], axis=1)
        zgm_in = jnp.concatenate([zgm, jnp.zeros((b, GM_CHUNK - sq, 1024), F32)], axis=1)

    n_cmp = l_tot // CMP_STRIDE - 1
    n_sel = -(-l_tot // SEL_BLOCK)
    nsp = -(-n_sel // LANE) * LANE
    kcvc = compress(kv_all, wc, cbias, w2b, nch)
    msel = jnp.asarray(_cmp_to_sel(n_cmp, n_sel, nch, nsp), BF16)
    ocmp, selmask = cmp_attend(q, kcvc, msel, tq=tq, n_cmp=n_cmp, n_sel=n_sel, qpos0=past)
    emat = jnp.asarray(_block_expand(nsp, kv_all.shape[1]), BF16)

    def sel_kt(i, j):
        last = (past + (i + 1) * tq - 1) // tk_sel
        return jnp.minimum(j, last), j <= last

    osel = nsa_flash(q, kv_all, 2, 3, mode="sel", tq=tq, tk=tk_sel, nk=nk_sel, qpos0=past, kpos0=0,
                     kt_fn=sel_kt, sel=selmask, emat=emat)
    owin = nsa_flash(q, win_all, 0, 1, mode="win", tq=tq_w, tk=tw, nk=nk_w, qpos0=past, kpos0=kpos0_w,
                     kt_fn=win_kt)

    opool = pool_mix(xcat, lw["pool_w"], lw["pool_scale"], past)
    ws_tril = jnp.tril(lw["gm_ws"]).astype(BF16)
    bs_exp = jnp.repeat(lw["gm_bs"].T, LANE, axis=1)
    ogm, gm_v = gmlp_mix(zgm_in, lw["gm_ng"], lw["gm_nb"], ws_tril, bs_exp)

    def diff_kt(i, j):
        last = (past + (i + 1) * tq_d - 1) // tk_d
        return jnp.minimum(j, last), j <= last

    odiff = diff_attend(lam_par, dq, dkv_all, gnorm, tq=tq_d, tk=tk_d, nk=nk_d, qpos0=past,
                        lam_init=lam_init, kt_fn=diff_kt)

    if sample is not None:
        ogm = ogm[:, :sq]
        gm_v = gm_v[:, :sq]
    wbr = lw["w_branch"].at[0].set(lw["w_branch"][0][qperm]).astype(BF16)
    out = finish_mixer(x, hn, g3, ocmp.reshape(t, 512), osel.reshape(t, 512), owin.reshape(t, 512),
                       opool.reshape(t, 512), ogm.reshape(t, 512), odiff.reshape(t, 512),
                       lw["w_gate"].astype(BF16), lw["b_gate"].reshape(1, -1), wbr, lw["w_o"].astype(BF16))
    states = dict(rows4=rows4, winkv=winkv, win_all=win_all, xcat=xcat, gm_v=gm_v, dkv=dkv)
    return out, states


def kernel(x_prompt, x_sample, cache_nsa, cache_diff, state_nsa_win, state_pool, page_table, p_prompt, p_sample, norm_mix_g, w_in, nsa_cmp_pe, nsa_cmp_w1, nsa_cmp_w2, pool_w, pool_scale, gm_norm_g, gm_norm_b, gm_ws, gm_bs, diff_lq1, diff_lk1, diff_lq2, diff_lk2, diff_norm_g, w_branch, w_gate, b_gate, w_o, norm_ffn_g, ffn_w_gate, ffn_w_up, ffn_w_down, moe_router, moe_router_b, moe_w_gate, moe_w_up, moe_w_down, ple_norm_g, ple_w_gate, ple_w_proj, final_norm_g):
    bp, sp, d = x_prompt.shape
    bs, ss, _ = x_sample.shape
    depth = w_in.shape[0]
    n_phys = cache_nsa.shape[1]
    wbuf = state_nsa_win.shape[2]
    cache_nsa2 = cache_nsa.reshape(depth, n_phys, PAGE, 512)
    cache_diff2 = cache_diff.reshape(depth, n_phys, PAGE, 1024)
    xp = x_prompt.reshape(bp * sp, d)
    xs = x_sample.reshape(bs * ss, d)
    outs = {k: [] for k in ("nsa_p", "nsa_s", "win_p", "win_s", "pool_p", "pool_s", "gmv_s", "diff_p", "diff_s")}
    for l in range(depth):
        lw = dict(norm_mix_g=norm_mix_g[l], w_in=w_in[l], cmp_pe=nsa_cmp_pe[l], cmp_w1=nsa_cmp_w1[l],
                  cmp_w2=nsa_cmp_w2[l], pool_w=pool_w[l], pool_scale=pool_scale[l], gm_ng=gm_norm_g[l],
                  gm_nb=gm_norm_b[l], gm_ws=gm_ws[l], gm_bs=gm_bs[l], diff_lq1=diff_lq1[l], diff_lk1=diff_lk1[l],
                  diff_lq2=diff_lq2[l], diff_lk2=diff_lk2[l], diff_norm_g=diff_norm_g[l], w_branch=w_branch[l],
                  w_gate=w_gate[l], b_gate=b_gate[l], w_o=w_o[l])
        lam_init = 0.8 - 0.6 * math.exp(-0.3 * l)
        xp, st_p = _mixer(xp, lw, lam_init, b=bp, sq=sp)
        sample = dict(cache_nsa=cache_nsa2, cache_diff=cache_diff2, win=state_nsa_win[l].reshape(bs, wbuf, 256),
                      pool=state_pool[l], page_table=page_table, layer=l)
        xs, st_s = _mixer(xs, lw, lam_init, b=bs, sq=ss, sample=sample)
        outs["nsa_p"].append(st_p["rows4"].reshape(bp, sp, 4, NSA_G, DH))
        outs["nsa_s"].append(st_s["rows4"].reshape(bs, ss, 4, NSA_G, DH))
        wkeep = min(WINDOW, sp)
        outs["win_p"].append(st_p["winkv"][:, sp - wkeep:].reshape(bp, wkeep, 2, NSA_G, DH))
        outs["win_s"].append(st_s["win_all"][:, ss:ss + wbuf].reshape(bs, wbuf, 2, NSA_G, DH))
        outs["pool_p"].append(st_p["xcat"][:, -POOL_MEM:])
        outs["pool_s"].append(st_s["xcat"][:, -POOL_MEM:])
        outs["gmv_s"].append(st_s["gm_v"])
        outs["diff_p"].append(st_p["dkv"].reshape(bp, sp, 2, DIFF_HEADS, 2 * DIFF_QK))
        outs["diff_s"].append(st_s["dkv"].reshape(bs, ss, 2, DIFF_HEADS, 2 * DIFF_QK))
        i = l // 2
        if l % 2 == 0:
            xp = ffn_swiglu(xp, norm_ffn_g[l], ffn_w_gate[i], ffn_w_up[i], ffn_w_down[i])
            xs = ffn_swiglu(xs, norm_ffn_g[l], ffn_w_gate[i], ffn_w_up[i], ffn_w_down[i])
        else:
            xp = moe_swiglu(xp, norm_ffn_g[l], moe_router[i], moe_router_b[i], moe_w_gate[i], moe_w_up[i], moe_w_down[i])
            xs = moe_swiglu(xs, norm_ffn_g[l], moe_router[i], moe_router_b[i], moe_w_gate[i], moe_w_up[i], moe_w_down[i])
        final = l == depth - 1
        xp = ple(xp, p_prompt[l].reshape(bp * sp, -1), ple_norm_g[l], ple_w_gate[l], ple_w_proj[l], final_norm_g, final)
        xs = ple(xs, p_sample[l].reshape(bs * ss, -1), ple_norm_g[l], ple_w_gate[l], ple_w_proj[l], final_norm_g, final)
    st = lambda k: jnp.stack(outs[k])
    return (xp.reshape(bp, sp, d), xs.reshape(bs, ss, d), st("nsa_p"), st("nsa_s"), st("win_p"), st("win_s"),
            st("pool_p"), st("pool_s"), st("gmv_s"), st("diff_p"), st("diff_s"))
```

```python
import functools
import math

import numpy as np
import jax
import jax.numpy as jnp
from jax import lax
from jax.experimental import pallas as pl
from jax.experimental.pallas import tpu as pltpu

F32 = jnp.float32
BF16 = jnp.bfloat16

D_MODEL = 1024
PAGE = 128
NSA_HEADS = 8
NSA_G = 2
NSA_J = 4
DH = 64
CMP_BLOCK = 32
CMP_STRIDE = 16
SEL_BLOCK = 64
SEL_TOPK = 16
FORCE_BONUS = 100.0
WINDOW = 512
POOL_WINDOWS = (2, 4, 8, 16)
POOL_MEM = 15
GM_W = 512
GM_CHUNK = 128
DIFF_HEADS = 8
DIFF_QK = 32
N_BRANCH = 4
BRANCH_W = 512
N_EXPERTS = 8
EPS = 1e-6
NEG = -1e30
LANE = 128
VMEM_LIMIT = 56 * 1024 * 1024

OFF_Q, OFF_KV, OFF_GATE, OFF_POOL, OFF_GM, OFF_DQ, OFF_DKV, OFF_END = 0, 512, 1280, 1304, 1816, 2840, 3352, 4376


def _tile(n, pref):
    t = min(n, pref)
    while n % t:
        t //= 2
    return t


def _cp(sem, vmem=VMEM_LIMIT):
    return pltpu.CompilerParams(dimension_semantics=sem, vmem_limit_bytes=vmem)


def _gelu(x):
    return 0.5 * x * (1.0 + jnp.tanh(0.7978845608028654 * (x + 0.044715 * (x * x * x))))


def _rms(x, g):
    return x * lax.rsqrt(jnp.mean(x * x, axis=-1, keepdims=True) + EPS) * g


def _dot(a, b):
    return jnp.dot(a, b, preferred_element_type=F32)


def _dot_nt(a, b):
    return lax.dot_general(a, b, (((1,), (1,)), ((), ())), preferred_element_type=F32)


def _rmsnorm_body(x_ref, g_ref, o_ref):
    o_ref[...] = _rms(x_ref[...], g_ref[...]).astype(o_ref.dtype)


def rmsnorm(x, g, out_dtype):
    t, d = x.shape
    tm = _tile(t, 512)
    return pl.pallas_call(
        _rmsnorm_body, name="rmsnorm", grid=(t // tm,),
        in_specs=[pl.BlockSpec((tm, d), lambda i: (i, 0)), pl.BlockSpec((1, d), lambda i: (0, 0))],
        out_specs=pl.BlockSpec((tm, d), lambda i: (i, 0)),
        out_shape=jax.ShapeDtypeStruct((t, d), out_dtype),
        compiler_params=_cp(("parallel",)),
    )(x, g.reshape(1, d))


def _mm_body(x_ref, w_ref, o_ref, *, scale, act):
    y = _dot(x_ref[...].astype(BF16), w_ref[...])
    if scale is not None:
        y = y * scale
    if act == "sigmoid":
        y = jax.nn.sigmoid(y)
    o_ref[...] = y.astype(o_ref.dtype)


def mm(x, w, *, scale=None, act=None, out_dtype=F32):
    t, k = x.shape
    n = w.shape[1]
    tm = _tile(t, 512)
    tn = _tile(n, 512)
    return pl.pallas_call(
        functools.partial(_mm_body, scale=scale, act=act), name="mm", grid=(t // tm, n // tn),
        in_specs=[pl.BlockSpec((tm, k), lambda i, j: (i, 0)), pl.BlockSpec((k, tn), lambda i, j: (0, j))],
        out_specs=pl.BlockSpec((tm, tn), lambda i, j: (i, j)),
        out_shape=jax.ShapeDtypeStruct((t, n), out_dtype),
        compiler_params=_cp(("parallel", "parallel")),
    )(x, w.astype(BF16))


def _compress_body(xk_ref, xv_ref, wc_ref, bias_ref, w2_ref, o_ref, *, nch):
    _compress_core(lambda s: xk_ref[pl.ds(s, nch, stride=CMP_STRIDE), :],
                   lambda s: xv_ref[pl.ds(s, nch, stride=CMP_STRIDE), :], wc_ref, bias_ref, w2_ref, o_ref, nch)


def _compress_core(load_k, load_v, wc_ref, bias_ref, w2_ref, o_ref, nch):
    acc = jnp.zeros((nch, 512), F32)
    for s in range(CMP_STRIDE):
        xs = jnp.concatenate([load_k(s), load_v(s)], axis=1).astype(BF16)
        acc = acc + _dot(xs, wc_ref[s])
    hi_next = pltpu.roll(acc[:, 256:], nch - 1, 0)
    hid = _gelu(acc[:, :256] + hi_next + bias_ref[...])
    o_ref[...] = _dot(hid.astype(BF16), w2_ref[...]).astype(o_ref.dtype)


def compress(rows, wc, bias, w2, nch):
    b = rows.shape[0]
    return pl.pallas_call(
        functools.partial(_compress_body, nch=nch), name="compress", grid=(b,),
        in_specs=[pl.BlockSpec((None, nch * CMP_STRIDE, LANE), lambda i: (i, 0, 0)),
                  pl.BlockSpec((None, nch * CMP_STRIDE, LANE), lambda i: (i, 0, 1)),
                  pl.BlockSpec((CMP_STRIDE, 256, 512), lambda i: (0, 0, 0)),
                  pl.BlockSpec((1, 256), lambda i: (0, 0)),
                  pl.BlockSpec((256, 256), lambda i: (0, 0))],
        out_specs=pl.BlockSpec((None, nch, 256), lambda i: (i, 0, 0)),
        out_shape=jax.ShapeDtypeStruct((b, nch, 256), BF16),
        compiler_params=_cp(("parallel",)),
    )(rows, rows, wc, bias, w2)


def _stack_heads(q, tq):
    lane = lax.broadcasted_iota(jnp.int32, (tq, LANE), 1)
    parts = []
    for g in range(NSA_G):
        keep = (lane < DH) if g == 0 else (lane >= DH)
        for j in range(NSA_J):
            parts.append(jnp.where(keep, q[:, j * LANE:(j + 1) * LANE], 0.0))
    return jnp.concatenate(parts, axis=0)


def _unstack_heads(o, tq):
    lane = lax.broadcasted_iota(jnp.int32, (tq, LANE), 1)
    outs = []
    for j in range(NSA_J):
        outs.append(jnp.where(lane < DH, o[j * tq:(j + 1) * tq], o[(NSA_J + j) * tq:(NSA_J + j + 1) * tq]))
    return jnp.concatenate(outs, axis=1)


def _cmp_body(q_ref, kv_ref, msel_ref, o_ref, sel_ref, *, tq, nch, n_cmp, n_sel, nsp, qpos0):
    i = pl.program_id(1)
    r = 8 * tq
    qs = _stack_heads(q_ref[...], tq).astype(BF16)
    kc = kv_ref[:, 0:LANE]
    vc = kv_ref[:, LANE:2 * LANE]
    s = _dot_nt(qs, kc)
    row = lax.broadcasted_iota(jnp.int32, (r, nch), 0)
    col = lax.broadcasted_iota(jnp.int32, (r, nch), 1)
    qpos = qpos0 + i * tq + (row & (tq - 1))
    cmask = (col * CMP_STRIDE + (CMP_BLOCK - 1) <= qpos) & (col < n_cmp)
    s = jnp.where(cmask, s, NEG)
    m = jnp.max(s, axis=1, keepdims=True)
    e = jnp.where(cmask, jnp.exp(s - m), 0.0)
    l = jnp.sum(e, axis=1, keepdims=True)
    p = (e * (1.0 / jnp.where(l > 0.0, l, 1.0))).astype(BF16)
    o_ref[...] = _unstack_heads(_dot(p, vc), tq)
    imp_all = _dot(p, msel_ref[...])
    blk = lax.broadcasted_iota(jnp.int32, (tq, nsp), 1)
    qp = qpos0 + i * tq + lax.broadcasted_iota(jnp.int32, (tq, nsp), 0)
    cur = qp // SEL_BLOCK
    valid = (blk <= cur) & (blk < n_sel)
    forced = (blk == 0) | (blk == cur) | (blk == cur - 1)
    for g in range(NSA_G):
        imp = imp_all[(g * NSA_J) * tq:(g * NSA_J + 1) * tq]
        for j in range(1, NSA_J):
            imp = imp + imp_all[(g * NSA_J + j) * tq:(g * NSA_J + j + 1) * tq]
        score = jnp.where(valid, imp + FORCE_BONUS * forced.astype(F32), -1.0)
        score = jnp.where(blk < n_sel, score, -2.0)
        rank = jnp.zeros((tq, nsp), F32)
        for mth in range(n_sel):
            cm = score[:, mth:mth + 1]
            beats = (cm > score) | ((cm == score) & (blk > mth))
            rank = rank + beats.astype(F32)
        chosen = (rank < float(min(SEL_TOPK, n_sel))) & valid
        sel_ref[:, g * nsp:(g + 1) * nsp] = chosen.astype(F32)


def cmp_attend(q, kcvc, msel, *, tq, n_cmp, n_sel, qpos0):
    b, sq, _ = q.shape
    nch = kcvc.shape[1]
    nsp = msel.shape[1]
    return pl.pallas_call(
        functools.partial(_cmp_body, tq=tq, nch=nch, n_cmp=n_cmp, n_sel=n_sel, nsp=nsp, qpos0=qpos0),
        name="cmp_attend",
        grid=(b, sq // tq),
        in_specs=[pl.BlockSpec((None, tq, 512), lambda bi, i: (bi, i, 0)),
                  pl.BlockSpec((None, nch, 256), lambda bi, i: (bi, 0, 0)),
                  pl.BlockSpec((nch, nsp), lambda bi, i: (0, 0))],
        out_specs=[pl.BlockSpec((None, tq, 512), lambda bi, i: (bi, i, 0)),
                   pl.BlockSpec((None, tq, 2 * nsp), lambda bi, i: (bi, i, 0))],
        out_shape=[jax.ShapeDtypeStruct((b, sq, 512), F32), jax.ShapeDtypeStruct((b, sq, 2 * nsp), F32)],
        compiler_params=_cp(("parallel", "parallel")),
    )(q, kcvc, msel)


def _flash_update(qs, k, v, mask, m, l, acc):
    s = _dot_nt(qs, k)
    if mask is not None:
        s = jnp.where(mask, s, NEG)
    m_new = jnp.maximum(m, jnp.max(s, axis=1, keepdims=True))
    alpha = jnp.exp(m - m_new)
    p = jnp.exp(s - m_new)
    if mask is not None:
        p = jnp.where(mask, p, 0.0)
    l_new = alpha * l + jnp.sum(p, axis=1, keepdims=True)
    acc_new = alpha * acc + _dot(p.astype(BF16), v)
    return m_new, l_new, acc_new


def _sel_mask(sel, e, nsp):
    m0 = _dot(sel[:, :nsp], e)
    m1 = _dot(sel[:, nsp:], e)
    return jnp.concatenate([m0] * NSA_J + [m1] * NSA_J, axis=0) > 0.5


def _nsa_flash_body(*refs, mode, tq, tk, nk, qpos0, kpos0, kt_fn, nsp):
    if mode == "sel":
        q_ref, k_ref, v_ref, sel_ref, e_ref, o_ref, qs_ref, m_ref, l_ref, acc_ref = refs
    else:
        q_ref, k_ref, v_ref, o_ref, qs_ref, m_ref, l_ref, acc_ref = refs
    i = pl.program_id(1)
    j = pl.program_id(2)
    r = 8 * tq

    @pl.when(j == 0)
    def _init():
        qs_ref[...] = _stack_heads(q_ref[...], tq).astype(BF16)
        m_ref[...] = jnp.full((r, 1), NEG, F32)
        l_ref[...] = jnp.zeros((r, 1), F32)
        acc_ref[...] = jnp.zeros((r, LANE), F32)

    kt, valid = kt_fn(i, j)

    @pl.when(valid)
    def _step():
        row = lax.broadcasted_iota(jnp.int32, (r, tk), 0)
        qpos = qpos0 + i * tq + (row & (tq - 1))
        kpos = kpos0 + kt * tk + lax.broadcasted_iota(jnp.int32, (r, tk), 1)
        mask = kpos <= qpos
        if mode == "win":
            mask = mask & (kpos > qpos - WINDOW)
        else:
            mask = mask & _sel_mask(sel_ref[...].astype(BF16), e_ref[...], nsp)
        m, l, acc = _flash_update(qs_ref[...], k_ref[...].astype(BF16), v_ref[...].astype(BF16), mask,
                                  m_ref[...], l_ref[...], acc_ref[...])
        m_ref[...] = m
        l_ref[...] = l
        acc_ref[...] = acc

    @pl.when(j == nk - 1)
    def _fin():
        l = l_ref[...]
        o = acc_ref[...] * (1.0 / jnp.where(l > 0.0, l, 1.0))
        o_ref[...] = _unstack_heads(o, tq)


def nsa_flash(q, kv, kcol, vcol, *, mode, tq, tk, nk, qpos0, kpos0, kt_fn, sel=None, emat=None):
    b, sq, _ = q.shape
    nsp = 0 if sel is None else sel.shape[2] // 2

    def kmap(col):
        return lambda bi, i, j: (bi, kt_fn(i, j)[0], col)

    in_specs = [pl.BlockSpec((None, tq, 512), lambda bi, i, j: (bi, i, 0)),
                pl.BlockSpec((None, tk, LANE), kmap(kcol)),
                pl.BlockSpec((None, tk, LANE), kmap(vcol))]
    args = [q, kv, kv]
    if mode == "sel":
        in_specs += [pl.BlockSpec((None, tq, 2 * nsp), lambda bi, i, j: (bi, i, 0)),
                     pl.BlockSpec((nsp, tk), lambda bi, i, j: (0, kt_fn(i, j)[0]))]
        args += [sel, emat]
    r = 8 * tq
    return pl.pallas_call(
        functools.partial(_nsa_flash_body, mode=mode, tq=tq, tk=tk, nk=nk, qpos0=qpos0, kpos0=kpos0,
                          kt_fn=kt_fn, nsp=nsp),
        name="nsa_" + mode, grid=(b, sq // tq, nk),
        in_specs=in_specs,
        out_specs=pl.BlockSpec((None, tq, 512), lambda bi, i, j: (bi, i, 0)),
        out_shape=jax.ShapeDtypeStruct((b, sq, 512), F32),
        scratch_shapes=[pltpu.VMEM((r, LANE), BF16), pltpu.VMEM((r, 1), F32), pltpu.VMEM((r, 1), F32),
                        pltpu.VMEM((r, LANE), F32)],
        compiler_params=_cp(("parallel", "parallel", "arbitrary")),
    )(*args)


def _diff_init(q_ref, qs_ref, m_ref, l_ref, acc_ref, tq):
    r = 4 * tq
    npair = DIFF_HEADS // 2
    lane = lax.broadcasted_iota(jnp.int32, (tq, LANE), 1)
    for hp in range(npair):
        q = q_ref[:, hp * LANE:(hp + 1) * LANE]
        parts = []
        for h in range(2):
            for mth in range(2):
                lo = h * 2 * DIFF_QK + mth * DIFF_QK
                parts.append(jnp.where((lane >= lo) & (lane < lo + DIFF_QK), q, 0.0))
        qs_ref[hp] = jnp.concatenate(parts, axis=0).astype(BF16)
    m_ref[...] = jnp.full((npair, r, 1), NEG, F32)
    l_ref[...] = jnp.zeros((npair, r, 1), F32)
    acc_ref[...] = jnp.zeros((npair, r, LANE), F32)


def _diff_step(k_of, v_of, mask, qs_ref, m_ref, l_ref, acc_ref):
    for hp in range(DIFF_HEADS // 2):
        m, l, acc = _flash_update(qs_ref[hp], k_of(hp), v_of(hp), mask, m_ref[hp], l_ref[hp], acc_ref[hp])
        m_ref[hp] = m
        l_ref[hp] = l
        acc_ref[hp] = acc


def _diff_fin(lam_ref, g_ref, o_ref, l_ref, acc_ref, tq, lam_init):
    lp = lam_ref[...]
    lam = (jnp.exp(jnp.sum(lp[0:1] * lp[1:2], axis=1, keepdims=True))
           - jnp.exp(jnp.sum(lp[2:3] * lp[3:4], axis=1, keepdims=True)) + lam_init)
    lane = lax.broadcasted_iota(jnp.int32, (tq, LANE), 1)
    lo_half = lane < 2 * DIFF_QK
    for hp in range(DIFF_HEADS // 2):
        l = l_ref[hp]
        a = acc_ref[hp] * (1.0 / jnp.where(l > 0.0, l, 1.0))
        o0 = a[0:tq] - lam * a[tq:2 * tq]
        o1 = a[2 * tq:3 * tq] - lam * a[3 * tq:4 * tq]
        o = jnp.where(lo_half, o0, o1)
        sq = o * o
        ms0 = jnp.sum(jnp.where(lo_half, sq, 0.0), axis=1, keepdims=True)
        ms1 = jnp.sum(jnp.where(lo_half, 0.0, sq), axis=1, keepdims=True)
        ms = jnp.where(lo_half, ms0, ms1) * (1.0 / (2 * DIFF_QK))
        y = o * lax.rsqrt(ms + EPS) * g_ref[...]
        o_ref[:, hp * LANE:(hp + 1) * LANE] = y * (1.0 - lam_init)


def _diff_body(lam_ref, q_ref, k_ref, v_ref, g_ref, o_ref, qs_ref, m_ref, l_ref, acc_ref, *,
               tq, tk, nk, qpos0, lam_init, kt_fn):
    i = pl.program_id(1)
    j = pl.program_id(2)
    r = 4 * tq

    @pl.when(j == 0)
    def _init():
        _diff_init(q_ref, qs_ref, m_ref, l_ref, acc_ref, tq)

    kt, valid = kt_fn(i, j)

    @pl.when(valid)
    def _step():
        row = lax.broadcasted_iota(jnp.int32, (r, tk), 0)
        qpos = qpos0 + i * tq + (row & (tq - 1))
        kpos = kt * tk + lax.broadcasted_iota(jnp.int32, (r, tk), 1)
        _diff_step(lambda hp: k_ref[:, hp * LANE:(hp + 1) * LANE].astype(BF16),
                   lambda hp: v_ref[:, hp * LANE:(hp + 1) * LANE].astype(BF16),
                   kpos <= qpos, qs_ref, m_ref, l_ref, acc_ref)

    @pl.when(j == nk - 1)
    def _fin():
        _diff_fin(lam_ref, g_ref, o_ref, l_ref, acc_ref, tq, lam_init)


def diff_attend(lam_par, q, kv, gnorm, *, tq, tk, nk, qpos0, lam_init, kt_fn):
    b, sq, _ = q.shape
    r = 4 * tq
    npair = DIFF_HEADS // 2
    return pl.pallas_call(
        functools.partial(_diff_body, tq=tq, tk=tk, nk=nk, qpos0=qpos0, lam_init=lam_init, kt_fn=kt_fn),
        name="diff_attend",
        grid=(b, sq // tq, nk),
        in_specs=[pl.BlockSpec((4, DIFF_QK), lambda bi, i, j: (0, 0)),
                  pl.BlockSpec((None, tq, 512), lambda bi, i, j: (bi, i, 0)),
                  pl.BlockSpec((None, tk, 512), lambda bi, i, j: (bi, kt_fn(i, j)[0], 0)),
                  pl.BlockSpec((None, tk, 512), lambda bi, i, j: (bi, kt_fn(i, j)[0], 1)),
                  pl.BlockSpec((1, LANE), lambda bi, i, j: (0, 0))],
        out_specs=pl.BlockSpec((None, tq, 512), lambda bi, i, j: (bi, i, 0)),
        out_shape=jax.ShapeDtypeStruct((b, sq, 512), F32),
        scratch_shapes=[pltpu.VMEM((npair, r, LANE), BF16), pltpu.VMEM((npair, r, 1), F32),
                        pltpu.VMEM((npair, r, 1), F32), pltpu.VMEM((npair, r, LANE), F32)],
        compiler_params=_cp(("parallel", "parallel", "arbitrary")),
    )(lam_par, q, kv, kv, gnorm)


def _pool_body(x_ref, pw_ref, sc_ref, o_ref, *, sq, start):
    pos = start + lax.broadcasted_iota(jnp.int32, (sq, LANE), 0)
    for g, w in enumerate(POOL_WINDOWS):
        c0 = g * LANE
        x = x_ref[pl.ds(16, sq), c0:c0 + LANE]
        tot = x
        for back in range(1, w):
            tot = tot + x_ref[pl.ds(16 - back, sq), c0:c0 + LANE]
        cnt = jnp.minimum(w, pos + 1).astype(F32)
        mix = tot / cnt - x
        y = _dot(mix.astype(BF16), pw_ref[g])
        o_ref[:, c0:c0 + LANE] = y * sc_ref[:, c0:c0 + LANE]


def pool_mix(xcat, pw, scale, start):
    b, rows, _ = xcat.shape
    sq = rows - 16
    return pl.pallas_call(
        functools.partial(_pool_body, sq=sq, start=start), name="pool_mix", grid=(b,),
        in_specs=[pl.BlockSpec((None, rows, 512), lambda i: (i, 0, 0)),
                  pl.BlockSpec((4, LANE, LANE), lambda i: (0, 0, 0)),
                  pl.BlockSpec((1, 512), lambda i: (0, 0))],
        out_specs=pl.BlockSpec((None, sq, 512), lambda i: (i, 0, 0)),
        out_shape=jax.ShapeDtypeStruct((b, sq, 512), F32),
        compiler_params=_cp(("parallel",)),
    )(xcat, pw.astype(BF16), scale.reshape(1, 512))


def _gmlp_body(z_ref, ng_ref, nb_ref, ws_ref, bs_ref, o_ref, v_ref, *, tg):
    z = _gelu(z_ref[...])
    u = z[:, :GM_W]
    vr = z[:, GM_W:]
    xc = vr - jnp.mean(vr, axis=-1, keepdims=True)
    v = xc * lax.rsqrt(jnp.mean(xc * xc, axis=-1, keepdims=True) + EPS) * ng_ref[...] + nb_ref[...]
    v_ref[...] = v
    vb = v.astype(BF16)
    for c in range(tg // GM_CHUNK):
        r0 = c * GM_CHUNK
        for g in range(4):
            c0 = g * LANE
            mixed = _dot(ws_ref[g], vb[r0:r0 + GM_CHUNK, c0:c0 + LANE]) + bs_ref[:, c0:c0 + LANE]
            o_ref[r0:r0 + GM_CHUNK, c0:c0 + LANE] = u[r0:r0 + GM_CHUNK, c0:c0 + LANE] * mixed


def gmlp_mix(z, ng, nb, ws_tril, bs_exp):
    b, s, _ = z.shape
    tg = _tile(s, 512)
    return pl.pallas_call(
        functools.partial(_gmlp_body, tg=tg), name="gmlp_mix", grid=(b, s // tg),
        in_specs=[pl.BlockSpec((None, tg, 1024), lambda bi, i: (bi, i, 0)),
                  pl.BlockSpec((1, 512), lambda bi, i: (0, 0)),
                  pl.BlockSpec((1, 512), lambda bi, i: (0, 0)),
                  pl.BlockSpec((4, GM_CHUNK, GM_CHUNK), lambda bi, i: (0, 0, 0)),
                  pl.BlockSpec((GM_CHUNK, 512), lambda bi, i: (0, 0))],
        out_specs=[pl.BlockSpec((None, tg, 512), lambda bi, i: (bi, i, 0)),
                   pl.BlockSpec((None, tg, 512), lambda bi, i: (bi, i, 0))],
        out_shape=[jax.ShapeDtypeStruct((b, s, 512), F32), jax.ShapeDtypeStruct((b, s, 512), F32)],
        compiler_params=_cp(("parallel", "parallel")),
    )(z, ng.reshape(1, 512), nb.reshape(1, 512), ws_tril, bs_exp)


def _finish_body(x_ref, hn_ref, g3_ref, ocmp_ref, osel_ref, owin_ref, opool_ref, ogm_ref, odiff_ref,
                 wgate_ref, bgate_ref, wbr_ref, wo_ref, o_ref):
    hn = hn_ref[...]
    g3 = g3_ref[...]
    onsa = g3[:, 0:512] * ocmp_ref[...] + g3[:, 512:1024] * osel_ref[...] + g3[:, 1024:1536] * owin_ref[...]
    branches = (onsa, opool_ref[...], ogm_ref[...], odiff_ref[...])
    acc = jnp.zeros(o_ref.shape, F32)
    for n in range(N_BRANCH):
        c0 = n * D_MODEL
        gate = jax.nn.sigmoid(_dot(hn, wgate_ref[:, c0:c0 + D_MODEL]) + bgate_ref[:, c0:c0 + D_MODEL])
        acc = acc + gate * _dot(branches[n].astype(BF16), wbr_ref[n])
    o_ref[...] = x_ref[...] + _dot(acc.astype(BF16), wo_ref[...])


def finish_mixer(x, hn, g3, ocmp, osel, owin, opool, ogm, odiff, wgate, bgate, wbr, wo):
    t = x.shape[0]
    tm = _tile(t, 256)
    row = lambda w: pl.BlockSpec((tm, w), lambda i: (i, 0))
    const = lambda shape: pl.BlockSpec(shape, lambda i: (0,) * len(shape), pipeline_mode=pl.Buffered(1))
    return pl.pallas_call(
        _finish_body, name="finish_mixer", grid=(t // tm,),
        in_specs=[row(1024), row(1024), row(1536), row(512), row(512), row(512), row(512), row(512), row(512),
                  const((D_MODEL, N_BRANCH * D_MODEL)), const((1, N_BRANCH * D_MODEL)),
                  const((N_BRANCH, BRANCH_W, D_MODEL)), const((D_MODEL, D_MODEL))],
        out_specs=row(1024),
        out_shape=jax.ShapeDtypeStruct((t, D_MODEL), F32),
        compiler_params=_cp(("parallel",)),
    )(x, hn, g3, ocmp, osel, owin, opool, ogm, odiff, wgate, bgate, wbr, wo)


def _ffn_body(x_ref, g_ref, wg_ref, wu_ref, wd_ref, o_ref, hn_ref, acc_ref, *, nf):
    j = pl.program_id(1)

    @pl.when(j == 0)
    def _init():
        hn_ref[...] = _rms(x_ref[...], g_ref[...]).astype(BF16)
        acc_ref[...] = jnp.zeros(acc_ref.shape, F32)

    h = hn_ref[...]
    a = _dot(h, wg_ref[...])
    act = a * jax.nn.sigmoid(a) * _dot(h, wu_ref[...])
    acc_ref[...] += _dot(act.astype(BF16), wd_ref[...])

    @pl.when(j == nf - 1)
    def _fin():
        o_ref[...] = x_ref[...] + acc_ref[...]


def ffn_swiglu(x, g, wg, wu, wd):
    t, d = x.shape
    f = wg.shape[1]
    tm = _tile(t, 512)
    tf = 256
    nf = f // tf
    return pl.pallas_call(
        functools.partial(_ffn_body, nf=nf), name="ffn_swiglu", grid=(t // tm, nf),
        in_specs=[pl.BlockSpec((tm, d), lambda i, j: (i, 0)), pl.BlockSpec((1, d), lambda i, j: (0, 0)),
                  pl.BlockSpec((d, tf), lambda i, j: (0, j)), pl.BlockSpec((d, tf), lambda i, j: (0, j)),
                  pl.BlockSpec((tf, d), lambda i, j: (j, 0))],
        out_specs=pl.BlockSpec((tm, d), lambda i, j: (i, 0)),
        out_shape=jax.ShapeDtypeStruct((t, d), F32),
        scratch_shapes=[pltpu.VMEM((tm, d), BF16), pltpu.VMEM((tm, d), F32)],
        compiler_params=_cp(("parallel", "arbitrary")),
    )(x, g.reshape(1, d), wg.astype(BF16), wu.astype(BF16), wd.astype(BF16))


def _moe_body(x_ref, g_ref, r_ref, rb_ref, wg_ref, wu_ref, wd_ref, o_ref, hn_ref, comb_ref, acc_ref, *, nf):
    e = pl.program_id(1)
    f = pl.program_id(2)
    tm = x_ref.shape[0]
    lane = lax.broadcasted_iota(jnp.int32, (tm, LANE), 1)

    @pl.when((e == 0) & (f == 0))
    def _init():
        hn = _rms(x_ref[...], g_ref[...]).astype(BF16)
        hn_ref[...] = hn
        lg = _dot(hn, r_ref[...]) + rb_ref[...]
        m1 = jnp.max(lg, axis=1, keepdims=True)
        i1 = jnp.min(jnp.where(lg == m1, lane, LANE), axis=1, keepdims=True)
        lg2 = jnp.where(lane == i1, -3e38, lg)
        m2 = jnp.max(lg2, axis=1, keepdims=True)
        i2 = jnp.min(jnp.where(lg2 == m2, lane, LANE), axis=1, keepdims=True)
        e2 = jnp.exp(m2 - m1)
        w1 = 1.0 / (1.0 + e2)
        comb_ref[...] = jnp.where(lane == i1, w1, 0.0) + jnp.where(lane == i2, e2 * w1, 0.0)
        acc_ref[...] = jnp.zeros(acc_ref.shape, F32)

    ce = jnp.sum(jnp.where(lane == e, comb_ref[...], 0.0), axis=1, keepdims=True)
    h = hn_ref[...]
    a = _dot(h, wg_ref[...])
    act = a * jax.nn.sigmoid(a) * _dot(h, wu_ref[...])
    acc_ref[...] += ce * _dot(act.astype(BF16), wd_ref[...])

    @pl.when((e == N_EXPERTS - 1) & (f == nf - 1))
    def _fin():
        o_ref[...] = x_ref[...] + acc_ref[...]


def moe_swiglu(x, g, router, router_b, wg, wu, wd):
    t, d = x.shape
    f = wg.shape[2]
    tm = _tile(t, 512)
    tf = 512
    nf = f // tf
    rpad = jnp.zeros((d, LANE), F32).at[:, :N_EXPERTS].set(router).astype(BF16)
    rbpad = jnp.full((1, LANE), NEG, F32).at[0, :N_EXPERTS].set(router_b)
    return pl.pallas_call(
        functools.partial(_moe_body, nf=nf), name="moe_swiglu", grid=(t // tm, N_EXPERTS, nf),
        in_specs=[pl.BlockSpec((tm, d), lambda i, e, j: (i, 0)), pl.BlockSpec((1, d), lambda i, e, j: (0, 0)),
                  pl.BlockSpec((d, LANE), lambda i, e, j: (0, 0)), pl.BlockSpec((1, LANE), lambda i, e, j: (0, 0)),
                  pl.BlockSpec((None, d, tf), lambda i, e, j: (e, 0, j)),
                  pl.BlockSpec((None, d, tf), lambda i, e, j: (e, 0, j)),
                  pl.BlockSpec((None, tf, d), lambda i, e, j: (e, j, 0))],
        out_specs=pl.BlockSpec((tm, d), lambda i, e, j: (i, 0)),
        out_shape=jax.ShapeDtypeStruct((t, d), F32),
        scratch_shapes=[pltpu.VMEM((tm, d), BF16), pltpu.VMEM((tm, LANE), F32), pltpu.VMEM((tm, d), F32)],
        compiler_params=_cp(("parallel", "arbitrary", "arbitrary")),
    )(x, g.reshape(1, d), rpad, rbpad, wg.astype(BF16), wu.astype(BF16), wd.astype(BF16))


def _ple_body(x_ref, p_ref, g_ref, wg_ref, wp_ref, fg_ref, o_ref, *, final):
    x = x_ref[...]
    hn = _rms(x, g_ref[...]).astype(BF16)
    gate = jax.nn.sigmoid(_dot(hn, wg_ref[...]))
    y = x + gate * _dot(p_ref[...].astype(BF16), wp_ref[...])
    if final:
        y = _rms(y, fg_ref[...])
    o_ref[...] = y


def ple(x, p, g, wg, wp, fg, final):
    t, d = x.shape
    pd = p.shape[1]
    tm = _tile(t, 512)
    return pl.pallas_call(
        functools.partial(_ple_body, final=final), name="ple", grid=(t // tm,),
        in_specs=[pl.BlockSpec((tm, d), lambda i: (i, 0)), pl.BlockSpec((tm, pd), lambda i: (i, 0)),
                  pl.BlockSpec((1, d), lambda i: (0, 0)), pl.BlockSpec((d, d), lambda i: (0, 0)),
                  pl.BlockSpec((pd, d), lambda i: (0, 0)), pl.BlockSpec((1, d), lambda i: (0, 0))],
        out_specs=pl.BlockSpec((tm, d), lambda i: (i, 0)),
        out_shape=jax.ShapeDtypeStruct((t, d), F32),
        compiler_params=_cp(("parallel",)),
    )(x, p, g.reshape(1, d), wg.astype(BF16), wp.astype(BF16), fg.reshape(1, d))


def _page_copies(pt_ref, cache_ref, bufs, sem, slot, b, c, *, layer, pc, cols):
    out = []
    for p in range(pc):
        pg = pt_ref[b, c * pc + p]
        for buf, (c0, w) in zip(bufs, cols):
            out.append(pltpu.make_async_copy(cache_ref.at[layer, pg, :, pl.ds(c0, w)],
                                             buf.at[slot, pl.ds(p * PAGE, PAGE)], sem.at[slot]))
    return out


def _paged_pipeline(pt_ref, cache_ref, bufs, sem, *, nb, nc, **kw):
    b = pl.program_id(0)
    c = pl.program_id(1) if nc > 1 else 0
    step = b * nc + c
    slot = step % 2

    @pl.when(step == 0)
    def _first():
        for d in _page_copies(pt_ref, cache_ref, bufs, sem, 0, 0, 0, **kw):
            d.start()

    @pl.when(step + 1 < nb * nc)
    def _prefetch():
        nxt = step + 1
        for d in _page_copies(pt_ref, cache_ref, bufs, sem, 1 - slot, nxt // nc, nxt % nc, **kw):
            d.start()

    for d in _page_copies(pt_ref, cache_ref, bufs, sem, slot, b, c, **kw):
        d.wait()
    return slot


def _compress_paged_body(pt_ref, cache_ref, wc_ref, bias_ref, w2_ref, o_ref, bufk, bufv, sem, *, layer, npg, nb, nch):
    slot = _paged_pipeline(pt_ref, cache_ref, (bufk, bufv), sem, nb=nb, nc=1, layer=layer, pc=npg,
                           cols=((0, LANE), (LANE, LANE)))
    _compress_core(lambda s: bufk[slot, pl.ds(s, nch, stride=CMP_STRIDE), :],
                   lambda s: bufv[slot, pl.ds(s, nch, stride=CMP_STRIDE), :], wc_ref, bias_ref, w2_ref, o_ref, nch)


def compress_paged(page_table, cache, layer, wc, bias, w2):
    b, npg = page_table.shape
    past = npg * PAGE
    nch = past // CMP_STRIDE
    const = lambda shape: pl.BlockSpec(shape, lambda i, pt: (0,) * len(shape))
    return pl.pallas_call(
        functools.partial(_compress_paged_body, layer=layer, npg=npg, nb=b, nch=nch), name="compress_paged",
        grid_spec=pltpu.PrefetchScalarGridSpec(
            num_scalar_prefetch=1, grid=(b,),
            in_specs=[pl.BlockSpec(memory_space=pl.ANY), const((CMP_STRIDE, 256, 512)), const((1, 256)),
                      const((256, 256))],
            out_specs=pl.BlockSpec((None, nch, 256), lambda i, pt: (i, 0, 0)),
            scratch_shapes=[pltpu.VMEM((2, past, LANE), F32), pltpu.VMEM((2, past, LANE), F32),
                            pltpu.SemaphoreType.DMA((2,))]),
        out_shape=jax.ShapeDtypeStruct((b, nch, 256), BF16),
        compiler_params=_cp(("arbitrary",)),
    )(page_table, cache, wc, bias, w2)


def _new_tile(new, c0, tq):
    return jnp.concatenate([new[:, c0:c0 + LANE], jnp.zeros((LANE - tq, LANE), F32)], axis=0).astype(BF16)


def _sel_paged_body(pt_ref, cache_ref, q_ref, new_ref, sel_ref, e_ref, o_ref, bufk, bufv, sem, *,
                    layer, npg, nb, tq, tk, nsp):
    slot = _paged_pipeline(pt_ref, cache_ref, (bufk, bufv), sem, nb=nb, nc=1, layer=layer, pc=npg,
                           cols=((2 * LANE, LANE), (3 * LANE, LANE)))
    past = npg * PAGE
    r = 8 * tq
    qs = _stack_heads(q_ref[...], tq).astype(BF16)
    sel = sel_ref[...].astype(BF16)
    m = jnp.full((r, 1), NEG, F32)
    l = jnp.zeros((r, 1), F32)
    acc = jnp.zeros((r, LANE), F32)
    for c in range(past // tk):
        mask = _sel_mask(sel, e_ref[:, c * tk:(c + 1) * tk], nsp)
        m, l, acc = _flash_update(qs, bufk[slot, c * tk:(c + 1) * tk, :].astype(BF16),
                                  bufv[slot, c * tk:(c + 1) * tk, :].astype(BF16), mask, m, l, acc)
    new = new_ref[...]
    row = lax.broadcasted_iota(jnp.int32, (r, LANE), 0)
    col = lax.broadcasted_iota(jnp.int32, (r, LANE), 1)
    mask = _sel_mask(sel, e_ref[:, past:past + LANE], nsp) & (col <= (row & (tq - 1)))
    m, l, acc = _flash_update(qs, _new_tile(new, 2 * LANE, tq), _new_tile(new, 3 * LANE, tq), mask, m, l, acc)
    o_ref[...] = _unstack_heads(acc * (1.0 / jnp.where(l > 0.0, l, 1.0)), tq)


def sel_paged(page_table, cache, layer, q, new_rows, sel, emat):
    b, npg = page_table.shape
    past = npg * PAGE
    tq = q.shape[1]
    nsp = sel.shape[2] // 2
    return pl.pallas_call(
        functools.partial(_sel_paged_body, layer=layer, npg=npg, nb=b, tq=tq, tk=_tile(past, 512), nsp=nsp),
        name="sel_paged",
        grid_spec=pltpu.PrefetchScalarGridSpec(
            num_scalar_prefetch=1, grid=(b,),
            in_specs=[pl.BlockSpec(memory_space=pl.ANY),
                      pl.BlockSpec((None, tq, 512), lambda i, pt: (i, 0, 0)),
                      pl.BlockSpec((None, tq, 512), lambda i, pt: (i, 0, 0)),
                      pl.BlockSpec((None, tq, 2 * nsp), lambda i, pt: (i, 0, 0)),
                      pl.BlockSpec((nsp, past + LANE), lambda i, pt: (0, 0))],
            out_specs=pl.BlockSpec((None, tq, 512), lambda i, pt: (i, 0, 0)),
            scratch_shapes=[pltpu.VMEM((2, past, LANE), F32), pltpu.VMEM((2, past, LANE), F32),
                            pltpu.SemaphoreType.DMA((2,))]),
        out_shape=jax.ShapeDtypeStruct((b, tq, 512), F32),
        compiler_params=_cp(("arbitrary",)),
    )(page_table, cache, q, new_rows, sel, emat)


def _diff_paged_body(pt_ref, cache_ref, lam_ref, q_ref, new_ref, g_ref, o_ref, bufk, bufv, sem,
                     qs_ref, m_ref, l_ref, acc_ref, *, layer, nb, nc, pc, tq, lam_init):
    c = pl.program_id(1)
    slot = _paged_pipeline(pt_ref, cache_ref, (bufk, bufv), sem, nb=nb, nc=nc, layer=layer, pc=pc,
                           cols=((0, 512), (512, 512)))
    r = 4 * tq

    @pl.when(c == 0)
    def _init():
        _diff_init(q_ref, qs_ref, m_ref, l_ref, acc_ref, tq)

    _diff_step(lambda hp: bufk[slot, :, hp * LANE:(hp + 1) * LANE].astype(BF16),
               lambda hp: bufv[slot, :, hp * LANE:(hp + 1) * LANE].astype(BF16),
               None, qs_ref, m_ref, l_ref, acc_ref)

    @pl.when(c == nc - 1)
    def _fin():
        new = new_ref[...]
        row = lax.broadcasted_iota(jnp.int32, (r, LANE), 0)
        col = lax.broadcasted_iota(jnp.int32, (r, LANE), 1)
        _diff_step(lambda hp: _new_tile(new, hp * LANE, tq), lambda hp: _new_tile(new, 512 + hp * LANE, tq),
                   col <= (row & (tq - 1)), qs_ref, m_ref, l_ref, acc_ref)
        _diff_fin(lam_ref, g_ref, o_ref, l_ref, acc_ref, tq, lam_init)


def diff_paged(page_table, cache, layer, lam_par, q, new_rows, gnorm, lam_init):
    b, npg = page_table.shape
    tq = q.shape[1]
    pc = _tile(npg, 16)
    nc = npg // pc
    r = 4 * tq
    npair = DIFF_HEADS // 2
    return pl.pallas_call(
        functools.partial(_diff_paged_body, layer=layer, nb=b, nc=nc, pc=pc, tq=tq, lam_init=lam_init),
        name="diff_paged",
        grid_spec=pltpu.PrefetchScalarGridSpec(
            num_scalar_prefetch=1, grid=(b, nc),
            in_specs=[pl.BlockSpec(memory_space=pl.ANY),
                      pl.BlockSpec((4, DIFF_QK), lambda i, c, pt: (0, 0)),
                      pl.BlockSpec((None, tq, 512), lambda i, c, pt: (i, 0, 0)),
                      pl.BlockSpec((None, tq, 1024), lambda i, c, pt: (i, 0, 0)),
                      pl.BlockSpec((1, LANE), lambda i, c, pt: (0, 0))],
            out_specs=pl.BlockSpec((None, tq, 512), lambda i, c, pt: (i, 0, 0)),
            scratch_shapes=[pltpu.VMEM((2, pc * PAGE, 512), F32), pltpu.VMEM((2, pc * PAGE, 512), F32),
                            pltpu.SemaphoreType.DMA((2,)),
                            pltpu.VMEM((npair, r, LANE), BF16), pltpu.VMEM((npair, r, 1), F32),
                            pltpu.VMEM((npair, r, 1), F32), pltpu.VMEM((npair, r, LANE), F32)]),
        out_shape=jax.ShapeDtypeStruct((b, tq, 512), F32),
        compiler_params=_cp(("arbitrary", "arbitrary")),
    )(page_table, cache, lam_par, q, new_rows, gnorm)


def _q_perm():
    idx = np.zeros(512, np.int32)
    for j in range(NSA_J):
        for g in range(NSA_G):
            for d in range(DH):
                idx[j * LANE + g * DH + d] = (g * NSA_J + j) * DH + d
    return idx


def _gate3_perm():
    idx = np.zeros(3 * 512, np.int32)
    for c in range(3):
        for j in range(NSA_J):
            for g in range(NSA_G):
                idx[c * 512 + j * LANE + g * DH:c * 512 + j * LANE + (g + 1) * DH] = (g * NSA_J + j) * 3 + c
    return idx


def _cmp_to_sel(n_cmp, n_sel, nch, nsp):
    r = SEL_BLOCK // CMP_STRIDE
    k = np.arange(n_cmp)[:, None] - r * np.arange(n_sel)[None, :]
    m = sum(((k + n >= 0) & (k + n < r)).astype(np.float32) for n in range(CMP_BLOCK // CMP_STRIDE))
    out = np.zeros((nch, nsp), np.float32)
    out[:n_cmp, :n_sel] = m
    return out


def _block_expand(nsp, lk):
    return (np.arange(lk)[None, :] // SEL_BLOCK == np.arange(nsp)[:, None]).astype(np.float32)


def _cmp_weights(pe, w1, w2):
    eye = jnp.eye(2, dtype=F32)
    w6 = w1.reshape(2, NSA_G, 2, CMP_STRIDE, DH, DH)
    wc = jnp.einsum("kgxsdh,kK,gG->skgdxKGh", w6, eye, eye).reshape(CMP_STRIDE, 256, 512).astype(BF16)
    w2b = jnp.einsum("kghd,kK,gG->kghKGd", w2, eye, eye).reshape(256, 256).astype(BF16)
    pe_rows = jnp.transpose(pe, (0, 2, 1, 3)).reshape(4, CMP_BLOCK * DH)
    xb = jnp.einsum("rc,rR->rRc", pe_rows, jnp.eye(4, dtype=F32)).reshape(4, 4 * CMP_BLOCK * DH)
    xb = jnp.zeros((16, 4 * CMP_BLOCK * DH), F32).at[:4].set(xb)
    wb = jnp.zeros((4 * CMP_BLOCK * DH, LANE), F32).at[:, :DH].set(w1.reshape(4 * CMP_BLOCK * DH, DH))
    bias = mm(xb, wb)[:4, :DH].reshape(1, 256)
    return wc, bias, w2b


def _mixer(x, lw, lam_init, *, b, sq, sample=None):
    t = b * sq
    hn = rmsnorm(x, lw["norm_mix_g"], BF16)
    w_in = lw["w_in"]
    qperm = _q_perm()
    q = mm(hn, w_in[:, OFF_Q:OFF_KV][:, qperm], scale=DH ** -0.5).reshape(b, sq, 512)
    rows4 = mm(hn, w_in[:, OFF_KV:OFF_KV + 512]).reshape(b, sq, 512)
    winkv = mm(hn, w_in[:, OFF_KV + 512:OFF_GATE]).reshape(b, sq, 256)
    g3 = mm(hn, w_in[:, OFF_GATE:OFF_POOL][:, _gate3_perm()], act="sigmoid")
    zpool = mm(hn, w_in[:, OFF_POOL:OFF_GM]).reshape(b, sq, 512)
    zgm = mm(hn, w_in[:, OFF_GM:OFF_DQ]).reshape(b, sq, 1024)
    dq = mm(hn, w_in[:, OFF_DQ:OFF_DKV], scale=DIFF_QK ** -0.5).reshape(b, sq, 512)
    dkv = mm(hn, w_in[:, OFF_DKV:OFF_END]).reshape(b, sq, 1024)

    wc, cbias, w2b = _cmp_weights(lw["cmp_pe"], lw["cmp_w1"], lw["cmp_w2"])
    lam_par = jnp.stack([lw["diff_lq1"], lw["diff_lk1"], lw["diff_lq2"], lw["diff_lk2"]])
    gnorm = jnp.tile(lw["diff_norm_g"], 2).reshape(1, LANE)

    if sample is None:
        past, l_tot = 0, sq
        nch = sq // CMP_STRIDE
        tq = _tile(sq, 128)
        tk_sel = _tile(sq, 256)
        nk_sel = sq // tk_sel
        tw = _tile(sq, 256)
        nband = WINDOW // tw + 1
        win_all, kpos0_w, nk_w = winkv, 0, nband
        win_kt = lambda i, j: (jnp.maximum(i - (nband - 1) + j, 0), i - (nband - 1) + j >= 0)
        tq_w = tw
        tq_d = _tile(sq, 256)
        tk_d = tq_d
        nk_d = sq // tk_d
        xcat = jnp.concatenate([jnp.zeros((b, 16, 512), F32), zpool], axis=1)
        zgm_in = zgm
    else:
        layer = sample["layer"]
        pt = sample["page_table"]
        past = pt.shape[1] * PAGE
        l_tot = past + sq
        nch = past // CMP_STRIDE
        tq = sq
        wbuf = sample["win"].shape[1]
        tw = 128
        wrows = -(-(wbuf + sq) // tw) * tw
        win_all = jnp.concatenate([sample["win"], winkv, jnp.zeros((b, wrows - wbuf - sq, 256), F32)], axis=1)
        kpos0_w, nk_w = past - wbuf, wrows // tw
        win_kt = lambda i, j: (j, j >= 0)
        tq_w = sq
        xcat = jnp.concatenate([jnp.zeros((b, 1, 512), F32), sample["pool"], zpool], axis=1)
        zgm_in = jnp.concatenate([zgm, jnp.zeros((b, GM_CHUNK - sq, 1024), F32)], axis=1)

    n_cmp = l_tot // CMP_STRIDE - 1
    n_sel = -(-l_tot // SEL_BLOCK)
    nsp = -(-n_sel // LANE) * LANE
    msel = jnp.asarray(_cmp_to_sel(n_cmp, n_sel, nch, nsp), BF16)
    if sample is None:
        kcvc = compress(rows4, wc, cbias, w2b, nch)
        ocmp, selmask = cmp_attend(q, kcvc, msel, tq=tq, n_cmp=n_cmp, n_sel=n_sel, qpos0=0)
        emat = jnp.asarray(_block_expand(nsp, sq), BF16)

        def sel_kt(i, j):
            last = ((i + 1) * tq - 1) // tk_sel
            return jnp.minimum(j, last), j <= last

        osel = nsa_flash(q, rows4, 2, 3, mode="sel", tq=tq, tk=tk_sel, nk=nk_sel, qpos0=0, kpos0=0,
                         kt_fn=sel_kt, sel=selmask, emat=emat)
    else:
        kcvc = compress_paged(pt, sample["cache_nsa"], layer, wc, cbias, w2b)
        ocmp, selmask = cmp_attend(q, kcvc, msel, tq=tq, n_cmp=n_cmp, n_sel=n_sel, qpos0=past)
        emat = jnp.asarray(_block_expand(nsp, past + LANE), BF16)
        osel = sel_paged(pt, sample["cache_nsa"], layer, q, rows4, selmask, emat)
    owin = nsa_flash(q, win_all, 0, 1, mode="win", tq=tq_w, tk=tw, nk=nk_w, qpos0=past, kpos0=kpos0_w,
                     kt_fn=win_kt)

    opool = pool_mix(xcat, lw["pool_w"], lw["pool_scale"], past)
    ws_tril = jnp.tril(lw["gm_ws"]).astype(BF16)
    bs_exp = jnp.repeat(lw["gm_bs"].T, LANE, axis=1)
    ogm, gm_v = gmlp_mix(zgm_in, lw["gm_ng"], lw["gm_nb"], ws_tril, bs_exp)

    if sample is None:
        def diff_kt(i, j):
            last = ((i + 1) * tq_d - 1) // tk_d
            return jnp.minimum(j, last), j <= last

        odiff = diff_attend(lam_par, dq, dkv, gnorm, tq=tq_d, tk=tk_d, nk=nk_d, qpos0=0,
                            lam_init=lam_init, kt_fn=diff_kt)
    else:
        odiff = diff_paged(pt, sample["cache_diff"], layer, lam_par, dq, dkv, gnorm, lam_init)
        ogm = ogm[:, :sq]
        gm_v = gm_v[:, :sq]
    wbr = lw["w_branch"].at[0].set(lw["w_branch"][0][qperm]).astype(BF16)
    out = finish_mixer(x, hn, g3, ocmp.reshape(t, 512), osel.reshape(t, 512), owin.reshape(t, 512),
                       opool.reshape(t, 512), ogm.reshape(t, 512), odiff.reshape(t, 512),
                       lw["w_gate"].astype(BF16), lw["b_gate"].reshape(1, -1), wbr, lw["w_o"].astype(BF16))
    states = dict(rows4=rows4, winkv=winkv, win_all=win_all, xcat=xcat, gm_v=gm_v, dkv=dkv)
    return out, states


def kernel(x_prompt, x_sample, cache_nsa, cache_diff, state_nsa_win, state_pool, page_table, p_prompt, p_sample, norm_mix_g, w_in, nsa_cmp_pe, nsa_cmp_w1, nsa_cmp_w2, pool_w, pool_scale, gm_norm_g, gm_norm_b, gm_ws, gm_bs, diff_lq1, diff_lk1, diff_lq2, diff_lk2, diff_norm_g, w_branch, w_gate, b_gate, w_o, norm_ffn_g, ffn_w_gate, ffn_w_up, ffn_w_down, moe_router, moe_router_b, moe_w_gate, moe_w_up, moe_w_down, ple_norm_g, ple_w_gate, ple_w_proj, final_norm_g):
    bp, sp, d = x_prompt.shape
    bs, ss, _ = x_sample.shape
    depth = w_in.shape[0]
    n_phys = cache_nsa.shape[1]
    wbuf = state_nsa_win.shape[2]
    cache_nsa2 = cache_nsa.reshape(depth, n_phys, PAGE, 512)
    cache_diff2 = cache_diff.reshape(depth, n_phys, PAGE, 1024)
    xp = x_prompt.reshape(bp * sp, d)
    xs = x_sample.reshape(bs * ss, d)
    outs = {k: [] for k in ("nsa_p", "nsa_s", "win_p", "win_s", "pool_p", "pool_s", "gmv_s", "diff_p", "diff_s")}
    for l in range(depth):
        lw = dict(norm_mix_g=norm_mix_g[l], w_in=w_in[l], cmp_pe=nsa_cmp_pe[l], cmp_w1=nsa_cmp_w1[l],
                  cmp_w2=nsa_cmp_w2[l], pool_w=pool_w[l], pool_scale=pool_scale[l], gm_ng=gm_norm_g[l],
                  gm_nb=gm_norm_b[l], gm_ws=gm_ws[l], gm_bs=gm_bs[l], diff_lq1=diff_lq1[l], diff_lk1=diff_lk1[l],
                  diff_lq2=diff_lq2[l], diff_lk2=diff_lk2[l], diff_norm_g=diff_norm_g[l], w_branch=w_branch[l],
                  w_gate=w_gate[l], b_gate=b_gate[l], w_o=w_o[l])
        lam_init = 0.8 - 0.6 * math.exp(-0.3 * l)
        xp, st_p = _mixer(xp, lw, lam_init, b=bp, sq=sp)
        sample = dict(cache_nsa=cache_nsa2, cache_diff=cache_diff2, win=state_nsa_win[l].reshape(bs, wbuf, 256),
                      pool=state_pool[l], page_table=page_table, layer=l)
        xs, st_s = _mixer(xs, lw, lam_init, b=bs, sq=ss, sample=sample)
        outs["nsa_p"].append(st_p["rows4"].reshape(bp, sp, 4, NSA_G, DH))
        outs["nsa_s"].append(st_s["rows4"].reshape(bs, ss, 4, NSA_G, DH))
        wkeep = min(WINDOW, sp)
        outs["win_p"].append(st_p["winkv"][:, sp - wkeep:].reshape(bp, wkeep, 2, NSA_G, DH))
        outs["win_s"].append(st_s["win_all"][:, ss:ss + wbuf].reshape(bs, wbuf, 2, NSA_G, DH))
        outs["pool_p"].append(st_p["xcat"][:, -POOL_MEM:])
        outs["pool_s"].append(st_s["xcat"][:, -POOL_MEM:])
        outs["gmv_s"].append(st_s["gm_v"])
        outs["diff_p"].append(st_p["dkv"].reshape(bp, sp, 2, DIFF_HEADS, 2 * DIFF_QK))
        outs["diff_s"].append(st_s["dkv"].reshape(bs, ss, 2, DIFF_HEADS, 2 * DIFF_QK))
        i = l // 2
        if l % 2 == 0:
            xp = ffn_swiglu(xp, norm_ffn_g[l], ffn_w_gate[i], ffn_w_up[i], ffn_w_down[i])
            xs = ffn_swiglu(xs, norm_ffn_g[l], ffn_w_gate[i], ffn_w_up[i], ffn_w_down[i])
        else:
            xp = moe_swiglu(xp, norm_ffn_g[l], moe_router[i], moe_router_b[i], moe_w_gate[i], moe_w_up[i], moe_w_down[i])
            xs = moe_swiglu(xs, norm_ffn_g[l], moe_router[i], moe_router_b[i], moe_w_gate[i], moe_w_up[i], moe_w_down[i])
        final = l == depth - 1
        xp = ple(xp, p_prompt[l].reshape(bp * sp, -1), ple_norm_g[l], ple_w_gate[l], ple_w_proj[l], final_norm_g, final)
        xs = ple(xs, p_sample[l].reshape(bs * ss, -1), ple_norm_g[l], ple_w_gate[l], ple_w_proj[l], final_norm_g, final)
    st = lambda k: jnp.stack(outs[k])
    return (xp.reshape(bp, sp, d), xs.reshape(bs, ss, d), st("nsa_p"), st("nsa_s"), st("win_p"), st("win_s"),
            st("pool_p"), st("pool_s"), st("gmv_s"), st("diff_p"), st("diff_s"))
```

```python
import functools
import math

import numpy as np
import jax
import jax.numpy as jnp
from jax import lax
from jax.experimental import pallas as pl
from jax.experimental.pallas import tpu as pltpu

F32 = jnp.float32
BF16 = jnp.bfloat16

D_MODEL = 1024
PAGE = 128
NSA_HEADS = 8
NSA_G = 2
NSA_J = 4
DH = 64
CMP_BLOCK = 32
CMP_STRIDE = 16
SEL_BLOCK = 64
SEL_TOPK = 16
FORCE_BONUS = 100.0
WINDOW = 512
POOL_WINDOWS = (2, 4, 8, 16)
POOL_MEM = 15
GM_W = 512
GM_CHUNK = 128
DIFF_HEADS = 8
DIFF_QK = 32
N_BRANCH = 4
BRANCH_W = 512
N_EXPERTS = 8
EPS = 1e-6
NEG = -1e30
LANE = 128
VMEM_LIMIT = 56 * 1024 * 1024

OFF_Q, OFF_KV, OFF_GATE, OFF_POOL, OFF_GM, OFF_DQ, OFF_DKV, OFF_END = 0, 512, 1280, 1304, 1816, 2840, 3352, 4376


def _tile(n, pref):
    t = min(n, pref)
    while n % t:
        t //= 2
    return t


def _cp(sem, vmem=VMEM_LIMIT):
    return pltpu.CompilerParams(dimension_semantics=sem, vmem_limit_bytes=vmem)


def _gelu(x):
    return 0.5 * x * (1.0 + jnp.tanh(0.7978845608028654 * (x + 0.044715 * (x * x * x))))


def _rms(x, g):
    return x * lax.rsqrt(jnp.mean(x * x, axis=-1, keepdims=True) + EPS) * g


def _dot(a, b):
    return jnp.dot(a, b, preferred_element_type=F32)


def _dot_nt(a, b):
    return lax.dot_general(a, b, (((1,), (1,)), ((), ())), preferred_element_type=F32)


def _rmsnorm_body(x_ref, g_ref, o_ref):
    o_ref[...] = _rms(x_ref[...], g_ref[...]).astype(o_ref.dtype)


def rmsnorm(x, g, out_dtype):
    t, d = x.shape
    tm = _tile(t, 512)
    return pl.pallas_call(
        _rmsnorm_body, name="rmsnorm", grid=(t // tm,),
        in_specs=[pl.BlockSpec((tm, d), lambda i: (i, 0)), pl.BlockSpec((1, d), lambda i: (0, 0))],
        out_specs=pl.BlockSpec((tm, d), lambda i: (i, 0)),
        out_shape=jax.ShapeDtypeStruct((t, d), out_dtype),
        compiler_params=_cp(("parallel",)),
    )(x, g.reshape(1, d))


def _mm_body(x_ref, w_ref, o_ref, *, scale, act):
    y = _dot(x_ref[...].astype(BF16), w_ref[...])
    if scale is not None:
        y = y * scale
    if act == "sigmoid":
        y = jax.nn.sigmoid(y)
    o_ref[...] = y.astype(o_ref.dtype)


def mm(x, w, *, scale=None, act=None, out_dtype=F32):
    t, k = x.shape
    n = w.shape[1]
    tm = _tile(t, 512)
    tn = _tile(n, 512)
    return pl.pallas_call(
        functools.partial(_mm_body, scale=scale, act=act), name="mm", grid=(t // tm, n // tn),
        in_specs=[pl.BlockSpec((tm, k), lambda i, j: (i, 0)), pl.BlockSpec((k, tn), lambda i, j: (0, j))],
        out_specs=pl.BlockSpec((tm, tn), lambda i, j: (i, j)),
        out_shape=jax.ShapeDtypeStruct((t, n), out_dtype),
        compiler_params=_cp(("parallel", "parallel")),
    )(x, w.astype(BF16))


def _compress_body(xk_ref, xv_ref, wc_ref, bias_ref, w2_ref, o_ref, *, nch):
    _compress_core(lambda s: xk_ref[pl.ds(s, nch, stride=CMP_STRIDE), :],
                   lambda s: xv_ref[pl.ds(s, nch, stride=CMP_STRIDE), :], wc_ref, bias_ref, w2_ref, o_ref, nch)


def _compress_core(load_k, load_v, wc_ref, bias_ref, w2_ref, o_ref, nch):
    acc = jnp.zeros((nch, 512), F32)
    for s in range(CMP_STRIDE):
        xs = jnp.concatenate([load_k(s), load_v(s)], axis=1).astype(BF16)
        acc = acc + _dot(xs, wc_ref[s])
    hi_next = pltpu.roll(acc[:, 256:], nch - 1, 0)
    hid = _gelu(acc[:, :256] + hi_next + bias_ref[...])
    o_ref[...] = _dot(hid.astype(BF16), w2_ref[...]).astype(o_ref.dtype)


def compress(rows, wc, bias, w2, nch):
    b = rows.shape[0]
    return pl.pallas_call(
        functools.partial(_compress_body, nch=nch), name="compress", grid=(b,),
        in_specs=[pl.BlockSpec((None, nch * CMP_STRIDE, LANE), lambda i: (i, 0, 0)),
                  pl.BlockSpec((None, nch * CMP_STRIDE, LANE), lambda i: (i, 0, 1)),
                  pl.BlockSpec((CMP_STRIDE, 256, 512), lambda i: (0, 0, 0)),
                  pl.BlockSpec((1, 256), lambda i: (0, 0)),
                  pl.BlockSpec((256, 256), lambda i: (0, 0))],
        out_specs=pl.BlockSpec((None, nch, 256), lambda i: (i, 0, 0)),
        out_shape=jax.ShapeDtypeStruct((b, nch, 256), BF16),
        compiler_params=_cp(("parallel",)),
    )(rows, rows, wc, bias, w2)


def _stack_heads(q, tq):
    lane = lax.broadcasted_iota(jnp.int32, (tq, LANE), 1)
    parts = []
    for g in range(NSA_G):
        keep = (lane < DH) if g == 0 else (lane >= DH)
        for j in range(NSA_J):
            parts.append(jnp.where(keep, q[:, j * LANE:(j + 1) * LANE], 0.0))
    return jnp.concatenate(parts, axis=0)


def _unstack_heads(o, tq):
    lane = lax.broadcasted_iota(jnp.int32, (tq, LANE), 1)
    outs = []
    for j in range(NSA_J):
        outs.append(jnp.where(lane < DH, o[j * tq:(j + 1) * tq], o[(NSA_J + j) * tq:(NSA_J + j + 1) * tq]))
    return jnp.concatenate(outs, axis=1)


def _cmp_body(q_ref, kv_ref, msel_ref, o_ref, sel_ref, *, tq, nch, n_cmp, n_sel, nsp, qpos0):
    i = pl.program_id(1)
    r = 8 * tq
    qs = _stack_heads(q_ref[...], tq).astype(BF16)
    kc = kv_ref[:, 0:LANE]
    vc = kv_ref[:, LANE:2 * LANE]
    s = _dot_nt(qs, kc)
    row = lax.broadcasted_iota(jnp.int32, (r, nch), 0)
    col = lax.broadcasted_iota(jnp.int32, (r, nch), 1)
    qpos = qpos0 + i * tq + (row & (tq - 1))
    cmask = (col * CMP_STRIDE + (CMP_BLOCK - 1) <= qpos) & (col < n_cmp)
    s = jnp.where(cmask, s, NEG)
    m = jnp.max(s, axis=1, keepdims=True)
    e = jnp.where(cmask, jnp.exp(s - m), 0.0)
    l = jnp.sum(e, axis=1, keepdims=True)
    p = (e * (1.0 / jnp.where(l > 0.0, l, 1.0))).astype(BF16)
    o_ref[...] = _unstack_heads(_dot(p, vc), tq)
    imp_all = _dot(p, msel_ref[...])
    blk = lax.broadcasted_iota(jnp.int32, (tq, nsp), 1)
    qp = qpos0 + i * tq + lax.broadcasted_iota(jnp.int32, (tq, nsp), 0)
    cur = qp // SEL_BLOCK
    valid = (blk <= cur) & (blk < n_sel)
    forced = (blk == 0) | (blk == cur) | (blk == cur - 1)
    for g in range(NSA_G):
        imp = imp_all[(g * NSA_J) * tq:(g * NSA_J + 1) * tq]
        for j in range(1, NSA_J):
            imp = imp + imp_all[(g * NSA_J + j) * tq:(g * NSA_J + j + 1) * tq]
        score = jnp.where(valid, imp + FORCE_BONUS * forced.astype(F32), -1.0)
        score = jnp.where(blk < n_sel, score, -2.0)
        rank = jnp.zeros((tq, nsp), F32)
        for mth in range(n_sel):
            cm = score[:, mth:mth + 1]
            beats = (cm > score) | ((cm == score) & (blk > mth))
            rank = rank + beats.astype(F32)
        chosen = (rank < float(min(SEL_TOPK, n_sel))) & valid
        sel_ref[:, g * nsp:(g + 1) * nsp] = chosen.astype(F32)


def cmp_attend(q, kcvc, msel, *, tq, n_cmp, n_sel, qpos0):
    b, sq, _ = q.shape
    nch = kcvc.shape[1]
    nsp = msel.shape[1]
    return pl.pallas_call(
        functools.partial(_cmp_body, tq=tq, nch=nch, n_cmp=n_cmp, n_sel=n_sel, nsp=nsp, qpos0=qpos0),
        name="cmp_attend",
        grid=(b, sq // tq),
        in_specs=[pl.BlockSpec((None, tq, 512), lambda bi, i: (bi, i, 0)),
                  pl.BlockSpec((None, nch, 256), lambda bi, i: (bi, 0, 0)),
                  pl.BlockSpec((nch, nsp), lambda bi, i: (0, 0))],
        out_specs=[pl.BlockSpec((None, tq, 512), lambda bi, i: (bi, i, 0)),
                   pl.BlockSpec((None, tq, 2 * nsp), lambda bi, i: (bi, i, 0))],
        out_shape=[jax.ShapeDtypeStruct((b, sq, 512), F32), jax.ShapeDtypeStruct((b, sq, 2 * nsp), F32)],
        compiler_params=_cp(("parallel", "parallel")),
    )(q, kcvc, msel)


ROW_BLOCK = 128


def _flash_update(qs, k, v, bias, m, l, acc, kv_t=False):
    s = _dot(qs, k) if kv_t else _dot_nt(qs, k)
    if bias is not None:
        s = s + bias
    m_new = jnp.maximum(m, jnp.max(s, axis=1, keepdims=True))
    alpha = jnp.exp(m - m_new)
    p = jnp.exp(s - m_new)
    l_new = alpha * l + jnp.sum(p, axis=1, keepdims=True)
    pb = p.astype(BF16)
    acc_new = alpha * acc + (_dot_nt(pb, v) if kv_t else _dot(pb, v))
    return m_new, l_new, acc_new


def _flash_rows(qs_ref, k, v, bias_of, m_ref, l_ref, acc_ref, nrows, tq, kv_t=False):
    if tq < ROW_BLOCK:
        parts = [bias_of(h, 0, tq) for h in range(nrows // tq)]
        bias = None if parts[0] is None else jnp.concatenate(parts, axis=0)
        blocks = [(pl.ds(0, nrows), bias)]
    else:
        blocks = [(pl.ds(r0, ROW_BLOCK), bias_of(r0 // tq, r0 % tq, ROW_BLOCK)) for r0 in range(0, nrows, ROW_BLOCK)]
    for sl, bias in blocks:
        m, l, acc = _flash_update(qs_ref[sl], k, v, bias, m_ref[sl], l_ref[sl], acc_ref[sl], kv_t)
        m_ref[sl] = m
        l_ref[sl] = l
        acc_ref[sl] = acc


def _sel_ok(sel, e, nsp):
    return [_dot(sel[:, g * nsp:(g + 1) * nsp], e) > 0.5 for g in range(NSA_G)]


def _nsa_flash_body(*refs, mode, tq, tk, nk, qpos0, kpos0, kt_fn, nsp, free_step):
    if mode == "sel":
        q_ref, k_ref, v_ref, sel_ref, e_ref, o_ref, qs_ref, m_ref, l_ref, acc_ref = refs
    else:
        q_ref, k_ref, v_ref, o_ref, qs_ref, m_ref, l_ref, acc_ref = refs
    i = pl.program_id(1)
    j = pl.program_id(2)
    r = 8 * tq

    @pl.when(j == 0)
    def _init():
        qs_ref[...] = _stack_heads(q_ref[...], tq).astype(BF16)
        m_ref[...] = jnp.full((r, 1), NEG, F32)
        l_ref[...] = jnp.zeros((r, 1), F32)
        acc_ref[...] = jnp.zeros((r, LANE), F32)

    kt, valid = kt_fn(i, j)

    def step(masked):
        k = k_ref[...].astype(BF16)
        v = v_ref[...].astype(BF16)
        if not masked:
            _flash_rows(qs_ref, k, v, lambda h, t0, n: None, m_ref, l_ref, acc_ref, r, tq)
            return
        qpos = qpos0 + i * tq + lax.broadcasted_iota(jnp.int32, (tq, tk), 0)
        kpos = kpos0 + kt * tk + lax.broadcasted_iota(jnp.int32, (tq, tk), 1)
        ok = kpos <= qpos
        if mode == "win":
            ok = ok & (kpos > qpos - WINDOW)
            bias = [jnp.where(ok, 0.0, NEG)] * NSA_G
        else:
            bias = [jnp.where(ok & okg, 0.0, NEG) for okg in _sel_ok(sel_ref[...].astype(BF16), e_ref[...], nsp)]
        _flash_rows(qs_ref, k, v, lambda h, t0, n: bias[h // NSA_J][t0:t0 + n], m_ref, l_ref, acc_ref, r, tq)

    if free_step is None:
        pl.when(valid)(lambda: step(True))
    else:
        pl.when(valid & (j != free_step))(lambda: step(True))
        pl.when(valid & (j == free_step))(lambda: step(False))

    @pl.when(j == nk - 1)
    def _fin():
        l = l_ref[...]
        o = acc_ref[...] * (1.0 / jnp.where(l > 0.0, l, 1.0))
        o_ref[...] = _unstack_heads(o, tq)


def nsa_flash(q, kv, kcol, vcol, *, mode, tq, tk, nk, qpos0, kpos0, kt_fn, sel=None, emat=None, free_step=None):
    b, sq, _ = q.shape
    nsp = 0 if sel is None else sel.shape[2] // 2

    def kmap(col):
        return lambda bi, i, j: (bi, kt_fn(i, j)[0], col)

    in_specs = [pl.BlockSpec((None, tq, 512), lambda bi, i, j: (bi, i, 0)),
                pl.BlockSpec((None, tk, LANE), kmap(kcol)),
                pl.BlockSpec((None, tk, LANE), kmap(vcol))]
    args = [q, kv, kv]
    if mode == "sel":
        in_specs += [pl.BlockSpec((None, tq, 2 * nsp), lambda bi, i, j: (bi, i, 0)),
                     pl.BlockSpec((nsp, tk), lambda bi, i, j: (0, kt_fn(i, j)[0]))]
        args += [sel, emat]
    r = 8 * tq
    return pl.pallas_call(
        functools.partial(_nsa_flash_body, mode=mode, tq=tq, tk=tk, nk=nk, qpos0=qpos0, kpos0=kpos0,
                          kt_fn=kt_fn, nsp=nsp, free_step=free_step),
        name="nsa_" + mode, grid=(b, sq // tq, nk),
        in_specs=in_specs,
        out_specs=pl.BlockSpec((None, tq, 512), lambda bi, i, j: (bi, i, 0)),
        out_shape=jax.ShapeDtypeStruct((b, sq, 512), F32),
        scratch_shapes=[pltpu.VMEM((r, LANE), BF16), pltpu.VMEM((r, 1), F32), pltpu.VMEM((r, 1), F32),
                        pltpu.VMEM((r, LANE), F32)],
        compiler_params=_cp(("parallel", "parallel", "arbitrary")),
    )(*args)


def _diff_init(q_ref, qs_ref, m_ref, l_ref, acc_ref, tq):
    r = 4 * tq
    npair = DIFF_HEADS // 2
    lane = lax.broadcasted_iota(jnp.int32, (tq, LANE), 1)
    for hp in range(npair):
        q = q_ref[:, hp * LANE:(hp + 1) * LANE]
        parts = []
        for h in range(2):
            for mth in range(2):
                lo = h * 2 * DIFF_QK + mth * DIFF_QK
                parts.append(jnp.where((lane >= lo) & (lane < lo + DIFF_QK), q, 0.0))
        qs_ref[hp] = jnp.concatenate(parts, axis=0).astype(BF16)
    m_ref[...] = jnp.full((npair, r, 1), NEG, F32)
    l_ref[...] = jnp.zeros((npair, r, 1), F32)
    acc_ref[...] = jnp.zeros((npair, r, LANE), F32)


def _diff_step(k_of, v_of, bias, qs_ref, m_ref, l_ref, acc_ref, tq, kv_t=False):
    bias_of = lambda h, t0, n: None if bias is None else bias[t0:t0 + n]
    for hp in range(DIFF_HEADS // 2):
        _flash_rows(qs_ref.at[hp], k_of(hp), v_of(hp), bias_of, m_ref.at[hp], l_ref.at[hp], acc_ref.at[hp],
                    4 * tq, tq, kv_t)


def _diff_fin(lam_ref, g_ref, o_ref, l_ref, acc_ref, tq, lam_init):
    lp = lam_ref[...]
    lam = (jnp.exp(jnp.sum(lp[0:1] * lp[1:2], axis=1, keepdims=True))
           - jnp.exp(jnp.sum(lp[2:3] * lp[3:4], axis=1, keepdims=True)) + lam_init)
    lane = lax.broadcasted_iota(jnp.int32, (tq, LANE), 1)
    lo_half = lane < 2 * DIFF_QK
    for hp in range(DIFF_HEADS // 2):
        l = l_ref[hp]
        a = acc_ref[hp] * (1.0 / jnp.where(l > 0.0, l, 1.0))
        o0 = a[0:tq] - lam * a[tq:2 * tq]
        o1 = a[2 * tq:3 * tq] - lam * a[3 * tq:4 * tq]
        o = jnp.where(lo_half, o0, o1)
        sq = o * o
        ms0 = jnp.sum(jnp.where(lo_half, sq, 0.0), axis=1, keepdims=True)
        ms1 = jnp.sum(jnp.where(lo_half, 0.0, sq), axis=1, keepdims=True)
        ms = jnp.where(lo_half, ms0, ms1) * (1.0 / (2 * DIFF_QK))
        y = o * lax.rsqrt(ms + EPS) * g_ref[...]
        o_ref[:, hp * LANE:(hp + 1) * LANE] = y * (1.0 - lam_init)


def _diff_body(lam_ref, q_ref, k_ref, v_ref, g_ref, o_ref, qs_ref, m_ref, l_ref, acc_ref, *,
               tq, tk, nk, qpos0, lam_init, kt_fn):
    i = pl.program_id(1)
    j = pl.program_id(2)
    r = 4 * tq

    @pl.when(j == 0)
    def _init():
        _diff_init(q_ref, qs_ref, m_ref, l_ref, acc_ref, tq)

    kt, valid, diag = kt_fn(i, j)
    k_of = lambda hp: k_ref[:, hp * LANE:(hp + 1) * LANE].astype(BF16)
    v_of = lambda hp: v_ref[:, hp * LANE:(hp + 1) * LANE].astype(BF16)

    @pl.when(valid & jnp.logical_not(diag))
    def _below():
        _diff_step(k_of, v_of, None, qs_ref, m_ref, l_ref, acc_ref, tq)

    @pl.when(valid & diag)
    def _diag():
        qpos = qpos0 + i * tq + lax.broadcasted_iota(jnp.int32, (tq, tk), 0)
        kpos = kt * tk + lax.broadcasted_iota(jnp.int32, (tq, tk), 1)
        _diff_step(k_of, v_of, jnp.where(kpos <= qpos, 0.0, NEG), qs_ref, m_ref, l_ref, acc_ref, tq)

    @pl.when(j == nk - 1)
    def _fin():
        _diff_fin(lam_ref, g_ref, o_ref, l_ref, acc_ref, tq, lam_init)


def diff_attend(lam_par, q, kv, gnorm, *, tq, tk, nk, qpos0, lam_init, kt_fn):
    b, sq, _ = q.shape
    r = 4 * tq
    npair = DIFF_HEADS // 2
    return pl.pallas_call(
        functools.partial(_diff_body, tq=tq, tk=tk, nk=nk, qpos0=qpos0, lam_init=lam_init, kt_fn=kt_fn),
        name="diff_attend",
        grid=(b, sq // tq, nk),
        in_specs=[pl.BlockSpec((4, DIFF_QK), lambda bi, i, j: (0, 0)),
                  pl.BlockSpec((None, tq, 512), lambda bi, i, j: (bi, i, 0)),
                  pl.BlockSpec((None, tk, 512), lambda bi, i, j: (bi, kt_fn(i, j)[0], 0)),
                  pl.BlockSpec((None, tk, 512), lambda bi, i, j: (bi, kt_fn(i, j)[0], 1)),
                  pl.BlockSpec((1, LANE), lambda bi, i, j: (0, 0))],
        out_specs=pl.BlockSpec((None, tq, 512), lambda bi, i, j: (bi, i, 0)),
        out_shape=jax.ShapeDtypeStruct((b, sq, 512), F32),
        scratch_shapes=[pltpu.VMEM((npair, r, LANE), BF16), pltpu.VMEM((npair, r, 1), F32),
                        pltpu.VMEM((npair, r, 1), F32), pltpu.VMEM((npair, r, LANE), F32)],
        compiler_params=_cp(("parallel", "parallel", "arbitrary")),
    )(lam_par, q, kv, kv, gnorm)


def _pool_body(x_ref, pw_ref, sc_ref, o_ref, *, sq, start):
    pos = start + lax.broadcasted_iota(jnp.int32, (sq, LANE), 0)
    for g, w in enumerate(POOL_WINDOWS):
        c0 = g * LANE
        x = x_ref[pl.ds(16, sq), c0:c0 + LANE]
        tot = x
        for back in range(1, w):
            tot = tot + x_ref[pl.ds(16 - back, sq), c0:c0 + LANE]
        cnt = jnp.minimum(w, pos + 1).astype(F32)
        mix = tot / cnt - x
        y = _dot(mix.astype(BF16), pw_ref[g])
        o_ref[:, c0:c0 + LANE] = y * sc_ref[:, c0:c0 + LANE]


def pool_mix(xcat, pw, scale, start):
    b, rows, _ = xcat.shape
    sq = rows - 16
    return pl.pallas_call(
        functools.partial(_pool_body, sq=sq, start=start), name="pool_mix", grid=(b,),
        in_specs=[pl.BlockSpec((None, rows, 512), lambda i: (i, 0, 0)),
                  pl.BlockSpec((4, LANE, LANE), lambda i: (0, 0, 0)),
                  pl.BlockSpec((1, 512), lambda i: (0, 0))],
        out_specs=pl.BlockSpec((None, sq, 512), lambda i: (i, 0, 0)),
        out_shape=jax.ShapeDtypeStruct((b, sq, 512), F32),
        compiler_params=_cp(("parallel",)),
    )(xcat, pw.astype(BF16), scale.reshape(1, 512))


def _gmlp_body(z_ref, ng_ref, nb_ref, ws_ref, bs_ref, o_ref, v_ref, *, tg):
    z = _gelu(z_ref[...])
    u = z[:, :GM_W]
    vr = z[:, GM_W:]
    xc = vr - jnp.mean(vr, axis=-1, keepdims=True)
    v = xc * lax.rsqrt(jnp.mean(xc * xc, axis=-1, keepdims=True) + EPS) * ng_ref[...] + nb_ref[...]
    v_ref[...] = v
    vb = v.astype(BF16)
    for c in range(tg // GM_CHUNK):
        r0 = c * GM_CHUNK
        for g in range(4):
            c0 = g * LANE
            mixed = _dot(ws_ref[g], vb[r0:r0 + GM_CHUNK, c0:c0 + LANE]) + bs_ref[:, c0:c0 + LANE]
            o_ref[r0:r0 + GM_CHUNK, c0:c0 + LANE] = u[r0:r0 + GM_CHUNK, c0:c0 + LANE] * mixed


def gmlp_mix(z, ng, nb, ws_tril, bs_exp):
    b, s, _ = z.shape
    tg = _tile(s, 512)
    return pl.pallas_call(
        functools.partial(_gmlp_body, tg=tg), name="gmlp_mix", grid=(b, s // tg),
        in_specs=[pl.BlockSpec((None, tg, 1024), lambda bi, i: (bi, i, 0)),
                  pl.BlockSpec((1, 512), lambda bi, i: (0, 0)),
                  pl.BlockSpec((1, 512), lambda bi, i: (0, 0)),
                  pl.BlockSpec((4, GM_CHUNK, GM_CHUNK), lambda bi, i: (0, 0, 0)),
                  pl.BlockSpec((GM_CHUNK, 512), lambda bi, i: (0, 0))],
        out_specs=[pl.BlockSpec((None, tg, 512), lambda bi, i: (bi, i, 0)),
                   pl.BlockSpec((None, tg, 512), lambda bi, i: (bi, i, 0))],
        out_shape=[jax.ShapeDtypeStruct((b, s, 512), F32), jax.ShapeDtypeStruct((b, s, 512), F32)],
        compiler_params=_cp(("parallel", "parallel")),
    )(z, ng.reshape(1, 512), nb.reshape(1, 512), ws_tril, bs_exp)


def _finish_body(x_ref, hn_ref, g3_ref, ocmp_ref, osel_ref, owin_ref, opool_ref, ogm_ref, odiff_ref,
                 wgate_ref, bgate_ref, wbr_ref, wo_ref, o_ref):
    hn = hn_ref[...]
    g3 = g3_ref[...]
    onsa = g3[:, 0:512] * ocmp_ref[...] + g3[:, 512:1024] * osel_ref[...] + g3[:, 1024:1536] * owin_ref[...]
    branches = (onsa, opool_ref[...], ogm_ref[...], odiff_ref[...])
    acc = jnp.zeros(o_ref.shape, F32)
    for n in range(N_BRANCH):
        c0 = n * D_MODEL
        gate = jax.nn.sigmoid(_dot(hn, wgate_ref[:, c0:c0 + D_MODEL]) + bgate_ref[:, c0:c0 + D_MODEL])
        acc = acc + gate * _dot(branches[n].astype(BF16), wbr_ref[n])
    o_ref[...] = x_ref[...] + _dot(acc.astype(BF16), wo_ref[...])


def finish_mixer(x, hn, g3, ocmp, osel, owin, opool, ogm, odiff, wgate, bgate, wbr, wo):
    t = x.shape[0]
    tm = _tile(t, 256)
    row = lambda w: pl.BlockSpec((tm, w), lambda i: (i, 0))
    const = lambda shape: pl.BlockSpec(shape, lambda i: (0,) * len(shape), pipeline_mode=pl.Buffered(1))
    return pl.pallas_call(
        _finish_body, name="finish_mixer", grid=(t // tm,),
        in_specs=[row(1024), row(1024), row(1536), row(512), row(512), row(512), row(512), row(512), row(512),
                  const((D_MODEL, N_BRANCH * D_MODEL)), const((1, N_BRANCH * D_MODEL)),
                  const((N_BRANCH, BRANCH_W, D_MODEL)), const((D_MODEL, D_MODEL))],
        out_specs=row(1024),
        out_shape=jax.ShapeDtypeStruct((t, D_MODEL), F32),
        compiler_params=_cp(("parallel",)),
    )(x, hn, g3, ocmp, osel, owin, opool, ogm, odiff, wgate, bgate, wbr, wo)


def _ffn_body(x_ref, g_ref, wg_ref, wu_ref, wd_ref, o_ref, hn_ref, acc_ref, *, nf):
    j = pl.program_id(1)

    @pl.when(j == 0)
    def _init():
        hn_ref[...] = _rms(x_ref[...], g_ref[...]).astype(BF16)
        acc_ref[...] = jnp.zeros(acc_ref.shape, F32)

    h = hn_ref[...]
    a = _dot(h, wg_ref[...])
    act = a * jax.nn.sigmoid(a) * _dot(h, wu_ref[...])
    acc_ref[...] += _dot(act.astype(BF16), wd_ref[...])

    @pl.when(j == nf - 1)
    def _fin():
        o_ref[...] = x_ref[...] + acc_ref[...]


def ffn_swiglu(x, g, wg, wu, wd):
    t, d = x.shape
    f = wg.shape[1]
    tm = _tile(t, 512)
    tf = 256
    nf = f // tf
    return pl.pallas_call(
        functools.partial(_ffn_body, nf=nf), name="ffn_swiglu", grid=(t // tm, nf),
        in_specs=[pl.BlockSpec((tm, d), lambda i, j: (i, 0)), pl.BlockSpec((1, d), lambda i, j: (0, 0)),
                  pl.BlockSpec((d, tf), lambda i, j: (0, j)), pl.BlockSpec((d, tf), lambda i, j: (0, j)),
                  pl.BlockSpec((tf, d), lambda i, j: (j, 0))],
        out_specs=pl.BlockSpec((tm, d), lambda i, j: (i, 0)),
        out_shape=jax.ShapeDtypeStruct((t, d), F32),
        scratch_shapes=[pltpu.VMEM((tm, d), BF16), pltpu.VMEM((tm, d), F32)],
        compiler_params=_cp(("parallel", "arbitrary")),
    )(x, g.reshape(1, d), wg.astype(BF16), wu.astype(BF16), wd.astype(BF16))


def _moe_body(x_ref, g_ref, r_ref, rb_ref, wg_ref, wu_ref, wd_ref, o_ref, hn_ref, comb_ref, acc_ref, *, nf):
    e = pl.program_id(1)
    f = pl.program_id(2)
    tm = x_ref.shape[0]
    lane = lax.broadcasted_iota(jnp.int32, (tm, LANE), 1)

    @pl.when((e == 0) & (f == 0))
    def _init():
        hn = _rms(x_ref[...], g_ref[...]).astype(BF16)
        hn_ref[...] = hn
        lg = _dot(hn, r_ref[...]) + rb_ref[...]
        m1 = jnp.max(lg, axis=1, keepdims=True)
        i1 = jnp.min(jnp.where(lg == m1, lane, LANE), axis=1, keepdims=True)
        lg2 = jnp.where(lane == i1, -3e38, lg)
        m2 = jnp.max(lg2, axis=1, keepdims=True)
        i2 = jnp.min(jnp.where(lg2 == m2, lane, LANE), axis=1, keepdims=True)
        e2 = jnp.exp(m2 - m1)
        w1 = 1.0 / (1.0 + e2)
        comb_ref[...] = jnp.where(lane == i1, w1, 0.0) + jnp.where(lane == i2, e2 * w1, 0.0)
        acc_ref[...] = jnp.zeros(acc_ref.shape, F32)

    ce = jnp.sum(jnp.where(lane == e, comb_ref[...], 0.0), axis=1, keepdims=True)
    h = hn_ref[...]
    a = _dot(h, wg_ref[...])
    act = a * jax.nn.sigmoid(a) * _dot(h, wu_ref[...])
    acc_ref[...] += ce * _dot(act.astype(BF16), wd_ref[...])

    @pl.when((e == N_EXPERTS - 1) & (f == nf - 1))
    def _fin():
        o_ref[...] = x_ref[...] + acc_ref[...]


def moe_swiglu(x, g, router, router_b, wg, wu, wd):
    t, d = x.shape
    f = wg.shape[2]
    tm = _tile(t, 512)
    tf = 512
    nf = f // tf
    rpad = jnp.zeros((d, LANE), F32).at[:, :N_EXPERTS].set(router).astype(BF16)
    rbpad = jnp.full((1, LANE), NEG, F32).at[0, :N_EXPERTS].set(router_b)
    return pl.pallas_call(
        functools.partial(_moe_body, nf=nf), name="moe_swiglu", grid=(t // tm, N_EXPERTS, nf),
        in_specs=[pl.BlockSpec((tm, d), lambda i, e, j: (i, 0)), pl.BlockSpec((1, d), lambda i, e, j: (0, 0)),
                  pl.BlockSpec((d, LANE), lambda i, e, j: (0, 0)), pl.BlockSpec((1, LANE), lambda i, e, j: (0, 0)),
                  pl.BlockSpec((None, d, tf), lambda i, e, j: (e, 0, j)),
                  pl.BlockSpec((None, d, tf), lambda i, e, j: (e, 0, j)),
                  pl.BlockSpec((None, tf, d), lambda i, e, j: (e, j, 0))],
        out_specs=pl.BlockSpec((tm, d), lambda i, e, j: (i, 0)),
        out_shape=jax.ShapeDtypeStruct((t, d), F32),
        scratch_shapes=[pltpu.VMEM((tm, d), BF16), pltpu.VMEM((tm, LANE), F32), pltpu.VMEM((tm, d), F32)],
        compiler_params=_cp(("parallel", "arbitrary", "arbitrary")),
    )(x, g.reshape(1, d), rpad, rbpad, wg.astype(BF16), wu.astype(BF16), wd.astype(BF16))


def _ple_body(x_ref, p_ref, g_ref, wg_ref, wp_ref, fg_ref, o_ref, *, final):
    x = x_ref[...]
    hn = _rms(x, g_ref[...]).astype(BF16)
    gate = jax.nn.sigmoid(_dot(hn, wg_ref[...]))
    y = x + gate * _dot(p_ref[...].astype(BF16), wp_ref[...])
    if final:
        y = _rms(y, fg_ref[...])
    o_ref[...] = y


def ple(x, p, g, wg, wp, fg, final):
    t, d = x.shape
    pd = p.shape[1]
    tm = _tile(t, 512)
    return pl.pallas_call(
        functools.partial(_ple_body, final=final), name="ple", grid=(t // tm,),
        in_specs=[pl.BlockSpec((tm, d), lambda i: (i, 0)), pl.BlockSpec((tm, pd), lambda i: (i, 0)),
                  pl.BlockSpec((1, d), lambda i: (0, 0)), pl.BlockSpec((d, d), lambda i: (0, 0)),
                  pl.BlockSpec((pd, d), lambda i: (0, 0)), pl.BlockSpec((1, d), lambda i: (0, 0))],
        out_specs=pl.BlockSpec((tm, d), lambda i: (i, 0)),
        out_shape=jax.ShapeDtypeStruct((t, d), F32),
        compiler_params=_cp(("parallel",)),
    )(x, p, g.reshape(1, d), wg.astype(BF16), wp.astype(BF16), fg.reshape(1, d))


def _page_copies(pt_ref, cache_ref, bufs, sem, slot, b, c, *, layer, pc, cols):
    out = []
    for p in range(pc):
        pg = pt_ref[b, c * pc + p]
        for buf, (f0, w) in zip(bufs, cols):
            out.append(pltpu.make_async_copy(cache_ref.at[layer, pg, pl.ds(f0, w), :], buf.at[slot, p],
                                             sem.at[slot]))
    return out


def _paged_pipeline(pt_ref, cache_ref, bufs, sem, *, nb, nc, **kw):
    b = pl.program_id(0)
    c = pl.program_id(1) if nc > 1 else 0
    step = b * nc + c
    slot = step % 2

    @pl.when(step == 0)
    def _first():
        for d in _page_copies(pt_ref, cache_ref, bufs, sem, 0, 0, 0, **kw):
            d.start()

    @pl.when(step + 1 < nb * nc)
    def _prefetch():
        nxt = step + 1
        for d in _page_copies(pt_ref, cache_ref, bufs, sem, 1 - slot, nxt // nc, nxt % nc, **kw):
            d.start()

    for d in _page_copies(pt_ref, cache_ref, bufs, sem, slot, b, c, **kw):
        d.wait()
    return slot


def _compress_paged_body(pt_ref, cache_ref, wc_ref, bias_ref, w2_ref, o_ref, bufk, bufv, sem, rowk, rowv, *,
                         layer, npg, nb, nch):
    slot = _paged_pipeline(pt_ref, cache_ref, (bufk, bufv), sem, nb=nb, nc=1, layer=layer, pc=npg,
                           cols=((0, LANE), (LANE, LANE)))

    def to_rows(p, carry):
        rows = pl.ds(pl.multiple_of(p * PAGE, PAGE), PAGE)
        rowk[rows, :] = bufk[slot, p].T
        rowv[rows, :] = bufv[slot, p].T
        return carry

    lax.fori_loop(0, npg, to_rows, 0)
    _compress_core(lambda s: rowk[pl.ds(s, nch, stride=CMP_STRIDE), :],
                   lambda s: rowv[pl.ds(s, nch, stride=CMP_STRIDE), :], wc_ref, bias_ref, w2_ref, o_ref, nch)


def compress_paged(page_table, cache, layer, wc, bias, w2):
    b, npg = page_table.shape
    past = npg * PAGE
    nch = past // CMP_STRIDE
    const = lambda shape: pl.BlockSpec(shape, lambda i, pt: (0,) * len(shape))
    return pl.pallas_call(
        functools.partial(_compress_paged_body, layer=layer, npg=npg, nb=b, nch=nch), name="compress_paged",
        grid_spec=pltpu.PrefetchScalarGridSpec(
            num_scalar_prefetch=1, grid=(b,),
            in_specs=[pl.BlockSpec(memory_space=pl.ANY), const((CMP_STRIDE, 256, 512)), const((1, 256)),
                      const((256, 256))],
            out_specs=pl.BlockSpec((None, nch, 256), lambda i, pt: (i, 0, 0)),
            scratch_shapes=[pltpu.VMEM((2, npg, LANE, PAGE), F32), pltpu.VMEM((2, npg, LANE, PAGE), F32),
                            pltpu.SemaphoreType.DMA((2,)),
                            pltpu.VMEM((past, LANE), F32), pltpu.VMEM((past, LANE), F32)]),
        out_shape=jax.ShapeDtypeStruct((b, nch, 256), BF16),
        compiler_params=_cp(("arbitrary",)),
    )(page_table, cache, wc, bias, w2)


def _new_tile(new, c0, tq):
    return jnp.concatenate([new[:, c0:c0 + LANE], jnp.zeros((LANE - tq, LANE), F32)], axis=0).astype(BF16)


def _sel_paged_body(pt_ref, cache_ref, q_ref, new_ref, sel_ref, e_ref, o_ref, bufk, bufv, sem, *,
                    layer, npg, nb, tq, tk, nsp):
    slot = _paged_pipeline(pt_ref, cache_ref, (bufk, bufv), sem, nb=nb, nc=1, layer=layer, pc=npg,
                           cols=((2 * LANE, LANE), (3 * LANE, LANE)))
    past = npg * PAGE
    ppc = tk // PAGE
    r = 8 * tq
    qs = _stack_heads(q_ref[...], tq).astype(BF16)
    sel = sel_ref[...].astype(BF16)
    m = jnp.full((r, 1), NEG, F32)
    l = jnp.zeros((r, 1), F32)
    acc = jnp.zeros((r, LANE), F32)

    def stacked_bias(oks):
        return jnp.concatenate([jnp.where(ok, 0.0, NEG) for ok in oks for _ in range(NSA_J)], axis=0)

    for c in range(npg // ppc):
        kt = jnp.concatenate([bufk[slot, c * ppc + u] for u in range(ppc)], axis=1).astype(BF16)
        vt = jnp.concatenate([bufv[slot, c * ppc + u] for u in range(ppc)], axis=1).astype(BF16)
        bias = stacked_bias(_sel_ok(sel, e_ref[:, c * tk:(c + 1) * tk], nsp))
        m, l, acc = _flash_update(qs, kt, vt, bias, m, l, acc, kv_t=True)
    new = new_ref[...]
    causal = (lax.broadcasted_iota(jnp.int32, (tq, LANE), 1) <= lax.broadcasted_iota(jnp.int32, (tq, LANE), 0))
    bias = stacked_bias([ok & causal for ok in _sel_ok(sel, e_ref[:, past:past + LANE], nsp)])
    m, l, acc = _flash_update(qs, _new_tile(new, 2 * LANE, tq), _new_tile(new, 3 * LANE, tq), bias, m, l, acc)
    o_ref[...] = _unstack_heads(acc * (1.0 / jnp.where(l > 0.0, l, 1.0)), tq)


def sel_paged(page_table, cache, layer, q, new_rows, sel, emat):
    b, npg = page_table.shape
    past = npg * PAGE
    tq = q.shape[1]
    nsp = sel.shape[2] // 2
    return pl.pallas_call(
        functools.partial(_sel_paged_body, layer=layer, npg=npg, nb=b, tq=tq, tk=_tile(past, 512), nsp=nsp),
        name="sel_paged",
        grid_spec=pltpu.PrefetchScalarGridSpec(
            num_scalar_prefetch=1, grid=(b,),
            in_specs=[pl.BlockSpec(memory_space=pl.ANY),
                      pl.BlockSpec((None, tq, 512), lambda i, pt: (i, 0, 0)),
                      pl.BlockSpec((None, tq, 512), lambda i, pt: (i, 0, 0)),
                      pl.BlockSpec((None, tq, 2 * nsp), lambda i, pt: (i, 0, 0)),
                      pl.BlockSpec((nsp, past + LANE), lambda i, pt: (0, 0))],
            out_specs=pl.BlockSpec((None, tq, 512), lambda i, pt: (i, 0, 0)),
            scratch_shapes=[pltpu.VMEM((2, npg, LANE, PAGE), F32), pltpu.VMEM((2, npg, LANE, PAGE), F32),
                            pltpu.SemaphoreType.DMA((2,))]),
        out_shape=jax.ShapeDtypeStruct((b, tq, 512), F32),
        compiler_params=_cp(("arbitrary",)),
    )(page_table, cache, q, new_rows, sel, emat)


def _diff_paged_body(pt_ref, cache_ref, lam_ref, q_ref, new_ref, g_ref, o_ref, bufk, bufv, sem,
                     qs_ref, m_ref, l_ref, acc_ref, *, layer, nb, nc, pc, tq, lam_init):
    c = pl.program_id(1)
    slot = _paged_pipeline(pt_ref, cache_ref, (bufk, bufv), sem, nb=nb, nc=nc, layer=layer, pc=pc,
                           cols=((0, 512), (512, 512)))
    ppc = _tile(pc, 4)

    @pl.when(c == 0)
    def _init():
        _diff_init(q_ref, qs_ref, m_ref, l_ref, acc_ref, tq)

    def pages_t(buf, hp, p0):
        return jnp.concatenate([buf[slot, p0 + u, hp * LANE:(hp + 1) * LANE, :] for u in range(ppc)],
                               axis=1).astype(BF16)

    for p0 in range(0, pc, ppc):
        _diff_step(lambda hp: pages_t(bufk, hp, p0), lambda hp: pages_t(bufv, hp, p0), None,
                   qs_ref, m_ref, l_ref, acc_ref, tq, kv_t=True)

    @pl.when(c == nc - 1)
    def _fin():
        new = new_ref[...]
        causal = (lax.broadcasted_iota(jnp.int32, (tq, LANE), 1) <= lax.broadcasted_iota(jnp.int32, (tq, LANE), 0))
        _diff_step(lambda hp: _new_tile(new, hp * LANE, tq), lambda hp: _new_tile(new, 512 + hp * LANE, tq),
                   jnp.where(causal, 0.0, NEG), qs_ref, m_ref, l_ref, acc_ref, tq)
        _diff_fin(lam_ref, g_ref, o_ref, l_ref, acc_ref, tq, lam_init)


def diff_paged(page_table, cache, layer, lam_par, q, new_rows, gnorm, lam_init):
    b, npg = page_table.shape
    tq = q.shape[1]
    pc = _tile(npg, 16)
    nc = npg // pc
    r = 4 * tq
    npair = DIFF_HEADS // 2
    return pl.pallas_call(
        functools.partial(_diff_paged_body, layer=layer, nb=b, nc=nc, pc=pc, tq=tq, lam_init=lam_init),
        name="diff_paged",
        grid_spec=pltpu.PrefetchScalarGridSpec(
            num_scalar_prefetch=1, grid=(b, nc),
            in_specs=[pl.BlockSpec(memory_space=pl.ANY),
                      pl.BlockSpec((4, DIFF_QK), lambda i, c, pt: (0, 0)),
                      pl.BlockSpec((None, tq, 512), lambda i, c, pt: (i, 0, 0)),
                      pl.BlockSpec((None, tq, 1024), lambda i, c, pt: (i, 0, 0)),
                      pl.BlockSpec((1, LANE), lambda i, c, pt: (0, 0))],
            out_specs=pl.BlockSpec((None, tq, 512), lambda i, c, pt: (i, 0, 0)),
            scratch_shapes=[pltpu.VMEM((2, pc, 512, PAGE), F32), pltpu.VMEM((2, pc, 512, PAGE), F32),
                            pltpu.SemaphoreType.DMA((2,)),
                            pltpu.VMEM((npair, r, LANE), BF16), pltpu.VMEM((npair, r, 1), F32),
                            pltpu.VMEM((npair, r, 1), F32), pltpu.VMEM((npair, r, LANE), F32)]),
        out_shape=jax.ShapeDtypeStruct((b, tq, 512), F32),
        compiler_params=_cp(("arbitrary", "arbitrary")),
    )(page_table, cache, lam_par, q, new_rows, gnorm)


def _q_perm():
    idx = np.zeros(512, np.int32)
    for j in range(NSA_J):
        for g in range(NSA_G):
            for d in range(DH):
                idx[j * LANE + g * DH + d] = (g * NSA_J + j) * DH + d
    return idx


def _gate3_perm():
    idx = np.zeros(3 * 512, np.int32)
    for c in range(3):
        for j in range(NSA_J):
            for g in range(NSA_G):
                idx[c * 512 + j * LANE + g * DH:c * 512 + j * LANE + (g + 1) * DH] = (g * NSA_J + j) * 3 + c
    return idx


def _cmp_to_sel(n_cmp, n_sel, nch, nsp):
    r = SEL_BLOCK // CMP_STRIDE
    k = np.arange(n_cmp)[:, None] - r * np.arange(n_sel)[None, :]
    m = sum(((k + n >= 0) & (k + n < r)).astype(np.float32) for n in range(CMP_BLOCK // CMP_STRIDE))
    out = np.zeros((nch, nsp), np.float32)
    out[:n_cmp, :n_sel] = m
    return out


def _block_expand(nsp, lk):
    return (np.arange(lk)[None, :] // SEL_BLOCK == np.arange(nsp)[:, None]).astype(np.float32)


def _cmp_weights(pe, w1, w2):
    eye = jnp.eye(2, dtype=F32)
    w6 = w1.reshape(2, NSA_G, 2, CMP_STRIDE, DH, DH)
    wc = jnp.einsum("kgxsdh,kK,gG->skgdxKGh", w6, eye, eye).reshape(CMP_STRIDE, 256, 512).astype(BF16)
    w2b = jnp.einsum("kghd,kK,gG->kghKGd", w2, eye, eye).reshape(256, 256).astype(BF16)
    pe_rows = jnp.transpose(pe, (0, 2, 1, 3)).reshape(4, CMP_BLOCK * DH)
    xb = jnp.einsum("rc,rR->rRc", pe_rows, jnp.eye(4, dtype=F32)).reshape(4, 4 * CMP_BLOCK * DH)
    xb = jnp.zeros((16, 4 * CMP_BLOCK * DH), F32).at[:4].set(xb)
    wb = jnp.zeros((4 * CMP_BLOCK * DH, LANE), F32).at[:, :DH].set(w1.reshape(4 * CMP_BLOCK * DH, DH))
    bias = mm(xb, wb)[:4, :DH].reshape(1, 256)
    return wc, bias, w2b


def _mixer(x, lw, lam_init, *, b, sq, sample=None):
    t = b * sq
    hn = rmsnorm(x, lw["norm_mix_g"], BF16)
    w_in = lw["w_in"]
    qperm = _q_perm()
    q = mm(hn, w_in[:, OFF_Q:OFF_KV][:, qperm], scale=DH ** -0.5).reshape(b, sq, 512)
    rows4 = mm(hn, w_in[:, OFF_KV:OFF_KV + 512]).reshape(b, sq, 512)
    winkv = mm(hn, w_in[:, OFF_KV + 512:OFF_GATE]).reshape(b, sq, 256)
    g3 = mm(hn, w_in[:, OFF_GATE:OFF_POOL][:, _gate3_perm()], act="sigmoid")
    zpool = mm(hn, w_in[:, OFF_POOL:OFF_GM]).reshape(b, sq, 512)
    zgm = mm(hn, w_in[:, OFF_GM:OFF_DQ]).reshape(b, sq, 1024)
    dq = mm(hn, w_in[:, OFF_DQ:OFF_DKV], scale=DIFF_QK ** -0.5).reshape(b, sq, 512)
    dkv = mm(hn, w_in[:, OFF_DKV:OFF_END]).reshape(b, sq, 1024)

    wc, cbias, w2b = _cmp_weights(lw["cmp_pe"], lw["cmp_w1"], lw["cmp_w2"])
    lam_par = jnp.stack([lw["diff_lq1"], lw["diff_lk1"], lw["diff_lq2"], lw["diff_lk2"]])
    gnorm = jnp.tile(lw["diff_norm_g"], 2).reshape(1, LANE)

    if sample is None:
        past, l_tot = 0, sq
        nch = sq // CMP_STRIDE
        tq = _tile(sq, 256)
        tk_sel = _tile(sq, 256)
        nk_sel = sq // tk_sel
        tw = _tile(sq, 256)
        nband = WINDOW // tw + 1
        win_all, kpos0_w, nk_w = winkv, 0, nband
        win_kt = lambda i, j: (jnp.maximum(i - (nband - 1) + j, 0), i - (nband - 1) + j >= 0)
        tq_w = tw
        free_w = nband - 2 if nband == 3 else None
        tq_d = _tile(sq, 256)
        tk_d = tq_d
        nk_d = sq // tk_d
        xcat = jnp.concatenate([jnp.zeros((b, 16, 512), F32), zpool], axis=1)
        zgm_in = zgm
    else:
        layer = sample["layer"]
        pt = sample["page_table"]
        past = pt.shape[1] * PAGE
        l_tot = past + sq
        nch = past // CMP_STRIDE
        tq = sq
        wbuf = sample["win"].shape[1]
        tw = 128
        wrows = -(-(wbuf + sq) // tw) * tw
        win_all = jnp.concatenate([sample["win"], winkv, jnp.zeros((b, wrows - wbuf - sq, 256), F32)], axis=1)
        kpos0_w, nk_w = past - wbuf, wrows // tw
        win_kt = lambda i, j: (j, j >= 0)
        tq_w = sq
        free_w = None
        xcat = jnp.concatenate([jnp.zeros((b, 1, 512), F32), sample["pool"], zpool], axis=1)
        zgm_in = jnp.concatenate([zgm, jnp.zeros((b, GM_CHUNK - sq, 1024), F32)], axis=1)

    n_cmp = l_tot // CMP_STRIDE - 1
    n_sel = -(-l_tot // SEL_BLOCK)
    nsp = -(-n_sel // LANE) * LANE
    msel = jnp.asarray(_cmp_to_sel(n_cmp, n_sel, nch, nsp), BF16)
    if sample is None:
        kcvc = compress(rows4, wc, cbias, w2b, nch)
        ocmp, selmask = cmp_attend(q, kcvc, msel, tq=tq, n_cmp=n_cmp, n_sel=n_sel, qpos0=0)
        emat = jnp.asarray(_block_expand(nsp, sq), BF16)

        def sel_kt(i, j):
            last = ((i + 1) * tq - 1) // tk_sel
            return jnp.minimum(j, last), j <= last

        osel = nsa_flash(q, rows4, 2, 3, mode="sel", tq=tq, tk=tk_sel, nk=nk_sel, qpos0=0, kpos0=0,
                         kt_fn=sel_kt, sel=selmask, emat=emat)
    else:
        kcvc = compress_paged(pt, sample["cache_nsa"], layer, wc, cbias, w2b)
        ocmp, selmask = cmp_attend(q, kcvc, msel, tq=tq, n_cmp=n_cmp, n_sel=n_sel, qpos0=past)
        emat = jnp.asarray(_block_expand(nsp, past + LANE), BF16)
        osel = sel_paged(pt, sample["cache_nsa"], layer, q, rows4, selmask, emat)
    owin = nsa_flash(q, win_all, 0, 1, mode="win", tq=tq_w, tk=tw, nk=nk_w, qpos0=past, kpos0=kpos0_w,
                     kt_fn=win_kt, free_step=free_w)

    opool = pool_mix(xcat, lw["pool_w"], lw["pool_scale"], past)
    ws_tril = jnp.tril(lw["gm_ws"]).astype(BF16)
    bs_exp = jnp.repeat(lw["gm_bs"].T, LANE, axis=1)
    ogm, gm_v = gmlp_mix(zgm_in, lw["gm_ng"], lw["gm_nb"], ws_tril, bs_exp)

    if sample is None:
        def diff_kt(i, j):
            last = ((i + 1) * tq_d - 1) // tk_d
            return jnp.minimum(j, last), j <= last, j == last

        odiff = diff_attend(lam_par, dq, dkv, gnorm, tq=tq_d, tk=tk_d, nk=nk_d, qpos0=0,
                            lam_init=lam_init, kt_fn=diff_kt)
    else:
        odiff = diff_paged(pt, sample["cache_diff"], layer, lam_par, dq, dkv, gnorm, lam_init)
        ogm = ogm[:, :sq]
        gm_v = gm_v[:, :sq]
    wbr = lw["w_branch"].at[0].set(lw["w_branch"][0][qperm]).astype(BF16)
    out = finish_mixer(x, hn, g3, ocmp.reshape(t, 512), osel.reshape(t, 512), owin.reshape(t, 512),
                       opool.reshape(t, 512), ogm.reshape(t, 512), odiff.reshape(t, 512),
                       lw["w_gate"].astype(BF16), lw["b_gate"].reshape(1, -1), wbr, lw["w_o"].astype(BF16))
    states = dict(rows4=rows4, winkv=winkv, win_all=win_all, xcat=xcat, gm_v=gm_v, dkv=dkv)
    return out, states


def kernel(x_prompt, x_sample, cache_nsa, cache_diff, state_nsa_win, state_pool, page_table, p_prompt, p_sample, norm_mix_g, w_in, nsa_cmp_pe, nsa_cmp_w1, nsa_cmp_w2, pool_w, pool_scale, gm_norm_g, gm_norm_b, gm_ws, gm_bs, diff_lq1, diff_lk1, diff_lq2, diff_lk2, diff_norm_g, w_branch, w_gate, b_gate, w_o, norm_ffn_g, ffn_w_gate, ffn_w_up, ffn_w_down, moe_router, moe_router_b, moe_w_gate, moe_w_up, moe_w_down, ple_norm_g, ple_w_gate, ple_w_proj, final_norm_g):
    bp, sp, d = x_prompt.shape
    bs, ss, _ = x_sample.shape
    depth = w_in.shape[0]
    n_phys = cache_nsa.shape[1]
    wbuf = state_nsa_win.shape[2]
    cache_nsa2 = jnp.transpose(cache_nsa, (0, 1, 3, 4, 5, 2)).reshape(depth, n_phys, 512, PAGE)
    cache_diff2 = jnp.transpose(cache_diff, (0, 1, 3, 4, 5, 2)).reshape(depth, n_phys, 1024, PAGE)
    xp = x_prompt.reshape(bp * sp, d)
    xs = x_sample.reshape(bs * ss, d)
    outs = {k: [] for k in ("nsa_p", "nsa_s", "win_p", "win_s", "pool_p", "pool_s", "gmv_s", "diff_p", "diff_s")}
    for l in range(depth):
        lw = dict(norm_mix_g=norm_mix_g[l], w_in=w_in[l], cmp_pe=nsa_cmp_pe[l], cmp_w1=nsa_cmp_w1[l],
                  cmp_w2=nsa_cmp_w2[l], pool_w=pool_w[l], pool_scale=pool_scale[l], gm_ng=gm_norm_g[l],
                  gm_nb=gm_norm_b[l], gm_ws=gm_ws[l], gm_bs=gm_bs[l], diff_lq1=diff_lq1[l], diff_lk1=diff_lk1[l],
                  diff_lq2=diff_lq2[l], diff_lk2=diff_lk2[l], diff_norm_g=diff_norm_g[l], w_branch=w_branch[l],
                  w_gate=w_gate[l], b_gate=b_gate[l], w_o=w_o[l])
        lam_init = 0.8 - 0.6 * math.exp(-0.3 * l)
        xp, st_p = _mixer(xp, lw, lam_init, b=bp, sq=sp)
        sample = dict(cache_nsa=cache_nsa2, cache_diff=cache_diff2, win=state_nsa_win[l].reshape(bs, wbuf, 256),
                      pool=state_pool[l], page_table=page_table, layer=l)
        xs, st_s = _mixer(xs, lw, lam_init, b=bs, sq=ss, sample=sample)
        outs["nsa_p"].append(st_p["rows4"].reshape(bp, sp, 4, NSA_G, DH))
        outs["nsa_s"].append(st_s["rows4"].reshape(bs, ss, 4, NSA_G, DH))
        wkeep = min(WINDOW, sp)
        outs["win_p"].append(st_p["winkv"][:, sp - wkeep:].reshape(bp, wkeep, 2, NSA_G, DH))
        outs["win_s"].append(st_s["win_all"][:, ss:ss + wbuf].reshape(bs, wbuf, 2, NSA_G, DH))
        outs["pool_p"].append(st_p["xcat"][:, -POOL_MEM:])
        outs["pool_s"].append(st_s["xcat"][:, -POOL_MEM:])
        outs["gmv_s"].append(st_s["gm_v"])
        outs["diff_p"].append(st_p["dkv"].reshape(bp, sp, 2, DIFF_HEADS, 2 * DIFF_QK))
        outs["diff_s"].append(st_s["dkv"].reshape(bs, ss, 2, DIFF_HEADS, 2 * DIFF_QK))
        i = l // 2
        if l % 2 == 0:
            xp = ffn_swiglu(xp, norm_ffn_g[l], ffn_w_gate[i], ffn_w_up[i], ffn_w_down[i])
            xs = ffn_swiglu(xs, norm_ffn_g[l], ffn_w_gate[i], ffn_w_up[i], ffn_w_down[i])
        else:
            xp = moe_swiglu(xp, norm_ffn_g[l], moe_router[i], moe_router_b[i], moe_w_gate[i], moe_w_up[i], moe_w_down[i])
            xs = moe_swiglu(xs, norm_ffn_g[l], moe_router[i], moe_router_b[i], moe_w_gate[i], moe_w_up[i], moe_w_down[i])
        final = l == depth - 1
        xp = ple(xp, p_prompt[l].reshape(bp * sp, -1), ple_norm_g[l], ple_w_gate[l], ple_w_proj[l], final_norm_g, final)
        xs = ple(xs, p_sample[l].reshape(bs * ss, -1), ple_norm_g[l], ple_w_gate[l], ple_w_proj[l], final_norm_g, final)
    st = lambda k: jnp.stack(outs[k])
    return (xp.reshape(bp, sp, d), xs.reshape(bs, ss, d), st("nsa_p"), st("nsa_s"), st("win_p"), st("win_s"),
            st("pool_p"), st("pool_s"), st("gmv_s"), st("diff_p"), st("diff_s"))
```

```python
import functools
import math

import numpy as np
import jax
import jax.numpy as jnp
from jax import lax
from jax.experimental import pallas as pl
from jax.experimental.pallas import tpu as pltpu

F32 = jnp.float32
BF16 = jnp.bfloat16

D_MODEL = 1024
PAGE = 128
NSA_HEADS = 8
NSA_G = 2
NSA_J = 4
DH = 64
CMP_BLOCK = 32
CMP_STRIDE = 16
SEL_BLOCK = 64
SEL_TOPK = 16
FORCE_BONUS = 100.0
WINDOW = 512
POOL_WINDOWS = (2, 4, 8, 16)
POOL_MEM = 15
GM_W = 512
GM_CHUNK = 128
DIFF_HEADS = 8
DIFF_QK = 32
N_BRANCH = 4
BRANCH_W = 512
N_EXPERTS = 8
EPS = 1e-6
NEG = -1e30
LANE = 128
VMEM_LIMIT = 56 * 1024 * 1024

OFF_Q, OFF_KV, OFF_GATE, OFF_POOL, OFF_GM, OFF_DQ, OFF_DKV, OFF_END = 0, 512, 1280, 1304, 1816, 2840, 3352, 4376


def _tile(n, pref):
    t = min(n, pref)
    while n % t:
        t //= 2
    return t


def _cp(sem, vmem=VMEM_LIMIT):
    return pltpu.CompilerParams(dimension_semantics=sem, vmem_limit_bytes=vmem)


def _gelu(x):
    return 0.5 * x * (1.0 + jnp.tanh(0.7978845608028654 * (x + 0.044715 * (x * x * x))))


def _rms(x, g):
    return x * lax.rsqrt(jnp.mean(x * x, axis=-1, keepdims=True) + EPS) * g


def _dot(a, b):
    return jnp.dot(a, b, preferred_element_type=F32)


def _dot_nt(a, b):
    return lax.dot_general(a, b, (((1,), (1,)), ((), ())), preferred_element_type=F32)


def _rmsnorm_body(x_ref, g_ref, o_ref):
    o_ref[...] = _rms(x_ref[...], g_ref[...]).astype(o_ref.dtype)


def rmsnorm(x, g, out_dtype):
    t, d = x.shape
    tm = _tile(t, 512)
    return pl.pallas_call(
        _rmsnorm_body, name="rmsnorm", grid=(t // tm,),
        in_specs=[pl.BlockSpec((tm, d), lambda i: (i, 0)), pl.BlockSpec((1, d), lambda i: (0, 0))],
        out_specs=pl.BlockSpec((tm, d), lambda i: (i, 0)),
        out_shape=jax.ShapeDtypeStruct((t, d), out_dtype),
        compiler_params=_cp(("parallel",)),
    )(x, g.reshape(1, d))


def _mm_body(x_ref, w_ref, o_ref, *, scale, act):
    y = _dot(x_ref[...].astype(BF16), w_ref[...])
    if scale is not None:
        y = y * scale
    if act == "sigmoid":
        y = jax.nn.sigmoid(y)
    o_ref[...] = y.astype(o_ref.dtype)


def mm(x, w, *, scale=None, act=None, out_dtype=F32):
    t, k = x.shape
    n = w.shape[1]
    tm = _tile(t, 512)
    tn = _tile(n, 512)
    return pl.pallas_call(
        functools.partial(_mm_body, scale=scale, act=act), name="mm", grid=(t // tm, n // tn),
        in_specs=[pl.BlockSpec((tm, k), lambda i, j: (i, 0)), pl.BlockSpec((k, tn), lambda i, j: (0, j))],
        out_specs=pl.BlockSpec((tm, tn), lambda i, j: (i, j)),
        out_shape=jax.ShapeDtypeStruct((t, n), out_dtype),
        compiler_params=_cp(("parallel", "parallel")),
    )(x, w.astype(BF16))


def _compress_body(xk_ref, xv_ref, wc_ref, bias_ref, w2_ref, o_ref, *, nch):
    _compress_core(lambda s: xk_ref[pl.ds(s, nch, stride=CMP_STRIDE), :],
                   lambda s: xv_ref[pl.ds(s, nch, stride=CMP_STRIDE), :], wc_ref, bias_ref, w2_ref, o_ref, nch)


def _compress_core(load_k, load_v, wc_ref, bias_ref, w2_ref, o_ref, nch):
    acc = jnp.zeros((nch, 512), F32)
    for s in range(CMP_STRIDE):
        xs = jnp.concatenate([load_k(s), load_v(s)], axis=1).astype(BF16)
        acc = acc + _dot(xs, wc_ref[s])
    hi_next = pltpu.roll(acc[:, 256:], nch - 1, 0)
    hid = _gelu(acc[:, :256] + hi_next + bias_ref[...])
    o_ref[...] = _dot(hid.astype(BF16), w2_ref[...]).astype(o_ref.dtype)


def compress(rows, wc, bias, w2, nch):
    b = rows.shape[0]
    return pl.pallas_call(
        functools.partial(_compress_body, nch=nch), name="compress", grid=(b,),
        in_specs=[pl.BlockSpec((None, nch * CMP_STRIDE, LANE), lambda i: (i, 0, 0)),
                  pl.BlockSpec((None, nch * CMP_STRIDE, LANE), lambda i: (i, 0, 1)),
                  pl.BlockSpec((CMP_STRIDE, 256, 512), lambda i: (0, 0, 0)),
                  pl.BlockSpec((1, 256), lambda i: (0, 0)),
                  pl.BlockSpec((256, 256), lambda i: (0, 0))],
        out_specs=pl.BlockSpec((None, nch, 256), lambda i: (i, 0, 0)),
        out_shape=jax.ShapeDtypeStruct((b, nch, 256), BF16),
        compiler_params=_cp(("parallel",)),
    )(rows, rows, wc, bias, w2)


def _stack_heads(q, tq):
    lane = lax.broadcasted_iota(jnp.int32, (tq, LANE), 1)
    parts = []
    for g in range(NSA_G):
        keep = (lane < DH) if g == 0 else (lane >= DH)
        for j in range(NSA_J):
            parts.append(jnp.where(keep, q[:, j * LANE:(j + 1) * LANE], 0.0))
    return jnp.concatenate(parts, axis=0)


def _unstack_heads(o, tq):
    lane = lax.broadcasted_iota(jnp.int32, (tq, LANE), 1)
    outs = []
    for j in range(NSA_J):
        outs.append(jnp.where(lane < DH, o[j * tq:(j + 1) * tq], o[(NSA_J + j) * tq:(NSA_J + j + 1) * tq]))
    return jnp.concatenate(outs, axis=1)


def _cmp_body(q_ref, kv_ref, msel_ref, o_ref, sel_ref, *, tq, nch, n_cmp, n_sel, nsp, qpos0):
    i = pl.program_id(1)
    r = 8 * tq
    qs = _stack_heads(q_ref[...], tq).astype(BF16)
    kc = kv_ref[:, 0:LANE]
    vc = kv_ref[:, LANE:2 * LANE]
    s = _dot_nt(qs, kc)
    row = lax.broadcasted_iota(jnp.int32, (r, nch), 0)
    col = lax.broadcasted_iota(jnp.int32, (r, nch), 1)
    qpos = qpos0 + i * tq + (row & (tq - 1))
    cmask = (col * CMP_STRIDE + (CMP_BLOCK - 1) <= qpos) & (col < n_cmp)
    s = jnp.where(cmask, s, NEG)
    m = jnp.max(s, axis=1, keepdims=True)
    e = jnp.where(cmask, jnp.exp(s - m), 0.0)
    l = jnp.sum(e, axis=1, keepdims=True)
    p = (e * (1.0 / jnp.where(l > 0.0, l, 1.0))).astype(BF16)
    o_ref[...] = _unstack_heads(_dot(p, vc), tq)
    imp_all = _dot(p, msel_ref[...])
    blk = lax.broadcasted_iota(jnp.int32, (tq, nsp), 1)
    qp = qpos0 + i * tq + lax.broadcasted_iota(jnp.int32, (tq, nsp), 0)
    cur = qp // SEL_BLOCK
    valid = (blk <= cur) & (blk < n_sel)
    forced = (blk == 0) | (blk == cur) | (blk == cur - 1)
    for g in range(NSA_G):
        imp = imp_all[(g * NSA_J) * tq:(g * NSA_J + 1) * tq]
        for j in range(1, NSA_J):
            imp = imp + imp_all[(g * NSA_J + j) * tq:(g * NSA_J + j + 1) * tq]
        score = jnp.where(valid, imp + FORCE_BONUS * forced.astype(F32), -1.0)
        score = jnp.where(blk < n_sel, score, -2.0)
        rank = jnp.zeros((tq, nsp), F32)
        for mth in range(n_sel):
            cm = score[:, mth:mth + 1]
            beats = (cm > score) | ((cm == score) & (blk > mth))
            rank = rank + beats.astype(F32)
        chosen = (rank < float(min(SEL_TOPK, n_sel))) & valid
        sel_ref[:, g * nsp:(g + 1) * nsp] = chosen.astype(F32)


def cmp_attend(q, kcvc, msel, *, tq, n_cmp, n_sel, qpos0):
    b, sq, _ = q.shape
    nch = kcvc.shape[1]
    nsp = msel.shape[1]
    return pl.pallas_call(
        functools.partial(_cmp_body, tq=tq, nch=nch, n_cmp=n_cmp, n_sel=n_sel, nsp=nsp, qpos0=qpos0),
        name="cmp_attend",
        grid=(b, sq // tq),
        in_specs=[pl.BlockSpec((None, tq, 512), lambda bi, i: (bi, i, 0)),
                  pl.BlockSpec((None, nch, 256), lambda bi, i: (bi, 0, 0)),
                  pl.BlockSpec((nch, nsp), lambda bi, i: (0, 0))],
        out_specs=[pl.BlockSpec((None, tq, 512), lambda bi, i: (bi, i, 0)),
                   pl.BlockSpec((None, tq, 2 * nsp), lambda bi, i: (bi, i, 0))],
        out_shape=[jax.ShapeDtypeStruct((b, sq, 512), F32), jax.ShapeDtypeStruct((b, sq, 2 * nsp), F32)],
        compiler_params=_cp(("parallel", "parallel")),
    )(q, kcvc, msel)


ROW_BLOCK = 128


def _flash_update(qs, k, v, bias, m, l, acc, kv_t=False):
    s = _dot(qs, k) if kv_t else _dot_nt(qs, k)
    if bias is not None:
        s = s + bias
    tiles = [s[:, c:c + LANE] for c in range(0, s.shape[1], LANE)]
    m_new = jnp.maximum(m, jnp.max(functools.reduce(jnp.maximum, tiles), axis=1, keepdims=True))
    alpha = jnp.exp(m - m_new)
    ps = [jnp.exp(t - m_new) for t in tiles]
    l_new = alpha * l + jnp.sum(functools.reduce(jnp.add, ps), axis=1, keepdims=True)
    pb = jnp.concatenate(ps, axis=1).astype(BF16)
    acc_new = alpha * acc + (_dot_nt(pb, v) if kv_t else _dot(pb, v))
    return m_new, l_new, acc_new


def _flash_rows(qs_ref, k, v, bias_of, m_ref, l_ref, acc_ref, nrows, tq, kv_t=False):
    if tq < ROW_BLOCK:
        parts = [bias_of(h, 0, tq) for h in range(nrows // tq)]
        bias = None if parts[0] is None else jnp.concatenate(parts, axis=0)
        blocks = [(pl.ds(0, nrows), bias)]
    else:
        blocks = [(pl.ds(r0, ROW_BLOCK), bias_of(r0 // tq, r0 % tq, ROW_BLOCK)) for r0 in range(0, nrows, ROW_BLOCK)]
    for sl, bias in blocks:
        m, l, acc = _flash_update(qs_ref[sl], k, v, bias, m_ref[sl], l_ref[sl], acc_ref[sl], kv_t)
        m_ref[sl] = m
        l_ref[sl] = l
        acc_ref[sl] = acc


def _sel_ok(sel, e, nsp):
    return [_dot(sel[:, g * nsp:(g + 1) * nsp], e) > 0.5 for g in range(NSA_G)]


def _nsa_flash_body(*refs, mode, tq, tk, nk, qpos0, kpos0, kt_fn, nsp, free_step):
    if mode == "sel":
        q_ref, k_ref, v_ref, sel_ref, e_ref, o_ref, qs_ref, m_ref, l_ref, acc_ref = refs
    else:
        q_ref, k_ref, v_ref, o_ref, qs_ref, m_ref, l_ref, acc_ref = refs
    i = pl.program_id(1)
    j = pl.program_id(2)
    r = 8 * tq

    @pl.when(j == 0)
    def _init():
        qs_ref[...] = _stack_heads(q_ref[...], tq).astype(BF16)
        m_ref[...] = jnp.full((r, LANE), NEG, F32)
        l_ref[...] = jnp.zeros((r, LANE), F32)
        acc_ref[...] = jnp.zeros((r, LANE), F32)

    kt, valid = kt_fn(i, j)

    def step(masked):
        k = k_ref[...].astype(BF16)
        v = v_ref[...].astype(BF16)
        if not masked:
            _flash_rows(qs_ref, k, v, lambda h, t0, n: None, m_ref, l_ref, acc_ref, r, tq)
            return
        qpos = qpos0 + i * tq + lax.broadcasted_iota(jnp.int32, (tq, tk), 0)
        kpos = kpos0 + kt * tk + lax.broadcasted_iota(jnp.int32, (tq, tk), 1)
        ok = kpos <= qpos
        if mode == "win":
            ok = ok & (kpos > qpos - WINDOW)
            bias = [jnp.where(ok, 0.0, NEG)] * NSA_G
        else:
            bias = [jnp.where(ok & okg, 0.0, NEG) for okg in _sel_ok(sel_ref[...].astype(BF16), e_ref[...], nsp)]
        _flash_rows(qs_ref, k, v, lambda h, t0, n: bias[h // NSA_J][t0:t0 + n], m_ref, l_ref, acc_ref, r, tq)

    if free_step is None:
        pl.when(valid)(lambda: step(True))
    else:
        pl.when(valid & (j != free_step))(lambda: step(True))
        pl.when(valid & (j == free_step))(lambda: step(False))

    @pl.when(j == nk - 1)
    def _fin():
        l = l_ref[...]
        o = acc_ref[...] * (1.0 / jnp.where(l > 0.0, l, 1.0))
        o_ref[...] = _unstack_heads(o, tq)


def nsa_flash(q, kv, kcol, vcol, *, mode, tq, tk, nk, qpos0, kpos0, kt_fn, sel=None, emat=None, free_step=None):
    b, sq, _ = q.shape
    nsp = 0 if sel is None else sel.shape[2] // 2

    def kmap(col):
        return lambda bi, i, j: (bi, kt_fn(i, j)[0], col)

    in_specs = [pl.BlockSpec((None, tq, 512), lambda bi, i, j: (bi, i, 0)),
                pl.BlockSpec((None, tk, LANE), kmap(kcol)),
                pl.BlockSpec((None, tk, LANE), kmap(vcol))]
    args = [q, kv, kv]
    if mode == "sel":
        in_specs += [pl.BlockSpec((None, tq, 2 * nsp), lambda bi, i, j: (bi, i, 0)),
                     pl.BlockSpec((nsp, tk), lambda bi, i, j: (0, kt_fn(i, j)[0]))]
        args += [sel, emat]
    r = 8 * tq
    return pl.pallas_call(
        functools.partial(_nsa_flash_body, mode=mode, tq=tq, tk=tk, nk=nk, qpos0=qpos0, kpos0=kpos0,
                          kt_fn=kt_fn, nsp=nsp, free_step=free_step),
        name="nsa_" + mode, grid=(b, sq // tq, nk),
        in_specs=in_specs,
        out_specs=pl.BlockSpec((None, tq, 512), lambda bi, i, j: (bi, i, 0)),
        out_shape=jax.ShapeDtypeStruct((b, sq, 512), F32),
        scratch_shapes=[pltpu.VMEM((r, LANE), BF16), pltpu.VMEM((r, LANE), F32), pltpu.VMEM((r, LANE), F32),
                        pltpu.VMEM((r, LANE), F32)],
        compiler_params=_cp(("parallel", "parallel", "arbitrary")),
    )(*args)


def _diff_init(q_ref, qs_ref, m_ref, l_ref, acc_ref, tq):
    r = 4 * tq
    npair = DIFF_HEADS // 2
    lane = lax.broadcasted_iota(jnp.int32, (tq, LANE), 1)
    for hp in range(npair):
        q = q_ref[:, hp * LANE:(hp + 1) * LANE]
        parts = []
        for h in range(2):
            for mth in range(2):
                lo = h * 2 * DIFF_QK + mth * DIFF_QK
                parts.append(jnp.where((lane >= lo) & (lane < lo + DIFF_QK), q, 0.0))
        qs_ref[hp] = jnp.concatenate(parts, axis=0).astype(BF16)
    m_ref[...] = jnp.full((npair, r, LANE), NEG, F32)
    l_ref[...] = jnp.zeros((npair, r, LANE), F32)
    acc_ref[...] = jnp.zeros((npair, r, LANE), F32)


def _diff_step(k_of, v_of, bias, qs_ref, m_ref, l_ref, acc_ref, tq, kv_t=False):
    bias_of = lambda h, t0, n: None if bias is None else bias[t0:t0 + n]
    for hp in range(DIFF_HEADS // 2):
        _flash_rows(qs_ref.at[hp], k_of(hp), v_of(hp), bias_of, m_ref.at[hp], l_ref.at[hp], acc_ref.at[hp],
                    4 * tq, tq, kv_t)


def _diff_fin(lam_ref, g_ref, o_ref, l_ref, acc_ref, tq, lam_init):
    lp = lam_ref[...]
    lam = (jnp.exp(jnp.sum(lp[0:1] * lp[1:2], axis=1, keepdims=True))
           - jnp.exp(jnp.sum(lp[2:3] * lp[3:4], axis=1, keepdims=True)) + lam_init)
    lane = lax.broadcasted_iota(jnp.int32, (tq, LANE), 1)
    lo_half = lane < 2 * DIFF_QK
    for hp in range(DIFF_HEADS // 2):
        l = l_ref[hp]
        a = acc_ref[hp] * (1.0 / jnp.where(l > 0.0, l, 1.0))
        o0 = a[0:tq] - lam * a[tq:2 * tq]
        o1 = a[2 * tq:3 * tq] - lam * a[3 * tq:4 * tq]
        o = jnp.where(lo_half, o0, o1)
        sq = o * o
        ms0 = jnp.sum(jnp.where(lo_half, sq, 0.0), axis=1, keepdims=True)
        ms1 = jnp.sum(jnp.where(lo_half, 0.0, sq), axis=1, keepdims=True)
        ms = jnp.where(lo_half, ms0, ms1) * (1.0 / (2 * DIFF_QK))
        y = o * lax.rsqrt(ms + EPS) * g_ref[...]
        o_ref[:, hp * LANE:(hp + 1) * LANE] = y * (1.0 - lam_init)


def _diff_body(lam_ref, q_ref, k_ref, v_ref, g_ref, o_ref, qs_ref, m_ref, l_ref, acc_ref, *,
               tq, tk, nk, qpos0, lam_init, kt_fn):
    i = pl.program_id(1)
    j = pl.program_id(2)
    r = 4 * tq

    @pl.when(j == 0)
    def _init():
        _diff_init(q_ref, qs_ref, m_ref, l_ref, acc_ref, tq)

    kt, valid, diag = kt_fn(i, j)
    k_of = lambda hp: k_ref[:, hp * LANE:(hp + 1) * LANE].astype(BF16)
    v_of = lambda hp: v_ref[:, hp * LANE:(hp + 1) * LANE].astype(BF16)

    @pl.when(valid & jnp.logical_not(diag))
    def _below():
        _diff_step(k_of, v_of, None, qs_ref, m_ref, l_ref, acc_ref, tq)

    @pl.when(valid & diag)
    def _diag():
        qpos = qpos0 + i * tq + lax.broadcasted_iota(jnp.int32, (tq, tk), 0)
        kpos = kt * tk + lax.broadcasted_iota(jnp.int32, (tq, tk), 1)
        _diff_step(k_of, v_of, jnp.where(kpos <= qpos, 0.0, NEG), qs_ref, m_ref, l_ref, acc_ref, tq)

    @pl.when(j == nk - 1)
    def _fin():
        _diff_fin(lam_ref, g_ref, o_ref, l_ref, acc_ref, tq, lam_init)


def diff_attend(lam_par, q, kv, gnorm, *, tq, tk, nk, qpos0, lam_init, kt_fn):
    b, sq, _ = q.shape
    r = 4 * tq
    npair = DIFF_HEADS // 2
    return pl.pallas_call(
        functools.partial(_diff_body, tq=tq, tk=tk, nk=nk, qpos0=qpos0, lam_init=lam_init, kt_fn=kt_fn),
        name="diff_attend",
        grid=(b, sq // tq, nk),
        in_specs=[pl.BlockSpec((4, DIFF_QK), lambda bi, i, j: (0, 0)),
                  pl.BlockSpec((None, tq, 512), lambda bi, i, j: (bi, i, 0)),
                  pl.BlockSpec((None, tk, 512), lambda bi, i, j: (bi, kt_fn(i, j)[0], 0)),
                  pl.BlockSpec((None, tk, 512), lambda bi, i, j: (bi, kt_fn(i, j)[0], 1)),
                  pl.BlockSpec((1, LANE), lambda bi, i, j: (0, 0))],
        out_specs=pl.BlockSpec((None, tq, 512), lambda bi, i, j: (bi, i, 0)),
        out_shape=jax.ShapeDtypeStruct((b, sq, 512), F32),
        scratch_shapes=[pltpu.VMEM((npair, r, LANE), BF16), pltpu.VMEM((npair, r, LANE), F32),
                        pltpu.VMEM((npair, r, LANE), F32), pltpu.VMEM((npair, r, LANE), F32)],
        compiler_params=_cp(("parallel", "parallel", "arbitrary")),
    )(lam_par, q, kv, kv, gnorm)


def _pool_body(x_ref, pw_ref, sc_ref, o_ref, *, sq, start):
    pos = start + lax.broadcasted_iota(jnp.int32, (sq, LANE), 0)
    for g, w in enumerate(POOL_WINDOWS):
        c0 = g * LANE
        x = x_ref[pl.ds(16, sq), c0:c0 + LANE]
        tot = x
        for back in range(1, w):
            tot = tot + x_ref[pl.ds(16 - back, sq), c0:c0 + LANE]
        cnt = jnp.minimum(w, pos + 1).astype(F32)
        mix = tot / cnt - x
        y = _dot(mix.astype(BF16), pw_ref[g])
        o_ref[:, c0:c0 + LANE] = y * sc_ref[:, c0:c0 + LANE]


def pool_mix(xcat, pw, scale, start):
    b, rows, _ = xcat.shape
    sq = rows - 16
    return pl.pallas_call(
        functools.partial(_pool_body, sq=sq, start=start), name="pool_mix", grid=(b,),
        in_specs=[pl.BlockSpec((None, rows, 512), lambda i: (i, 0, 0)),
                  pl.BlockSpec((4, LANE, LANE), lambda i: (0, 0, 0)),
                  pl.BlockSpec((1, 512), lambda i: (0, 0))],
        out_specs=pl.BlockSpec((None, sq, 512), lambda i: (i, 0, 0)),
        out_shape=jax.ShapeDtypeStruct((b, sq, 512), F32),
        compiler_params=_cp(("parallel",)),
    )(xcat, pw.astype(BF16), scale.reshape(1, 512))


def _gmlp_body(z_ref, ng_ref, nb_ref, ws_ref, bs_ref, o_ref, v_ref, *, tg):
    z = _gelu(z_ref[...])
    u = z[:, :GM_W]
    vr = z[:, GM_W:]
    xc = vr - jnp.mean(vr, axis=-1, keepdims=True)
    v = xc * lax.rsqrt(jnp.mean(xc * xc, axis=-1, keepdims=True) + EPS) * ng_ref[...] + nb_ref[...]
    v_ref[...] = v
    vb = v.astype(BF16)
    for c in range(tg // GM_CHUNK):
        r0 = c * GM_CHUNK
        for g in range(4):
            c0 = g * LANE
            mixed = _dot(ws_ref[g], vb[r0:r0 + GM_CHUNK, c0:c0 + LANE]) + bs_ref[:, c0:c0 + LANE]
            o_ref[r0:r0 + GM_CHUNK, c0:c0 + LANE] = u[r0:r0 + GM_CHUNK, c0:c0 + LANE] * mixed


def gmlp_mix(z, ng, nb, ws_tril, bs_exp):
    b, s, _ = z.shape
    tg = _tile(s, 512)
    return pl.pallas_call(
        functools.partial(_gmlp_body, tg=tg), name="gmlp_mix", grid=(b, s // tg),
        in_specs=[pl.BlockSpec((None, tg, 1024), lambda bi, i: (bi, i, 0)),
                  pl.BlockSpec((1, 512), lambda bi, i: (0, 0)),
                  pl.BlockSpec((1, 512), lambda bi, i: (0, 0)),
                  pl.BlockSpec((4, GM_CHUNK, GM_CHUNK), lambda bi, i: (0, 0, 0)),
                  pl.BlockSpec((GM_CHUNK, 512), lambda bi, i: (0, 0))],
        out_specs=[pl.BlockSpec((None, tg, 512), lambda bi, i: (bi, i, 0)),
                   pl.BlockSpec((None, tg, 512), lambda bi, i: (bi, i, 0))],
        out_shape=[jax.ShapeDtypeStruct((b, s, 512), F32), jax.ShapeDtypeStruct((b, s, 512), F32)],
        compiler_params=_cp(("parallel", "parallel")),
    )(z, ng.reshape(1, 512), nb.reshape(1, 512), ws_tril, bs_exp)


def _finish_body(x_ref, hn_ref, g3_ref, ocmp_ref, osel_ref, owin_ref, opool_ref, ogm_ref, odiff_ref,
                 wgate_ref, bgate_ref, wbr_ref, wo_ref, o_ref):
    hn = hn_ref[...]
    g3 = g3_ref[...]
    onsa = g3[:, 0:512] * ocmp_ref[...] + g3[:, 512:1024] * osel_ref[...] + g3[:, 1024:1536] * owin_ref[...]
    branches = (onsa, opool_ref[...], ogm_ref[...], odiff_ref[...])
    acc = jnp.zeros(o_ref.shape, F32)
    for n in range(N_BRANCH):
        c0 = n * D_MODEL
        gate = jax.nn.sigmoid(_dot(hn, wgate_ref[:, c0:c0 + D_MODEL]) + bgate_ref[:, c0:c0 + D_MODEL])
        acc = acc + gate * _dot(branches[n].astype(BF16), wbr_ref[n])
    o_ref[...] = x_ref[...] + _dot(acc.astype(BF16), wo_ref[...])


def finish_mixer(x, hn, g3, ocmp, osel, owin, opool, ogm, odiff, wgate, bgate, wbr, wo):
    t = x.shape[0]
    tm = _tile(t, 256)
    row = lambda w: pl.BlockSpec((tm, w), lambda i: (i, 0))
    const = lambda shape: pl.BlockSpec(shape, lambda i: (0,) * len(shape), pipeline_mode=pl.Buffered(1))
    return pl.pallas_call(
        _finish_body, name="finish_mixer", grid=(t // tm,),
        in_specs=[row(1024), row(1024), row(1536), row(512), row(512), row(512), row(512), row(512), row(512),
                  const((D_MODEL, N_BRANCH * D_MODEL)), const((1, N_BRANCH * D_MODEL)),
                  const((N_BRANCH, BRANCH_W, D_MODEL)), const((D_MODEL, D_MODEL))],
        out_specs=row(1024),
        out_shape=jax.ShapeDtypeStruct((t, D_MODEL), F32),
        compiler_params=_cp(("parallel",)),
    )(x, hn, g3, ocmp, osel, owin, opool, ogm, odiff, wgate, bgate, wbr, wo)


def _ffn_body(x_ref, g_ref, wg_ref, wu_ref, wd_ref, o_ref, hn_ref, acc_ref, *, nf):
    j = pl.program_id(1)

    @pl.when(j == 0)
    def _init():
        hn_ref[...] = _rms(x_ref[...], g_ref[...]).astype(BF16)
        acc_ref[...] = jnp.zeros(acc_ref.shape, F32)

    h = hn_ref[...]
    a = _dot(h, wg_ref[...])
    act = a * jax.nn.sigmoid(a) * _dot(h, wu_ref[...])
    acc_ref[...] += _dot(act.astype(BF16), wd_ref[...])

    @pl.when(j == nf - 1)
    def _fin():
        o_ref[...] = x_ref[...] + acc_ref[...]


def ffn_swiglu(x, g, wg, wu, wd):
    t, d = x.shape
    f = wg.shape[1]
    tm = _tile(t, 512)
    tf = 256
    nf = f // tf
    return pl.pallas_call(
        functools.partial(_ffn_body, nf=nf), name="ffn_swiglu", grid=(t // tm, nf),
        in_specs=[pl.BlockSpec((tm, d), lambda i, j: (i, 0)), pl.BlockSpec((1, d), lambda i, j: (0, 0)),
                  pl.BlockSpec((d, tf), lambda i, j: (0, j)), pl.BlockSpec((d, tf), lambda i, j: (0, j)),
                  pl.BlockSpec((tf, d), lambda i, j: (j, 0))],
        out_specs=pl.BlockSpec((tm, d), lambda i, j: (i, 0)),
        out_shape=jax.ShapeDtypeStruct((t, d), F32),
        scratch_shapes=[pltpu.VMEM((tm, d), BF16), pltpu.VMEM((tm, d), F32)],
        compiler_params=_cp(("parallel", "arbitrary")),
    )(x, g.reshape(1, d), wg.astype(BF16), wu.astype(BF16), wd.astype(BF16))


MOE_SUB = 256


def _moe_body(x_ref, g_ref, r_ref, rb_ref, u_ref, wg_ref, wu_ref, wd_ref, o_ref,
              hn_ref, comb_ref, post_ref, posr_ref, cnt_ref, xe_ref, ye_ref, acc_ref, *, nf):
    e = pl.program_id(1)
    f = pl.program_id(2)
    tm = x_ref.shape[0]
    lane = lax.broadcasted_iota(jnp.int32, (tm, LANE), 1)

    @pl.when((e == 0) & (f == 0))
    def _init():
        hn = _rms(x_ref[...], g_ref[...]).astype(BF16)
        hn_ref[...] = hn
        lg = _dot(hn, r_ref[...]) + rb_ref[...]
        m1 = jnp.max(lg, axis=1, keepdims=True)
        i1 = jnp.min(jnp.where(lg == m1, lane, LANE), axis=1, keepdims=True)
        lg2 = jnp.where(lane == i1, -3e38, lg)
        m2 = jnp.max(lg2, axis=1, keepdims=True)
        i2 = jnp.min(jnp.where(lg2 == m2, lane, LANE), axis=1, keepdims=True)
        e2 = jnp.exp(m2 - m1)
        w1 = 1.0 / (1.0 + e2)
        comb = jnp.where(lane == i1, w1, 0.0) + jnp.where(lane == i2, e2 * w1, 0.0)
        comb_ref[...] = comb
        ind_t = (comb.T > 0.0).astype(F32)
        pos_t = _dot(ind_t.astype(BF16), u_ref[...])
        post_ref[...] = jnp.where(ind_t > 0.0, pos_t, -1.0)
        posr_ref[...] = post_ref[...].T
        cnt = jnp.sum(ind_t, axis=1, keepdims=True)
        for ex in range(N_EXPERTS):
            cnt_ref[ex] = cnt[ex, 0].astype(jnp.int32)
        acc_ref[...] = jnp.zeros(acc_ref.shape, F32)

    nsub = (cnt_ref[e] + (MOE_SUB - 1)) // MOE_SUB

    @pl.when(f == 0)
    def _gather():
        slot_t = post_ref[pl.ds(e, 1), :]
        base = lax.broadcasted_iota(jnp.int32, (MOE_SUB, tm), 0).astype(F32)

        def body(u, carry):
            onehot = jnp.where(slot_t == base + (u * MOE_SUB).astype(F32), 1.0, 0.0).astype(BF16)
            xe_ref[u] = _dot(onehot, hn_ref[...]).astype(BF16)
            ye_ref[u] = jnp.zeros((MOE_SUB, x_ref.shape[1]), F32)
            return carry

        lax.fori_loop(0, nsub, body, 0)

    def expert(u, carry):
        xe = xe_ref[u]
        a = _dot(xe, wg_ref[...])
        act = a * jax.nn.sigmoid(a) * _dot(xe, wu_ref[...])
        ye_ref[u] += _dot(act.astype(BF16), wd_ref[...])
        return carry

    lax.fori_loop(0, nsub, expert, 0)

    @pl.when(f == nf - 1)
    def _scatter():
        slot_r = jnp.sum(jnp.where(lane == e, posr_ref[...], 0.0), axis=1, keepdims=True)
        w_r = jnp.sum(jnp.where(lane == e, comb_ref[...], 0.0), axis=1, keepdims=True)
        base = lax.broadcasted_iota(jnp.int32, (tm, MOE_SUB), 1).astype(F32)

        def body(u, carry):
            onehot = jnp.where(slot_r == base + (u * MOE_SUB).astype(F32), 1.0, 0.0).astype(BF16)
            y = ye_ref[u]
            y_hi = y.astype(BF16)
            y_lo = (y - y_hi.astype(F32)).astype(BF16)
            acc_ref[...] += w_r * (_dot(onehot, y_hi) + _dot(onehot, y_lo))
            return carry

        lax.fori_loop(0, nsub, body, 0)

    @pl.when((e == N_EXPERTS - 1) & (f == nf - 1))
    def _fin():
        o_ref[...] = x_ref[...] + acc_ref[...]


def moe_swiglu(x, g, router, router_b, wg, wu, wd):
    t, d = x.shape
    f = wg.shape[2]
    tm = _tile(t, 1024)
    tf = 512
    nf = f // tf
    nsubmax = -(-tm // MOE_SUB)
    rpad = jnp.zeros((d, LANE), F32).at[:, :N_EXPERTS].set(router).astype(BF16)
    rbpad = jnp.full((1, LANE), NEG, F32).at[0, :N_EXPERTS].set(router_b)
    before = jnp.asarray(np.triu(np.ones((tm, tm), np.float32), 1), BF16)
    const = lambda shape: pl.BlockSpec(shape, lambda i, e, j: (0,) * len(shape))
    return pl.pallas_call(
        functools.partial(_moe_body, nf=nf), name="moe_swiglu", grid=(t // tm, N_EXPERTS, nf),
        in_specs=[pl.BlockSpec((tm, d), lambda i, e, j: (i, 0)), const((1, d)), const((d, LANE)), const((1, LANE)),
                  const((tm, tm)),
                  pl.BlockSpec((None, d, tf), lambda i, e, j: (e, 0, j)),
                  pl.BlockSpec((None, d, tf), lambda i, e, j: (e, 0, j)),
                  pl.BlockSpec((None, tf, d), lambda i, e, j: (e, j, 0))],
        out_specs=pl.BlockSpec((tm, d), lambda i, e, j: (i, 0)),
        out_shape=jax.ShapeDtypeStruct((t, d), F32),
        scratch_shapes=[pltpu.VMEM((tm, d), BF16), pltpu.VMEM((tm, LANE), F32), pltpu.VMEM((LANE, tm), F32),
                        pltpu.VMEM((tm, LANE), F32), pltpu.SMEM((N_EXPERTS,), jnp.int32),
                        pltpu.VMEM((nsubmax, MOE_SUB, d), BF16), pltpu.VMEM((nsubmax, MOE_SUB, d), F32),
                        pltpu.VMEM((tm, d), F32)],
        compiler_params=_cp(("parallel", "arbitrary", "arbitrary")),
    )(x, g.reshape(1, d), rpad, rbpad, before, wg.astype(BF16), wu.astype(BF16), wd.astype(BF16))


def _ple_body(x_ref, p_ref, g_ref, wg_ref, wp_ref, fg_ref, o_ref, *, final):
    x = x_ref[...]
    hn = _rms(x, g_ref[...]).astype(BF16)
    gate = jax.nn.sigmoid(_dot(hn, wg_ref[...]))
    y = x + gate * _dot(p_ref[...].astype(BF16), wp_ref[...])
    if final:
        y = _rms(y, fg_ref[...])
    o_ref[...] = y


def ple(x, p, g, wg, wp, fg, final):
    t, d = x.shape
    pd = p.shape[1]
    tm = _tile(t, 512)
    return pl.pallas_call(
        functools.partial(_ple_body, final=final), name="ple", grid=(t // tm,),
        in_specs=[pl.BlockSpec((tm, d), lambda i: (i, 0)), pl.BlockSpec((tm, pd), lambda i: (i, 0)),
                  pl.BlockSpec((1, d), lambda i: (0, 0)), pl.BlockSpec((d, d), lambda i: (0, 0)),
                  pl.BlockSpec((pd, d), lambda i: (0, 0)), pl.BlockSpec((1, d), lambda i: (0, 0))],
        out_specs=pl.BlockSpec((tm, d), lambda i: (i, 0)),
        out_shape=jax.ShapeDtypeStruct((t, d), F32),
        compiler_params=_cp(("parallel",)),
    )(x, p, g.reshape(1, d), wg.astype(BF16), wp.astype(BF16), fg.reshape(1, d))


def _page_copies(pt_ref, cache_ref, bufs, sem, slot, b, c, *, layer, pc, cols):
    out = []
    for p in range(pc):
        pg = pt_ref[b, c * pc + p]
        for buf, (f0, w) in zip(bufs, cols):
            out.append(pltpu.make_async_copy(cache_ref.at[layer, pg, pl.ds(f0, w), :], buf.at[slot, p],
                                             sem.at[slot]))
    return out


def _paged_pipeline(pt_ref, cache_ref, bufs, sem, *, nb, nc, **kw):
    b = pl.program_id(0)
    c = pl.program_id(1) if nc > 1 else 0
    step = b * nc + c
    slot = step % 2

    @pl.when(step == 0)
    def _first():
        for d in _page_copies(pt_ref, cache_ref, bufs, sem, 0, 0, 0, **kw):
            d.start()

    @pl.when(step + 1 < nb * nc)
    def _prefetch():
        nxt = step + 1
        for d in _page_copies(pt_ref, cache_ref, bufs, sem, 1 - slot, nxt // nc, nxt % nc, **kw):
            d.start()

    for d in _page_copies(pt_ref, cache_ref, bufs, sem, slot, b, c, **kw):
        d.wait()
    return slot


def _compress_paged_body(pt_ref, cache_ref, wc_ref, bias_ref, w2_ref, o_ref, bufk, bufv, sem, rowk, rowv, *,
                         layer, npg, nb, nch):
    slot = _paged_pipeline(pt_ref, cache_ref, (bufk, bufv), sem, nb=nb, nc=1, layer=layer, pc=npg,
                           cols=((0, LANE), (LANE, LANE)))

    def to_rows(p, carry):
        rows = pl.ds(pl.multiple_of(p * PAGE, PAGE), PAGE)
        rowk[rows, :] = bufk[slot, p].T
        rowv[rows, :] = bufv[slot, p].T
        return carry

    lax.fori_loop(0, npg, to_rows, 0)
    _compress_core(lambda s: rowk[pl.ds(s, nch, stride=CMP_STRIDE), :],
                   lambda s: rowv[pl.ds(s, nch, stride=CMP_STRIDE), :], wc_ref, bias_ref, w2_ref, o_ref, nch)


def compress_paged(page_table, cache, layer, wc, bias, w2):
    b, npg = page_table.shape
    past = npg * PAGE
    nch = past // CMP_STRIDE
    const = lambda shape: pl.BlockSpec(shape, lambda i, pt: (0,) * len(shape))
    return pl.pallas_call(
        functools.partial(_compress_paged_body, layer=layer, npg=npg, nb=b, nch=nch), name="compress_paged",
        grid_spec=pltpu.PrefetchScalarGridSpec(
            num_scalar_prefetch=1, grid=(b,),
            in_specs=[pl.BlockSpec(memory_space=pl.ANY), const((CMP_STRIDE, 256, 512)), const((1, 256)),
                      const((256, 256))],
            out_specs=pl.BlockSpec((None, nch, 256), lambda i, pt: (i, 0, 0)),
            scratch_shapes=[pltpu.VMEM((2, npg, LANE, PAGE), F32), pltpu.VMEM((2, npg, LANE, PAGE), F32),
                            pltpu.SemaphoreType.DMA((2,)),
                            pltpu.VMEM((past, LANE), F32), pltpu.VMEM((past, LANE), F32)]),
        out_shape=jax.ShapeDtypeStruct((b, nch, 256), BF16),
        compiler_params=_cp(("arbitrary",)),
    )(page_table, cache, wc, bias, w2)


def _new_tile(new, c0, tq):
    return jnp.concatenate([new[:, c0:c0 + LANE], jnp.zeros((LANE - tq, LANE), F32)], axis=0).astype(BF16)


def _sel_paged_body(pt_ref, cache_ref, q_ref, new_ref, sel_ref, e_ref, o_ref, bufk, bufv, sem, *,
                    layer, npg, nb, tq, tk, nsp):
    slot = _paged_pipeline(pt_ref, cache_ref, (bufk, bufv), sem, nb=nb, nc=1, layer=layer, pc=npg,
                           cols=((2 * LANE, LANE), (3 * LANE, LANE)))
    past = npg * PAGE
    ppc = tk // PAGE
    r = 8 * tq
    qs = _stack_heads(q_ref[...], tq).astype(BF16)
    sel = sel_ref[...].astype(BF16)
    m = jnp.full((r, LANE), NEG, F32)
    l = jnp.zeros((r, LANE), F32)
    acc = jnp.zeros((r, LANE), F32)

    def stacked_bias(oks):
        return jnp.concatenate([jnp.where(ok, 0.0, NEG) for ok in oks for _ in range(NSA_J)], axis=0)

    for c in range(npg // ppc):
        kt = jnp.concatenate([bufk[slot, c * ppc + u] for u in range(ppc)], axis=1).astype(BF16)
        vt = jnp.concatenate([bufv[slot, c * ppc + u] for u in range(ppc)], axis=1).astype(BF16)
        bias = stacked_bias(_sel_ok(sel, e_ref[:, c * tk:(c + 1) * tk], nsp))
        m, l, acc = _flash_update(qs, kt, vt, bias, m, l, acc, kv_t=True)
    new = new_ref[...]
    causal = (lax.broadcasted_iota(jnp.int32, (tq, LANE), 1) <= lax.broadcasted_iota(jnp.int32, (tq, LANE), 0))
    bias = stacked_bias([ok & causal for ok in _sel_ok(sel, e_ref[:, past:past + LANE], nsp)])
    m, l, acc = _flash_update(qs, _new_tile(new, 2 * LANE, tq), _new_tile(new, 3 * LANE, tq), bias, m, l, acc)
    o_ref[...] = _unstack_heads(acc * (1.0 / jnp.where(l > 0.0, l, 1.0)), tq)


def sel_paged(page_table, cache, layer, q, new_rows, sel, emat):
    b, npg = page_table.shape
    past = npg * PAGE
    tq = q.shape[1]
    nsp = sel.shape[2] // 2
    return pl.pallas_call(
        functools.partial(_sel_paged_body, layer=layer, npg=npg, nb=b, tq=tq, tk=_tile(past, 512), nsp=nsp),
        name="sel_paged",
        grid_spec=pltpu.PrefetchScalarGridSpec(
            num_scalar_prefetch=1, grid=(b,),
            in_specs=[pl.BlockSpec(memory_space=pl.ANY),
                      pl.BlockSpec((None, tq, 512), lambda i, pt: (i, 0, 0)),
                      pl.BlockSpec((None, tq, 512), lambda i, pt: (i, 0, 0)),
                      pl.BlockSpec((None, tq, 2 * nsp), lambda i, pt: (i, 0, 0)),
                      pl.BlockSpec((nsp, past + LANE), lambda i, pt: (0, 0))],
            out_specs=pl.BlockSpec((None, tq, 512), lambda i, pt: (i, 0, 0)),
            scratch_shapes=[pltpu.VMEM((2, npg, LANE, PAGE), F32), pltpu.VMEM((2, npg, LANE, PAGE), F32),
                            pltpu.SemaphoreType.DMA((2,))]),
        out_shape=jax.ShapeDtypeStruct((b, tq, 512), F32),
        compiler_params=_cp(("arbitrary",)),
    )(page_table, cache, q, new_rows, sel, emat)


def _diff_paged_body(pt_ref, cache_ref, lam_ref, q_ref, new_ref, g_ref, o_ref, bufk, bufv, sem,
                     qs_ref, m_ref, l_ref, acc_ref, *, layer, nb, nc, pc, tq, lam_init):
    c = pl.program_id(1)
    slot = _paged_pipeline(pt_ref, cache_ref, (bufk, bufv), sem, nb=nb, nc=nc, layer=layer, pc=pc,
                           cols=((0, 512), (512, 512)))
    ppc = _tile(pc, 4)

    @pl.when(c == 0)
    def _init():
        _diff_init(q_ref, qs_ref, m_ref, l_ref, acc_ref, tq)

    def pages_t(buf, hp, p0):
        return jnp.concatenate([buf[slot, p0 + u, hp * LANE:(hp + 1) * LANE, :] for u in range(ppc)],
                               axis=1).astype(BF16)

    for p0 in range(0, pc, ppc):
        _diff_step(lambda hp: pages_t(bufk, hp, p0), lambda hp: pages_t(bufv, hp, p0), None,
                   qs_ref, m_ref, l_ref, acc_ref, tq, kv_t=True)

    @pl.when(c == nc - 1)
    def _fin():
        new = new_ref[...]
        causal = (lax.broadcasted_iota(jnp.int32, (tq, LANE), 1) <= lax.broadcasted_iota(jnp.int32, (tq, LANE), 0))
        _diff_step(lambda hp: _new_tile(new, hp * LANE, tq), lambda hp: _new_tile(new, 512 + hp * LANE, tq),
                   jnp.where(causal, 0.0, NEG), qs_ref, m_ref, l_ref, acc_ref, tq)
        _diff_fin(lam_ref, g_ref, o_ref, l_ref, acc_ref, tq, lam_init)


def diff_paged(page_table, cache, layer, lam_par, q, new_rows, gnorm, lam_init):
    b, npg = page_table.shape
    tq = q.shape[1]
    pc = _tile(npg, 16)
    nc = npg // pc
    r = 4 * tq
    npair = DIFF_HEADS // 2
    return pl.pallas_call(
        functools.partial(_diff_paged_body, layer=layer, nb=b, nc=nc, pc=pc, tq=tq, lam_init=lam_init),
        name="diff_paged",
        grid_spec=pltpu.PrefetchScalarGridSpec(
            num_scalar_prefetch=1, grid=(b, nc),
            in_specs=[pl.BlockSpec(memory_space=pl.ANY),
                      pl.BlockSpec((4, DIFF_QK), lambda i, c, pt: (0, 0)),
                      pl.BlockSpec((None, tq, 512), lambda i, c, pt: (i, 0, 0)),
                      pl.BlockSpec((None, tq, 1024), lambda i, c, pt: (i, 0, 0)),
                      pl.BlockSpec((1, LANE), lambda i, c, pt: (0, 0))],
            out_specs=pl.BlockSpec((None, tq, 512), lambda i, c, pt: (i, 0, 0)),
            scratch_shapes=[pltpu.VMEM((2, pc, 512, PAGE), F32), pltpu.VMEM((2, pc, 512, PAGE), F32),
                            pltpu.SemaphoreType.DMA((2,)),
                            pltpu.VMEM((npair, r, LANE), BF16), pltpu.VMEM((npair, r, LANE), F32),
                            pltpu.VMEM((npair, r, LANE), F32), pltpu.VMEM((npair, r, LANE), F32)]),
        out_shape=jax.ShapeDtypeStruct((b, tq, 512), F32),
        compiler_params=_cp(("arbitrary", "arbitrary")),
    )(page_table, cache, lam_par, q, new_rows, gnorm)


def _q_perm():
    idx = np.zeros(512, np.int32)
    for j in range(NSA_J):
        for g in range(NSA_G):
            for d in range(DH):
                idx[j * LANE + g * DH + d] = (g * NSA_J + j) * DH + d
    return idx


def _gate3_perm():
    idx = np.zeros(3 * 512, np.int32)
    for c in range(3):
        for j in range(NSA_J):
            for g in range(NSA_G):
                idx[c * 512 + j * LANE + g * DH:c * 512 + j * LANE + (g + 1) * DH] = (g * NSA_J + j) * 3 + c
    return idx


def _cmp_to_sel(n_cmp, n_sel, nch, nsp):
    r = SEL_BLOCK // CMP_STRIDE
    k = np.arange(n_cmp)[:, None] - r * np.arange(n_sel)[None, :]
    m = sum(((k + n >= 0) & (k + n < r)).astype(np.float32) for n in range(CMP_BLOCK // CMP_STRIDE))
    out = np.zeros((nch, nsp), np.float32)
    out[:n_cmp, :n_sel] = m
    return out


def _block_expand(nsp, lk):
    return (np.arange(lk)[None, :] // SEL_BLOCK == np.arange(nsp)[:, None]).astype(np.float32)


def _cmp_weights(pe, w1, w2):
    eye = jnp.eye(2, dtype=F32)
    w6 = w1.reshape(2, NSA_G, 2, CMP_STRIDE, DH, DH)
    wc = jnp.einsum("kgxsdh,kK,gG->skgdxKGh", w6, eye, eye).reshape(CMP_STRIDE, 256, 512).astype(BF16)
    w2b = jnp.einsum("kghd,kK,gG->kghKGd", w2, eye, eye).reshape(256, 256).astype(BF16)
    pe_rows = jnp.transpose(pe, (0, 2, 1, 3)).reshape(4, CMP_BLOCK * DH)
    xb = jnp.einsum("rc,rR->rRc", pe_rows, jnp.eye(4, dtype=F32)).reshape(4, 4 * CMP_BLOCK * DH)
    xb = jnp.zeros((16, 4 * CMP_BLOCK * DH), F32).at[:4].set(xb)
    wb = jnp.zeros((4 * CMP_BLOCK * DH, LANE), F32).at[:, :DH].set(w1.reshape(4 * CMP_BLOCK * DH, DH))
    bias = mm(xb, wb)[:4, :DH].reshape(1, 256)
    return wc, bias, w2b


def _mixer(x, lw, lam_init, *, b, sq, sample=None):
    t = b * sq
    hn = rmsnorm(x, lw["norm_mix_g"], BF16)
    w_in = lw["w_in"]
    qperm = _q_perm()
    q = mm(hn, w_in[:, OFF_Q:OFF_KV][:, qperm], scale=DH ** -0.5).reshape(b, sq, 512)
    rows4 = mm(hn, w_in[:, OFF_KV:OFF_KV + 512]).reshape(b, sq, 512)
    winkv = mm(hn, w_in[:, OFF_KV + 512:OFF_GATE]).reshape(b, sq, 256)
    g3 = mm(hn, w_in[:, OFF_GATE:OFF_POOL][:, _gate3_perm()], act="sigmoid")
    zpool = mm(hn, w_in[:, OFF_POOL:OFF_GM]).reshape(b, sq, 512)
    zgm = mm(hn, w_in[:, OFF_GM:OFF_DQ]).reshape(b, sq, 1024)
    dq = mm(hn, w_in[:, OFF_DQ:OFF_DKV], scale=DIFF_QK ** -0.5).reshape(b, sq, 512)
    dkv = mm(hn, w_in[:, OFF_DKV:OFF_END]).reshape(b, sq, 1024)

    wc, cbias, w2b = _cmp_weights(lw["cmp_pe"], lw["cmp_w1"], lw["cmp_w2"])
    lam_par = jnp.stack([lw["diff_lq1"], lw["diff_lk1"], lw["diff_lq2"], lw["diff_lk2"]])
    gnorm = jnp.tile(lw["diff_norm_g"], 2).reshape(1, LANE)

    if sample is None:
        past, l_tot = 0, sq
        nch = sq // CMP_STRIDE
        tq = _tile(sq, 256)
        tk_sel = _tile(sq, 256)
        nk_sel = sq // tk_sel
        tw = _tile(sq, 256)
        nband = WINDOW // tw + 1
        win_all, kpos0_w, nk_w = winkv, 0, nband
        win_kt = lambda i, j: (jnp.maximum(i - (nband - 1) + j, 0), i - (nband - 1) + j >= 0)
        tq_w = tw
        free_w = nband - 2 if nband == 3 else None
        tq_d = _tile(sq, 256)
        tk_d = tq_d
        nk_d = sq // tk_d
        xcat = jnp.concatenate([jnp.zeros((b, 16, 512), F32), zpool], axis=1)
        zgm_in = zgm
    else:
        layer = sample["layer"]
        pt = sample["page_table"]
        past = pt.shape[1] * PAGE
        l_tot = past + sq
        nch = past // CMP_STRIDE
        tq = sq
        wbuf = sample["win"].shape[1]
        tw = 128
        wrows = -(-(wbuf + sq) // tw) * tw
        win_all = jnp.concatenate([sample["win"], winkv, jnp.zeros((b, wrows - wbuf - sq, 256), F32)], axis=1)
        kpos0_w, nk_w = past - wbuf, wrows // tw
        win_kt = lambda i, j: (j, j >= 0)
        tq_w = sq
        free_w = None
        xcat = jnp.concatenate([jnp.zeros((b, 1, 512), F32), sample["pool"], zpool], axis=1)
        zgm_in = jnp.concatenate([zgm, jnp.zeros((b, GM_CHUNK - sq, 1024), F32)], axis=1)

    n_cmp = l_tot // CMP_STRIDE - 1
    n_sel = -(-l_tot // SEL_BLOCK)
    nsp = -(-n_sel // LANE) * LANE
    msel = jnp.asarray(_cmp_to_sel(n_cmp, n_sel, nch, nsp), BF16)
    if sample is None:
        kcvc = compress(rows4, wc, cbias, w2b, nch)
        ocmp, selmask = cmp_attend(q, kcvc, msel, tq=tq, n_cmp=n_cmp, n_sel=n_sel, qpos0=0)
        emat = jnp.asarray(_block_expand(nsp, sq), BF16)

        def sel_kt(i, j):
            last = ((i + 1) * tq - 1) // tk_sel
            return jnp.minimum(j, last), j <= last

        osel = nsa_flash(q, rows4, 2, 3, mode="sel", tq=tq, tk=tk_sel, nk=nk_sel, qpos0=0, kpos0=0,
                         kt_fn=sel_kt, sel=selmask, emat=emat)
    else:
        kcvc = compress_paged(pt, sample["cache_nsa"], layer, wc, cbias, w2b)
        ocmp, selmask = cmp_attend(q, kcvc, msel, tq=tq, n_cmp=n_cmp, n_sel=n_sel, qpos0=past)
        emat = jnp.asarray(_block_expand(nsp, past + LANE), BF16)
        osel = sel_paged(pt, sample["cache_nsa"], layer, q, rows4, selmask, emat)
    owin = nsa_flash(q, win_all, 0, 1, mode="win", tq=tq_w, tk=tw, nk=nk_w, qpos0=past, kpos0=kpos0_w,
                     kt_fn=win_kt, free_step=free_w)

    opool = pool_mix(xcat, lw["pool_w"], lw["pool_scale"], past)
    ws_tril = jnp.tril(lw["gm_ws"]).astype(BF16)
    bs_exp = jnp.repeat(lw["gm_bs"].T, LANE, axis=1)
    ogm, gm_v = gmlp_mix(zgm_in, lw["gm_ng"], lw["gm_nb"], ws_tril, bs_exp)

    if sample is None:
        def diff_kt(i, j):
            last = ((i + 1) * tq_d - 1) // tk_d
            return jnp.minimum(j, last), j <= last, j == last

        odiff = diff_attend(lam_par, dq, dkv, gnorm, tq=tq_d, tk=tk_d, nk=nk_d, qpos0=0,
                            lam_init=lam_init, kt_fn=diff_kt)
    else:
        odiff = diff_paged(pt, sample["cache_diff"], layer, lam_par, dq, dkv, gnorm, lam_init)
        ogm = ogm[:, :sq]
        gm_v = gm_v[:, :sq]
    wbr = lw["w_branch"].at[0].set(lw["w_branch"][0][qperm]).astype(BF16)
    out = finish_mixer(x, hn, g3, ocmp.reshape(t, 512), osel.reshape(t, 512), owin.reshape(t, 512),
                       opool.reshape(t, 512), ogm.reshape(t, 512), odiff.reshape(t, 512),
                       lw["w_gate"].astype(BF16), lw["b_gate"].reshape(1, -1), wbr, lw["w_o"].astype(BF16))
    states = dict(rows4=rows4, winkv=winkv, win_all=win_all, xcat=xcat, gm_v=gm_v, dkv=dkv)
    return out, states


def kernel(x_prompt, x_sample, cache_nsa, cache_diff, state_nsa_win, state_pool, page_table, p_prompt, p_sample, norm_mix_g, w_in, nsa_cmp_pe, nsa_cmp_w1, nsa_cmp_w2, pool_w, pool_scale, gm_norm_g, gm_norm_b, gm_ws, gm_bs, diff_lq1, diff_lk1, diff_lq2, diff_lk2, diff_norm_g, w_branch, w_gate, b_gate, w_o, norm_ffn_g, ffn_w_gate, ffn_w_up, ffn_w_down, moe_router, moe_router_b, moe_w_gate, moe_w_up, moe_w_down, ple_norm_g, ple_w_gate, ple_w_proj, final_norm_g):
    bp, sp, d = x_prompt.shape
    bs, ss, _ = x_sample.shape
    depth = w_in.shape[0]
    n_phys = cache_nsa.shape[1]
    wbuf = state_nsa_win.shape[2]
    cache_nsa2 = jnp.transpose(cache_nsa, (0, 1, 3, 4, 5, 2)).reshape(depth, n_phys, 512, PAGE)
    cache_diff2 = jnp.transpose(cache_diff, (0, 1, 3, 4, 5, 2)).reshape(depth, n_phys, 1024, PAGE)
    xp = x_prompt.reshape(bp * sp, d)
    xs = x_sample.reshape(bs * ss, d)
    outs = {k: [] for k in ("nsa_p", "nsa_s", "win_p", "win_s", "pool_p", "pool_s", "gmv_s", "diff_p", "diff_s")}
    for l in range(depth):
        lw = dict(norm_mix_g=norm_mix_g[l], w_in=w_in[l], cmp_pe=nsa_cmp_pe[l], cmp_w1=nsa_cmp_w1[l],
                  cmp_w2=nsa_cmp_w2[l], pool_w=pool_w[l], pool_scale=pool_scale[l], gm_ng=gm_norm_g[l],
                  gm_nb=gm_norm_b[l], gm_ws=gm_ws[l], gm_bs=gm_bs[l], diff_lq1=diff_lq1[l], diff_lk1=diff_lk1[l],
                  diff_lq2=diff_lq2[l], diff_lk2=diff_lk2[l], diff_norm_g=diff_norm_g[l], w_branch=w_branch[l],
                  w_gate=w_gate[l], b_gate=b_gate[l], w_o=w_o[l])
        lam_init = 0.8 - 0.6 * math.exp(-0.3 * l)
        xp, st_p = _mixer(xp, lw, lam_init, b=bp, sq=sp)
        sample = dict(cache_nsa=cache_nsa2, cache_diff=cache_diff2, win=state_nsa_win[l].reshape(bs, wbuf, 256),
                      pool=state_pool[l], page_table=page_table, layer=l)
        xs, st_s = _mixer(xs, lw, lam_init, b=bs, sq=ss, sample=sample)
        outs["nsa_p"].append(st_p["rows4"].reshape(bp, sp, 4, NSA_G, DH))
        outs["nsa_s"].append(st_s["rows4"].reshape(bs, ss, 4, NSA_G, DH))
        wkeep = min(WINDOW, sp)
        outs["win_p"].append(st_p["winkv"][:, sp - wkeep:].reshape(bp, wkeep, 2, NSA_G, DH))
        outs["win_s"].append(st_s["win_all"][:, ss:ss + wbuf].reshape(bs, wbuf, 2, NSA_G, DH))
        outs["pool_p"].append(st_p["xcat"][:, -POOL_MEM:])
        outs["pool_s"].append(st_s["xcat"][:, -POOL_MEM:])
        outs["gmv_s"].append(st_s["gm_v"])
        outs["diff_p"].append(st_p["dkv"].reshape(bp, sp, 2, DIFF_HEADS, 2 * DIFF_QK))
        outs["diff_s"].append(st_s["dkv"].reshape(bs, ss, 2, DIFF_HEADS, 2 * DIFF_QK))
        i = l // 2
        if l % 2 == 0:
            xp = ffn_swiglu(xp, norm_ffn_g[l], ffn_w_gate[i], ffn_w_up[i], ffn_w_down[i])
            xs = ffn_swiglu(xs, norm_ffn_g[l], ffn_w_gate[i], ffn_w_up[i], ffn_w_down[i])
        else:
            xp = moe_swiglu(xp, norm_ffn_g[l], moe_router[i], moe_router_b[i], moe_w_gate[i], moe_w_up[i], moe_w_down[i])
            xs = moe_swiglu(xs, norm_ffn_g[l], moe_router[i], moe_router_b[i], moe_w_gate[i], moe_w_up[i], moe_w_down[i])
        final = l == depth - 1
        xp = ple(xp, p_prompt[l].reshape(bp * sp, -1), ple_norm_g[l], ple_w_gate[l], ple_w_proj[l], final_norm_g, final)
        xs = ple(xs, p_sample[l].reshape(bs * ss, -1), ple_norm_g[l], ple_w_gate[l], ple_w_proj[l], final_norm_g, final)
    st = lambda k: jnp.stack(outs[k])
    return (xp.reshape(bp, sp, d), xs.reshape(bs, ss, d), st("nsa_p"), st("nsa_s"), st("win_p"), st("win_s"),
            st("pool_p"), st("pool_s"), st("gmv_s"), st("diff_p"), st("diff_s"))
```

```python
import functools
import math

import numpy as np
import jax
import jax.numpy as jnp
from jax import lax
from jax.experimental import pallas as pl
from jax.experimental.pallas import tpu as pltpu

F32 = jnp.float32
BF16 = jnp.bfloat16

D_MODEL = 1024
PAGE = 128
NSA_HEADS = 8
NSA_G = 2
NSA_J = 4
DH = 64
CMP_BLOCK = 32
CMP_STRIDE = 16
SEL_BLOCK = 64
SEL_TOPK = 16
FORCE_BONUS = 100.0
WINDOW = 512
POOL_WINDOWS = (2, 4, 8, 16)
POOL_MEM = 15
GM_W = 512
GM_CHUNK = 128
DIFF_HEADS = 8
DIFF_QK = 32
N_BRANCH = 4
BRANCH_W = 512
N_EXPERTS = 8
EPS = 1e-6
NEG = -1e30
LANE = 128
VMEM_LIMIT = 56 * 1024 * 1024

OFF_Q, OFF_KV, OFF_GATE, OFF_POOL, OFF_GM, OFF_DQ, OFF_DKV, OFF_END = 0, 512, 1280, 1304, 1816, 2840, 3352, 4376


def _tile(n, pref):
    t = min(n, pref)
    while n % t:
        t //= 2
    return t


def _cp(sem, vmem=VMEM_LIMIT):
    return pltpu.CompilerParams(dimension_semantics=sem, vmem_limit_bytes=vmem)


def _gelu(x):
    return 0.5 * x * (1.0 + jnp.tanh(0.7978845608028654 * (x + 0.044715 * (x * x * x))))


def _rms(x, g):
    return x * lax.rsqrt(jnp.mean(x * x, axis=-1, keepdims=True) + EPS) * g


def _dot(a, b):
    return jnp.dot(a, b, preferred_element_type=F32)


def _dot_nt(a, b):
    return lax.dot_general(a, b, (((1,), (1,)), ((), ())), preferred_element_type=F32)


def _rmsnorm_body(x_ref, g_ref, o_ref):
    o_ref[...] = _rms(x_ref[...], g_ref[...]).astype(o_ref.dtype)


def rmsnorm(x, g, out_dtype):
    t, d = x.shape
    tm = _tile(t, 512)
    return pl.pallas_call(
        _rmsnorm_body, name="rmsnorm", grid=(t // tm,),
        in_specs=[pl.BlockSpec((tm, d), lambda i: (i, 0)), pl.BlockSpec((1, d), lambda i: (0, 0))],
        out_specs=pl.BlockSpec((tm, d), lambda i: (i, 0)),
        out_shape=jax.ShapeDtypeStruct((t, d), out_dtype),
        compiler_params=_cp(("parallel",)),
    )(x, g.reshape(1, d))


def _mm_body(x_ref, w_ref, o_ref, *, scale, act):
    y = _dot(x_ref[...].astype(BF16), w_ref[...])
    if scale is not None:
        y = y * scale
    if act == "sigmoid":
        y = jax.nn.sigmoid(y)
    o_ref[...] = y.astype(o_ref.dtype)


def mm(x, w, *, scale=None, act=None, out_dtype=F32):
    t, k = x.shape
    n = w.shape[1]
    tm = _tile(t, 512)
    tn = _tile(n, 512)
    return pl.pallas_call(
        functools.partial(_mm_body, scale=scale, act=act), name="mm", grid=(t // tm, n // tn),
        in_specs=[pl.BlockSpec((tm, k), lambda i, j: (i, 0)), pl.BlockSpec((k, tn), lambda i, j: (0, j))],
        out_specs=pl.BlockSpec((tm, tn), lambda i, j: (i, j)),
        out_shape=jax.ShapeDtypeStruct((t, n), out_dtype),
        compiler_params=_cp(("parallel", "parallel")),
    )(x, w.astype(BF16))


def _in_proj_body(x_ref, g_ref, w_ref, hn_ref, *o_refs, segs):
    hn = _rms(x_ref[...], g_ref[...]).astype(BF16)
    hn_ref[...] = hn
    c0 = 0
    for (n, scale, act), o_ref in zip(segs, o_refs):
        y = _dot(hn, w_ref[:, c0:c0 + n])
        if scale is not None:
            y = y * scale
        if act == "sigmoid":
            y = jax.nn.sigmoid(y)
        o_ref[...] = y
        c0 += n


def in_proj(x, g, w_all, segs):
    t, d = x.shape
    tm = _tile(t, 256)
    ntot = w_all.shape[1]
    return pl.pallas_call(
        functools.partial(_in_proj_body, segs=segs), name="in_proj", grid=(t // tm,),
        in_specs=[pl.BlockSpec((tm, d), lambda i: (i, 0)), pl.BlockSpec((1, d), lambda i: (0, 0)),
                  pl.BlockSpec((d, ntot), lambda i: (0, 0), pipeline_mode=pl.Buffered(1))],
        out_specs=[pl.BlockSpec((tm, d), lambda i: (i, 0))] + [pl.BlockSpec((tm, n), lambda i: (i, 0)) for n, _, _ in segs],
        out_shape=[jax.ShapeDtypeStruct((t, d), BF16)] + [jax.ShapeDtypeStruct((t, n), F32) for n, _, _ in segs],
        compiler_params=_cp(("parallel",)),
    )(x, g.reshape(1, d), w_all)


def _compress_body(xk_ref, xv_ref, wc_ref, bias_ref, w2_ref, o_ref, *, nch):
    _compress_core(lambda s: xk_ref[pl.ds(s, nch, stride=CMP_STRIDE), :],
                   lambda s: xv_ref[pl.ds(s, nch, stride=CMP_STRIDE), :], wc_ref, bias_ref, w2_ref, o_ref, nch)


def _compress_core(load_k, load_v, wc_ref, bias_ref, w2_ref, o_ref, nch):
    acc = jnp.zeros((nch, 512), F32)
    for s in range(CMP_STRIDE):
        xs = jnp.concatenate([load_k(s), load_v(s)], axis=1).astype(BF16)
        acc = acc + _dot(xs, wc_ref[s])
    hi_next = pltpu.roll(acc[:, 256:], nch - 1, 0)
    hid = _gelu(acc[:, :256] + hi_next + bias_ref[...])
    o_ref[...] = _dot(hid.astype(BF16), w2_ref[...]).astype(o_ref.dtype)


def compress(rows, wc, bias, w2, nch):
    b = rows.shape[0]
    return pl.pallas_call(
        functools.partial(_compress_body, nch=nch), name="compress", grid=(b,),
        in_specs=[pl.BlockSpec((None, nch * CMP_STRIDE, LANE), lambda i: (i, 0, 0)),
                  pl.BlockSpec((None, nch * CMP_STRIDE, LANE), lambda i: (i, 0, 1)),
                  pl.BlockSpec((CMP_STRIDE, 256, 512), lambda i: (0, 0, 0)),
                  pl.BlockSpec((1, 256), lambda i: (0, 0)),
                  pl.BlockSpec((256, 256), lambda i: (0, 0))],
        out_specs=pl.BlockSpec((None, nch, 256), lambda i: (i, 0, 0)),
        out_shape=jax.ShapeDtypeStruct((b, nch, 256), BF16),
        compiler_params=_cp(("parallel",)),
    )(rows, rows, wc, bias, w2)


def _stack_heads(q, tq):
    lane = lax.broadcasted_iota(jnp.int32, (tq, LANE), 1)
    parts = []
    for g in range(NSA_G):
        keep = (lane < DH) if g == 0 else (lane >= DH)
        for j in range(NSA_J):
            parts.append(jnp.where(keep, q[:, j * LANE:(j + 1) * LANE], 0.0))
    return jnp.concatenate(parts, axis=0)


def _unstack_heads(o, tq):
    lane = lax.broadcasted_iota(jnp.int32, (tq, LANE), 1)
    outs = []
    for j in range(NSA_J):
        outs.append(jnp.where(lane < DH, o[j * tq:(j + 1) * tq], o[(NSA_J + j) * tq:(NSA_J + j + 1) * tq]))
    return jnp.concatenate(outs, axis=1)


def _cmp_body(q_ref, kv_ref, msel_ref, o_ref, sel_ref, *, tq, nch, n_cmp, n_sel, nsp, qpos0):
    i = pl.program_id(1)
    r = 8 * tq
    qs = _stack_heads(q_ref[...], tq).astype(BF16)
    kc = kv_ref[:, 0:LANE]
    vc = kv_ref[:, LANE:2 * LANE]
    s = _dot_nt(qs, kc)
    row = lax.broadcasted_iota(jnp.int32, (r, nch), 0)
    col = lax.broadcasted_iota(jnp.int32, (r, nch), 1)
    qpos = qpos0 + i * tq + (row & (tq - 1))
    cmask = (col * CMP_STRIDE + (CMP_BLOCK - 1) <= qpos) & (col < n_cmp)
    s = jnp.where(cmask, s, NEG)
    m = jnp.max(s, axis=1, keepdims=True)
    e = jnp.where(cmask, jnp.exp(s - m), 0.0)
    l = jnp.sum(e, axis=1, keepdims=True)
    p = (e * (1.0 / jnp.where(l > 0.0, l, 1.0))).astype(BF16)
    o_ref[...] = _unstack_heads(_dot(p, vc), tq)
    imp_all = _dot(p, msel_ref[...])
    blocks_on_rows = tq % LANE == 0
    nrow = -(-n_sel // 8) * 8
    shape = (nrow, tq) if blocks_on_rows else (tq, nsp)
    baxis, qaxis = (0, 1) if blocks_on_rows else (1, 0)
    blk = lax.broadcasted_iota(jnp.int32, shape, baxis)
    qp = qpos0 + i * tq + lax.broadcasted_iota(jnp.int32, shape, qaxis)
    cur = lax.shift_right_logical(qp, SEL_BLOCK.bit_length() - 1)
    valid = (blk <= cur) & (blk < n_sel)
    forced = (blk == 0) | (blk == cur) | (blk == cur - 1)
    for g in range(NSA_G):
        imp = imp_all[(g * NSA_J) * tq:(g * NSA_J + 1) * tq]
        for j in range(1, NSA_J):
            imp = imp + imp_all[(g * NSA_J + j) * tq:(g * NSA_J + j + 1) * tq]
        if blocks_on_rows:
            imp = imp.T[:nrow]
        score = jnp.where(valid, imp + FORCE_BONUS * forced.astype(F32), -1.0)
        score = jnp.where(blk < n_sel, score, -2.0)
        rank = jnp.zeros(shape, F32)
        for mth in range(n_sel):
            cm = score[mth:mth + 1, :] if blocks_on_rows else score[:, mth:mth + 1]
            beats = (cm > score) | ((cm == score) & (blk > mth))
            rank = rank + beats.astype(F32)
        chosen = ((rank < float(min(SEL_TOPK, n_sel))) & valid).astype(F32)
        if blocks_on_rows:
            chosen = jnp.concatenate([chosen, jnp.zeros((nsp - nrow, tq), F32)], axis=0).T
        sel_ref[:, g * nsp:(g + 1) * nsp] = chosen


def cmp_attend(q, kcvc, msel, *, tq, n_cmp, n_sel, qpos0):
    b, sq, _ = q.shape
    nch = kcvc.shape[1]
    nsp = msel.shape[1]
    return pl.pallas_call(
        functools.partial(_cmp_body, tq=tq, nch=nch, n_cmp=n_cmp, n_sel=n_sel, nsp=nsp, qpos0=qpos0),
        name="cmp_attend",
        grid=(b, sq // tq),
        in_specs=[pl.BlockSpec((None, tq, 512), lambda bi, i: (bi, i, 0)),
                  pl.BlockSpec((None, nch, 256), lambda bi, i: (bi, 0, 0)),
                  pl.BlockSpec((nch, nsp), lambda bi, i: (0, 0))],
        out_specs=[pl.BlockSpec((None, tq, 512), lambda bi, i: (bi, i, 0)),
                   pl.BlockSpec((None, tq, 2 * nsp), lambda bi, i: (bi, i, 0))],
        out_shape=[jax.ShapeDtypeStruct((b, sq, 512), F32), jax.ShapeDtypeStruct((b, sq, 2 * nsp), F32)],
        compiler_params=_cp(("parallel", "parallel")),
    )(q, kcvc, msel)


ROW_BLOCK = 128
KEY_SUB = 256


def _flash_update(qs, k, v, bias, m, l, acc, kv_t=False):
    s = _dot(qs, k) if kv_t else _dot_nt(qs, k)
    if bias is not None:
        s = s + bias
    tiles = [s[:, c:c + LANE] for c in range(0, s.shape[1], LANE)]
    m_new = jnp.maximum(m, jnp.max(functools.reduce(jnp.maximum, tiles), axis=1, keepdims=True))
    alpha = jnp.exp(m - m_new)
    ps = [jnp.exp(t - m_new) for t in tiles]
    l_new = alpha * l + jnp.sum(functools.reduce(jnp.add, ps), axis=1, keepdims=True)
    pb = jnp.concatenate(ps, axis=1).astype(BF16)
    acc_new = alpha * acc + (_dot_nt(pb, v) if kv_t else _dot(pb, v))
    return m_new, l_new, acc_new


def _flash_rows(qs_ref, k, v, bias_of, m_ref, l_ref, acc_ref, nrows, tq, kv_t=False):
    if tq < ROW_BLOCK:
        parts = [bias_of(h, 0, tq) for h in range(nrows // tq)]
        bias = None if parts[0] is None else jnp.concatenate(parts, axis=0)
        blocks = [(pl.ds(0, nrows), bias)]
    else:
        blocks = [(pl.ds(r0, ROW_BLOCK), bias_of(r0 // tq, r0 % tq, ROW_BLOCK)) for r0 in range(0, nrows, ROW_BLOCK)]
    for sl, bias in blocks:
        m, l, acc = _flash_update(qs_ref[sl], k, v, bias, m_ref[sl], l_ref[sl], acc_ref[sl], kv_t)
        m_ref[sl] = m
        l_ref[sl] = l
        acc_ref[sl] = acc


def _sel_ok(sel, e, nsp):
    return [_dot(sel[:, g * nsp:(g + 1) * nsp], e) > 0.5 for g in range(NSA_G)]


def _nsa_flash_body(*refs, mode, tq, tk, nk, qpos0, kpos0, kt_fn, nsp, free_step):
    if mode == "sel":
        q_ref, k_ref, v_ref, sel_ref, e_ref, o_ref, qs_ref, m_ref, l_ref, acc_ref = refs
    else:
        q_ref, k_ref, v_ref, o_ref, qs_ref, m_ref, l_ref, acc_ref = refs
    i = pl.program_id(1)
    j = pl.program_id(2)
    r = 8 * tq

    @pl.when(j == 0)
    def _init():
        qs_ref[...] = _stack_heads(q_ref[...], tq).astype(BF16)
        m_ref[...] = jnp.full((r, LANE), NEG, F32)
        l_ref[...] = jnp.zeros((r, LANE), F32)
        acc_ref[...] = jnp.zeros((r, LANE), F32)

    kt, valid = kt_fn(i, j)

    ks = min(tk, KEY_SUB)

    def step(c0, masked):
        k = k_ref[c0:c0 + ks, :].astype(BF16)
        v = v_ref[c0:c0 + ks, :].astype(BF16)
        if not masked:
            _flash_rows(qs_ref, k, v, lambda h, t0, n: None, m_ref, l_ref, acc_ref, r, tq)
            return
        qpos = qpos0 + i * tq + lax.broadcasted_iota(jnp.int32, (tq, ks), 0)
        kpos = kpos0 + kt * tk + c0 + lax.broadcasted_iota(jnp.int32, (tq, ks), 1)
        ok = kpos <= qpos
        if mode == "win":
            ok = ok & (kpos > qpos - WINDOW)
            bias = [jnp.where(ok, 0.0, NEG)] * NSA_G
        else:
            oks = _sel_ok(sel_ref[...].astype(BF16), e_ref[:, c0:c0 + ks], nsp)
            bias = [jnp.where(ok & okg, 0.0, NEG) for okg in oks]
        _flash_rows(qs_ref, k, v, lambda h, t0, n: bias[h // NSA_J][t0:t0 + n], m_ref, l_ref, acc_ref, r, tq)

    for c0 in range(0, tk, ks):
        live = valid & (kpos0 + kt * tk + c0 <= qpos0 + (i + 1) * tq - 1)
        if free_step is None:
            pl.when(live)(functools.partial(step, c0, True))
        else:
            pl.when(live & (j != free_step))(functools.partial(step, c0, True))
            pl.when(live & (j == free_step))(functools.partial(step, c0, False))

    @pl.when(j == nk - 1)
    def _fin():
        l = l_ref[...]
        o = acc_ref[...] * (1.0 / jnp.where(l > 0.0, l, 1.0))
        o_ref[...] = _unstack_heads(o, tq)


def nsa_flash(q, kv, kcol, vcol, *, mode, tq, tk, nk, qpos0, kpos0, kt_fn, sel=None, emat=None, free_step=None):
    b, sq, _ = q.shape
    nsp = 0 if sel is None else sel.shape[2] // 2

    def kmap(col):
        return lambda bi, i, j: (bi, kt_fn(i, j)[0], col)

    in_specs = [pl.BlockSpec((None, tq, 512), lambda bi, i, j: (bi, i, 0)),
                pl.BlockSpec((None, tk, LANE), kmap(kcol)),
                pl.BlockSpec((None, tk, LANE), kmap(vcol))]
    args = [q, kv, kv]
    if mode == "sel":
        in_specs += [pl.BlockSpec((None, tq, 2 * nsp), lambda bi, i, j: (bi, i, 0)),
                     pl.BlockSpec((nsp, tk), lambda bi, i, j: (0, kt_fn(i, j)[0]))]
        args += [sel, emat]
    r = 8 * tq
    return pl.pallas_call(
        functools.partial(_nsa_flash_body, mode=mode, tq=tq, tk=tk, nk=nk, qpos0=qpos0, kpos0=kpos0,
                          kt_fn=kt_fn, nsp=nsp, free_step=free_step),
        name="nsa_" + mode, grid=(b, sq // tq, nk),
        in_specs=in_specs,
        out_specs=pl.BlockSpec((None, tq, 512), lambda bi, i, j: (bi, i, 0)),
        out_shape=jax.ShapeDtypeStruct((b, sq, 512), F32),
        scratch_shapes=[pltpu.VMEM((r, LANE), BF16), pltpu.VMEM((r, LANE), F32), pltpu.VMEM((r, LANE), F32),
                        pltpu.VMEM((r, LANE), F32)],
        compiler_params=_cp(("parallel", "parallel", "arbitrary")),
    )(*args)


def _diff_init(q_ref, qs_ref, m_ref, l_ref, acc_ref, tq):
    r = 4 * tq
    npair = DIFF_HEADS // 2
    lane = lax.broadcasted_iota(jnp.int32, (tq, LANE), 1)
    for hp in range(npair):
        q = q_ref[:, hp * LANE:(hp + 1) * LANE]
        parts = []
        for h in range(2):
            for mth in range(2):
                lo = h * 2 * DIFF_QK + mth * DIFF_QK
                parts.append(jnp.where((lane >= lo) & (lane < lo + DIFF_QK), q, 0.0))
        qs_ref[hp] = jnp.concatenate(parts, axis=0).astype(BF16)
    m_ref[...] = jnp.full((npair, r, LANE), NEG, F32)
    l_ref[...] = jnp.zeros((npair, r, LANE), F32)
    acc_ref[...] = jnp.zeros((npair, r, LANE), F32)


def _diff_step(k_of, v_of, bias, qs_ref, m_ref, l_ref, acc_ref, tq, kv_t=False):
    bias_of = lambda h, t0, n: None if bias is None else bias[t0:t0 + n]
    for hp in range(DIFF_HEADS // 2):
        _flash_rows(qs_ref.at[hp], k_of(hp), v_of(hp), bias_of, m_ref.at[hp], l_ref.at[hp], acc_ref.at[hp],
                    4 * tq, tq, kv_t)


def _diff_fin(lam_ref, g_ref, o_ref, l_ref, acc_ref, tq, lam_init):
    lp = lam_ref[...]
    lam = (jnp.exp(jnp.sum(lp[0:1] * lp[1:2], axis=1, keepdims=True))
           - jnp.exp(jnp.sum(lp[2:3] * lp[3:4], axis=1, keepdims=True)) + lam_init)
    lane = lax.broadcasted_iota(jnp.int32, (tq, LANE), 1)
    lo_half = lane < 2 * DIFF_QK
    for hp in range(DIFF_HEADS // 2):
        l = l_ref[hp]
        a = acc_ref[hp] * (1.0 / jnp.where(l > 0.0, l, 1.0))
        o0 = a[0:tq] - lam * a[tq:2 * tq]
        o1 = a[2 * tq:3 * tq] - lam * a[3 * tq:4 * tq]
        o = jnp.where(lo_half, o0, o1)
        sq = o * o
        ms0 = jnp.sum(jnp.where(lo_half, sq, 0.0), axis=1, keepdims=True)
        ms1 = jnp.sum(jnp.where(lo_half, 0.0, sq), axis=1, keepdims=True)
        ms = jnp.where(lo_half, ms0, ms1) * (1.0 / (2 * DIFF_QK))
        y = o * lax.rsqrt(ms + EPS) * g_ref[...]
        o_ref[:, hp * LANE:(hp + 1) * LANE] = y * (1.0 - lam_init)


def _diff_body(lam_ref, q_ref, k_ref, v_ref, g_ref, o_ref, qs_ref, m_ref, l_ref, acc_ref, *,
               tq, tk, nk, qpos0, lam_init, kt_fn):
    i = pl.program_id(1)
    j = pl.program_id(2)
    r = 4 * tq

    @pl.when(j == 0)
    def _init():
        _diff_init(q_ref, qs_ref, m_ref, l_ref, acc_ref, tq)

    kt, valid = kt_fn(i, j)
    ks = min(tk, KEY_SUB)

    def step(c0, masked):
        k_of = lambda hp: k_ref[c0:c0 + ks, hp * LANE:(hp + 1) * LANE].astype(BF16)
        v_of = lambda hp: v_ref[c0:c0 + ks, hp * LANE:(hp + 1) * LANE].astype(BF16)
        bias = None
        if masked:
            qpos = qpos0 + i * tq + lax.broadcasted_iota(jnp.int32, (tq, ks), 0)
            kpos = kt * tk + c0 + lax.broadcasted_iota(jnp.int32, (tq, ks), 1)
            bias = jnp.where(kpos <= qpos, 0.0, NEG)
        _diff_step(k_of, v_of, bias, qs_ref, m_ref, l_ref, acc_ref, tq)

    for c0 in range(0, tk, ks):
        k_first = kt * tk + c0
        live = valid & (k_first <= qpos0 + (i + 1) * tq - 1)
        below = k_first + ks - 1 <= qpos0 + i * tq
        pl.when(live & below)(functools.partial(step, c0, False))
        pl.when(live & jnp.logical_not(below))(functools.partial(step, c0, True))

    @pl.when(j == nk - 1)
    def _fin():
        _diff_fin(lam_ref, g_ref, o_ref, l_ref, acc_ref, tq, lam_init)


def diff_attend(lam_par, q, kv, gnorm, *, tq, tk, nk, qpos0, lam_init, kt_fn):
    b, sq, _ = q.shape
    r = 4 * tq
    npair = DIFF_HEADS // 2
    return pl.pallas_call(
        functools.partial(_diff_body, tq=tq, tk=tk, nk=nk, qpos0=qpos0, lam_init=lam_init, kt_fn=kt_fn),
        name="diff_attend",
        grid=(b, sq // tq, nk),
        in_specs=[pl.BlockSpec((4, DIFF_QK), lambda bi, i, j: (0, 0)),
                  pl.BlockSpec((None, tq, 512), lambda bi, i, j: (bi, i, 0)),
                  pl.BlockSpec((None, tk, 512), lambda bi, i, j: (bi, kt_fn(i, j)[0], 0)),
                  pl.BlockSpec((None, tk, 512), lambda bi, i, j: (bi, kt_fn(i, j)[0], 1)),
                  pl.BlockSpec((1, LANE), lambda bi, i, j: (0, 0))],
        out_specs=pl.BlockSpec((None, tq, 512), lambda bi, i, j: (bi, i, 0)),
        out_shape=jax.ShapeDtypeStruct((b, sq, 512), F32),
        scratch_shapes=[pltpu.VMEM((npair, r, LANE), BF16), pltpu.VMEM((npair, r, LANE), F32),
                        pltpu.VMEM((npair, r, LANE), F32), pltpu.VMEM((npair, r, LANE), F32)],
        compiler_params=_cp(("parallel", "parallel", "arbitrary")),
    )(lam_par, q, kv, kv, gnorm)


def _pool_body(x_ref, pw_ref, sc_ref, o_ref, *, sq, start):
    pos = start + lax.broadcasted_iota(jnp.int32, (sq, LANE), 0)
    for g, w in enumerate(POOL_WINDOWS):
        c0 = g * LANE
        x = x_ref[pl.ds(16, sq), c0:c0 + LANE]
        tot = x
        for back in range(1, w):
            tot = tot + x_ref[pl.ds(16 - back, sq), c0:c0 + LANE]
        cnt = jnp.minimum(w, pos + 1).astype(F32)
        mix = tot / cnt - x
        y = _dot(mix.astype(BF16), pw_ref[g])
        o_ref[:, c0:c0 + LANE] = y * sc_ref[:, c0:c0 + LANE]


def pool_mix(xcat, pw, scale, start):
    b, rows, _ = xcat.shape
    sq = rows - 16
    return pl.pallas_call(
        functools.partial(_pool_body, sq=sq, start=start), name="pool_mix", grid=(b,),
        in_specs=[pl.BlockSpec((None, rows, 512), lambda i: (i, 0, 0)),
                  pl.BlockSpec((4, LANE, LANE), lambda i: (0, 0, 0)),
                  pl.BlockSpec((1, 512), lambda i: (0, 0))],
        out_specs=pl.BlockSpec((None, sq, 512), lambda i: (i, 0, 0)),
        out_shape=jax.ShapeDtypeStruct((b, sq, 512), F32),
        compiler_params=_cp(("parallel",)),
    )(xcat, pw.astype(BF16), scale.reshape(1, 512))


def _gmlp_body(z_ref, ng_ref, nb_ref, ws_ref, bs_ref, o_ref, v_ref, *, tg):
    z = _gelu(z_ref[...])
    u = z[:, :GM_W]
    vr = z[:, GM_W:]
    xc = vr - jnp.mean(vr, axis=-1, keepdims=True)
    v = xc * lax.rsqrt(jnp.mean(xc * xc, axis=-1, keepdims=True) + EPS) * ng_ref[...] + nb_ref[...]
    v_ref[...] = v
    vb = v.astype(BF16)
    for c in range(tg // GM_CHUNK):
        r0 = c * GM_CHUNK
        for g in range(4):
            c0 = g * LANE
            mixed = _dot(ws_ref[g], vb[r0:r0 + GM_CHUNK, c0:c0 + LANE]) + bs_ref[:, c0:c0 + LANE]
            o_ref[r0:r0 + GM_CHUNK, c0:c0 + LANE] = u[r0:r0 + GM_CHUNK, c0:c0 + LANE] * mixed


def gmlp_mix(z, ng, nb, ws_tril, bs_exp):
    b, s, _ = z.shape
    tg = _tile(s, 512)
    return pl.pallas_call(
        functools.partial(_gmlp_body, tg=tg), name="gmlp_mix", grid=(b, s // tg),
        in_specs=[pl.BlockSpec((None, tg, 1024), lambda bi, i: (bi, i, 0)),
                  pl.BlockSpec((1, 512), lambda bi, i: (0, 0)),
                  pl.BlockSpec((1, 512), lambda bi, i: (0, 0)),
                  pl.BlockSpec((4, GM_CHUNK, GM_CHUNK), lambda bi, i: (0, 0, 0)),
                  pl.BlockSpec((GM_CHUNK, 512), lambda bi, i: (0, 0))],
        out_specs=[pl.BlockSpec((None, tg, 512), lambda bi, i: (bi, i, 0)),
                   pl.BlockSpec((None, tg, 512), lambda bi, i: (bi, i, 0))],
        out_shape=[jax.ShapeDtypeStruct((b, s, 512), F32), jax.ShapeDtypeStruct((b, s, 512), F32)],
        compiler_params=_cp(("parallel", "parallel")),
    )(z, ng.reshape(1, 512), nb.reshape(1, 512), ws_tril, bs_exp)


def _finish_body(x_ref, hn_ref, g3_ref, ocmp_ref, osel_ref, owin_ref, opool_ref, ogm_ref, odiff_ref,
                 wgate_ref, bgate_ref, wbr_ref, wo_ref, o_ref):
    hn = hn_ref[...]
    g3 = g3_ref[...]
    onsa = g3[:, 0:512] * ocmp_ref[...] + g3[:, 512:1024] * osel_ref[...] + g3[:, 1024:1536] * owin_ref[...]
    branches = (onsa, opool_ref[...], ogm_ref[...], odiff_ref[...])
    acc = jnp.zeros(o_ref.shape, F32)
    for n in range(N_BRANCH):
        c0 = n * D_MODEL
        gate = jax.nn.sigmoid(_dot(hn, wgate_ref[:, c0:c0 + D_MODEL]) + bgate_ref[:, c0:c0 + D_MODEL])
        acc = acc + gate * _dot(branches[n].astype(BF16), wbr_ref[n])
    o_ref[...] = x_ref[...] + _dot(acc.astype(BF16), wo_ref[...])


def finish_mixer(x, hn, g3, ocmp, osel, owin, opool, ogm, odiff, wgate, bgate, wbr, wo):
    t = x.shape[0]
    tm = _tile(t, 256)
    row = lambda w: pl.BlockSpec((tm, w), lambda i: (i, 0))
    const = lambda shape: pl.BlockSpec(shape, lambda i: (0,) * len(shape), pipeline_mode=pl.Buffered(1))
    return pl.pallas_call(
        _finish_body, name="finish_mixer", grid=(t // tm,),
        in_specs=[row(1024), row(1024), row(1536), row(512), row(512), row(512), row(512), row(512), row(512),
                  const((D_MODEL, N_BRANCH * D_MODEL)), const((1, N_BRANCH * D_MODEL)),
                  const((N_BRANCH, BRANCH_W, D_MODEL)), const((D_MODEL, D_MODEL))],
        out_specs=row(1024),
        out_shape=jax.ShapeDtypeStruct((t, D_MODEL), F32),
        compiler_params=_cp(("parallel",)),
    )(x, hn, g3, ocmp, osel, owin, opool, ogm, odiff, wgate, bgate, wbr, wo)


def _ffn_body(x_ref, g_ref, wg_ref, wu_ref, wd_ref, o_ref, hn_ref, acc_ref, *, nf):
    j = pl.program_id(1)

    @pl.when(j == 0)
    def _init():
        hn_ref[...] = _rms(x_ref[...], g_ref[...]).astype(BF16)
        acc_ref[...] = jnp.zeros(acc_ref.shape, F32)

    h = hn_ref[...]
    a = _dot(h, wg_ref[...])
    act = a * jax.nn.sigmoid(a) * _dot(h, wu_ref[...])
    acc_ref[...] += _dot(act.astype(BF16), wd_ref[...])

    @pl.when(j == nf - 1)
    def _fin():
        o_ref[...] = x_ref[...] + acc_ref[...]


def ffn_swiglu(x, g, wg, wu, wd):
    t, d = x.shape
    f = wg.shape[1]
    tm = _tile(t, 512)
    tf = 256
    nf = f // tf
    return pl.pallas_call(
        functools.partial(_ffn_body, nf=nf), name="ffn_swiglu", grid=(t // tm, nf),
        in_specs=[pl.BlockSpec((tm, d), lambda i, j: (i, 0)), pl.BlockSpec((1, d), lambda i, j: (0, 0)),
                  pl.BlockSpec((d, tf), lambda i, j: (0, j)), pl.BlockSpec((d, tf), lambda i, j: (0, j)),
                  pl.BlockSpec((tf, d), lambda i, j: (j, 0))],
        out_specs=pl.BlockSpec((tm, d), lambda i, j: (i, 0)),
        out_shape=jax.ShapeDtypeStruct((t, d), F32),
        scratch_shapes=[pltpu.VMEM((tm, d), BF16), pltpu.VMEM((tm, d), F32)],
        compiler_params=_cp(("parallel", "arbitrary")),
    )(x, g.reshape(1, d), wg.astype(BF16), wu.astype(BF16), wd.astype(BF16))


MOE_SUB = 256


def _moe_body(x_ref, g_ref, r_ref, rb_ref, u_ref, wg_ref, wu_ref, wd_ref, o_ref,
              hn_ref, comb_ref, post_ref, posr_ref, cnt_ref, xe_ref, ye_ref, acc_ref, *, nf):
    e = pl.program_id(1)
    f = pl.program_id(2)
    tm = x_ref.shape[0]
    lane = lax.broadcasted_iota(jnp.int32, (tm, LANE), 1)

    @pl.when((e == 0) & (f == 0))
    def _init():
        hn = _rms(x_ref[...], g_ref[...]).astype(BF16)
        hn_ref[...] = hn
        lg = _dot(hn, r_ref[...]) + rb_ref[...]
        m1 = jnp.max(lg, axis=1, keepdims=True)
        i1 = jnp.min(jnp.where(lg == m1, lane, LANE), axis=1, keepdims=True)
        lg2 = jnp.where(lane == i1, -3e38, lg)
        m2 = jnp.max(lg2, axis=1, keepdims=True)
        i2 = jnp.min(jnp.where(lg2 == m2, lane, LANE), axis=1, keepdims=True)
        e2 = jnp.exp(m2 - m1)
        w1 = 1.0 / (1.0 + e2)
        comb = jnp.where(lane == i1, w1, 0.0) + jnp.where(lane == i2, e2 * w1, 0.0)
        comb_ref[...] = comb
        ind_t = (comb.T > 0.0).astype(F32)
        pos_t = _dot(ind_t.astype(BF16), u_ref[...])
        post_ref[...] = jnp.where(ind_t > 0.0, pos_t, -1.0)
        posr_ref[...] = post_ref[...].T
        cnt = jnp.sum(ind_t, axis=1, keepdims=True)
        for ex in range(N_EXPERTS):
            cnt_ref[ex] = cnt[ex, 0].astype(jnp.int32)
        acc_ref[...] = jnp.zeros(acc_ref.shape, F32)

    nsub = (cnt_ref[e] + (MOE_SUB - 1)) // MOE_SUB

    @pl.when(f == 0)
    def _gather():
        slot_t = post_ref[pl.ds(e, 1), :]
        base = lax.broadcasted_iota(jnp.int32, (MOE_SUB, tm), 0).astype(F32)

        def body(u, carry):
            onehot = jnp.where(slot_t == base + (u * MOE_SUB).astype(F32), 1.0, 0.0).astype(BF16)
            xe_ref[u] = _dot(onehot, hn_ref[...]).astype(BF16)
            ye_ref[u] = jnp.zeros((MOE_SUB, x_ref.shape[1]), F32)
            return carry

        lax.fori_loop(0, nsub, body, 0)

    def expert(u, carry):
        xe = xe_ref[u]
        a = _dot(xe, wg_ref[...])
        act = a * jax.nn.sigmoid(a) * _dot(xe, wu_ref[...])
        ye_ref[u] += _dot(act.astype(BF16), wd_ref[...])
        return carry

    lax.fori_loop(0, nsub, expert, 0)

    @pl.when(f == nf - 1)
    def _scatter():
        slot_r = jnp.sum(jnp.where(lane == e, posr_ref[...], 0.0), axis=1, keepdims=True)
        w_r = jnp.sum(jnp.where(lane == e, comb_ref[...], 0.0), axis=1, keepdims=True)
        base = lax.broadcasted_iota(jnp.int32, (tm, MOE_SUB), 1).astype(F32)

        def body(u, carry):
            onehot = jnp.where(slot_r == base + (u * MOE_SUB).astype(F32), 1.0, 0.0).astype(BF16)
            y = ye_ref[u]
            y_hi = y.astype(BF16)
            y_lo = (y - y_hi.astype(F32)).astype(BF16)
            acc_ref[...] += w_r * (_dot(onehot, y_hi) + _dot(onehot, y_lo))
            return carry

        lax.fori_loop(0, nsub, body, 0)

    @pl.when((e == N_EXPERTS - 1) & (f == nf - 1))
    def _fin():
        o_ref[...] = x_ref[...] + acc_ref[...]


def moe_swiglu(x, g, router, router_b, wg, wu, wd):
    t, d = x.shape
    f = wg.shape[2]
    tm = _tile(t, 1024)
    tf = 896 if f % 896 == 0 else _tile(f, 512)
    nf = f // tf
    nsubmax = -(-tm // MOE_SUB)
    rpad = jnp.zeros((d, LANE), F32).at[:, :N_EXPERTS].set(router).astype(BF16)
    rbpad = jnp.full((1, LANE), NEG, F32).at[0, :N_EXPERTS].set(router_b)
    before = jnp.asarray(np.triu(np.ones((tm, tm), np.float32), 1), BF16)
    const = lambda shape: pl.BlockSpec(shape, lambda i, e, j: (0,) * len(shape))
    return pl.pallas_call(
        functools.partial(_moe_body, nf=nf), name="moe_swiglu", grid=(t // tm, N_EXPERTS, nf),
        in_specs=[pl.BlockSpec((tm, d), lambda i, e, j: (i, 0)), const((1, d)), const((d, LANE)), const((1, LANE)),
                  const((tm, tm)),
                  pl.BlockSpec((None, d, tf), lambda i, e, j: (e, 0, j)),
                  pl.BlockSpec((None, d, tf), lambda i, e, j: (e, 0, j)),
                  pl.BlockSpec((None, tf, d), lambda i, e, j: (e, j, 0))],
        out_specs=pl.BlockSpec((tm, d), lambda i, e, j: (i, 0)),
        out_shape=jax.ShapeDtypeStruct((t, d), F32),
        scratch_shapes=[pltpu.VMEM((tm, d), BF16), pltpu.VMEM((tm, LANE), F32), pltpu.VMEM((LANE, tm), F32),
                        pltpu.VMEM((tm, LANE), F32), pltpu.SMEM((N_EXPERTS,), jnp.int32),
                        pltpu.VMEM((nsubmax, MOE_SUB, d), BF16), pltpu.VMEM((nsubmax, MOE_SUB, d), F32),
                        pltpu.VMEM((tm, d), F32)],
        compiler_params=_cp(("parallel", "arbitrary", "arbitrary")),
    )(x, g.reshape(1, d), rpad, rbpad, before, wg.astype(BF16), wu.astype(BF16), wd.astype(BF16))


def _ple_body(x_ref, p_ref, g_ref, wg_ref, wp_ref, fg_ref, o_ref, *, final):
    x = x_ref[...]
    hn = _rms(x, g_ref[...]).astype(BF16)
    gate = jax.nn.sigmoid(_dot(hn, wg_ref[...]))
    y = x + gate * _dot(p_ref[...].astype(BF16), wp_ref[...])
    if final:
        y = _rms(y, fg_ref[...])
    o_ref[...] = y


def ple(x, p, g, wg, wp, fg, final):
    t, d = x.shape
    pd = p.shape[1]
    tm = _tile(t, 512)
    return pl.pallas_call(
        functools.partial(_ple_body, final=final), name="ple", grid=(t // tm,),
        in_specs=[pl.BlockSpec((tm, d), lambda i: (i, 0)), pl.BlockSpec((tm, pd), lambda i: (i, 0)),
                  pl.BlockSpec((1, d), lambda i: (0, 0)), pl.BlockSpec((d, d), lambda i: (0, 0)),
                  pl.BlockSpec((pd, d), lambda i: (0, 0)), pl.BlockSpec((1, d), lambda i: (0, 0))],
        out_specs=pl.BlockSpec((tm, d), lambda i: (i, 0)),
        out_shape=jax.ShapeDtypeStruct((t, d), F32),
        compiler_params=_cp(("parallel",)),
    )(x, p, g.reshape(1, d), wg.astype(BF16), wp.astype(BF16), fg.reshape(1, d))


def _page_copies(pt_ref, cache_ref, bufs, sem, slot, b, c, *, layer, pc, cols):
    out = []
    for p in range(pc):
        pg = pt_ref[b, c * pc + p]
        for buf, (f0, w) in zip(bufs, cols):
            out.append(pltpu.make_async_copy(cache_ref.at[layer, pg, pl.ds(f0, w), :], buf.at[slot, p],
                                             sem.at[slot]))
    return out


def _paged_pipeline(pt_ref, cache_ref, bufs, sem, *, nb, nc, **kw):
    b = pl.program_id(0)
    c = pl.program_id(1) if nc > 1 else 0
    step = b * nc + c
    slot = step % 2

    @pl.when(step == 0)
    def _first():
        for d in _page_copies(pt_ref, cache_ref, bufs, sem, 0, 0, 0, **kw):
            d.start()

    @pl.when(step + 1 < nb * nc)
    def _prefetch():
        nxt = step + 1
        for d in _page_copies(pt_ref, cache_ref, bufs, sem, 1 - slot, nxt // nc, nxt % nc, **kw):
            d.start()

    for d in _page_copies(pt_ref, cache_ref, bufs, sem, slot, b, c, **kw):
        d.wait()
    return slot


def _compress_paged_body(pt_ref, cache_ref, wc_ref, bias_ref, w2_ref, o_ref, bufk, bufv, sem, rowk, rowv, *,
                         layer, npg, nb, nch):
    slot = _paged_pipeline(pt_ref, cache_ref, (bufk, bufv), sem, nb=nb, nc=1, layer=layer, pc=npg,
                           cols=((0, LANE), (LANE, LANE)))

    def to_rows(p, carry):
        rows = pl.ds(pl.multiple_of(p * PAGE, PAGE), PAGE)
        rowk[rows, :] = bufk[slot, p].T
        rowv[rows, :] = bufv[slot, p].T
        return carry

    lax.fori_loop(0, npg, to_rows, 0)
    _compress_core(lambda s: rowk[pl.ds(s, nch, stride=CMP_STRIDE), :],
                   lambda s: rowv[pl.ds(s, nch, stride=CMP_STRIDE), :], wc_ref, bias_ref, w2_ref, o_ref, nch)


def compress_paged(page_table, cache, layer, wc, bias, w2):
    b, npg = page_table.shape
    past = npg * PAGE
    nch = past // CMP_STRIDE
    const = lambda shape: pl.BlockSpec(shape, lambda i, pt: (0,) * len(shape))
    return pl.pallas_call(
        functools.partial(_compress_paged_body, layer=layer, npg=npg, nb=b, nch=nch), name="compress_paged",
        grid_spec=pltpu.PrefetchScalarGridSpec(
            num_scalar_prefetch=1, grid=(b,),
            in_specs=[pl.BlockSpec(memory_space=pl.ANY), const((CMP_STRIDE, 256, 512)), const((1, 256)),
                      const((256, 256))],
            out_specs=pl.BlockSpec((None, nch, 256), lambda i, pt: (i, 0, 0)),
            scratch_shapes=[pltpu.VMEM((2, npg, LANE, PAGE), F32), pltpu.VMEM((2, npg, LANE, PAGE), F32),
                            pltpu.SemaphoreType.DMA((2,)),
                            pltpu.VMEM((past, LANE), F32), pltpu.VMEM((past, LANE), F32)]),
        out_shape=jax.ShapeDtypeStruct((b, nch, 256), BF16),
        compiler_params=_cp(("arbitrary",)),
    )(page_table, cache, wc, bias, w2)


def _new_tile(new, c0, tq):
    return jnp.concatenate([new[:, c0:c0 + LANE], jnp.zeros((LANE - tq, LANE), F32)], axis=0).astype(BF16)


def _sel_paged_body(pt_ref, cache_ref, q_ref, new_ref, sel_ref, e_ref, o_ref, bufk, bufv, sem, *,
                    layer, npg, nb, tq, tk, nsp):
    slot = _paged_pipeline(pt_ref, cache_ref, (bufk, bufv), sem, nb=nb, nc=1, layer=layer, pc=npg,
                           cols=((2 * LANE, LANE), (3 * LANE, LANE)))
    past = npg * PAGE
    ppc = tk // PAGE
    r = 8 * tq
    qs = _stack_heads(q_ref[...], tq).astype(BF16)
    sel = sel_ref[...].astype(BF16)
    m = jnp.full((r, LANE), NEG, F32)
    l = jnp.zeros((r, LANE), F32)
    acc = jnp.zeros((r, LANE), F32)

    def stacked_bias(oks):
        return jnp.concatenate([jnp.where(ok, 0.0, NEG) for ok in oks for _ in range(NSA_J)], axis=0)

    for c in range(npg // ppc):
        kt = jnp.concatenate([bufk[slot, c * ppc + u] for u in range(ppc)], axis=1).astype(BF16)
        vt = jnp.concatenate([bufv[slot, c * ppc + u] for u in range(ppc)], axis=1).astype(BF16)
        bias = stacked_bias(_sel_ok(sel, e_ref[:, c * tk:(c + 1) * tk], nsp))
        m, l, acc = _flash_update(qs, kt, vt, bias, m, l, acc, kv_t=True)
    new = new_ref[...]
    causal = (lax.broadcasted_iota(jnp.int32, (tq, LANE), 1) <= lax.broadcasted_iota(jnp.int32, (tq, LANE), 0))
    bias = stacked_bias([ok & causal for ok in _sel_ok(sel, e_ref[:, past:past + LANE], nsp)])
    m, l, acc = _flash_update(qs, _new_tile(new, 2 * LANE, tq), _new_tile(new, 3 * LANE, tq), bias, m, l, acc)
    o_ref[...] = _unstack_heads(acc * (1.0 / jnp.where(l > 0.0, l, 1.0)), tq)


def sel_paged(page_table, cache, layer, q, new_rows, sel, emat):
    b, npg = page_table.shape
    past = npg * PAGE
    tq = q.shape[1]
    nsp = sel.shape[2] // 2
    return pl.pallas_call(
        functools.partial(_sel_paged_body, layer=layer, npg=npg, nb=b, tq=tq, tk=_tile(past, 512), nsp=nsp),
        name="sel_paged",
        grid_spec=pltpu.PrefetchScalarGridSpec(
            num_scalar_prefetch=1, grid=(b,),
            in_specs=[pl.BlockSpec(memory_space=pl.ANY),
                      pl.BlockSpec((None, tq, 512), lambda i, pt: (i, 0, 0)),
                      pl.BlockSpec((None, tq, 512), lambda i, pt: (i, 0, 0)),
                      pl.BlockSpec((None, tq, 2 * nsp), lambda i, pt: (i, 0, 0)),
                      pl.BlockSpec((nsp, past + LANE), lambda i, pt: (0, 0))],
            out_specs=pl.BlockSpec((None, tq, 512), lambda i, pt: (i, 0, 0)),
            scratch_shapes=[pltpu.VMEM((2, npg, LANE, PAGE), F32), pltpu.VMEM((2, npg, LANE, PAGE), F32),
                            pltpu.SemaphoreType.DMA((2,))]),
        out_shape=jax.ShapeDtypeStruct((b, tq, 512), F32),
        compiler_params=_cp(("arbitrary",)),
    )(page_table, cache, q, new_rows, sel, emat)


def _diff_paged_body(pt_ref, cache_ref, lam_ref, q_ref, new_ref, g_ref, o_ref, bufk, bufv, sem,
                     qs_ref, m_ref, l_ref, acc_ref, *, layer, nb, nc, pc, tq, lam_init):
    c = pl.program_id(1)
    slot = _paged_pipeline(pt_ref, cache_ref, (bufk, bufv), sem, nb=nb, nc=nc, layer=layer, pc=pc,
                           cols=((0, 512), (512, 512)))
    ppc = _tile(pc, 4)

    @pl.when(c == 0)
    def _init():
        _diff_init(q_ref, qs_ref, m_ref, l_ref, acc_ref, tq)

    def pages_t(buf, hp, p0):
        return jnp.concatenate([buf[slot, p0 + u, hp * LANE:(hp + 1) * LANE, :] for u in range(ppc)],
                               axis=1).astype(BF16)

    for p0 in range(0, pc, ppc):
        _diff_step(lambda hp: pages_t(bufk, hp, p0), lambda hp: pages_t(bufv, hp, p0), None,
                   qs_ref, m_ref, l_ref, acc_ref, tq, kv_t=True)

    @pl.when(c == nc - 1)
    def _fin():
        new = new_ref[...]
        causal = (lax.broadcasted_iota(jnp.int32, (tq, LANE), 1) <= lax.broadcasted_iota(jnp.int32, (tq, LANE), 0))
        _diff_step(lambda hp: _new_tile(new, hp * LANE, tq), lambda hp: _new_tile(new, 512 + hp * LANE, tq),
                   jnp.where(causal, 0.0, NEG), qs_ref, m_ref, l_ref, acc_ref, tq)
        _diff_fin(lam_ref, g_ref, o_ref, l_ref, acc_ref, tq, lam_init)


def diff_paged(page_table, cache, layer, lam_par, q, new_rows, gnorm, lam_init):
    b, npg = page_table.shape
    tq = q.shape[1]
    pc = _tile(npg, 16)
    nc = npg // pc
    r = 4 * tq
    npair = DIFF_HEADS // 2
    return pl.pallas_call(
        functools.partial(_diff_paged_body, layer=layer, nb=b, nc=nc, pc=pc, tq=tq, lam_init=lam_init),
        name="diff_paged",
        grid_spec=pltpu.PrefetchScalarGridSpec(
            num_scalar_prefetch=1, grid=(b, nc),
            in_specs=[pl.BlockSpec(memory_space=pl.ANY),
                      pl.BlockSpec((4, DIFF_QK), lambda i, c, pt: (0, 0)),
                      pl.BlockSpec((None, tq, 512), lambda i, c, pt: (i, 0, 0)),
                      pl.BlockSpec((None, tq, 1024), lambda i, c, pt: (i, 0, 0)),
                      pl.BlockSpec((1, LANE), lambda i, c, pt: (0, 0))],
            out_specs=pl.BlockSpec((None, tq, 512), lambda i, c, pt: (i, 0, 0)),
            scratch_shapes=[pltpu.VMEM((2, pc, 512, PAGE), F32), pltpu.VMEM((2, pc, 512, PAGE), F32),
                            pltpu.SemaphoreType.DMA((2,)),
                            pltpu.VMEM((npair, r, LANE), BF16), pltpu.VMEM((npair, r, LANE), F32),
                            pltpu.VMEM((npair, r, LANE), F32), pltpu.VMEM((npair, r, LANE), F32)]),
        out_shape=jax.ShapeDtypeStruct((b, tq, 512), F32),
        compiler_params=_cp(("arbitrary", "arbitrary")),
    )(page_table, cache, lam_par, q, new_rows, gnorm)


def _q_perm():
    idx = np.zeros(512, np.int32)
    for j in range(NSA_J):
        for g in range(NSA_G):
            for d in range(DH):
                idx[j * LANE + g * DH + d] = (g * NSA_J + j) * DH + d
    return idx


def _gate3_perm():
    idx = np.zeros(3 * 512, np.int32)
    for c in range(3):
        for j in range(NSA_J):
            for g in range(NSA_G):
                idx[c * 512 + j * LANE + g * DH:c * 512 + j * LANE + (g + 1) * DH] = (g * NSA_J + j) * 3 + c
    return idx


def _cmp_to_sel(n_cmp, n_sel, nch, nsp):
    r = SEL_BLOCK // CMP_STRIDE
    k = np.arange(n_cmp)[:, None] - r * np.arange(n_sel)[None, :]
    m = sum(((k + n >= 0) & (k + n < r)).astype(np.float32) for n in range(CMP_BLOCK // CMP_STRIDE))
    out = np.zeros((nch, nsp), np.float32)
    out[:n_cmp, :n_sel] = m
    return out


def _block_expand(nsp, lk):
    return (np.arange(lk)[None, :] // SEL_BLOCK == np.arange(nsp)[:, None]).astype(np.float32)


def _cmp_weights(pe, w1, w2):
    eye = jnp.eye(2, dtype=F32)
    w6 = w1.reshape(2, NSA_G, 2, CMP_STRIDE, DH, DH)
    wc = jnp.einsum("kgxsdh,kK,gG->skgdxKGh", w6, eye, eye).reshape(CMP_STRIDE, 256, 512).astype(BF16)
    w2b = jnp.einsum("kghd,kK,gG->kghKGd", w2, eye, eye).reshape(256, 256).astype(BF16)
    pe_rows = jnp.transpose(pe, (0, 2, 1, 3)).reshape(4, CMP_BLOCK * DH)
    xb = jnp.einsum("rc,rR->rRc", pe_rows, jnp.eye(4, dtype=F32)).reshape(4, 4 * CMP_BLOCK * DH)
    xb = jnp.zeros((16, 4 * CMP_BLOCK * DH), F32).at[:4].set(xb)
    wb = jnp.zeros((4 * CMP_BLOCK * DH, LANE), F32).at[:, :DH].set(w1.reshape(4 * CMP_BLOCK * DH, DH))
    bias = mm(xb, wb)[:4, :DH].reshape(1, 256)
    return wc, bias, w2b


def _mixer(x, lw, lam_init, *, b, sq, sample=None):
    t = b * sq
    w_in = lw["w_in"]
    qperm = _q_perm()
    segs = ((512, DH ** -0.5, None), (512, None, None), (256, None, None), (1536, None, "sigmoid"),
            (512, None, None), (1024, None, None), (512, DIFF_QK ** -0.5, None), (1024, None, None))
    w_all = jnp.concatenate([w_in[:, OFF_Q:OFF_KV][:, qperm], w_in[:, OFF_KV:OFF_GATE],
                             w_in[:, OFF_GATE:OFF_POOL][:, _gate3_perm()], w_in[:, OFF_POOL:OFF_END]],
                            axis=1).astype(BF16)
    hn, q, rows4, winkv, g3, zpool, zgm, dq, dkv = in_proj(x, lw["norm_mix_g"], w_all, segs)
    q, rows4, winkv = q.reshape(b, sq, 512), rows4.reshape(b, sq, 512), winkv.reshape(b, sq, 256)
    zpool, zgm = zpool.reshape(b, sq, 512), zgm.reshape(b, sq, 1024)
    dq, dkv = dq.reshape(b, sq, 512), dkv.reshape(b, sq, 1024)

    wc, cbias, w2b = _cmp_weights(lw["cmp_pe"], lw["cmp_w1"], lw["cmp_w2"])
    lam_par = jnp.stack([lw["diff_lq1"], lw["diff_lk1"], lw["diff_lq2"], lw["diff_lk2"]])
    gnorm = jnp.tile(lw["diff_norm_g"], 2).reshape(1, LANE)

    if sample is None:
        past, l_tot = 0, sq
        nch = sq // CMP_STRIDE
        tq = _tile(sq, 256)
        tk_sel = _tile(sq, 512)
        nk_sel = sq // tk_sel
        tw = _tile(sq, 256)
        nband = WINDOW // tw + 1
        win_all, kpos0_w, nk_w = winkv, 0, nband
        win_kt = lambda i, j: (jnp.maximum(i - (nband - 1) + j, 0), i - (nband - 1) + j >= 0)
        tq_w = tw
        free_w = nband - 2 if nband == 3 else None
        tq_d = _tile(sq, 256)
        tk_d = _tile(sq, 512)
        nk_d = sq // tk_d
        xcat = jnp.concatenate([jnp.zeros((b, 16, 512), F32), zpool], axis=1)
        zgm_in = zgm
    else:
        layer = sample["layer"]
        pt = sample["page_table"]
        past = pt.shape[1] * PAGE
        l_tot = past + sq
        nch = past // CMP_STRIDE
        tq = sq
        wbuf = sample["win"].shape[1]
        tw = 128
        wrows = -(-(wbuf + sq) // tw) * tw
        win_all = jnp.concatenate([sample["win"], winkv, jnp.zeros((b, wrows - wbuf - sq, 256), F32)], axis=1)
        kpos0_w, nk_w = past - wbuf, wrows // tw
        win_kt = lambda i, j: (j, j >= 0)
        tq_w = sq
        free_w = None
        xcat = jnp.concatenate([jnp.zeros((b, 1, 512), F32), sample["pool"], zpool], axis=1)
        zgm_in = jnp.concatenate([zgm, jnp.zeros((b, GM_CHUNK - sq, 1024), F32)], axis=1)

    n_cmp = l_tot // CMP_STRIDE - 1
    n_sel = -(-l_tot // SEL_BLOCK)
    nsp = -(-n_sel // LANE) * LANE
    msel = jnp.asarray(_cmp_to_sel(n_cmp, n_sel, nch, nsp), BF16)
    if sample is None:
        kcvc = compress(rows4, wc, cbias, w2b, nch)
        ocmp, selmask = cmp_attend(q, kcvc, msel, tq=tq, n_cmp=n_cmp, n_sel=n_sel, qpos0=0)
        emat = jnp.asarray(_block_expand(nsp, sq), BF16)

        def sel_kt(i, j):
            last = ((i + 1) * tq - 1) // tk_sel
            return jnp.minimum(j, last), j <= last

        osel = nsa_flash(q, rows4, 2, 3, mode="sel", tq=tq, tk=tk_sel, nk=nk_sel, qpos0=0, kpos0=0,
                         kt_fn=sel_kt, sel=selmask, emat=emat)
    else:
        kcvc = compress_paged(pt, sample["cache_nsa"], layer, wc, cbias, w2b)
        ocmp, selmask = cmp_attend(q, kcvc, msel, tq=tq, n_cmp=n_cmp, n_sel=n_sel, qpos0=past)
        emat = jnp.asarray(_block_expand(nsp, past + LANE), BF16)
        osel = sel_paged(pt, sample["cache_nsa"], layer, q, rows4, selmask, emat)
    owin = nsa_flash(q, win_all, 0, 1, mode="win", tq=tq_w, tk=tw, nk=nk_w, qpos0=past, kpos0=kpos0_w,
                     kt_fn=win_kt, free_step=free_w)

    opool = pool_mix(xcat, lw["pool_w"], lw["pool_scale"], past)
    ws_tril = jnp.tril(lw["gm_ws"]).astype(BF16)
    bs_exp = jnp.repeat(lw["gm_bs"].T, LANE, axis=1)
    ogm, gm_v = gmlp_mix(zgm_in, lw["gm_ng"], lw["gm_nb"], ws_tril, bs_exp)

    if sample is None:
        def diff_kt(i, j):
            last = ((i + 1) * tq_d - 1) // tk_d
            return jnp.minimum(j, last), j <= last

        odiff = diff_attend(lam_par, dq, dkv, gnorm, tq=tq_d, tk=tk_d, nk=nk_d, qpos0=0,
                            lam_init=lam_init, kt_fn=diff_kt)
    else:
        odiff = diff_paged(pt, sample["cache_diff"], layer, lam_par, dq, dkv, gnorm, lam_init)
        ogm = ogm[:, :sq]
        gm_v = gm_v[:, :sq]
    wbr = lw["w_branch"].at[0].set(lw["w_branch"][0][qperm]).astype(BF16)
    out = finish_mixer(x, hn, g3, ocmp.reshape(t, 512), osel.reshape(t, 512), owin.reshape(t, 512),
                       opool.reshape(t, 512), ogm.reshape(t, 512), odiff.reshape(t, 512),
                       lw["w_gate"].astype(BF16), lw["b_gate"].reshape(1, -1), wbr, lw["w_o"].astype(BF16))
    states = dict(rows4=rows4, winkv=winkv, win_all=win_all, xcat=xcat, gm_v=gm_v, dkv=dkv)
    return out, states


def kernel(x_prompt, x_sample, cache_nsa, cache_diff, state_nsa_win, state_pool, page_table, p_prompt, p_sample, norm_mix_g, w_in, nsa_cmp_pe, nsa_cmp_w1, nsa_cmp_w2, pool_w, pool_scale, gm_norm_g, gm_norm_b, gm_ws, gm_bs, diff_lq1, diff_lk1, diff_lq2, diff_lk2, diff_norm_g, w_branch, w_gate, b_gate, w_o, norm_ffn_g, ffn_w_gate, ffn_w_up, ffn_w_down, moe_router, moe_router_b, moe_w_gate, moe_w_up, moe_w_down, ple_norm_g, ple_w_gate, ple_w_proj, final_norm_g):
    bp, sp, d = x_prompt.shape
    bs, ss, _ = x_sample.shape
    depth = w_in.shape[0]
    n_phys = cache_nsa.shape[1]
    wbuf = state_nsa_win.shape[2]
    cache_nsa2 = jnp.transpose(cache_nsa, (0, 1, 3, 4, 5, 2)).reshape(depth, n_phys, 512, PAGE)
    cache_diff2 = jnp.transpose(cache_diff, (0, 1, 3, 4, 5, 2)).reshape(depth, n_phys, 1024, PAGE)
    xp = x_prompt.reshape(bp * sp, d)
    xs = x_sample.reshape(bs * ss, d)
    outs = {k: [] for k in ("nsa_p", "nsa_s", "win_p", "win_s", "pool_p", "pool_s", "gmv_s", "diff_p", "diff_s")}
    for l in range(depth):
        lw = dict(norm_mix_g=norm_mix_g[l], w_in=w_in[l], cmp_pe=nsa_cmp_pe[l], cmp_w1=nsa_cmp_w1[l],
                  cmp_w2=nsa_cmp_w2[l], pool_w=pool_w[l], pool_scale=pool_scale[l], gm_ng=gm_norm_g[l],
                  gm_nb=gm_norm_b[l], gm_ws=gm_ws[l], gm_bs=gm_bs[l], diff_lq1=diff_lq1[l], diff_lk1=diff_lk1[l],
                  diff_lq2=diff_lq2[l], diff_lk2=diff_lk2[l], diff_norm_g=diff_norm_g[l], w_branch=w_branch[l],
                  w_gate=w_gate[l], b_gate=b_gate[l], w_o=w_o[l])
        lam_init = 0.8 - 0.6 * math.exp(-0.3 * l)
        xp, st_p = _mixer(xp, lw, lam_init, b=bp, sq=sp)
        sample = dict(cache_nsa=cache_nsa2, cache_diff=cache_diff2, win=state_nsa_win[l].reshape(bs, wbuf, 256),
                      pool=state_pool[l], page_table=page_table, layer=l)
        xs, st_s = _mixer(xs, lw, lam_init, b=bs, sq=ss, sample=sample)
        outs["nsa_p"].append(st_p["rows4"].reshape(bp, sp, 4, NSA_G, DH))
        outs["nsa_s"].append(st_s["rows4"].reshape(bs, ss, 4, NSA_G, DH))
        wkeep = min(WINDOW, sp)
        outs["win_p"].append(st_p["winkv"][:, sp - wkeep:].reshape(bp, wkeep, 2, NSA_G, DH))
        outs["win_s"].append(st_s["win_all"][:, ss:ss + wbuf].reshape(bs, wbuf, 2, NSA_G, DH))
        outs["pool_p"].append(st_p["xcat"][:, -POOL_MEM:])
        outs["pool_s"].append(st_s["xcat"][:, -POOL_MEM:])
        outs["gmv_s"].append(st_s["gm_v"])
        outs["diff_p"].append(st_p["dkv"].reshape(bp, sp, 2, DIFF_HEADS, 2 * DIFF_QK))
        outs["diff_s"].append(st_s["dkv"].reshape(bs, ss, 2, DIFF_HEADS, 2 * DIFF_QK))
        i = l // 2
        if l % 2 == 0:
            xp = ffn_swiglu(xp, norm_ffn_g[l], ffn_w_gate[i], ffn_w_up[i], ffn_w_down[i])
            xs = ffn_swiglu(xs, norm_ffn_g[l], ffn_w_gate[i], ffn_w_up[i], ffn_w_down[i])
        else:
            xp = moe_swiglu(xp, norm_ffn_g[l], moe_router[i], moe_router_b[i], moe_w_gate[i], moe_w_up[i], moe_w_down[i])
            xs = moe_swiglu(xs, norm_ffn_g[l], moe_router[i], moe_router_b[i], moe_w_gate[i], moe_w_up[i], moe_w_down[i])
        final = l == depth - 1
        xp = ple(xp, p_prompt[l].reshape(bp * sp, -1), ple_norm_g[l], ple_w_gate[l], ple_w_proj[l], final_norm_g, final)
        xs = ple(xs, p_sample[l].reshape(bs * ss, -1), ple_norm_g[l], ple_w_gate[l], ple_w_proj[l], final_norm_g, final)
    st = lambda k: jnp.stack(outs[k])
    return (xp.reshape(bp, sp, d), xs.reshape(bs, ss, d), st("nsa_p"), st("nsa_s"), st("win_p"), st("win_s"),
            st("pool_p"), st("pool_s"), st("gmv_s"), st("diff_p"), st("diff_s"))
```

```python
import functools
import math

import numpy as np
import jax
import jax.numpy as jnp
from jax import lax
from jax.experimental import pallas as pl
from jax.experimental.pallas import tpu as pltpu

F32 = jnp.float32
BF16 = jnp.bfloat16

D_MODEL = 1024
PAGE = 128
NSA_HEADS = 8
NSA_G = 2
NSA_J = 4
DH = 64
CMP_BLOCK = 32
CMP_STRIDE = 16
SEL_BLOCK = 64
SEL_TOPK = 16
FORCE_BONUS = 100.0
WINDOW = 512
POOL_WINDOWS = (2, 4, 8, 16)
POOL_MEM = 15
GM_W = 512
GM_CHUNK = 128
DIFF_HEADS = 8
DIFF_QK = 32
N_BRANCH = 4
BRANCH_W = 512
N_EXPERTS = 8
EPS = 1e-6
NEG = -1e30
LANE = 128
VMEM_LIMIT = 56 * 1024 * 1024

OFF_Q, OFF_KV, OFF_GATE, OFF_POOL, OFF_GM, OFF_DQ, OFF_DKV, OFF_END = 0, 512, 1280, 1304, 1816, 2840, 3352, 4376


def _tile(n, pref):
    t = min(n, pref)
    while n % t:
        t //= 2
    return t


def _cp(sem, vmem=VMEM_LIMIT):
    return pltpu.CompilerParams(dimension_semantics=sem, vmem_limit_bytes=vmem)


def _gelu(x):
    return 0.5 * x * (1.0 + jnp.tanh(0.7978845608028654 * (x + 0.044715 * (x * x * x))))


def _rms(x, g):
    return x * lax.rsqrt(jnp.mean(x * x, axis=-1, keepdims=True) + EPS) * g


def _dot(a, b):
    return jnp.dot(a, b, preferred_element_type=F32)


def _dot_nt(a, b):
    return lax.dot_general(a, b, (((1,), (1,)), ((), ())), preferred_element_type=F32)


def _rmsnorm_body(x_ref, g_ref, o_ref):
    o_ref[...] = _rms(x_ref[...], g_ref[...]).astype(o_ref.dtype)


def rmsnorm(x, g, out_dtype):
    t, d = x.shape
    tm = _tile(t, 512)
    return pl.pallas_call(
        _rmsnorm_body, name="rmsnorm", grid=(t // tm,),
        in_specs=[pl.BlockSpec((tm, d), lambda i: (i, 0)), pl.BlockSpec((1, d), lambda i: (0, 0))],
        out_specs=pl.BlockSpec((tm, d), lambda i: (i, 0)),
        out_shape=jax.ShapeDtypeStruct((t, d), out_dtype),
        compiler_params=_cp(("parallel",)),
    )(x, g.reshape(1, d))


def _mm_body(x_ref, w_ref, o_ref, *, scale, act):
    y = _dot(x_ref[...].astype(BF16), w_ref[...])
    if scale is not None:
        y = y * scale
    if act == "sigmoid":
        y = jax.nn.sigmoid(y)
    o_ref[...] = y.astype(o_ref.dtype)


def mm(x, w, *, scale=None, act=None, out_dtype=F32):
    t, k = x.shape
    n = w.shape[1]
    tm = _tile(t, 512)
    tn = _tile(n, 512)
    return pl.pallas_call(
        functools.partial(_mm_body, scale=scale, act=act), name="mm", grid=(t // tm, n // tn),
        in_specs=[pl.BlockSpec((tm, k), lambda i, j: (i, 0)), pl.BlockSpec((k, tn), lambda i, j: (0, j))],
        out_specs=pl.BlockSpec((tm, tn), lambda i, j: (i, j)),
        out_shape=jax.ShapeDtypeStruct((t, n), out_dtype),
        compiler_params=_cp(("parallel", "parallel")),
    )(x, w.astype(BF16))


def _in_proj_body(x_ref, g_ref, w_ref, hn_ref, *o_refs, segs):
    hn = _rms(x_ref[...], g_ref[...]).astype(BF16)
    hn_ref[...] = hn
    c0 = 0
    for (n, scale, act), o_ref in zip(segs, o_refs):
        y = _dot(hn, w_ref[:, c0:c0 + n])
        if scale is not None:
            y = y * scale
        if act == "sigmoid":
            y = jax.nn.sigmoid(y)
        o_ref[...] = y
        c0 += n


def in_proj(x, g, w_all, segs):
    t, d = x.shape
    tm = _tile(t, 256)
    ntot = w_all.shape[1]
    return pl.pallas_call(
        functools.partial(_in_proj_body, segs=segs), name="in_proj", grid=(t // tm,),
        in_specs=[pl.BlockSpec((tm, d), lambda i: (i, 0)), pl.BlockSpec((1, d), lambda i: (0, 0)),
                  pl.BlockSpec((d, ntot), lambda i: (0, 0), pipeline_mode=pl.Buffered(1))],
        out_specs=[pl.BlockSpec((tm, d), lambda i: (i, 0))] + [pl.BlockSpec((tm, n), lambda i: (i, 0)) for n, _, _ in segs],
        out_shape=[jax.ShapeDtypeStruct((t, d), BF16)] + [jax.ShapeDtypeStruct((t, n), F32) for n, _, _ in segs],
        compiler_params=_cp(("parallel",)),
    )(x, g.reshape(1, d), w_all)


def _compress_body(xk_ref, xv_ref, wc_ref, bias_ref, w2_ref, o_ref, *, nch):
    _compress_core(lambda s: xk_ref[pl.ds(s, nch, stride=CMP_STRIDE), :],
                   lambda s: xv_ref[pl.ds(s, nch, stride=CMP_STRIDE), :], wc_ref, bias_ref, w2_ref, o_ref, nch)


def _compress_core(load_k, load_v, wc_ref, bias_ref, w2_ref, o_ref, nch):
    xs = jnp.concatenate([part(s) for s in range(CMP_STRIDE) for part in (load_k, load_v)], axis=1).astype(BF16)
    acc = _dot(xs, wc_ref[...])
    hi_next = pltpu.roll(acc[:, 256:], nch - 1, 0)
    hid = _gelu(acc[:, :256] + hi_next + bias_ref[...])
    o_ref[...] = _dot(hid.astype(BF16), w2_ref[...]).astype(o_ref.dtype)


def compress(rows, wc, bias, w2, nch):
    b = rows.shape[0]
    return pl.pallas_call(
        functools.partial(_compress_body, nch=nch), name="compress", grid=(b,),
        in_specs=[pl.BlockSpec((None, nch * CMP_STRIDE, LANE), lambda i: (i, 0, 0)),
                  pl.BlockSpec((None, nch * CMP_STRIDE, LANE), lambda i: (i, 0, 1)),
                  pl.BlockSpec((CMP_STRIDE * 256, 512), lambda i: (0, 0)),
                  pl.BlockSpec((1, 256), lambda i: (0, 0)),
                  pl.BlockSpec((256, 256), lambda i: (0, 0))],
        out_specs=pl.BlockSpec((None, nch, 256), lambda i: (i, 0, 0)),
        out_shape=jax.ShapeDtypeStruct((b, nch, 256), BF16),
        compiler_params=_cp(("parallel",)),
    )(rows, rows, wc, bias, w2)


def _stack_heads(q, tq):
    lane = lax.broadcasted_iota(jnp.int32, (tq, LANE), 1)
    parts = []
    for g in range(NSA_G):
        keep = (lane < DH) if g == 0 else (lane >= DH)
        for j in range(NSA_J):
            parts.append(jnp.where(keep, q[:, j * LANE:(j + 1) * LANE], 0.0))
    return jnp.concatenate(parts, axis=0)


def _unstack_heads(o, tq):
    lane = lax.broadcasted_iota(jnp.int32, (tq, LANE), 1)
    outs = []
    for j in range(NSA_J):
        outs.append(jnp.where(lane < DH, o[j * tq:(j + 1) * tq], o[(NSA_J + j) * tq:(NSA_J + j + 1) * tq]))
    return jnp.concatenate(outs, axis=1)


def _cmp_body(q_ref, kv_ref, msel_ref, o_ref, sel_ref, *, tq, nch, n_cmp, n_sel, nsp, qpos0):
    i = pl.program_id(1)
    r = 8 * tq
    qs = _stack_heads(q_ref[...], tq).astype(BF16)
    kc = kv_ref[:, 0:LANE]
    vc = kv_ref[:, LANE:2 * LANE]
    s = _dot_nt(qs, kc)
    row = lax.broadcasted_iota(jnp.int32, (r, nch), 0)
    col = lax.broadcasted_iota(jnp.int32, (r, nch), 1)
    qpos = qpos0 + i * tq + (row & (tq - 1))
    cmask = (col * CMP_STRIDE + (CMP_BLOCK - 1) <= qpos) & (col < n_cmp)
    s = jnp.where(cmask, s, NEG)
    m = jnp.max(s, axis=1, keepdims=True)
    e = jnp.where(cmask, jnp.exp(s - m), 0.0)
    l = jnp.sum(e, axis=1, keepdims=True)
    p = (e * (1.0 / jnp.where(l > 0.0, l, 1.0))).astype(BF16)
    o_ref[...] = _unstack_heads(_dot(p, vc), tq)
    imp_all = _dot(p, msel_ref[...])
    blocks_on_rows = tq % LANE == 0
    nrow = -(-n_sel // 8) * 8
    shape = (nrow, tq) if blocks_on_rows else (tq, nsp)
    baxis, qaxis = (0, 1) if blocks_on_rows else (1, 0)
    blk = lax.broadcasted_iota(jnp.int32, shape, baxis)
    qp = qpos0 + i * tq + lax.broadcasted_iota(jnp.int32, shape, qaxis)
    cur = lax.shift_right_logical(qp, SEL_BLOCK.bit_length() - 1)
    valid = (blk <= cur) & (blk < n_sel)
    forced = (blk == 0) | (blk == cur) | (blk == cur - 1)
    for g in range(NSA_G):
        imp = imp_all[(g * NSA_J) * tq:(g * NSA_J + 1) * tq]
        for j in range(1, NSA_J):
            imp = imp + imp_all[(g * NSA_J + j) * tq:(g * NSA_J + j + 1) * tq]
        if blocks_on_rows:
            imp = imp.T[:nrow]
        score = jnp.where(valid, imp + FORCE_BONUS * forced.astype(F32), -1.0)
        score = jnp.where(blk < n_sel, score, -2.0)
        rank = jnp.zeros(shape, F32)
        for mth in range(n_sel):
            cm = score[mth:mth + 1, :] if blocks_on_rows else score[:, mth:mth + 1]
            beats = (cm > score) | ((cm == score) & (blk > mth))
            rank = rank + beats.astype(F32)
        chosen = ((rank < float(min(SEL_TOPK, n_sel))) & valid).astype(F32)
        if blocks_on_rows:
            chosen = jnp.concatenate([chosen, jnp.zeros((nsp - nrow, tq), F32)], axis=0).T
        sel_ref[:, g * nsp:(g + 1) * nsp] = chosen


def cmp_attend(q, kcvc, msel, *, tq, n_cmp, n_sel, qpos0):
    b, sq, _ = q.shape
    nch = kcvc.shape[1]
    nsp = msel.shape[1]
    return pl.pallas_call(
        functools.partial(_cmp_body, tq=tq, nch=nch, n_cmp=n_cmp, n_sel=n_sel, nsp=nsp, qpos0=qpos0),
        name="cmp_attend",
        grid=(b, sq // tq),
        in_specs=[pl.BlockSpec((None, tq, 512), lambda bi, i: (bi, i, 0)),
                  pl.BlockSpec((None, nch, 256), lambda bi, i: (bi, 0, 0)),
                  pl.BlockSpec((nch, nsp), lambda bi, i: (0, 0))],
        out_specs=[pl.BlockSpec((None, tq, 512), lambda bi, i: (bi, i, 0)),
                   pl.BlockSpec((None, tq, 2 * nsp), lambda bi, i: (bi, i, 0))],
        out_shape=[jax.ShapeDtypeStruct((b, sq, 512), F32), jax.ShapeDtypeStruct((b, sq, 2 * nsp), F32)],
        compiler_params=_cp(("parallel", "parallel")),
    )(q, kcvc, msel)


ROW_BLOCK = 128
KEY_SUB = 256


def _flash_update(qs, k, v, bias, m, l, acc, kv_t=False):
    s = _dot(qs, k) if kv_t else _dot_nt(qs, k)
    if bias is not None:
        s = s + bias
    tiles = [s[:, c:c + LANE] for c in range(0, s.shape[1], LANE)]
    m_new = jnp.maximum(m, jnp.max(functools.reduce(jnp.maximum, tiles), axis=1, keepdims=True))
    alpha = jnp.exp(m - m_new)
    ps = [jnp.exp(t - m_new) for t in tiles]
    l_new = alpha * l + jnp.sum(functools.reduce(jnp.add, ps), axis=1, keepdims=True)
    pb = jnp.concatenate(ps, axis=1).astype(BF16)
    acc_new = alpha * acc + (_dot_nt(pb, v) if kv_t else _dot(pb, v))
    return m_new, l_new, acc_new


def _flash_rows(qs_ref, k, v, bias_of, m_ref, l_ref, acc_ref, nrows, tq, kv_t=False):
    if tq < ROW_BLOCK:
        parts = [bias_of(h, 0, tq) for h in range(nrows // tq)]
        bias = None if parts[0] is None else jnp.concatenate(parts, axis=0)
        blocks = [(pl.ds(0, nrows), bias)]
    else:
        blocks = [(pl.ds(r0, ROW_BLOCK), bias_of(r0 // tq, r0 % tq, ROW_BLOCK)) for r0 in range(0, nrows, ROW_BLOCK)]
    for sl, bias in blocks:
        m, l, acc = _flash_update(qs_ref[sl], k, v, bias, m_ref[sl], l_ref[sl], acc_ref[sl], kv_t)
        m_ref[sl] = m
        l_ref[sl] = l
        acc_ref[sl] = acc


def _sel_ok(sel, e, nsp):
    return [_dot(sel[:, g * nsp:(g + 1) * nsp], e) > 0.5 for g in range(NSA_G)]


def _nsa_flash_body(*refs, mode, tq, tk, nk, qpos0, kpos0, kt_fn, nsp, free_step):
    if mode == "sel":
        q_ref, k_ref, v_ref, sel_ref, e_ref, o_ref, qs_ref, m_ref, l_ref, acc_ref = refs
    else:
        q_ref, k_ref, v_ref, o_ref, qs_ref, m_ref, l_ref, acc_ref = refs
    i = pl.program_id(1)
    j = pl.program_id(2)
    r = 8 * tq

    @pl.when(j == 0)
    def _init():
        qs_ref[...] = _stack_heads(q_ref[...], tq).astype(BF16)
        m_ref[...] = jnp.full((r, LANE), NEG, F32)
        l_ref[...] = jnp.zeros((r, LANE), F32)
        acc_ref[...] = jnp.zeros((r, LANE), F32)

    kt, valid = kt_fn(i, j)

    ks = min(tk, KEY_SUB)

    def step(c0, masked):
        k = k_ref[c0:c0 + ks, :].astype(BF16)
        v = v_ref[c0:c0 + ks, :].astype(BF16)
        if not masked:
            _flash_rows(qs_ref, k, v, lambda h, t0, n: None, m_ref, l_ref, acc_ref, r, tq)
            return
        qpos = qpos0 + i * tq + lax.broadcasted_iota(jnp.int32, (tq, ks), 0)
        kpos = kpos0 + kt * tk + c0 + lax.broadcasted_iota(jnp.int32, (tq, ks), 1)
        ok = kpos <= qpos
        if mode == "win":
            ok = ok & (kpos > qpos - WINDOW)
            bias = [jnp.where(ok, 0.0, NEG)] * NSA_G
        else:
            oks = _sel_ok(sel_ref[...].astype(BF16), e_ref[:, c0:c0 + ks], nsp)
            bias = [jnp.where(ok & okg, 0.0, NEG) for okg in oks]
        _flash_rows(qs_ref, k, v, lambda h, t0, n: bias[h // NSA_J][t0:t0 + n], m_ref, l_ref, acc_ref, r, tq)

    for c0 in range(0, tk, ks):
        live = valid & (kpos0 + kt * tk + c0 <= qpos0 + (i + 1) * tq - 1)
        if free_step is None:
            pl.when(live)(functools.partial(step, c0, True))
        else:
            pl.when(live & (j != free_step))(functools.partial(step, c0, True))
            pl.when(live & (j == free_step))(functools.partial(step, c0, False))

    @pl.when(j == nk - 1)
    def _fin():
        l = l_ref[...]
        o = acc_ref[...] * (1.0 / jnp.where(l > 0.0, l, 1.0))
        o_ref[...] = _unstack_heads(o, tq)


def nsa_flash(q, kv, kcol, vcol, *, mode, tq, tk, nk, qpos0, kpos0, kt_fn, sel=None, emat=None, free_step=None):
    b, sq, _ = q.shape
    nsp = 0 if sel is None else sel.shape[2] // 2

    def kmap(col):
        return lambda bi, i, j: (bi, kt_fn(i, j)[0], col)

    in_specs = [pl.BlockSpec((None, tq, 512), lambda bi, i, j: (bi, i, 0)),
                pl.BlockSpec((None, tk, LANE), kmap(kcol)),
                pl.BlockSpec((None, tk, LANE), kmap(vcol))]
    args = [q, kv, kv]
    if mode == "sel":
        in_specs += [pl.BlockSpec((None, tq, 2 * nsp), lambda bi, i, j: (bi, i, 0)),
                     pl.BlockSpec((nsp, tk), lambda bi, i, j: (0, kt_fn(i, j)[0]))]
        args += [sel, emat]
    r = 8 * tq
    return pl.pallas_call(
        functools.partial(_nsa_flash_body, mode=mode, tq=tq, tk=tk, nk=nk, qpos0=qpos0, kpos0=kpos0,
                          kt_fn=kt_fn, nsp=nsp, free_step=free_step),
        name="nsa_" + mode, grid=(b, sq // tq, nk),
        in_specs=in_specs,
        out_specs=pl.BlockSpec((None, tq, 512), lambda bi, i, j: (bi, i, 0)),
        out_shape=jax.ShapeDtypeStruct((b, sq, 512), F32),
        scratch_shapes=[pltpu.VMEM((r, LANE), BF16), pltpu.VMEM((r, LANE), F32), pltpu.VMEM((r, LANE), F32),
                        pltpu.VMEM((r, LANE), F32)],
        compiler_params=_cp(("parallel", "parallel", "arbitrary")),
    )(*args)


def _diff_init(q_ref, qs_ref, m_ref, l_ref, acc_ref, tq):
    r = 4 * tq
    npair = DIFF_HEADS // 2
    lane = lax.broadcasted_iota(jnp.int32, (tq, LANE), 1)
    for hp in range(npair):
        q = q_ref[:, hp * LANE:(hp + 1) * LANE]
        parts = []
        for h in range(2):
            for mth in range(2):
                lo = h * 2 * DIFF_QK + mth * DIFF_QK
                parts.append(jnp.where((lane >= lo) & (lane < lo + DIFF_QK), q, 0.0))
        qs_ref[hp] = jnp.concatenate(parts, axis=0).astype(BF16)
    m_ref[...] = jnp.full((npair, r, LANE), NEG, F32)
    l_ref[...] = jnp.zeros((npair, r, LANE), F32)
    acc_ref[...] = jnp.zeros((npair, r, LANE), F32)


def _diff_step(k_of, v_of, bias, qs_ref, m_ref, l_ref, acc_ref, tq, kv_t=False):
    bias_of = lambda h, t0, n: None if bias is None else bias[t0:t0 + n]
    for hp in range(DIFF_HEADS // 2):
        _flash_rows(qs_ref.at[hp], k_of(hp), v_of(hp), bias_of, m_ref.at[hp], l_ref.at[hp], acc_ref.at[hp],
                    4 * tq, tq, kv_t)


def _diff_fin(lam_ref, g_ref, o_ref, l_ref, acc_ref, tq, lam_init):
    lp = lam_ref[...]
    lam = (jnp.exp(jnp.sum(lp[0:1] * lp[1:2], axis=1, keepdims=True))
           - jnp.exp(jnp.sum(lp[2:3] * lp[3:4], axis=1, keepdims=True)) + lam_init)
    lane = lax.broadcasted_iota(jnp.int32, (tq, LANE), 1)
    lo_half = lane < 2 * DIFF_QK
    for hp in range(DIFF_HEADS // 2):
        l = l_ref[hp]
        a = acc_ref[hp] * (1.0 / jnp.where(l > 0.0, l, 1.0))
        o0 = a[0:tq] - lam * a[tq:2 * tq]
        o1 = a[2 * tq:3 * tq] - lam * a[3 * tq:4 * tq]
        o = jnp.where(lo_half, o0, o1)
        sq = o * o
        ms0 = jnp.sum(jnp.where(lo_half, sq, 0.0), axis=1, keepdims=True)
        ms1 = jnp.sum(jnp.where(lo_half, 0.0, sq), axis=1, keepdims=True)
        ms = jnp.where(lo_half, ms0, ms1) * (1.0 / (2 * DIFF_QK))
        y = o * lax.rsqrt(ms + EPS) * g_ref[...]
        o_ref[:, hp * LANE:(hp + 1) * LANE] = y * (1.0 - lam_init)


def _diff_body(lam_ref, q_ref, k_ref, v_ref, g_ref, o_ref, qs_ref, m_ref, l_ref, acc_ref, *,
               tq, tk, nk, qpos0, lam_init, kt_fn):
    i = pl.program_id(1)
    j = pl.program_id(2)
    r = 4 * tq

    @pl.when(j == 0)
    def _init():
        _diff_init(q_ref, qs_ref, m_ref, l_ref, acc_ref, tq)

    kt, valid = kt_fn(i, j)
    ks = min(tk, KEY_SUB)

    def step(c0, masked):
        k_of = lambda hp: k_ref[c0:c0 + ks, hp * LANE:(hp + 1) * LANE].astype(BF16)
        v_of = lambda hp: v_ref[c0:c0 + ks, hp * LANE:(hp + 1) * LANE].astype(BF16)
        bias = None
        if masked:
            qpos = qpos0 + i * tq + lax.broadcasted_iota(jnp.int32, (tq, ks), 0)
            kpos = kt * tk + c0 + lax.broadcasted_iota(jnp.int32, (tq, ks), 1)
            bias = jnp.where(kpos <= qpos, 0.0, NEG)
        _diff_step(k_of, v_of, bias, qs_ref, m_ref, l_ref, acc_ref, tq)

    for c0 in range(0, tk, ks):
        k_first = kt * tk + c0
        live = valid & (k_first <= qpos0 + (i + 1) * tq - 1)
        below = k_first + ks - 1 <= qpos0 + i * tq
        pl.when(live & below)(functools.partial(step, c0, False))
        pl.when(live & jnp.logical_not(below))(functools.partial(step, c0, True))

    @pl.when(j == nk - 1)
    def _fin():
        _diff_fin(lam_ref, g_ref, o_ref, l_ref, acc_ref, tq, lam_init)


def diff_attend(lam_par, q, kv, gnorm, *, tq, tk, nk, qpos0, lam_init, kt_fn):
    b, sq, _ = q.shape
    r = 4 * tq
    npair = DIFF_HEADS // 2
    return pl.pallas_call(
        functools.partial(_diff_body, tq=tq, tk=tk, nk=nk, qpos0=qpos0, lam_init=lam_init, kt_fn=kt_fn),
        name="diff_attend",
        grid=(b, sq // tq, nk),
        in_specs=[pl.BlockSpec((4, DIFF_QK), lambda bi, i, j: (0, 0)),
                  pl.BlockSpec((None, tq, 512), lambda bi, i, j: (bi, i, 0)),
                  pl.BlockSpec((None, tk, 512), lambda bi, i, j: (bi, kt_fn(i, j)[0], 0)),
                  pl.BlockSpec((None, tk, 512), lambda bi, i, j: (bi, kt_fn(i, j)[0], 1)),
                  pl.BlockSpec((1, LANE), lambda bi, i, j: (0, 0))],
        out_specs=pl.BlockSpec((None, tq, 512), lambda bi, i, j: (bi, i, 0)),
        out_shape=jax.ShapeDtypeStruct((b, sq, 512), F32),
        scratch_shapes=[pltpu.VMEM((npair, r, LANE), BF16), pltpu.VMEM((npair, r, LANE), F32),
                        pltpu.VMEM((npair, r, LANE), F32), pltpu.VMEM((npair, r, LANE), F32)],
        compiler_params=_cp(("parallel", "parallel", "arbitrary")),
    )(lam_par, q, kv, kv, gnorm)


def _pool_body(x_ref, pw_ref, sc_ref, o_ref, *, sq, start):
    pos = start + lax.broadcasted_iota(jnp.int32, (sq, LANE), 0)
    for g, w in enumerate(POOL_WINDOWS):
        c0 = g * LANE
        x = x_ref[pl.ds(16, sq), c0:c0 + LANE]
        tot = x
        for back in range(1, w):
            tot = tot + x_ref[pl.ds(16 - back, sq), c0:c0 + LANE]
        cnt = jnp.minimum(w, pos + 1).astype(F32)
        mix = tot / cnt - x
        y = _dot(mix.astype(BF16), pw_ref[g])
        o_ref[:, c0:c0 + LANE] = y * sc_ref[:, c0:c0 + LANE]


def pool_mix(xcat, pw, scale, start):
    b, rows, _ = xcat.shape
    sq = rows - 16
    return pl.pallas_call(
        functools.partial(_pool_body, sq=sq, start=start), name="pool_mix", grid=(b,),
        in_specs=[pl.BlockSpec((None, rows, 512), lambda i: (i, 0, 0)),
                  pl.BlockSpec((4, LANE, LANE), lambda i: (0, 0, 0)),
                  pl.BlockSpec((1, 512), lambda i: (0, 0))],
        out_specs=pl.BlockSpec((None, sq, 512), lambda i: (i, 0, 0)),
        out_shape=jax.ShapeDtypeStruct((b, sq, 512), F32),
        compiler_params=_cp(("parallel",)),
    )(xcat, pw.astype(BF16), scale.reshape(1, 512))


def _gmlp_body(z_ref, ng_ref, nb_ref, ws_ref, bs_ref, o_ref, v_ref, *, tg):
    z = _gelu(z_ref[...])
    u = z[:, :GM_W]
    vr = z[:, GM_W:]
    xc = vr - jnp.mean(vr, axis=-1, keepdims=True)
    v = xc * lax.rsqrt(jnp.mean(xc * xc, axis=-1, keepdims=True) + EPS) * ng_ref[...] + nb_ref[...]
    v_ref[...] = v
    vb = v.astype(BF16)
    for c in range(tg // GM_CHUNK):
        r0 = c * GM_CHUNK
        for g in range(4):
            c0 = g * LANE
            mixed = _dot(ws_ref[g], vb[r0:r0 + GM_CHUNK, c0:c0 + LANE]) + bs_ref[:, c0:c0 + LANE]
            o_ref[r0:r0 + GM_CHUNK, c0:c0 + LANE] = u[r0:r0 + GM_CHUNK, c0:c0 + LANE] * mixed


def gmlp_mix(z, ng, nb, ws_tril, bs_exp):
    b, s, _ = z.shape
    tg = _tile(s, 512)
    return pl.pallas_call(
        functools.partial(_gmlp_body, tg=tg), name="gmlp_mix", grid=(b, s // tg),
        in_specs=[pl.BlockSpec((None, tg, 1024), lambda bi, i: (bi, i, 0)),
                  pl.BlockSpec((1, 512), lambda bi, i: (0, 0)),
                  pl.BlockSpec((1, 512), lambda bi, i: (0, 0)),
                  pl.BlockSpec((4, GM_CHUNK, GM_CHUNK), lambda bi, i: (0, 0, 0)),
                  pl.BlockSpec((GM_CHUNK, 512), lambda bi, i: (0, 0))],
        out_specs=[pl.BlockSpec((None, tg, 512), lambda bi, i: (bi, i, 0)),
                   pl.BlockSpec((None, tg, 512), lambda bi, i: (bi, i, 0))],
        out_shape=[jax.ShapeDtypeStruct((b, s, 512), F32), jax.ShapeDtypeStruct((b, s, 512), F32)],
        compiler_params=_cp(("parallel", "parallel")),
    )(z, ng.reshape(1, 512), nb.reshape(1, 512), ws_tril, bs_exp)


def _finish_body(x_ref, hn_ref, g3_ref, ocmp_ref, osel_ref, owin_ref, opool_ref, ogm_ref, odiff_ref,
                 wgate_ref, bgate_ref, wbr_ref, wo_ref, o_ref):
    hn = hn_ref[...]
    g3 = g3_ref[...]
    onsa = g3[:, 0:512] * ocmp_ref[...] + g3[:, 512:1024] * osel_ref[...] + g3[:, 1024:1536] * owin_ref[...]
    branches = (onsa, opool_ref[...], ogm_ref[...], odiff_ref[...])
    acc = jnp.zeros(o_ref.shape, F32)
    for n in range(N_BRANCH):
        c0 = n * D_MODEL
        gate = jax.nn.sigmoid(_dot(hn, wgate_ref[:, c0:c0 + D_MODEL]) + bgate_ref[:, c0:c0 + D_MODEL])
        acc = acc + gate * _dot(branches[n].astype(BF16), wbr_ref[n])
    o_ref[...] = x_ref[...] + _dot(acc.astype(BF16), wo_ref[...])


def finish_mixer(x, hn, g3, ocmp, osel, owin, opool, ogm, odiff, wgate, bgate, wbr, wo):
    t = x.shape[0]
    tm = _tile(t, 256)
    row = lambda w: pl.BlockSpec((tm, w), lambda i: (i, 0))
    const = lambda shape: pl.BlockSpec(shape, lambda i: (0,) * len(shape), pipeline_mode=pl.Buffered(1))
    return pl.pallas_call(
        _finish_body, name="finish_mixer", grid=(t // tm,),
        in_specs=[row(1024), row(1024), row(1536), row(512), row(512), row(512), row(512), row(512), row(512),
                  const((D_MODEL, N_BRANCH * D_MODEL)), const((1, N_BRANCH * D_MODEL)),
                  const((N_BRANCH, BRANCH_W, D_MODEL)), const((D_MODEL, D_MODEL))],
        out_specs=row(1024),
        out_shape=jax.ShapeDtypeStruct((t, D_MODEL), F32),
        compiler_params=_cp(("parallel",)),
    )(x, hn, g3, ocmp, osel, owin, opool, ogm, odiff, wgate, bgate, wbr, wo)


def _ffn_body(x_ref, g_ref, wg_ref, wu_ref, wd_ref, o_ref, hn_ref, acc_ref, *, nf):
    j = pl.program_id(1)

    @pl.when(j == 0)
    def _init():
        hn_ref[...] = _rms(x_ref[...], g_ref[...]).astype(BF16)
        acc_ref[...] = jnp.zeros(acc_ref.shape, F32)

    h = hn_ref[...]
    a = _dot(h, wg_ref[...])
    act = a * jax.nn.sigmoid(a) * _dot(h, wu_ref[...])
    acc_ref[...] += _dot(act.astype(BF16), wd_ref[...])

    @pl.when(j == nf - 1)
    def _fin():
        o_ref[...] = x_ref[...] + acc_ref[...]


def ffn_swiglu(x, g, wg, wu, wd):
    t, d = x.shape
    f = wg.shape[1]
    tm = _tile(t, 512)
    tf = 1408 if f % 1408 == 0 else _tile(f, 512)
    nf = f // tf
    return pl.pallas_call(
        functools.partial(_ffn_body, nf=nf), name="ffn_swiglu", grid=(t // tm, nf),
        in_specs=[pl.BlockSpec((tm, d), lambda i, j: (i, 0)), pl.BlockSpec((1, d), lambda i, j: (0, 0)),
                  pl.BlockSpec((d, tf), lambda i, j: (0, j)), pl.BlockSpec((d, tf), lambda i, j: (0, j)),
                  pl.BlockSpec((tf, d), lambda i, j: (j, 0))],
        out_specs=pl.BlockSpec((tm, d), lambda i, j: (i, 0)),
        out_shape=jax.ShapeDtypeStruct((t, d), F32),
        scratch_shapes=[pltpu.VMEM((tm, d), BF16), pltpu.VMEM((tm, d), F32)],
        compiler_params=_cp(("parallel", "arbitrary")),
    )(x, g.reshape(1, d), wg.astype(BF16), wu.astype(BF16), wd.astype(BF16))


MOE_SUB = 256


def _moe_body(x_ref, g_ref, r_ref, rb_ref, u_ref, wg_ref, wu_ref, wd_ref, o_ref,
              hn_ref, comb_ref, post_ref, posr_ref, cnt_ref, xe_ref, ye_ref, acc_ref, *, nf):
    e = pl.program_id(1)
    f = pl.program_id(2)
    tm = x_ref.shape[0]
    lane = lax.broadcasted_iota(jnp.int32, (tm, LANE), 1)

    @pl.when((e == 0) & (f == 0))
    def _init():
        hn = _rms(x_ref[...], g_ref[...]).astype(BF16)
        hn_ref[...] = hn
        lg = _dot(hn, r_ref[...]) + rb_ref[...]
        m1 = jnp.max(lg, axis=1, keepdims=True)
        i1 = jnp.min(jnp.where(lg == m1, lane, LANE), axis=1, keepdims=True)
        lg2 = jnp.where(lane == i1, -3e38, lg)
        m2 = jnp.max(lg2, axis=1, keepdims=True)
        i2 = jnp.min(jnp.where(lg2 == m2, lane, LANE), axis=1, keepdims=True)
        e2 = jnp.exp(m2 - m1)
        w1 = 1.0 / (1.0 + e2)
        comb = jnp.where(lane == i1, w1, 0.0) + jnp.where(lane == i2, e2 * w1, 0.0)
        comb_ref[...] = comb
        ind_t = (comb.T > 0.0).astype(F32)
        pos_t = _dot(ind_t.astype(BF16), u_ref[...])
        post_ref[...] = jnp.where(ind_t > 0.0, pos_t, -1.0)
        posr_ref[...] = post_ref[...].T
        cnt = jnp.sum(ind_t, axis=1, keepdims=True)
        for ex in range(N_EXPERTS):
            cnt_ref[ex] = cnt[ex, 0].astype(jnp.int32)
        acc_ref[...] = jnp.zeros(acc_ref.shape, F32)

    nsub = (cnt_ref[e] + (MOE_SUB - 1)) // MOE_SUB

    @pl.when(f == 0)
    def _gather():
        slot_t = post_ref[pl.ds(e, 1), :]
        base = lax.broadcasted_iota(jnp.int32, (MOE_SUB, tm), 0).astype(F32)

        def body(u, carry):
            onehot = jnp.where(slot_t == base + (u * MOE_SUB).astype(F32), 1.0, 0.0).astype(BF16)
            xe_ref[u] = _dot(onehot, hn_ref[...]).astype(BF16)
            ye_ref[u] = jnp.zeros((MOE_SUB, x_ref.shape[1]), F32)
            return carry

        lax.fori_loop(0, nsub, body, 0)

    def expert(u, carry):
        xe = xe_ref[u]
        a = _dot(xe, wg_ref[...])
        act = a * jax.nn.sigmoid(a) * _dot(xe, wu_ref[...])
        ye_ref[u] += _dot(act.astype(BF16), wd_ref[...])
        return carry

    lax.fori_loop(0, nsub, expert, 0)

    @pl.when(f == nf - 1)
    def _scatter():
        slot_r = jnp.sum(jnp.where(lane == e, posr_ref[...], 0.0), axis=1, keepdims=True)
        w_r = jnp.sum(jnp.where(lane == e, comb_ref[...], 0.0), axis=1, keepdims=True)
        base = lax.broadcasted_iota(jnp.int32, (tm, MOE_SUB), 1).astype(F32)

        def body(u, carry):
            onehot = jnp.where(slot_r == base + (u * MOE_SUB).astype(F32), 1.0, 0.0).astype(BF16)
            y = ye_ref[u]
            y_hi = y.astype(BF16)
            y_lo = (y - y_hi.astype(F32)).astype(BF16)
            acc_ref[...] += w_r * (_dot(onehot, y_hi) + _dot(onehot, y_lo))
            return carry

        lax.fori_loop(0, nsub, body, 0)

    @pl.when((e == N_EXPERTS - 1) & (f == nf - 1))
    def _fin():
        o_ref[...] = x_ref[...] + acc_ref[...]


def moe_swiglu(x, g, router, router_b, wg, wu, wd):
    t, d = x.shape
    f = wg.shape[2]
    tm = _tile(t, 1024)
    tf = 896 if f % 896 == 0 else _tile(f, 512)
    nf = f // tf
    nsubmax = -(-tm // MOE_SUB)
    rpad = jnp.zeros((d, LANE), F32).at[:, :N_EXPERTS].set(router).astype(BF16)
    rbpad = jnp.full((1, LANE), NEG, F32).at[0, :N_EXPERTS].set(router_b)
    before = jnp.asarray(np.triu(np.ones((tm, tm), np.float32), 1), BF16)
    const = lambda shape: pl.BlockSpec(shape, lambda i, e, j: (0,) * len(shape))
    return pl.pallas_call(
        functools.partial(_moe_body, nf=nf), name="moe_swiglu", grid=(t // tm, N_EXPERTS, nf),
        in_specs=[pl.BlockSpec((tm, d), lambda i, e, j: (i, 0)), const((1, d)), const((d, LANE)), const((1, LANE)),
                  const((tm, tm)),
                  pl.BlockSpec((None, d, tf), lambda i, e, j: (e, 0, j)),
                  pl.BlockSpec((None, d, tf), lambda i, e, j: (e, 0, j)),
                  pl.BlockSpec((None, tf, d), lambda i, e, j: (e, j, 0))],
        out_specs=pl.BlockSpec((tm, d), lambda i, e, j: (i, 0)),
        out_shape=jax.ShapeDtypeStruct((t, d), F32),
        scratch_shapes=[pltpu.VMEM((tm, d), BF16), pltpu.VMEM((tm, LANE), F32), pltpu.VMEM((LANE, tm), F32),
                        pltpu.VMEM((tm, LANE), F32), pltpu.SMEM((N_EXPERTS,), jnp.int32),
                        pltpu.VMEM((nsubmax, MOE_SUB, d), BF16), pltpu.VMEM((nsubmax, MOE_SUB, d), F32),
                        pltpu.VMEM((tm, d), F32)],
        compiler_params=_cp(("parallel", "arbitrary", "arbitrary")),
    )(x, g.reshape(1, d), rpad, rbpad, before, wg.astype(BF16), wu.astype(BF16), wd.astype(BF16))


def _ple_body(x_ref, p_ref, g_ref, wg_ref, wp_ref, fg_ref, o_ref, *, final):
    x = x_ref[...]
    hn = _rms(x, g_ref[...]).astype(BF16)
    gate = jax.nn.sigmoid(_dot(hn, wg_ref[...]))
    y = x + gate * _dot(p_ref[...].astype(BF16), wp_ref[...])
    if final:
        y = _rms(y, fg_ref[...])
    o_ref[...] = y


def ple(x, p, g, wg, wp, fg, final):
    t, d = x.shape
    pd = p.shape[1]
    tm = _tile(t, 512)
    return pl.pallas_call(
        functools.partial(_ple_body, final=final), name="ple", grid=(t // tm,),
        in_specs=[pl.BlockSpec((tm, d), lambda i: (i, 0)), pl.BlockSpec((tm, pd), lambda i: (i, 0)),
                  pl.BlockSpec((1, d), lambda i: (0, 0)), pl.BlockSpec((d, d), lambda i: (0, 0)),
                  pl.BlockSpec((pd, d), lambda i: (0, 0)), pl.BlockSpec((1, d), lambda i: (0, 0))],
        out_specs=pl.BlockSpec((tm, d), lambda i: (i, 0)),
        out_shape=jax.ShapeDtypeStruct((t, d), F32),
        compiler_params=_cp(("parallel",)),
    )(x, p, g.reshape(1, d), wg.astype(BF16), wp.astype(BF16), fg.reshape(1, d))


def _page_copies(pt_ref, cache_ref, bufs, sem, slot, b, c, *, layer, pc, cols):
    out = []
    for p in range(pc):
        pg = pt_ref[b, c * pc + p]
        for buf, (f0, w) in zip(bufs, cols):
            out.append(pltpu.make_async_copy(cache_ref.at[layer, pg, pl.ds(f0, w), :], buf.at[slot, p],
                                             sem.at[slot]))
    return out


def _paged_pipeline(pt_ref, cache_ref, bufs, sem, *, nb, nc, **kw):
    b = pl.program_id(0)
    c = pl.program_id(1) if nc > 1 else 0
    step = b * nc + c
    slot = step % 2

    @pl.when(step == 0)
    def _first():
        for d in _page_copies(pt_ref, cache_ref, bufs, sem, 0, 0, 0, **kw):
            d.start()

    @pl.when(step + 1 < nb * nc)
    def _prefetch():
        nxt = step + 1
        for d in _page_copies(pt_ref, cache_ref, bufs, sem, 1 - slot, nxt // nc, nxt % nc, **kw):
            d.start()

    for d in _page_copies(pt_ref, cache_ref, bufs, sem, slot, b, c, **kw):
        d.wait()
    return slot


def _compress_paged_body(pt_ref, cache_ref, wc_ref, bias_ref, w2_ref, o_ref, bufk, bufv, sem, rowk, rowv, *,
                         layer, npg, nb, nch):
    slot = _paged_pipeline(pt_ref, cache_ref, (bufk, bufv), sem, nb=nb, nc=1, layer=layer, pc=npg,
                           cols=((0, LANE), (LANE, LANE)))

    def to_rows(p, carry):
        rows = pl.ds(pl.multiple_of(p * PAGE, PAGE), PAGE)
        rowk[rows, :] = bufk[slot, p].T
        rowv[rows, :] = bufv[slot, p].T
        return carry

    lax.fori_loop(0, npg, to_rows, 0, unroll=4)
    _compress_core(lambda s: rowk[pl.ds(s, nch, stride=CMP_STRIDE), :],
                   lambda s: rowv[pl.ds(s, nch, stride=CMP_STRIDE), :], wc_ref, bias_ref, w2_ref, o_ref, nch)


def compress_paged(page_table, cache, layer, wc, bias, w2):
    b, npg = page_table.shape
    past = npg * PAGE
    nch = past // CMP_STRIDE
    const = lambda shape: pl.BlockSpec(shape, lambda i, pt: (0,) * len(shape))
    return pl.pallas_call(
        functools.partial(_compress_paged_body, layer=layer, npg=npg, nb=b, nch=nch), name="compress_paged",
        grid_spec=pltpu.PrefetchScalarGridSpec(
            num_scalar_prefetch=1, grid=(b,),
            in_specs=[pl.BlockSpec(memory_space=pl.ANY), const((CMP_STRIDE * 256, 512)), const((1, 256)),
                      const((256, 256))],
            out_specs=pl.BlockSpec((None, nch, 256), lambda i, pt: (i, 0, 0)),
            scratch_shapes=[pltpu.VMEM((2, npg, LANE, PAGE), F32), pltpu.VMEM((2, npg, LANE, PAGE), F32),
                            pltpu.SemaphoreType.DMA((2,)),
                            pltpu.VMEM((past, LANE), F32), pltpu.VMEM((past, LANE), F32)]),
        out_shape=jax.ShapeDtypeStruct((b, nch, 256), BF16),
        compiler_params=_cp(("arbitrary",)),
    )(page_table, cache, wc, bias, w2)


def _new_tile(new, c0, tq):
    return jnp.concatenate([new[:, c0:c0 + LANE], jnp.zeros((LANE - tq, LANE), F32)], axis=0).astype(BF16)


def _sel_paged_body(pt_ref, cache_ref, q_ref, new_ref, sel_ref, e_ref, o_ref, bufk, bufv, sem, *,
                    layer, npg, nb, tq, tk, nsp):
    slot = _paged_pipeline(pt_ref, cache_ref, (bufk, bufv), sem, nb=nb, nc=1, layer=layer, pc=npg,
                           cols=((2 * LANE, LANE), (3 * LANE, LANE)))
    past = npg * PAGE
    ppc = tk // PAGE
    r = 8 * tq
    qs = _stack_heads(q_ref[...], tq).astype(BF16)
    sel = sel_ref[...].astype(BF16)
    m = jnp.full((r, LANE), NEG, F32)
    l = jnp.zeros((r, LANE), F32)
    acc = jnp.zeros((r, LANE), F32)

    def stacked_bias(oks):
        return jnp.concatenate([jnp.where(ok, 0.0, NEG) for ok in oks for _ in range(NSA_J)], axis=0)

    for c in range(npg // ppc):
        kt = jnp.concatenate([bufk[slot, c * ppc + u] for u in range(ppc)], axis=1).astype(BF16)
        vt = jnp.concatenate([bufv[slot, c * ppc + u] for u in range(ppc)], axis=1).astype(BF16)
        bias = stacked_bias(_sel_ok(sel, e_ref[:, c * tk:(c + 1) * tk], nsp))
        m, l, acc = _flash_update(qs, kt, vt, bias, m, l, acc, kv_t=True)
    new = new_ref[...]
    causal = (lax.broadcasted_iota(jnp.int32, (tq, LANE), 1) <= lax.broadcasted_iota(jnp.int32, (tq, LANE), 0))
    bias = stacked_bias([ok & causal for ok in _sel_ok(sel, e_ref[:, past:past + LANE], nsp)])
    m, l, acc = _flash_update(qs, _new_tile(new, 2 * LANE, tq), _new_tile(new, 3 * LANE, tq), bias, m, l, acc)
    o_ref[...] = _unstack_heads(acc * (1.0 / jnp.where(l > 0.0, l, 1.0)), tq)


def sel_paged(page_table, cache, layer, q, new_rows, sel, emat):
    b, npg = page_table.shape
    past = npg * PAGE
    tq = q.shape[1]
    nsp = sel.shape[2] // 2
    return pl.pallas_call(
        functools.partial(_sel_paged_body, layer=layer, npg=npg, nb=b, tq=tq, tk=_tile(past, 512), nsp=nsp),
        name="sel_paged",
        grid_spec=pltpu.PrefetchScalarGridSpec(
            num_scalar_prefetch=1, grid=(b,),
            in_specs=[pl.BlockSpec(memory_space=pl.ANY),
                      pl.BlockSpec((None, tq, 512), lambda i, pt: (i, 0, 0)),
                      pl.BlockSpec((None, tq, 512), lambda i, pt: (i, 0, 0)),
                      pl.BlockSpec((None, tq, 2 * nsp), lambda i, pt: (i, 0, 0)),
                      pl.BlockSpec((nsp, past + LANE), lambda i, pt: (0, 0))],
            out_specs=pl.BlockSpec((None, tq, 512), lambda i, pt: (i, 0, 0)),
            scratch_shapes=[pltpu.VMEM((2, npg, LANE, PAGE), F32), pltpu.VMEM((2, npg, LANE, PAGE), F32),
                            pltpu.SemaphoreType.DMA((2,))]),
        out_shape=jax.ShapeDtypeStruct((b, tq, 512), F32),
        compiler_params=_cp(("arbitrary",)),
    )(page_table, cache, q, new_rows, sel, emat)


def _diff_paged_body(pt_ref, cache_ref, lam_ref, q_ref, new_ref, g_ref, o_ref, bufk, bufv, sem,
                     qs_ref, m_ref, l_ref, acc_ref, *, layer, nb, nc, pc, tq, lam_init):
    c = pl.program_id(1)
    slot = _paged_pipeline(pt_ref, cache_ref, (bufk, bufv), sem, nb=nb, nc=nc, layer=layer, pc=pc,
                           cols=((0, 512), (512, 512)))
    ppc = _tile(pc, 4)

    @pl.when(c == 0)
    def _init():
        _diff_init(q_ref, qs_ref, m_ref, l_ref, acc_ref, tq)

    def pages_t(buf, hp, p0):
        return jnp.concatenate([buf[slot, p0 + u, hp * LANE:(hp + 1) * LANE, :] for u in range(ppc)],
                               axis=1).astype(BF16)

    for p0 in range(0, pc, ppc):
        _diff_step(lambda hp: pages_t(bufk, hp, p0), lambda hp: pages_t(bufv, hp, p0), None,
                   qs_ref, m_ref, l_ref, acc_ref, tq, kv_t=True)

    @pl.when(c == nc - 1)
    def _fin():
        new = new_ref[...]
        causal = (lax.broadcasted_iota(jnp.int32, (tq, LANE), 1) <= lax.broadcasted_iota(jnp.int32, (tq, LANE), 0))
        _diff_step(lambda hp: _new_tile(new, hp * LANE, tq), lambda hp: _new_tile(new, 512 + hp * LANE, tq),
                   jnp.where(causal, 0.0, NEG), qs_ref, m_ref, l_ref, acc_ref, tq)
        _diff_fin(lam_ref, g_ref, o_ref, l_ref, acc_ref, tq, lam_init)


def diff_paged(page_table, cache, layer, lam_par, q, new_rows, gnorm, lam_init):
    b, npg = page_table.shape
    tq = q.shape[1]
    pc = _tile(npg, 16)
    nc = npg // pc
    r = 4 * tq
    npair = DIFF_HEADS // 2
    return pl.pallas_call(
        functools.partial(_diff_paged_body, layer=layer, nb=b, nc=nc, pc=pc, tq=tq, lam_init=lam_init),
        name="diff_paged",
        grid_spec=pltpu.PrefetchScalarGridSpec(
            num_scalar_prefetch=1, grid=(b, nc),
            in_specs=[pl.BlockSpec(memory_space=pl.ANY),
                      pl.BlockSpec((4, DIFF_QK), lambda i, c, pt: (0, 0)),
                      pl.BlockSpec((None, tq, 512), lambda i, c, pt: (i, 0, 0)),
                      pl.BlockSpec((None, tq, 1024), lambda i, c, pt: (i, 0, 0)),
                      pl.BlockSpec((1, LANE), lambda i, c, pt: (0, 0))],
            out_specs=pl.BlockSpec((None, tq, 512), lambda i, c, pt: (i, 0, 0)),
            scratch_shapes=[pltpu.VMEM((2, pc, 512, PAGE), F32), pltpu.VMEM((2, pc, 512, PAGE), F32),
                            pltpu.SemaphoreType.DMA((2,)),
                            pltpu.VMEM((npair, r, LANE), BF16), pltpu.VMEM((npair, r, LANE), F32),
                            pltpu.VMEM((npair, r, LANE), F32), pltpu.VMEM((npair, r, LANE), F32)]),
        out_shape=jax.ShapeDtypeStruct((b, tq, 512), F32),
        compiler_params=_cp(("arbitrary", "arbitrary")),
    )(page_table, cache, lam_par, q, new_rows, gnorm)


def _q_perm():
    idx = np.zeros(512, np.int32)
    for j in range(NSA_J):
        for g in range(NSA_G):
            for d in range(DH):
                idx[j * LANE + g * DH + d] = (g * NSA_J + j) * DH + d
    return idx


def _gate3_perm():
    idx = np.zeros(3 * 512, np.int32)
    for c in range(3):
        for j in range(NSA_J):
            for g in range(NSA_G):
                idx[c * 512 + j * LANE + g * DH:c * 512 + j * LANE + (g + 1) * DH] = (g * NSA_J + j) * 3 + c
    return idx


def _cmp_to_sel(n_cmp, n_sel, nch, nsp):
    r = SEL_BLOCK // CMP_STRIDE
    k = np.arange(n_cmp)[:, None] - r * np.arange(n_sel)[None, :]
    m = sum(((k + n >= 0) & (k + n < r)).astype(np.float32) for n in range(CMP_BLOCK // CMP_STRIDE))
    out = np.zeros((nch, nsp), np.float32)
    out[:n_cmp, :n_sel] = m
    return out


def _block_expand(nsp, lk):
    return (np.arange(lk)[None, :] // SEL_BLOCK == np.arange(nsp)[:, None]).astype(np.float32)


def _cmp_weights(pe, w1, w2):
    eye = jnp.eye(2, dtype=F32)
    w6 = w1.reshape(2, NSA_G, 2, CMP_STRIDE, DH, DH)
    wc = jnp.einsum("kgxsdh,kK,gG->skgdxKGh", w6, eye, eye).reshape(CMP_STRIDE * 256, 512).astype(BF16)
    w2b = jnp.einsum("kghd,kK,gG->kghKGd", w2, eye, eye).reshape(256, 256).astype(BF16)
    pe_rows = jnp.transpose(pe, (0, 2, 1, 3)).reshape(4, CMP_BLOCK * DH)
    xb = jnp.einsum("rc,rR->rRc", pe_rows, jnp.eye(4, dtype=F32)).reshape(4, 4 * CMP_BLOCK * DH)
    xb = jnp.zeros((16, 4 * CMP_BLOCK * DH), F32).at[:4].set(xb)
    wb = jnp.zeros((4 * CMP_BLOCK * DH, LANE), F32).at[:, :DH].set(w1.reshape(4 * CMP_BLOCK * DH, DH))
    bias = mm(xb, wb)[:4, :DH].reshape(1, 256)
    return wc, bias, w2b


def _mixer(x, lw, lam_init, *, b, sq, sample=None):
    t = b * sq
    w_in = lw["w_in"]
    qperm = _q_perm()
    segs = ((512, DH ** -0.5, None), (512, None, None), (256, None, None), (1536, None, "sigmoid"),
            (512, None, None), (1024, None, None), (512, DIFF_QK ** -0.5, None), (1024, None, None))
    w_all = jnp.concatenate([w_in[:, OFF_Q:OFF_KV][:, qperm], w_in[:, OFF_KV:OFF_GATE],
                             w_in[:, OFF_GATE:OFF_POOL][:, _gate3_perm()], w_in[:, OFF_POOL:OFF_END]],
                            axis=1).astype(BF16)
    hn, q, rows4, winkv, g3, zpool, zgm, dq, dkv = in_proj(x, lw["norm_mix_g"], w_all, segs)
    q, rows4, winkv = q.reshape(b, sq, 512), rows4.reshape(b, sq, 512), winkv.reshape(b, sq, 256)
    zpool, zgm = zpool.reshape(b, sq, 512), zgm.reshape(b, sq, 1024)
    dq, dkv = dq.reshape(b, sq, 512), dkv.reshape(b, sq, 1024)

    wc, cbias, w2b = _cmp_weights(lw["cmp_pe"], lw["cmp_w1"], lw["cmp_w2"])
    lam_par = jnp.stack([lw["diff_lq1"], lw["diff_lk1"], lw["diff_lq2"], lw["diff_lk2"]])
    gnorm = jnp.tile(lw["diff_norm_g"], 2).reshape(1, LANE)

    if sample is None:
        past, l_tot = 0, sq
        nch = sq // CMP_STRIDE
        tq = _tile(sq, 256)
        tk_sel = _tile(sq, 512)
        nk_sel = sq // tk_sel
        tw = _tile(sq, 256)
        nband = WINDOW // tw + 1
        win_all, kpos0_w, nk_w = winkv, 0, nband
        win_kt = lambda i, j: (jnp.maximum(i - (nband - 1) + j, 0), i - (nband - 1) + j >= 0)
        tq_w = tw
        free_w = nband - 2 if nband == 3 else None
        tq_d = _tile(sq, 256)
        tk_d = _tile(sq, 512)
        nk_d = sq // tk_d
        xcat = jnp.concatenate([jnp.zeros((b, 16, 512), F32), zpool], axis=1)
        zgm_in = zgm
    else:
        layer = sample["layer"]
        pt = sample["page_table"]
        past = pt.shape[1] * PAGE
        l_tot = past + sq
        nch = past // CMP_STRIDE
        tq = sq
        wbuf = sample["win"].shape[1]
        wrows = -(-(wbuf + sq) // KEY_SUB) * KEY_SUB
        tw = wrows
        win_all = jnp.concatenate([sample["win"], winkv, jnp.zeros((b, wrows - wbuf - sq, 256), F32)], axis=1)
        kpos0_w, nk_w = past - wbuf, 1
        win_kt = lambda i, j: (j, j >= 0)
        tq_w = sq
        free_w = None
        xcat = jnp.concatenate([jnp.zeros((b, 1, 512), F32), sample["pool"], zpool], axis=1)
        zgm_in = jnp.concatenate([zgm, jnp.zeros((b, GM_CHUNK - sq, 1024), F32)], axis=1)

    n_cmp = l_tot // CMP_STRIDE - 1
    n_sel = -(-l_tot // SEL_BLOCK)
    nsp = -(-n_sel // LANE) * LANE
    msel = jnp.asarray(_cmp_to_sel(n_cmp, n_sel, nch, nsp), BF16)
    if sample is None:
        kcvc = compress(rows4, wc, cbias, w2b, nch)
        ocmp, selmask = cmp_attend(q, kcvc, msel, tq=tq, n_cmp=n_cmp, n_sel=n_sel, qpos0=0)
        emat = jnp.asarray(_block_expand(nsp, sq), BF16)

        def sel_kt(i, j):
            last = ((i + 1) * tq - 1) // tk_sel
            return jnp.minimum(j, last), j <= last

        osel = nsa_flash(q, rows4, 2, 3, mode="sel", tq=tq, tk=tk_sel, nk=nk_sel, qpos0=0, kpos0=0,
                         kt_fn=sel_kt, sel=selmask, emat=emat)
    else:
        kcvc = compress_paged(pt, sample["cache_nsa"], layer, wc, cbias, w2b)
        ocmp, selmask = cmp_attend(q, kcvc, msel, tq=tq, n_cmp=n_cmp, n_sel=n_sel, qpos0=past)
        emat = jnp.asarray(_block_expand(nsp, past + LANE), BF16)
        osel = sel_paged(pt, sample["cache_nsa"], layer, q, rows4, selmask, emat)
    owin = nsa_flash(q, win_all, 0, 1, mode="win", tq=tq_w, tk=tw, nk=nk_w, qpos0=past, kpos0=kpos0_w,
                     kt_fn=win_kt, free_step=free_w)

    opool = pool_mix(xcat, lw["pool_w"], lw["pool_scale"], past)
    ws_tril = jnp.tril(lw["gm_ws"]).astype(BF16)
    bs_exp = jnp.repeat(lw["gm_bs"].T, LANE, axis=1)
    ogm, gm_v = gmlp_mix(zgm_in, lw["gm_ng"], lw["gm_nb"], ws_tril, bs_exp)

    if sample is None:
        def diff_kt(i, j):
            last = ((i + 1) * tq_d - 1) // tk_d
            return jnp.minimum(j, last), j <= last

        odiff = diff_attend(lam_par, dq, dkv, gnorm, tq=tq_d, tk=tk_d, nk=nk_d, qpos0=0,
                            lam_init=lam_init, kt_fn=diff_kt)
    else:
        odiff = diff_paged(pt, sample["cache_diff"], layer, lam_par, dq, dkv, gnorm, lam_init)
        ogm = ogm[:, :sq]
        gm_v = gm_v[:, :sq]
    wbr = lw["w_branch"].at[0].set(lw["w_branch"][0][qperm]).astype(BF16)
    out = finish_mixer(x, hn, g3, ocmp.reshape(t, 512), osel.reshape(t, 512), owin.reshape(t, 512),
                       opool.reshape(t, 512), ogm.reshape(t, 512), odiff.reshape(t, 512),
                       lw["w_gate"].astype(BF16), lw["b_gate"].reshape(1, -1), wbr, lw["w_o"].astype(BF16))
    states = dict(rows4=rows4, winkv=winkv, win_all=win_all, xcat=xcat, gm_v=gm_v, dkv=dkv)
    return out, states


def kernel(x_prompt, x_sample, cache_nsa, cache_diff, state_nsa_win, state_pool, page_table, p_prompt, p_sample, norm_mix_g, w_in, nsa_cmp_pe, nsa_cmp_w1, nsa_cmp_w2, pool_w, pool_scale, gm_norm_g, gm_norm_b, gm_ws, gm_bs, diff_lq1, diff_lk1, diff_lq2, diff_lk2, diff_norm_g, w_branch, w_gate, b_gate, w_o, norm_ffn_g, ffn_w_gate, ffn_w_up, ffn_w_down, moe_router, moe_router_b, moe_w_gate, moe_w_up, moe_w_down, ple_norm_g, ple_w_gate, ple_w_proj, final_norm_g):
    bp, sp, d = x_prompt.shape
    bs, ss, _ = x_sample.shape
    depth = w_in.shape[0]
    n_phys = cache_nsa.shape[1]
    wbuf = state_nsa_win.shape[2]
    cache_nsa2 = jnp.transpose(cache_nsa, (0, 1, 3, 4, 5, 2)).reshape(depth, n_phys, 512, PAGE)
    cache_diff2 = jnp.transpose(cache_diff, (0, 1, 3, 4, 5, 2)).reshape(depth, n_phys, 1024, PAGE)
    xp = x_prompt.reshape(bp * sp, d)
    xs = x_sample.reshape(bs * ss, d)
    outs = {k: [] for k in ("nsa_p", "nsa_s", "win_p", "win_s", "pool_p", "pool_s", "gmv_s", "diff_p", "diff_s")}
    for l in range(depth):
        lw = dict(norm_mix_g=norm_mix_g[l], w_in=w_in[l], cmp_pe=nsa_cmp_pe[l], cmp_w1=nsa_cmp_w1[l],
                  cmp_w2=nsa_cmp_w2[l], pool_w=pool_w[l], pool_scale=pool_scale[l], gm_ng=gm_norm_g[l],
                  gm_nb=gm_norm_b[l], gm_ws=gm_ws[l], gm_bs=gm_bs[l], diff_lq1=diff_lq1[l], diff_lk1=diff_lk1[l],
                  diff_lq2=diff_lq2[l], diff_lk2=diff_lk2[l], diff_norm_g=diff_norm_g[l], w_branch=w_branch[l],
                  w_gate=w_gate[l], b_gate=b_gate[l], w_o=w_o[l])
        lam_init = 0.8 - 0.6 * math.exp(-0.3 * l)
        xp, st_p = _mixer(xp, lw, lam_init, b=bp, sq=sp)
        sample = dict(cache_nsa=cache_nsa2, cache_diff=cache_diff2, win=state_nsa_win[l].reshape(bs, wbuf, 256),
                      pool=state_pool[l], page_table=page_table, layer=l)
        xs, st_s = _mixer(xs, lw, lam_init, b=bs, sq=ss, sample=sample)
        outs["nsa_p"].append(st_p["rows4"].reshape(bp, sp, 4, NSA_G, DH))
        outs["nsa_s"].append(st_s["rows4"].reshape(bs, ss, 4, NSA_G, DH))
        wkeep = min(WINDOW, sp)
        outs["win_p"].append(st_p["winkv"][:, sp - wkeep:].reshape(bp, wkeep, 2, NSA_G, DH))
        outs["win_s"].append(st_s["win_all"][:, ss:ss + wbuf].reshape(bs, wbuf, 2, NSA_G, DH))
        outs["pool_p"].append(st_p["xcat"][:, -POOL_MEM:])
        outs["pool_s"].append(st_s["xcat"][:, -POOL_MEM:])
        outs["gmv_s"].append(st_s["gm_v"])
        outs["diff_p"].append(st_p["dkv"].reshape(bp, sp, 2, DIFF_HEADS, 2 * DIFF_QK))
        outs["diff_s"].append(st_s["dkv"].reshape(bs, ss, 2, DIFF_HEADS, 2 * DIFF_QK))
        i = l // 2
        if l % 2 == 0:
            xp = ffn_swiglu(xp, norm_ffn_g[l], ffn_w_gate[i], ffn_w_up[i], ffn_w_down[i])
            xs = ffn_swiglu(xs, norm_ffn_g[l], ffn_w_gate[i], ffn_w_up[i], ffn_w_down[i])
        else:
            xp = moe_swiglu(xp, norm_ffn_g[l], moe_router[i], moe_router_b[i], moe_w_gate[i], moe_w_up[i], moe_w_down[i])
            xs = moe_swiglu(xs, norm_ffn_g[l], moe_router[i], moe_router_b[i], moe_w_gate[i], moe_w_up[i], moe_w_down[i])
        final = l == depth - 1
        xp = ple(xp, p_prompt[l].reshape(bp * sp, -1), ple_norm_g[l], ple_w_gate[l], ple_w_proj[l], final_norm_g, final)
        xs = ple(xs, p_sample[l].reshape(bs * ss, -1), ple_norm_g[l], ple_w_gate[l], ple_w_proj[l], final_norm_g, final)
    st = lambda k: jnp.stack(outs[k])
    return (xp.reshape(bp, sp, d), xs.reshape(bs, ss, d), st("nsa_p"), st("nsa_s"), st("win_p"), st("win_s"),
            st("pool_p"), st("pool_s"), st("gmv_s"), st("diff_p"), st("diff_s"))
```

```python
import functools
import math

import numpy as np
import jax
import jax.numpy as jnp
from jax import lax
from jax.experimental import pallas as pl
from jax.experimental.pallas import tpu as pltpu

F32 = jnp.float32
BF16 = jnp.bfloat16

D_MODEL = 1024
PAGE = 128
NSA_HEADS = 8
NSA_G = 2
NSA_J = 4
DH = 64
CMP_BLOCK = 32
CMP_STRIDE = 16
SEL_BLOCK = 64
SEL_TOPK = 16
FORCE_BONUS = 100.0
WINDOW = 512
POOL_WINDOWS = (2, 4, 8, 16)
POOL_MEM = 15
GM_W = 512
GM_CHUNK = 128
DIFF_HEADS = 8
DIFF_QK = 32
N_BRANCH = 4
BRANCH_W = 512
N_EXPERTS = 8
EPS = 1e-6
NEG = -1e30
LOG2E = 1.4426950408889634
LANE = 128
VMEM_LIMIT = 56 * 1024 * 1024

OFF_Q, OFF_KV, OFF_GATE, OFF_POOL, OFF_GM, OFF_DQ, OFF_DKV, OFF_END = 0, 512, 1280, 1304, 1816, 2840, 3352, 4376


def _tile(n, pref):
    t = min(n, pref)
    while n % t:
        t //= 2
    return t


def _cp(sem, vmem=VMEM_LIMIT):
    return pltpu.CompilerParams(dimension_semantics=sem, vmem_limit_bytes=vmem)


def _gelu(x):
    return 0.5 * x * (1.0 + jnp.tanh(0.7978845608028654 * (x + 0.044715 * (x * x * x))))


def _rms(x, g):
    return x * lax.rsqrt(jnp.mean(x * x, axis=-1, keepdims=True) + EPS) * g


def _dot(a, b):
    return jnp.dot(a, b, preferred_element_type=F32)


def _dot_nt(a, b):
    return lax.dot_general(a, b, (((1,), (1,)), ((), ())), preferred_element_type=F32)


def _rmsnorm_body(x_ref, g_ref, o_ref):
    o_ref[...] = _rms(x_ref[...], g_ref[...]).astype(o_ref.dtype)


def rmsnorm(x, g, out_dtype):
    t, d = x.shape
    tm = _tile(t, 512)
    return pl.pallas_call(
        _rmsnorm_body, name="rmsnorm", grid=(t // tm,),
        in_specs=[pl.BlockSpec((tm, d), lambda i: (i, 0)), pl.BlockSpec((1, d), lambda i: (0, 0))],
        out_specs=pl.BlockSpec((tm, d), lambda i: (i, 0)),
        out_shape=jax.ShapeDtypeStruct((t, d), out_dtype),
        compiler_params=_cp(("parallel",)),
    )(x, g.reshape(1, d))


def _mm_body(x_ref, w_ref, o_ref, *, scale, act):
    y = _dot(x_ref[...].astype(BF16), w_ref[...])
    if scale is not None:
        y = y * scale
    if act == "sigmoid":
        y = jax.nn.sigmoid(y)
    o_ref[...] = y.astype(o_ref.dtype)


def mm(x, w, *, scale=None, act=None, out_dtype=F32):
    t, k = x.shape
    n = w.shape[1]
    tm = _tile(t, 512)
    tn = _tile(n, 512)
    return pl.pallas_call(
        functools.partial(_mm_body, scale=scale, act=act), name="mm", grid=(t // tm, n // tn),
        in_specs=[pl.BlockSpec((tm, k), lambda i, j: (i, 0)), pl.BlockSpec((k, tn), lambda i, j: (0, j))],
        out_specs=pl.BlockSpec((tm, tn), lambda i, j: (i, j)),
        out_shape=jax.ShapeDtypeStruct((t, n), out_dtype),
        compiler_params=_cp(("parallel", "parallel")),
    )(x, w.astype(BF16))


def _in_proj_body(x_ref, g_ref, w_ref, hn_ref, *o_refs, segs):
    hn = _rms(x_ref[...], g_ref[...]).astype(BF16)
    hn_ref[...] = hn
    c0 = 0
    for (n, scale, act), o_ref in zip(segs, o_refs):
        y = _dot(hn, w_ref[:, c0:c0 + n])
        if scale is not None:
            y = y * scale
        if act == "sigmoid":
            y = jax.nn.sigmoid(y)
        o_ref[...] = y
        c0 += n


def in_proj(x, g, w_all, segs):
    t, d = x.shape
    tm = _tile(t, 256)
    ntot = w_all.shape[1]
    return pl.pallas_call(
        functools.partial(_in_proj_body, segs=segs), name="in_proj", grid=(t // tm,),
        in_specs=[pl.BlockSpec((tm, d), lambda i: (i, 0)), pl.BlockSpec((1, d), lambda i: (0, 0)),
                  pl.BlockSpec((d, ntot), lambda i: (0, 0), pipeline_mode=pl.Buffered(1))],
        out_specs=[pl.BlockSpec((tm, d), lambda i: (i, 0))] + [pl.BlockSpec((tm, n), lambda i: (i, 0)) for n, _, _ in segs],
        out_shape=[jax.ShapeDtypeStruct((t, d), BF16)] + [jax.ShapeDtypeStruct((t, n), F32) for n, _, _ in segs],
        compiler_params=_cp(("parallel",)),
    )(x, g.reshape(1, d), w_all)


def _compress_body(xk_ref, xv_ref, wc_ref, bias_ref, w2_ref, o_ref, *, nch):
    _compress_core(lambda s: xk_ref[pl.ds(s, nch, stride=CMP_STRIDE), :],
                   lambda s: xv_ref[pl.ds(s, nch, stride=CMP_STRIDE), :], wc_ref, bias_ref, w2_ref, o_ref, nch)


def _compress_core(load_k, load_v, wc_ref, bias_ref, w2_ref, o_ref, nch):
    xs = jnp.concatenate([part(s) for s in range(CMP_STRIDE) for part in (load_k, load_v)], axis=1).astype(BF16)
    acc = _dot(xs, wc_ref[...])
    hi_next = pltpu.roll(acc[:, 256:], nch - 1, 0)
    hid = _gelu(acc[:, :256] + hi_next + bias_ref[...])
    o_ref[...] = _dot(hid.astype(BF16), w2_ref[...]).astype(o_ref.dtype)


def compress(rows, wc, bias, w2, nch):
    b = rows.shape[0]
    return pl.pallas_call(
        functools.partial(_compress_body, nch=nch), name="compress", grid=(b,),
        in_specs=[pl.BlockSpec((None, nch * CMP_STRIDE, LANE), lambda i: (i, 0, 0)),
                  pl.BlockSpec((None, nch * CMP_STRIDE, LANE), lambda i: (i, 0, 1)),
                  pl.BlockSpec((CMP_STRIDE * 256, 512), lambda i: (0, 0)),
                  pl.BlockSpec((1, 256), lambda i: (0, 0)),
                  pl.BlockSpec((256, 256), lambda i: (0, 0))],
        out_specs=pl.BlockSpec((None, nch, 256), lambda i: (i, 0, 0)),
        out_shape=jax.ShapeDtypeStruct((b, nch, 256), BF16),
        compiler_params=_cp(("parallel",)),
    )(rows, rows, wc, bias, w2)


def _stack_heads(q, tq):
    lane = lax.broadcasted_iota(jnp.int32, (tq, LANE), 1)
    parts = []
    for g in range(NSA_G):
        keep = (lane < DH) if g == 0 else (lane >= DH)
        for j in range(NSA_J):
            parts.append(jnp.where(keep, q[:, j * LANE:(j + 1) * LANE], 0.0))
    return jnp.concatenate(parts, axis=0)


def _unstack_heads(o, tq):
    lane = lax.broadcasted_iota(jnp.int32, (tq, LANE), 1)
    outs = []
    for j in range(NSA_J):
        outs.append(jnp.where(lane < DH, o[j * tq:(j + 1) * tq], o[(NSA_J + j) * tq:(NSA_J + j + 1) * tq]))
    return jnp.concatenate(outs, axis=1)


def _cmp_body(q_ref, kv_ref, msel_ref, o_ref, sel_ref, *, tq, nch, n_cmp, n_sel, nsp, qpos0):
    i = pl.program_id(1)
    r = 8 * tq
    qs = _stack_heads(q_ref[...], tq).astype(BF16)
    kc = kv_ref[:, 0:LANE]
    vc = kv_ref[:, LANE:2 * LANE]
    s = _dot_nt(qs, kc)
    row = lax.broadcasted_iota(jnp.int32, (r, nch), 0)
    col = lax.broadcasted_iota(jnp.int32, (r, nch), 1)
    qpos = qpos0 + i * tq + (row & (tq - 1))
    cmask = (col * CMP_STRIDE + (CMP_BLOCK - 1) <= qpos) & (col < n_cmp)
    s = jnp.where(cmask, s, NEG)
    m = jnp.max(s, axis=1, keepdims=True)
    e = jnp.where(cmask, jnp.exp2(s - m), 0.0)
    l = jnp.sum(e, axis=1, keepdims=True)
    p = (e * (1.0 / jnp.where(l > 0.0, l, 1.0))).astype(BF16)
    o_ref[...] = _unstack_heads(_dot(p, vc), tq)
    imp_all = _dot(p, msel_ref[...])
    blocks_on_rows = tq % LANE == 0
    nrow = -(-n_sel // 8) * 8
    shape = (nrow, tq) if blocks_on_rows else (tq, nsp)
    baxis, qaxis = (0, 1) if blocks_on_rows else (1, 0)
    blk = lax.broadcasted_iota(jnp.int32, shape, baxis)
    qp = qpos0 + i * tq + lax.broadcasted_iota(jnp.int32, shape, qaxis)
    cur = lax.shift_right_logical(qp, SEL_BLOCK.bit_length() - 1)
    valid = (blk <= cur) & (blk < n_sel)
    forced = (blk == 0) | (blk == cur) | (blk == cur - 1)
    for g in range(NSA_G):
        imp = imp_all[(g * NSA_J) * tq:(g * NSA_J + 1) * tq]
        for j in range(1, NSA_J):
            imp = imp + imp_all[(g * NSA_J + j) * tq:(g * NSA_J + j + 1) * tq]
        if blocks_on_rows:
            imp = imp.T[:nrow]
        score = jnp.where(valid, imp + FORCE_BONUS * forced.astype(F32), -1.0)
        score = jnp.where(blk < n_sel, score, -2.0)
        rank = jnp.zeros(shape, F32)
        for mth in range(n_sel):
            cm = score[mth:mth + 1, :] if blocks_on_rows else score[:, mth:mth + 1]
            beats = (cm > score) | ((cm == score) & (blk > mth))
            rank = rank + beats.astype(F32)
        chosen = ((rank < float(min(SEL_TOPK, n_sel))) & valid).astype(F32)
        if blocks_on_rows:
            chosen = jnp.concatenate([chosen, jnp.zeros((nsp - nrow, tq), F32)], axis=0).T
        sel_ref[:, g * nsp:(g + 1) * nsp] = chosen


def cmp_attend(q, kcvc, msel, *, tq, n_cmp, n_sel, qpos0):
    b, sq, _ = q.shape
    nch = kcvc.shape[1]
    nsp = msel.shape[1]
    return pl.pallas_call(
        functools.partial(_cmp_body, tq=tq, nch=nch, n_cmp=n_cmp, n_sel=n_sel, nsp=nsp, qpos0=qpos0),
        name="cmp_attend",
        grid=(b, sq // tq),
        in_specs=[pl.BlockSpec((None, tq, 512), lambda bi, i: (bi, i, 0)),
                  pl.BlockSpec((None, nch, 256), lambda bi, i: (bi, 0, 0)),
                  pl.BlockSpec((nch, nsp), lambda bi, i: (0, 0))],
        out_specs=[pl.BlockSpec((None, tq, 512), lambda bi, i: (bi, i, 0)),
                   pl.BlockSpec((None, tq, 2 * nsp), lambda bi, i: (bi, i, 0))],
        out_shape=[jax.ShapeDtypeStruct((b, sq, 512), F32), jax.ShapeDtypeStruct((b, sq, 2 * nsp), F32)],
        compiler_params=_cp(("parallel", "parallel")),
    )(q, kcvc, msel)


ROW_BLOCK = 128
KEY_SUB = 256


def _flash_update(qs, k, v, bias, m, l, acc, kv_t=False):
    s = _dot(qs, k) if kv_t else _dot_nt(qs, k)
    if bias is not None:
        s = s + bias
    m_new, l_new, alpha, pb = _softmax_part(s, m, l)
    acc_new = alpha * acc + (_dot_nt(pb, v) if kv_t else _dot(pb, v))
    return m_new, l_new, acc_new


def _softmax_part(s, m, l):
    tiles = [s[:, c:c + LANE] for c in range(0, s.shape[1], LANE)]
    m_new = jnp.maximum(m, jnp.max(functools.reduce(jnp.maximum, tiles), axis=1, keepdims=True))
    alpha = jnp.exp2(m - m_new)
    ps = [jnp.exp2(t - m_new) for t in tiles]
    l_new = alpha * l + jnp.sum(functools.reduce(jnp.add, ps), axis=1, keepdims=True)
    return m_new, l_new, alpha, jnp.concatenate(ps, axis=1).astype(BF16)


def _flash_rows(qs_ref, k, v, bias_of, m_ref, l_ref, acc_ref, nrows, tq, kv_t=False):
    if tq < ROW_BLOCK:
        parts = [bias_of(h, 0, tq) for h in range(nrows // tq)]
        bias = None if parts[0] is None else jnp.concatenate(parts, axis=0)
        blocks = [(pl.ds(0, nrows), bias)]
    else:
        blocks = [(pl.ds(r0, ROW_BLOCK), bias_of(r0 // tq, r0 % tq, ROW_BLOCK)) for r0 in range(0, nrows, ROW_BLOCK)]
    for sl, bias in blocks:
        m, l, acc = _flash_update(qs_ref[sl], k, v, bias, m_ref[sl], l_ref[sl], acc_ref[sl], kv_t)
        m_ref[sl] = m
        l_ref[sl] = l
        acc_ref[sl] = acc


def _sel_ok(sel, e, nsp):
    return [_dot(sel[:, g * nsp:(g + 1) * nsp], e) > 0.5 for g in range(NSA_G)]


def _nsa_flash_body(*refs, mode, tq, tk, nk, qpos0, kpos0, kt_fn, nsp, free_step):
    if mode == "sel":
        q_ref, k_ref, v_ref, sel_ref, e_ref, o_ref, qs_ref, m_ref, l_ref, acc_ref = refs
    else:
        q_ref, k_ref, v_ref, o_ref, qs_ref, m_ref, l_ref, acc_ref = refs
    i = pl.program_id(1)
    j = pl.program_id(2)
    r = 8 * tq

    @pl.when(j == 0)
    def _init():
        qs_ref[...] = _stack_heads(q_ref[...], tq).astype(BF16)
        m_ref[...] = jnp.full((r, LANE), NEG, F32)
        l_ref[...] = jnp.zeros((r, LANE), F32)
        acc_ref[...] = jnp.zeros((r, LANE), F32)

    kt, valid = kt_fn(i, j)

    ks = min(tk, KEY_SUB)

    def step(c0, masked):
        k = k_ref[c0:c0 + ks, :].astype(BF16)
        v = v_ref[c0:c0 + ks, :].astype(BF16)
        if not masked:
            _flash_rows(qs_ref, k, v, lambda h, t0, n: None, m_ref, l_ref, acc_ref, r, tq)
            return
        qpos = qpos0 + i * tq + lax.broadcasted_iota(jnp.int32, (tq, ks), 0)
        kpos = kpos0 + kt * tk + c0 + lax.broadcasted_iota(jnp.int32, (tq, ks), 1)
        ok = kpos <= qpos
        if mode == "win":
            ok = ok & (kpos > qpos - WINDOW)
            bias = [jnp.where(ok, 0.0, NEG)] * NSA_G
        else:
            oks = _sel_ok(sel_ref[...].astype(BF16), e_ref[:, c0:c0 + ks], nsp)
            bias = [jnp.where(ok & okg, 0.0, NEG) for okg in oks]
        _flash_rows(qs_ref, k, v, lambda h, t0, n: bias[h // NSA_J][t0:t0 + n], m_ref, l_ref, acc_ref, r, tq)

    for c0 in range(0, tk, ks):
        live = valid & (kpos0 + kt * tk + c0 <= qpos0 + (i + 1) * tq - 1)
        if free_step is None:
            pl.when(live)(functools.partial(step, c0, True))
        else:
            pl.when(live & (j != free_step))(functools.partial(step, c0, True))
            pl.when(live & (j == free_step))(functools.partial(step, c0, False))

    @pl.when(j == nk - 1)
    def _fin():
        l = l_ref[...]
        o = acc_ref[...] * (1.0 / jnp.where(l > 0.0, l, 1.0))
        o_ref[...] = _unstack_heads(o, tq)


def nsa_flash(q, kv, kcol, vcol, *, mode, tq, tk, nk, qpos0, kpos0, kt_fn, sel=None, emat=None, free_step=None):
    b, sq, _ = q.shape
    nsp = 0 if sel is None else sel.shape[2] // 2

    def kmap(col):
        return lambda bi, i, j: (bi, kt_fn(i, j)[0], col)

    in_specs = [pl.BlockSpec((None, tq, 512), lambda bi, i, j: (bi, i, 0)),
                pl.BlockSpec((None, tk, LANE), kmap(kcol)),
                pl.BlockSpec((None, tk, LANE), kmap(vcol))]
    args = [q, kv, kv]
    if mode == "sel":
        in_specs += [pl.BlockSpec((None, tq, 2 * nsp), lambda bi, i, j: (bi, i, 0)),
                     pl.BlockSpec((nsp, tk), lambda bi, i, j: (0, kt_fn(i, j)[0]))]
        args += [sel, emat]
    r = 8 * tq
    return pl.pallas_call(
        functools.partial(_nsa_flash_body, mode=mode, tq=tq, tk=tk, nk=nk, qpos0=qpos0, kpos0=kpos0,
                          kt_fn=kt_fn, nsp=nsp, free_step=free_step),
        name="nsa_" + mode, grid=(b, sq // tq, nk),
        in_specs=in_specs,
        out_specs=pl.BlockSpec((None, tq, 512), lambda bi, i, j: (bi, i, 0)),
        out_shape=jax.ShapeDtypeStruct((b, sq, 512), F32),
        scratch_shapes=[pltpu.VMEM((r, LANE), BF16), pltpu.VMEM((r, LANE), F32), pltpu.VMEM((r, LANE), F32),
                        pltpu.VMEM((r, LANE), F32)],
        compiler_params=_cp(("parallel", "parallel", "arbitrary")),
    )(*args)


def _diff_init(q_ref, qs_ref, m_ref, l_ref, acc_ref, tq):
    r = 4 * tq
    npair = DIFF_HEADS // 2
    lane = lax.broadcasted_iota(jnp.int32, (tq, LANE), 1)
    for hp in range(npair):
        q = q_ref[:, hp * LANE:(hp + 1) * LANE]
        parts = []
        for h in range(2):
            for mth in range(2):
                lo = h * 2 * DIFF_QK + mth * DIFF_QK
                parts.append(jnp.where((lane >= lo) & (lane < lo + DIFF_QK), q, 0.0))
        qs_ref[hp] = jnp.concatenate(parts, axis=0).astype(BF16)
    m_ref[...] = jnp.full((npair, r, LANE), NEG, F32)
    l_ref[...] = jnp.zeros((npair, r, LANE), F32)
    acc_ref[...] = jnp.zeros((npair, r, LANE), F32)


def _diff_step(k_of, v_of, bias, qs_ref, m_ref, l_ref, acc_ref, tq, kv_t=False):
    bias_of = lambda h, t0, n: None if bias is None else bias[t0:t0 + n]
    for hp in range(DIFF_HEADS // 2):
        _flash_rows(qs_ref.at[hp], k_of(hp), v_of(hp), bias_of, m_ref.at[hp], l_ref.at[hp], acc_ref.at[hp],
                    4 * tq, tq, kv_t)


def _diff_fin(lam_ref, g_ref, o_ref, l_ref, acc_ref, tq, lam_init):
    lp = lam_ref[...]
    lam = (jnp.exp(jnp.sum(lp[0:1] * lp[1:2], axis=1, keepdims=True))
           - jnp.exp(jnp.sum(lp[2:3] * lp[3:4], axis=1, keepdims=True)) + lam_init)
    lane = lax.broadcasted_iota(jnp.int32, (tq, LANE), 1)
    lo_half = lane < 2 * DIFF_QK
    for hp in range(DIFF_HEADS // 2):
        l = l_ref[hp]
        a = acc_ref[hp] * (1.0 / jnp.where(l > 0.0, l, 1.0))
        o0 = a[0:tq] - lam * a[tq:2 * tq]
        o1 = a[2 * tq:3 * tq] - lam * a[3 * tq:4 * tq]
        o = jnp.where(lo_half, o0, o1)
        sq = o * o
        ms0 = jnp.sum(jnp.where(lo_half, sq, 0.0), axis=1, keepdims=True)
        ms1 = jnp.sum(jnp.where(lo_half, 0.0, sq), axis=1, keepdims=True)
        ms = jnp.where(lo_half, ms0, ms1) * (1.0 / (2 * DIFF_QK))
        y = o * lax.rsqrt(ms + EPS) * g_ref[...]
        o_ref[:, hp * LANE:(hp + 1) * LANE] = y * (1.0 - lam_init)


def _diff_body(lam_ref, q_ref, k_ref, v_ref, g_ref, o_ref, qs_ref, m_ref, l_ref, acc_ref, *,
               tq, tk, nk, qpos0, lam_init, kt_fn):
    i = pl.program_id(1)
    j = pl.program_id(2)
    r = 4 * tq

    @pl.when(j == 0)
    def _init():
        _diff_init(q_ref, qs_ref, m_ref, l_ref, acc_ref, tq)

    kt, valid = kt_fn(i, j)
    ks = min(tk, KEY_SUB)

    def step(c0, masked):
        k_of = lambda hp: k_ref[c0:c0 + ks, hp * LANE:(hp + 1) * LANE].astype(BF16)
        v_of = lambda hp: v_ref[c0:c0 + ks, hp * LANE:(hp + 1) * LANE].astype(BF16)
        bias = None
        if masked:
            qpos = qpos0 + i * tq + lax.broadcasted_iota(jnp.int32, (tq, ks), 0)
            kpos = kt * tk + c0 + lax.broadcasted_iota(jnp.int32, (tq, ks), 1)
            bias = jnp.where(kpos <= qpos, 0.0, NEG)
        _diff_step(k_of, v_of, bias, qs_ref, m_ref, l_ref, acc_ref, tq)

    for c0 in range(0, tk, ks):
        k_first = kt * tk + c0
        live = valid & (k_first <= qpos0 + (i + 1) * tq - 1)
        below = k_first + ks - 1 <= qpos0 + i * tq
        pl.when(live & below)(functools.partial(step, c0, False))
        pl.when(live & jnp.logical_not(below))(functools.partial(step, c0, True))

    @pl.when(j == nk - 1)
    def _fin():
        _diff_fin(lam_ref, g_ref, o_ref, l_ref, acc_ref, tq, lam_init)


def diff_attend(lam_par, q, kv, gnorm, *, tq, tk, nk, qpos0, lam_init, kt_fn):
    b, sq, _ = q.shape
    r = 4 * tq
    npair = DIFF_HEADS // 2
    return pl.pallas_call(
        functools.partial(_diff_body, tq=tq, tk=tk, nk=nk, qpos0=qpos0, lam_init=lam_init, kt_fn=kt_fn),
        name="diff_attend",
        grid=(b, sq // tq, nk),
        in_specs=[pl.BlockSpec((4, DIFF_QK), lambda bi, i, j: (0, 0)),
                  pl.BlockSpec((None, tq, 512), lambda bi, i, j: (bi, i, 0)),
                  pl.BlockSpec((None, tk, 512), lambda bi, i, j: (bi, kt_fn(i, j)[0], 0)),
                  pl.BlockSpec((None, tk, 512), lambda bi, i, j: (bi, kt_fn(i, j)[0], 1)),
                  pl.BlockSpec((1, LANE), lambda bi, i, j: (0, 0))],
        out_specs=pl.BlockSpec((None, tq, 512), lambda bi, i, j: (bi, i, 0)),
        out_shape=jax.ShapeDtypeStruct((b, sq, 512), F32),
        scratch_shapes=[pltpu.VMEM((npair, r, LANE), BF16), pltpu.VMEM((npair, r, LANE), F32),
                        pltpu.VMEM((npair, r, LANE), F32), pltpu.VMEM((npair, r, LANE), F32)],
        compiler_params=_cp(("parallel", "parallel", "arbitrary")),
    )(lam_par, q, kv, kv, gnorm)


def _pool_body(x_ref, pw_ref, sc_ref, o_ref, *, sq, start):
    pos = start + lax.broadcasted_iota(jnp.int32, (sq, LANE), 0)
    for g, w in enumerate(POOL_WINDOWS):
        c0 = g * LANE
        x = x_ref[pl.ds(16, sq), c0:c0 + LANE]
        tot = x
        for back in range(1, w):
            tot = tot + x_ref[pl.ds(16 - back, sq), c0:c0 + LANE]
        cnt = jnp.minimum(w, pos + 1).astype(F32)
        mix = tot / cnt - x
        y = _dot(mix.astype(BF16), pw_ref[g])
        o_ref[:, c0:c0 + LANE] = y * sc_ref[:, c0:c0 + LANE]


def pool_mix(xcat, pw, scale, start):
    b, rows, _ = xcat.shape
    sq = rows - 16
    return pl.pallas_call(
        functools.partial(_pool_body, sq=sq, start=start), name="pool_mix", grid=(b,),
        in_specs=[pl.BlockSpec((None, rows, 512), lambda i: (i, 0, 0)),
                  pl.BlockSpec((4, LANE, LANE), lambda i: (0, 0, 0)),
                  pl.BlockSpec((1, 512), lambda i: (0, 0))],
        out_specs=pl.BlockSpec((None, sq, 512), lambda i: (i, 0, 0)),
        out_shape=jax.ShapeDtypeStruct((b, sq, 512), F32),
        compiler_params=_cp(("parallel",)),
    )(xcat, pw.astype(BF16), scale.reshape(1, 512))


def _gmlp_body(z_ref, ng_ref, nb_ref, ws_ref, bs_ref, o_ref, v_ref, *, tg):
    z = _gelu(z_ref[...])
    u = z[:, :GM_W]
    vr = z[:, GM_W:]
    xc = vr - jnp.mean(vr, axis=-1, keepdims=True)
    v = xc * lax.rsqrt(jnp.mean(xc * xc, axis=-1, keepdims=True) + EPS) * ng_ref[...] + nb_ref[...]
    v_ref[...] = v
    vb = v.astype(BF16)
    for c in range(tg // GM_CHUNK):
        r0 = c * GM_CHUNK
        for g in range(4):
            c0 = g * LANE
            mixed = _dot(ws_ref[g], vb[r0:r0 + GM_CHUNK, c0:c0 + LANE]) + bs_ref[:, c0:c0 + LANE]
            o_ref[r0:r0 + GM_CHUNK, c0:c0 + LANE] = u[r0:r0 + GM_CHUNK, c0:c0 + LANE] * mixed


def gmlp_mix(z, ng, nb, ws_tril, bs_exp):
    b, s, _ = z.shape
    tg = _tile(s, 512)
    return pl.pallas_call(
        functools.partial(_gmlp_body, tg=tg), name="gmlp_mix", grid=(b, s // tg),
        in_specs=[pl.BlockSpec((None, tg, 1024), lambda bi, i: (bi, i, 0)),
                  pl.BlockSpec((1, 512), lambda bi, i: (0, 0)),
                  pl.BlockSpec((1, 512), lambda bi, i: (0, 0)),
                  pl.BlockSpec((4, GM_CHUNK, GM_CHUNK), lambda bi, i: (0, 0, 0)),
                  pl.BlockSpec((GM_CHUNK, 512), lambda bi, i: (0, 0))],
        out_specs=[pl.BlockSpec((None, tg, 512), lambda bi, i: (bi, i, 0)),
                   pl.BlockSpec((None, tg, 512), lambda bi, i: (bi, i, 0))],
        out_shape=[jax.ShapeDtypeStruct((b, s, 512), F32), jax.ShapeDtypeStruct((b, s, 512), F32)],
        compiler_params=_cp(("parallel", "parallel")),
    )(z, ng.reshape(1, 512), nb.reshape(1, 512), ws_tril, bs_exp)


def _finish_body(x_ref, hn_ref, g3_ref, ocmp_ref, osel_ref, owin_ref, opool_ref, ogm_ref, odiff_ref,
                 wgate_ref, bgate_ref, wbr_ref, wo_ref, o_ref):
    hn = hn_ref[...]
    g3 = g3_ref[...]
    onsa = g3[:, 0:512] * ocmp_ref[...] + g3[:, 512:1024] * osel_ref[...] + g3[:, 1024:1536] * owin_ref[...]
    branches = (onsa, opool_ref[...], ogm_ref[...], odiff_ref[...])
    acc = jnp.zeros(o_ref.shape, F32)
    for n in range(N_BRANCH):
        c0 = n * D_MODEL
        gate = jax.nn.sigmoid(_dot(hn, wgate_ref[:, c0:c0 + D_MODEL]) + bgate_ref[:, c0:c0 + D_MODEL])
        acc = acc + gate * _dot(branches[n].astype(BF16), wbr_ref[n])
    o_ref[...] = x_ref[...] + _dot(acc.astype(BF16), wo_ref[...])


def finish_mixer(x, hn, g3, ocmp, osel, owin, opool, ogm, odiff, wgate, bgate, wbr, wo):
    t = x.shape[0]
    tm = _tile(t, 256)
    row = lambda w: pl.BlockSpec((tm, w), lambda i: (i, 0))
    const = lambda shape: pl.BlockSpec(shape, lambda i: (0,) * len(shape), pipeline_mode=pl.Buffered(1))
    return pl.pallas_call(
        _finish_body, name="finish_mixer", grid=(t // tm,),
        in_specs=[row(1024), row(1024), row(1536), row(512), row(512), row(512), row(512), row(512), row(512),
                  const((D_MODEL, N_BRANCH * D_MODEL)), const((1, N_BRANCH * D_MODEL)),
                  const((N_BRANCH, BRANCH_W, D_MODEL)), const((D_MODEL, D_MODEL))],
        out_specs=row(1024),
        out_shape=jax.ShapeDtypeStruct((t, D_MODEL), F32),
        compiler_params=_cp(("parallel",)),
    )(x, hn, g3, ocmp, osel, owin, opool, ogm, odiff, wgate, bgate, wbr, wo)


def _ffn_body(x_ref, g_ref, wg_ref, wu_ref, wd_ref, o_ref, hn_ref, acc_ref, *, nf):
    j = pl.program_id(1)

    @pl.when(j == 0)
    def _init():
        hn_ref[...] = _rms(x_ref[...], g_ref[...]).astype(BF16)
        acc_ref[...] = jnp.zeros(acc_ref.shape, F32)

    h = hn_ref[...]
    a = _dot(h, wg_ref[...])
    act = a * jax.nn.sigmoid(a) * _dot(h, wu_ref[...])
    acc_ref[...] += _dot(act.astype(BF16), wd_ref[...])

    @pl.when(j == nf - 1)
    def _fin():
        o_ref[...] = x_ref[...] + acc_ref[...]


def ffn_swiglu(x, g, wg, wu, wd):
    t, d = x.shape
    f = wg.shape[1]
    tm = _tile(t, 512)
    tf = 1408 if f % 1408 == 0 else _tile(f, 512)
    nf = f // tf
    return pl.pallas_call(
        functools.partial(_ffn_body, nf=nf), name="ffn_swiglu", grid=(t // tm, nf),
        in_specs=[pl.BlockSpec((tm, d), lambda i, j: (i, 0)), pl.BlockSpec((1, d), lambda i, j: (0, 0)),
                  pl.BlockSpec((d, tf), lambda i, j: (0, j)), pl.BlockSpec((d, tf), lambda i, j: (0, j)),
                  pl.BlockSpec((tf, d), lambda i, j: (j, 0))],
        out_specs=pl.BlockSpec((tm, d), lambda i, j: (i, 0)),
        out_shape=jax.ShapeDtypeStruct((t, d), F32),
        scratch_shapes=[pltpu.VMEM((tm, d), BF16), pltpu.VMEM((tm, d), F32)],
        compiler_params=_cp(("parallel", "arbitrary")),
    )(x, g.reshape(1, d), wg.astype(BF16), wu.astype(BF16), wd.astype(BF16))


MOE_SUB = 256


def _moe_body(x_ref, g_ref, r_ref, rb_ref, u_ref, wg_ref, wu_ref, wd_ref, o_ref,
              hn_ref, comb_ref, post_ref, posr_ref, cnt_ref, xe_ref, ye_ref, acc_ref, *, nf):
    e = pl.program_id(1)
    f = pl.program_id(2)
    tm = x_ref.shape[0]
    lane = lax.broadcasted_iota(jnp.int32, (tm, LANE), 1)

    @pl.when((e == 0) & (f == 0))
    def _init():
        hn = _rms(x_ref[...], g_ref[...]).astype(BF16)
        hn_ref[...] = hn
        lg = _dot(hn, r_ref[...]) + rb_ref[...]
        m1 = jnp.max(lg, axis=1, keepdims=True)
        i1 = jnp.min(jnp.where(lg == m1, lane, LANE), axis=1, keepdims=True)
        lg2 = jnp.where(lane == i1, -3e38, lg)
        m2 = jnp.max(lg2, axis=1, keepdims=True)
        i2 = jnp.min(jnp.where(lg2 == m2, lane, LANE), axis=1, keepdims=True)
        e2 = jnp.exp(m2 - m1)
        w1 = 1.0 / (1.0 + e2)
        comb = jnp.where(lane == i1, w1, 0.0) + jnp.where(lane == i2, e2 * w1, 0.0)
        comb_ref[...] = comb
        ind_t = (comb.T > 0.0).astype(F32)
        pos_t = _dot(ind_t.astype(BF16), u_ref[...])
        post_ref[...] = jnp.where(ind_t > 0.0, pos_t, -1.0)
        posr_ref[...] = post_ref[...].T
        cnt = jnp.sum(ind_t, axis=1, keepdims=True)
        for ex in range(N_EXPERTS):
            cnt_ref[ex] = cnt[ex, 0].astype(jnp.int32)
        acc_ref[...] = jnp.zeros(acc_ref.shape, F32)

    nsub = (cnt_ref[e] + (MOE_SUB - 1)) // MOE_SUB

    @pl.when(f == 0)
    def _gather():
        slot_t = post_ref[pl.ds(e, 1), :]
        base = lax.broadcasted_iota(jnp.int32, (MOE_SUB, tm), 0).astype(F32)

        def body(u, carry):
            onehot = jnp.where(slot_t == base + (u * MOE_SUB).astype(F32), 1.0, 0.0).astype(BF16)
            xe_ref[u] = _dot(onehot, hn_ref[...]).astype(BF16)
            ye_ref[u] = jnp.zeros((MOE_SUB, x_ref.shape[1]), F32)
            return carry

        lax.fori_loop(0, nsub, body, 0)

    def expert(u, carry):
        xe = xe_ref[u]
        a = _dot(xe, wg_ref[...])
        act = a * jax.nn.sigmoid(a) * _dot(xe, wu_ref[...])
        ye_ref[u] += _dot(act.astype(BF16), wd_ref[...])
        return carry

    lax.fori_loop(0, nsub, expert, 0)

    @pl.when(f == nf - 1)
    def _scatter():
        slot_r = jnp.sum(jnp.where(lane == e, posr_ref[...], 0.0), axis=1, keepdims=True)
        w_r = jnp.sum(jnp.where(lane == e, comb_ref[...], 0.0), axis=1, keepdims=True)
        base = lax.broadcasted_iota(jnp.int32, (tm, MOE_SUB), 1).astype(F32)

        def body(u, carry):
            onehot = jnp.where(slot_r == base + (u * MOE_SUB).astype(F32), 1.0, 0.0).astype(BF16)
            y = ye_ref[u]
            y_hi = y.astype(BF16)
            y_lo = (y - y_hi.astype(F32)).astype(BF16)
            acc_ref[...] += w_r * (_dot(onehot, y_hi) + _dot(onehot, y_lo))
            return carry

        lax.fori_loop(0, nsub, body, 0)

    @pl.when((e == N_EXPERTS - 1) & (f == nf - 1))
    def _fin():
        o_ref[...] = x_ref[...] + acc_ref[...]


def moe_swiglu(x, g, router, router_b, wg, wu, wd):
    t, d = x.shape
    f = wg.shape[2]
    tm = _tile(t, 1024)
    tf = 896 if f % 896 == 0 else _tile(f, 512)
    nf = f // tf
    nsubmax = -(-tm // MOE_SUB)
    rpad = jnp.zeros((d, LANE), F32).at[:, :N_EXPERTS].set(router).astype(BF16)
    rbpad = jnp.full((1, LANE), NEG, F32).at[0, :N_EXPERTS].set(router_b)
    before = jnp.asarray(np.triu(np.ones((tm, tm), np.float32), 1), BF16)
    const = lambda shape: pl.BlockSpec(shape, lambda i, e, j: (0,) * len(shape))
    return pl.pallas_call(
        functools.partial(_moe_body, nf=nf), name="moe_swiglu", grid=(t // tm, N_EXPERTS, nf),
        in_specs=[pl.BlockSpec((tm, d), lambda i, e, j: (i, 0)), const((1, d)), const((d, LANE)), const((1, LANE)),
                  const((tm, tm)),
                  pl.BlockSpec((None, d, tf), lambda i, e, j: (e, 0, j)),
                  pl.BlockSpec((None, d, tf), lambda i, e, j: (e, 0, j)),
                  pl.BlockSpec((None, tf, d), lambda i, e, j: (e, j, 0))],
        out_specs=pl.BlockSpec((tm, d), lambda i, e, j: (i, 0)),
        out_shape=jax.ShapeDtypeStruct((t, d), F32),
        scratch_shapes=[pltpu.VMEM((tm, d), BF16), pltpu.VMEM((tm, LANE), F32), pltpu.VMEM((LANE, tm), F32),
                        pltpu.VMEM((tm, LANE), F32), pltpu.SMEM((N_EXPERTS,), jnp.int32),
                        pltpu.VMEM((nsubmax, MOE_SUB, d), BF16), pltpu.VMEM((nsubmax, MOE_SUB, d), F32),
                        pltpu.VMEM((tm, d), F32)],
        compiler_params=_cp(("parallel", "arbitrary", "arbitrary")),
    )(x, g.reshape(1, d), rpad, rbpad, before, wg.astype(BF16), wu.astype(BF16), wd.astype(BF16))


def _ple_body(x_ref, p_ref, g_ref, wg_ref, wp_ref, fg_ref, o_ref, *, final):
    x = x_ref[...]
    hn = _rms(x, g_ref[...]).astype(BF16)
    gate = jax.nn.sigmoid(_dot(hn, wg_ref[...]))
    y = x + gate * _dot(p_ref[...].astype(BF16), wp_ref[...])
    if final:
        y = _rms(y, fg_ref[...])
    o_ref[...] = y


def ple(x, p, g, wg, wp, fg, final):
    t, d = x.shape
    pd = p.shape[1]
    tm = _tile(t, 512)
    return pl.pallas_call(
        functools.partial(_ple_body, final=final), name="ple", grid=(t // tm,),
        in_specs=[pl.BlockSpec((tm, d), lambda i: (i, 0)), pl.BlockSpec((tm, pd), lambda i: (i, 0)),
                  pl.BlockSpec((1, d), lambda i: (0, 0)), pl.BlockSpec((d, d), lambda i: (0, 0)),
                  pl.BlockSpec((pd, d), lambda i: (0, 0)), pl.BlockSpec((1, d), lambda i: (0, 0))],
        out_specs=pl.BlockSpec((tm, d), lambda i: (i, 0)),
        out_shape=jax.ShapeDtypeStruct((t, d), F32),
        compiler_params=_cp(("parallel",)),
    )(x, p, g.reshape(1, d), wg.astype(BF16), wp.astype(BF16), fg.reshape(1, d))


def _page_copies(pt_ref, cache_ref, bufs, sem, slot, b, c, *, layer, pc, cols):
    out = []
    for p in range(pc):
        pg = pt_ref[b, c * pc + p]
        for buf, (f0, w) in zip(bufs, cols):
            out.append(pltpu.make_async_copy(cache_ref.at[layer, pg, pl.ds(f0, w), :], buf.at[slot, p],
                                             sem.at[slot]))
    return out


def _paged_pipeline(pt_ref, cache_ref, bufs, sem, *, nb, nc, **kw):
    b = pl.program_id(0)
    c = pl.program_id(1) if nc > 1 else 0
    step = b * nc + c
    slot = step % 2

    @pl.when(step == 0)
    def _first():
        for d in _page_copies(pt_ref, cache_ref, bufs, sem, 0, 0, 0, **kw):
            d.start()

    @pl.when(step + 1 < nb * nc)
    def _prefetch():
        nxt = step + 1
        for d in _page_copies(pt_ref, cache_ref, bufs, sem, 1 - slot, nxt // nc, nxt % nc, **kw):
            d.start()

    for d in _page_copies(pt_ref, cache_ref, bufs, sem, slot, b, c, **kw):
        d.wait()
    return slot


def _compress_paged_body(pt_ref, cache_ref, wc_ref, bias_ref, w2_ref, o_ref, bufk, bufv, sem, rowk, rowv, *,
                         layer, npg, nb, nch):
    slot = _paged_pipeline(pt_ref, cache_ref, (bufk, bufv), sem, nb=nb, nc=1, layer=layer, pc=npg,
                           cols=((0, LANE), (LANE, LANE)))

    def to_rows(p, carry):
        rows = pl.ds(pl.multiple_of(p * PAGE, PAGE), PAGE)
        rowk[rows, :] = bufk[slot, p].T
        rowv[rows, :] = bufv[slot, p].T
        return carry

    lax.fori_loop(0, npg, to_rows, 0, unroll=4)
    _compress_core(lambda s: rowk[pl.ds(s, nch, stride=CMP_STRIDE), :],
                   lambda s: rowv[pl.ds(s, nch, stride=CMP_STRIDE), :], wc_ref, bias_ref, w2_ref, o_ref, nch)


def compress_paged(page_table, cache, layer, wc, bias, w2):
    b, npg = page_table.shape
    past = npg * PAGE
    nch = past // CMP_STRIDE
    const = lambda shape: pl.BlockSpec(shape, lambda i, pt: (0,) * len(shape))
    return pl.pallas_call(
        functools.partial(_compress_paged_body, layer=layer, npg=npg, nb=b, nch=nch), name="compress_paged",
        grid_spec=pltpu.PrefetchScalarGridSpec(
            num_scalar_prefetch=1, grid=(b,),
            in_specs=[pl.BlockSpec(memory_space=pl.ANY), const((CMP_STRIDE * 256, 512)), const((1, 256)),
                      const((256, 256))],
            out_specs=pl.BlockSpec((None, nch, 256), lambda i, pt: (i, 0, 0)),
            scratch_shapes=[pltpu.VMEM((2, npg, LANE, PAGE), F32), pltpu.VMEM((2, npg, LANE, PAGE), F32),
                            pltpu.SemaphoreType.DMA((2,)),
                            pltpu.VMEM((past, LANE), F32), pltpu.VMEM((past, LANE), F32)]),
        out_shape=jax.ShapeDtypeStruct((b, nch, 256), BF16),
        compiler_params=_cp(("arbitrary",)),
    )(page_table, cache, wc, bias, w2)


def _new_tile(new, c0, tq):
    return jnp.concatenate([new[:, c0:c0 + LANE], jnp.zeros((LANE - tq, LANE), F32)], axis=0).astype(BF16)


def _sel_paged_body(pt_ref, cache_ref, q_ref, new_ref, sel_ref, e_ref, o_ref, bufk, bufv, sem, *,
                    layer, npg, nb, tq, tk, nsp):
    slot = _paged_pipeline(pt_ref, cache_ref, (bufk, bufv), sem, nb=nb, nc=1, layer=layer, pc=npg,
                           cols=((2 * LANE, LANE), (3 * LANE, LANE)))
    past = npg * PAGE
    ppc = tk // PAGE
    r = 8 * tq
    qs = _stack_heads(q_ref[...], tq).astype(BF16)
    sel = sel_ref[...].astype(BF16)
    m = jnp.full((r, LANE), NEG, F32)
    l = jnp.zeros((r, LANE), F32)
    acc = jnp.zeros((r, LANE), F32)

    def stacked_bias(oks):
        return jnp.concatenate([jnp.where(ok, 0.0, NEG) for ok in oks for _ in range(NSA_J)], axis=0)

    for c in range(npg // ppc):
        kt = jnp.concatenate([bufk[slot, c * ppc + u] for u in range(ppc)], axis=1).astype(BF16)
        vt = jnp.concatenate([bufv[slot, c * ppc + u] for u in range(ppc)], axis=1).astype(BF16)
        bias = stacked_bias(_sel_ok(sel, e_ref[:, c * tk:(c + 1) * tk], nsp))
        m, l, acc = _flash_update(qs, kt, vt, bias, m, l, acc, kv_t=True)
    new = new_ref[...]
    causal = (lax.broadcasted_iota(jnp.int32, (tq, LANE), 1) <= lax.broadcasted_iota(jnp.int32, (tq, LANE), 0))
    bias = stacked_bias([ok & causal for ok in _sel_ok(sel, e_ref[:, past:past + LANE], nsp)])
    m, l, acc = _flash_update(qs, _new_tile(new, 2 * LANE, tq), _new_tile(new, 3 * LANE, tq), bias, m, l, acc)
    o_ref[...] = _unstack_heads(acc * (1.0 / jnp.where(l > 0.0, l, 1.0)), tq)


def sel_paged(page_table, cache, layer, q, new_rows, sel, emat):
    b, npg = page_table.shape
    past = npg * PAGE
    tq = q.shape[1]
    nsp = sel.shape[2] // 2
    return pl.pallas_call(
        functools.partial(_sel_paged_body, layer=layer, npg=npg, nb=b, tq=tq, tk=_tile(past, 512), nsp=nsp),
        name="sel_paged",
        grid_spec=pltpu.PrefetchScalarGridSpec(
            num_scalar_prefetch=1, grid=(b,),
            in_specs=[pl.BlockSpec(memory_space=pl.ANY),
                      pl.BlockSpec((None, tq, 512), lambda i, pt: (i, 0, 0)),
                      pl.BlockSpec((None, tq, 512), lambda i, pt: (i, 0, 0)),
                      pl.BlockSpec((None, tq, 2 * nsp), lambda i, pt: (i, 0, 0)),
                      pl.BlockSpec((nsp, past + LANE), lambda i, pt: (0, 0))],
            out_specs=pl.BlockSpec((None, tq, 512), lambda i, pt: (i, 0, 0)),
            scratch_shapes=[pltpu.VMEM((2, npg, LANE, PAGE), F32), pltpu.VMEM((2, npg, LANE, PAGE), F32),
                            pltpu.SemaphoreType.DMA((2,))]),
        out_shape=jax.ShapeDtypeStruct((b, tq, 512), F32),
        compiler_params=_cp(("arbitrary",)),
    )(page_table, cache, q, new_rows, sel, emat)


def _diff_paged_body(pt_ref, cache_ref, lam_ref, q_ref, new_ref, g_ref, o_ref, bufk, bufv, sem,
                     qs_ref, m_ref, l_ref, acc_ref, acc2_ref, *, layer, nb, nc, pc, tq, lam_init):
    c = pl.program_id(1)
    slot = _paged_pipeline(pt_ref, cache_ref, (bufk, bufv), sem, nb=nb, nc=nc, layer=layer, pc=pc,
                           cols=((0, 512), (512, 512)))
    ppc = _tile(pc, 4)
    r = 4 * tq

    @pl.when(c == 0)
    def _init():
        _diff_init(q_ref, qs_ref, m_ref, l_ref, acc_ref, tq)
        acc2_ref[...] = jnp.zeros(acc2_ref.shape, F32)

    def pages_t(buf, f0, nf, p0):
        return jnp.concatenate([buf[slot, p0 + u, f0:f0 + nf, :] for u in range(ppc)], axis=1).astype(BF16)

    for p0 in range(0, pc, ppc):
        for d in range(DIFF_HEADS // 4):
            probs, alphas = [], []
            for hp in (2 * d, 2 * d + 1):
                s = _dot(qs_ref[hp], pages_t(bufk, hp * LANE, LANE, p0))
                m_new, l_new, alpha, pb = _softmax_part(s, m_ref[hp], l_ref[hp])
                m_ref[hp] = m_new
                l_ref[hp] = l_new
                probs.append(pb)
                alphas.append(jnp.concatenate([alpha, alpha], axis=1))
            pv = _dot_nt(jnp.concatenate(probs, axis=0), pages_t(bufv, d * 2 * LANE, 2 * LANE, p0))
            acc2_ref[d] = jnp.concatenate(alphas, axis=0) * acc2_ref[d] + pv

    @pl.when(c == nc - 1)
    def _fin():
        for hp in range(DIFF_HEADS // 2):
            j = hp % 2
            acc_ref[hp] = acc2_ref[hp // 2][j * r:(j + 1) * r, j * LANE:(j + 1) * LANE]
        new = new_ref[...]
        causal = (lax.broadcasted_iota(jnp.int32, (tq, LANE), 1) <= lax.broadcasted_iota(jnp.int32, (tq, LANE), 0))
        _diff_step(lambda hp: _new_tile(new, hp * LANE, tq), lambda hp: _new_tile(new, 512 + hp * LANE, tq),
                   jnp.where(causal, 0.0, NEG), qs_ref, m_ref, l_ref, acc_ref, tq)
        _diff_fin(lam_ref, g_ref, o_ref, l_ref, acc_ref, tq, lam_init)


def diff_paged(page_table, cache, layer, lam_par, q, new_rows, gnorm, lam_init):
    b, npg = page_table.shape
    tq = q.shape[1]
    pc = _tile(npg, 16)
    nc = npg // pc
    r = 4 * tq
    npair = DIFF_HEADS // 2
    return pl.pallas_call(
        functools.partial(_diff_paged_body, layer=layer, nb=b, nc=nc, pc=pc, tq=tq, lam_init=lam_init),
        name="diff_paged",
        grid_spec=pltpu.PrefetchScalarGridSpec(
            num_scalar_prefetch=1, grid=(b, nc),
            in_specs=[pl.BlockSpec(memory_space=pl.ANY),
                      pl.BlockSpec((4, DIFF_QK), lambda i, c, pt: (0, 0)),
                      pl.BlockSpec((None, tq, 512), lambda i, c, pt: (i, 0, 0)),
                      pl.BlockSpec((None, tq, 1024), lambda i, c, pt: (i, 0, 0)),
                      pl.BlockSpec((1, LANE), lambda i, c, pt: (0, 0))],
            out_specs=pl.BlockSpec((None, tq, 512), lambda i, c, pt: (i, 0, 0)),
            scratch_shapes=[pltpu.VMEM((2, pc, 512, PAGE), F32), pltpu.VMEM((2, pc, 512, PAGE), F32),
                            pltpu.SemaphoreType.DMA((2,)),
                            pltpu.VMEM((npair, r, LANE), BF16), pltpu.VMEM((npair, r, LANE), F32),
                            pltpu.VMEM((npair, r, LANE), F32), pltpu.VMEM((npair, r, LANE), F32),
                            pltpu.VMEM((npair // 2, 2 * r, 2 * LANE), F32)]),
        out_shape=jax.ShapeDtypeStruct((b, tq, 512), F32),
        compiler_params=_cp(("arbitrary", "arbitrary")),
    )(page_table, cache, lam_par, q, new_rows, gnorm)


def _q_perm():
    idx = np.zeros(512, np.int32)
    for j in range(NSA_J):
        for g in range(NSA_G):
            for d in range(DH):
                idx[j * LANE + g * DH + d] = (g * NSA_J + j) * DH + d
    return idx


def _gate3_perm():
    idx = np.zeros(3 * 512, np.int32)
    for c in range(3):
        for j in range(NSA_J):
            for g in range(NSA_G):
                idx[c * 512 + j * LANE + g * DH:c * 512 + j * LANE + (g + 1) * DH] = (g * NSA_J + j) * 3 + c
    return idx


def _cmp_to_sel(n_cmp, n_sel, nch, nsp):
    r = SEL_BLOCK // CMP_STRIDE
    k = np.arange(n_cmp)[:, None] - r * np.arange(n_sel)[None, :]
    m = sum(((k + n >= 0) & (k + n < r)).astype(np.float32) for n in range(CMP_BLOCK // CMP_STRIDE))
    out = np.zeros((nch, nsp), np.float32)
    out[:n_cmp, :n_sel] = m
    return out


def _block_expand(nsp, lk):
    return (np.arange(lk)[None, :] // SEL_BLOCK == np.arange(nsp)[:, None]).astype(np.float32)


def _cmp_weights(pe, w1, w2):
    eye = jnp.eye(2, dtype=F32)
    w6 = w1.reshape(2, NSA_G, 2, CMP_STRIDE, DH, DH)
    wc = jnp.zeros((CMP_STRIDE, 2, NSA_G, DH, 2, 2, NSA_G, DH), BF16)
    for kv in range(2):
        for g in range(NSA_G):
            wc = wc.at[:, kv, g, :, :, kv, g, :].set(jnp.transpose(w6[kv, g], (1, 2, 0, 3)).astype(BF16))
    wc = wc.reshape(CMP_STRIDE * 256, 512)
    w2b = jnp.einsum("kghd,kK,gG->kghKGd", w2, eye, eye).reshape(256, 256).astype(BF16)
    pe_rows = jnp.transpose(pe, (0, 2, 1, 3)).reshape(4, CMP_BLOCK * DH)
    xb = jnp.einsum("rc,rR->rRc", pe_rows, jnp.eye(4, dtype=F32)).reshape(4, 4 * CMP_BLOCK * DH)
    xb = jnp.zeros((16, 4 * CMP_BLOCK * DH), F32).at[:4].set(xb)
    wb = jnp.zeros((4 * CMP_BLOCK * DH, LANE), F32).at[:, :DH].set(w1.reshape(4 * CMP_BLOCK * DH, DH))
    bias = mm(xb, wb)[:4, :DH].reshape(1, 256)
    return wc, bias, w2b


def _mixer(x, lw, lam_init, *, b, sq, sample=None):
    t = b * sq
    w_in = lw["w_in"]
    qperm = _q_perm()
    segs = ((512, DH ** -0.5 * LOG2E, None), (512, None, None), (256, None, None), (1536, None, "sigmoid"),
            (512, None, None), (1024, None, None), (512, DIFF_QK ** -0.5 * LOG2E, None), (1024, None, None))
    w_all = jnp.concatenate([w_in[:, OFF_Q:OFF_KV][:, qperm], w_in[:, OFF_KV:OFF_GATE],
                             w_in[:, OFF_GATE:OFF_POOL][:, _gate3_perm()], w_in[:, OFF_POOL:OFF_END]],
                            axis=1).astype(BF16)
    hn, q, rows4, winkv, g3, zpool, zgm, dq, dkv = in_proj(x, lw["norm_mix_g"], w_all, segs)
    q, rows4, winkv = q.reshape(b, sq, 512), rows4.reshape(b, sq, 512), winkv.reshape(b, sq, 256)
    zpool, zgm = zpool.reshape(b, sq, 512), zgm.reshape(b, sq, 1024)
    dq, dkv = dq.reshape(b, sq, 512), dkv.reshape(b, sq, 1024)

    wc, cbias, w2b = _cmp_weights(lw["cmp_pe"], lw["cmp_w1"], lw["cmp_w2"])
    lam_par = jnp.stack([lw["diff_lq1"], lw["diff_lk1"], lw["diff_lq2"], lw["diff_lk2"]])
    gnorm = jnp.tile(lw["diff_norm_g"], 2).reshape(1, LANE)

    if sample is None:
        past, l_tot = 0, sq
        nch = sq // CMP_STRIDE
        tq = _tile(sq, 256)
        tk_sel = _tile(sq, 512)
        nk_sel = sq // tk_sel
        tw = _tile(sq, 256)
        nband = WINDOW // tw + 1
        win_all, kpos0_w, nk_w = winkv, 0, nband
        win_kt = lambda i, j: (jnp.maximum(i - (nband - 1) + j, 0), i - (nband - 1) + j >= 0)
        tq_w = tw
        free_w = nband - 2 if nband == 3 else None
        tq_d = _tile(sq, 256)
        tk_d = _tile(sq, 512)
        nk_d = sq // tk_d
        xcat = jnp.concatenate([jnp.zeros((b, 16, 512), F32), zpool], axis=1)
        zgm_in = zgm
    else:
        layer = sample["layer"]
        pt = sample["page_table"]
        past = pt.shape[1] * PAGE
        l_tot = past + sq
        nch = past // CMP_STRIDE
        tq = sq
        wbuf = sample["win"].shape[1]
        wrows = -(-(wbuf + sq) // KEY_SUB) * KEY_SUB
        tw = wrows
        win_all = jnp.concatenate([sample["win"], winkv, jnp.zeros((b, wrows - wbuf - sq, 256), F32)], axis=1)
        kpos0_w, nk_w = past - wbuf, 1
        win_kt = lambda i, j: (j, j >= 0)
        tq_w = sq
        free_w = None
        xcat = jnp.concatenate([jnp.zeros((b, 1, 512), F32), sample["pool"], zpool], axis=1)
        zgm_in = jnp.concatenate([zgm, jnp.zeros((b, GM_CHUNK - sq, 1024), F32)], axis=1)

    n_cmp = l_tot // CMP_STRIDE - 1
    n_sel = -(-l_tot // SEL_BLOCK)
    nsp = -(-n_sel // LANE) * LANE
    msel = jnp.asarray(_cmp_to_sel(n_cmp, n_sel, nch, nsp), BF16)
    if sample is None:
        kcvc = compress(rows4, wc, cbias, w2b, nch)
        ocmp, selmask = cmp_attend(q, kcvc, msel, tq=tq, n_cmp=n_cmp, n_sel=n_sel, qpos0=0)
        emat = jnp.asarray(_block_expand(nsp, sq), BF16)

        def sel_kt(i, j):
            last = ((i + 1) * tq - 1) // tk_sel
            return jnp.minimum(j, last), j <= last

        osel = nsa_flash(q, rows4, 2, 3, mode="sel", tq=tq, tk=tk_sel, nk=nk_sel, qpos0=0, kpos0=0,
                         kt_fn=sel_kt, sel=selmask, emat=emat)
    else:
        kcvc = compress_paged(pt, sample["cache_nsa"], layer, wc, cbias, w2b)
        ocmp, selmask = cmp_attend(q, kcvc, msel, tq=tq, n_cmp=n_cmp, n_sel=n_sel, qpos0=past)
        emat = jnp.asarray(_block_expand(nsp, past + LANE), BF16)
        osel = sel_paged(pt, sample["cache_nsa"], layer, q, rows4, selmask, emat)
    owin = nsa_flash(q, win_all, 0, 1, mode="win", tq=tq_w, tk=tw, nk=nk_w, qpos0=past, kpos0=kpos0_w,
                     kt_fn=win_kt, free_step=free_w)

    opool = pool_mix(xcat, lw["pool_w"], lw["pool_scale"], past)
    ws_tril = jnp.tril(lw["gm_ws"]).astype(BF16)
    bs_exp = jnp.repeat(lw["gm_bs"].T, LANE, axis=1)
    ogm, gm_v = gmlp_mix(zgm_in, lw["gm_ng"], lw["gm_nb"], ws_tril, bs_exp)

    if sample is None:
        def diff_kt(i, j):
            last = ((i + 1) * tq_d - 1) // tk_d
            return jnp.minimum(j, last), j <= last

        odiff = diff_attend(lam_par, dq, dkv, gnorm, tq=tq_d, tk=tk_d, nk=nk_d, qpos0=0,
                            lam_init=lam_init, kt_fn=diff_kt)
    else:
        odiff = diff_paged(pt, sample["cache_diff"], layer, lam_par, dq, dkv, gnorm, lam_init)
        ogm = ogm[:, :sq]
        gm_v = gm_v[:, :sq]
    wbr = lw["w_branch"].at[0].set(lw["w_branch"][0][qperm]).astype(BF16)
    out = finish_mixer(x, hn, g3, ocmp.reshape(t, 512), osel.reshape(t, 512), owin.reshape(t, 512),
                       opool.reshape(t, 512), ogm.reshape(t, 512), odiff.reshape(t, 512),
                       lw["w_gate"].astype(BF16), lw["b_gate"].reshape(1, -1), wbr, lw["w_o"].astype(BF16))
    states = dict(rows4=rows4, winkv=winkv, win_all=win_all, xcat=xcat, gm_v=gm_v, dkv=dkv)
    return out, states


def kernel(x_prompt, x_sample, cache_nsa, cache_diff, state_nsa_win, state_pool, page_table, p_prompt, p_sample, norm_mix_g, w_in, nsa_cmp_pe, nsa_cmp_w1, nsa_cmp_w2, pool_w, pool_scale, gm_norm_g, gm_norm_b, gm_ws, gm_bs, diff_lq1, diff_lk1, diff_lq2, diff_lk2, diff_norm_g, w_branch, w_gate, b_gate, w_o, norm_ffn_g, ffn_w_gate, ffn_w_up, ffn_w_down, moe_router, moe_router_b, moe_w_gate, moe_w_up, moe_w_down, ple_norm_g, ple_w_gate, ple_w_proj, final_norm_g):
    bp, sp, d = x_prompt.shape
    bs, ss, _ = x_sample.shape
    depth = w_in.shape[0]
    n_phys = cache_nsa.shape[1]
    wbuf = state_nsa_win.shape[2]
    cache_nsa2 = jnp.transpose(cache_nsa, (0, 1, 3, 4, 5, 2)).reshape(depth, n_phys, 512, PAGE)
    cache_diff2 = jnp.transpose(cache_diff, (0, 1, 3, 4, 5, 2)).reshape(depth, n_phys, 1024, PAGE)
    xp = x_prompt.reshape(bp * sp, d)
    xs = x_sample.reshape(bs * ss, d)
    outs = {k: [] for k in ("nsa_p", "nsa_s", "win_p", "win_s", "pool_p", "pool_s", "gmv_s", "diff_p", "diff_s")}
    for l in range(depth):
        lw = dict(norm_mix_g=norm_mix_g[l], w_in=w_in[l], cmp_pe=nsa_cmp_pe[l], cmp_w1=nsa_cmp_w1[l],
                  cmp_w2=nsa_cmp_w2[l], pool_w=pool_w[l], pool_scale=pool_scale[l], gm_ng=gm_norm_g[l],
                  gm_nb=gm_norm_b[l], gm_ws=gm_ws[l], gm_bs=gm_bs[l], diff_lq1=diff_lq1[l], diff_lk1=diff_lk1[l],
                  diff_lq2=diff_lq2[l], diff_lk2=diff_lk2[l], diff_norm_g=diff_norm_g[l], w_branch=w_branch[l],
                  w_gate=w_gate[l], b_gate=b_gate[l], w_o=w_o[l])
        lam_init = 0.8 - 0.6 * math.exp(-0.3 * l)
        xp, st_p = _mixer(xp, lw, lam_init, b=bp, sq=sp)
        sample = dict(cache_nsa=cache_nsa2, cache_diff=cache_diff2, win=state_nsa_win[l].reshape(bs, wbuf, 256),
                      pool=state_pool[l], page_table=page_table, layer=l)
        xs, st_s = _mixer(xs, lw, lam_init, b=bs, sq=ss, sample=sample)
        outs["nsa_p"].append(st_p["rows4"].reshape(bp, sp, 4, NSA_G, DH))
        outs["nsa_s"].append(st_s["rows4"].reshape(bs, ss, 4, NSA_G, DH))
        wkeep = min(WINDOW, sp)
        outs["win_p"].append(st_p["winkv"][:, sp - wkeep:].reshape(bp, wkeep, 2, NSA_G, DH))
        outs["win_s"].append(st_s["win_all"][:, ss:ss + wbuf].reshape(bs, wbuf, 2, NSA_G, DH))
        outs["pool_p"].append(st_p["xcat"][:, -POOL_MEM:])
        outs["pool_s"].append(st_s["xcat"][:, -POOL_MEM:])
        outs["gmv_s"].append(st_s["gm_v"])
        outs["diff_p"].append(st_p["dkv"].reshape(bp, sp, 2, DIFF_HEADS, 2 * DIFF_QK))
        outs["diff_s"].append(st_s["dkv"].reshape(bs, ss, 2, DIFF_HEADS, 2 * DIFF_QK))
        i = l // 2
        if l % 2 == 0:
            xp = ffn_swiglu(xp, norm_ffn_g[l], ffn_w_gate[i], ffn_w_up[i], ffn_w_down[i])
            xs = ffn_swiglu(xs, norm_ffn_g[l], ffn_w_gate[i], ffn_w_up[i], ffn_w_down[i])
        else:
            xp = moe_swiglu(xp, norm_ffn_g[l], moe_router[i], moe_router_b[i], moe_w_gate[i], moe_w_up[i], moe_w_down[i])
            xs = moe_swiglu(xs, norm_ffn_g[l], moe_router[i], moe_router_b[i], moe_w_gate[i], moe_w_up[i], moe_w_down[i])
        final = l == depth - 1
        xp = ple(xp, p_prompt[l].reshape(bp * sp, -1), ple_norm_g[l], ple_w_gate[l], ple_w_proj[l], final_norm_g, final)
        xs = ple(xs, p_sample[l].reshape(bs * ss, -1), ple_norm_g[l], ple_w_gate[l], ple_w_proj[l], final_norm_g, final)
    st = lambda k: jnp.stack(outs[k])
    return (xp.reshape(bp, sp, d), xs.reshape(bs, ss, d), st("nsa_p"), st("nsa_s"), st("win_p"), st("win_s"),
            st("pool_p"), st("pool_s"), st("gmv_s"), st("diff_p"), st("diff_s"))
```

```python
import functools
import math

import numpy as np
import jax
import jax.numpy as jnp
from jax import lax
from jax.experimental import pallas as pl
from jax.experimental.pallas import tpu as pltpu

F32 = jnp.float32
BF16 = jnp.bfloat16

D_MODEL = 1024
PAGE = 128
NSA_HEADS = 8
NSA_G = 2
NSA_J = 4
DH = 64
CMP_BLOCK = 32
CMP_STRIDE = 16
SEL_BLOCK = 64
SEL_TOPK = 16
FORCE_BONUS = 100.0
WINDOW = 512
POOL_WINDOWS = (2, 4, 8, 16)
POOL_MEM = 15
GM_W = 512
GM_CHUNK = 128
DIFF_HEADS = 8
DIFF_QK = 32
N_BRANCH = 4
BRANCH_W = 512
N_EXPERTS = 8
EPS = 1e-6
NEG = -1e30
LOG2E = 1.4426950408889634
LANE = 128
VMEM_LIMIT = 56 * 1024 * 1024

OFF_Q, OFF_KV, OFF_GATE, OFF_POOL, OFF_GM, OFF_DQ, OFF_DKV, OFF_END = 0, 512, 1280, 1304, 1816, 2840, 3352, 4376


def _tile(n, pref):
    t = min(n, pref)
    while n % t:
        t //= 2
    return t


def _cp(sem, vmem=VMEM_LIMIT):
    return pltpu.CompilerParams(dimension_semantics=sem, vmem_limit_bytes=vmem)


def _gelu(x):
    return 0.5 * x * (1.0 + jnp.tanh(0.7978845608028654 * (x + 0.044715 * (x * x * x))))


def _rms(x, g):
    return x * lax.rsqrt(jnp.mean(x * x, axis=-1, keepdims=True) + EPS) * g


def _dot(a, b):
    return jnp.dot(a, b, preferred_element_type=F32)


def _dot_nt(a, b):
    return lax.dot_general(a, b, (((1,), (1,)), ((), ())), preferred_element_type=F32)


def _rmsnorm_body(x_ref, g_ref, o_ref):
    o_ref[...] = _rms(x_ref[...], g_ref[...]).astype(o_ref.dtype)


def rmsnorm(x, g, out_dtype):
    t, d = x.shape
    tm = _tile(t, 512)
    return pl.pallas_call(
        _rmsnorm_body, name="rmsnorm", grid=(t // tm,),
        in_specs=[pl.BlockSpec((tm, d), lambda i: (i, 0)), pl.BlockSpec((1, d), lambda i: (0, 0))],
        out_specs=pl.BlockSpec((tm, d), lambda i: (i, 0)),
        out_shape=jax.ShapeDtypeStruct((t, d), out_dtype),
        compiler_params=_cp(("parallel",)),
    )(x, g.reshape(1, d))


def _mm_body(x_ref, w_ref, o_ref, *, scale, act):
    y = _dot(x_ref[...].astype(BF16), w_ref[...])
    if scale is not None:
        y = y * scale
    if act == "sigmoid":
        y = jax.nn.sigmoid(y)
    o_ref[...] = y.astype(o_ref.dtype)


def mm(x, w, *, scale=None, act=None, out_dtype=F32):
    t, k = x.shape
    n = w.shape[1]
    tm = _tile(t, 512)
    tn = _tile(n, 512)
    return pl.pallas_call(
        functools.partial(_mm_body, scale=scale, act=act), name="mm", grid=(t // tm, n // tn),
        in_specs=[pl.BlockSpec((tm, k), lambda i, j: (i, 0)), pl.BlockSpec((k, tn), lambda i, j: (0, j))],
        out_specs=pl.BlockSpec((tm, tn), lambda i, j: (i, j)),
        out_shape=jax.ShapeDtypeStruct((t, n), out_dtype),
        compiler_params=_cp(("parallel", "parallel")),
    )(x, w.astype(BF16))


def _in_proj_body(x_ref, g_ref, w_ref, hn_ref, *o_refs, segs):
    hn = _rms(x_ref[...], g_ref[...]).astype(BF16)
    hn_ref[...] = hn
    c0 = 0
    for (n, scale, act), o_ref in zip(segs, o_refs):
        y = _dot(hn, w_ref[:, c0:c0 + n])
        if scale is not None:
            y = y * scale
        if act == "sigmoid":
            y = jax.nn.sigmoid(y)
        o_ref[...] = y
        c0 += n


def in_proj(x, g, w_all, segs):
    t, d = x.shape
    tm = _tile(t, 256)
    ntot = w_all.shape[1]
    return pl.pallas_call(
        functools.partial(_in_proj_body, segs=segs), name="in_proj", grid=(t // tm,),
        in_specs=[pl.BlockSpec((tm, d), lambda i: (i, 0)), pl.BlockSpec((1, d), lambda i: (0, 0)),
                  pl.BlockSpec((d, ntot), lambda i: (0, 0), pipeline_mode=pl.Buffered(1))],
        out_specs=[pl.BlockSpec((tm, d), lambda i: (i, 0))] + [pl.BlockSpec((tm, n), lambda i: (i, 0)) for n, _, _ in segs],
        out_shape=[jax.ShapeDtypeStruct((t, d), BF16)] + [jax.ShapeDtypeStruct((t, n), F32) for n, _, _ in segs],
        compiler_params=_cp(("parallel",)),
    )(x, g.reshape(1, d), w_all)


def _compress_body(xk_ref, xv_ref, wc_ref, bias_ref, w2_ref, o_ref, *, nch):
    _compress_core(lambda s: xk_ref[pl.ds(s, nch, stride=CMP_STRIDE), :],
                   lambda s: xv_ref[pl.ds(s, nch, stride=CMP_STRIDE), :], wc_ref, bias_ref, w2_ref, o_ref, nch)


def _compress_core(load_k, load_v, wc_ref, bias_ref, w2_ref, o_ref, nch):
    xs = jnp.concatenate([part(s) for s in range(CMP_STRIDE) for part in (load_k, load_v)], axis=1).astype(BF16)
    acc = _dot(xs, wc_ref[...])
    hi_next = pltpu.roll(acc[:, 256:], nch - 1, 0)
    hid = _gelu(acc[:, :256] + hi_next + bias_ref[...])
    o_ref[...] = _dot(hid.astype(BF16), w2_ref[...]).astype(o_ref.dtype)


def compress(rows, wc, bias, w2, nch):
    b = rows.shape[0]
    return pl.pallas_call(
        functools.partial(_compress_body, nch=nch), name="compress", grid=(b,),
        in_specs=[pl.BlockSpec((None, nch * CMP_STRIDE, LANE), lambda i: (i, 0, 0)),
                  pl.BlockSpec((None, nch * CMP_STRIDE, LANE), lambda i: (i, 0, 1)),
                  pl.BlockSpec((CMP_STRIDE * 256, 512), lambda i: (0, 0)),
                  pl.BlockSpec((1, 256), lambda i: (0, 0)),
                  pl.BlockSpec((256, 256), lambda i: (0, 0))],
        out_specs=pl.BlockSpec((None, nch, 256), lambda i: (i, 0, 0)),
        out_shape=jax.ShapeDtypeStruct((b, nch, 256), BF16),
        compiler_params=_cp(("parallel",)),
    )(rows, rows, wc, bias, w2)


def _stack_heads(q, tq):
    lane = lax.broadcasted_iota(jnp.int32, (tq, LANE), 1)
    parts = []
    for g in range(NSA_G):
        keep = (lane < DH) if g == 0 else (lane >= DH)
        for j in range(NSA_J):
            parts.append(jnp.where(keep, q[:, j * LANE:(j + 1) * LANE], 0.0))
    return jnp.concatenate(parts, axis=0)


def _unstack_heads(o, tq):
    lane = lax.broadcasted_iota(jnp.int32, (tq, LANE), 1)
    outs = []
    for j in range(NSA_J):
        outs.append(jnp.where(lane < DH, o[j * tq:(j + 1) * tq], o[(NSA_J + j) * tq:(NSA_J + j + 1) * tq]))
    return jnp.concatenate(outs, axis=1)


def _cmp_body(q_ref, kv_ref, msel_ref, o_ref, sel_ref, *, tq, nch, n_cmp, n_sel, nsp, qpos0):
    i = pl.program_id(1)
    r = 8 * tq
    qs = _stack_heads(q_ref[...], tq).astype(BF16)
    kc = kv_ref[:, 0:LANE]
    vc = kv_ref[:, LANE:2 * LANE]
    s = _dot_nt(qs, kc)
    row = lax.broadcasted_iota(jnp.int32, (r, nch), 0)
    col = lax.broadcasted_iota(jnp.int32, (r, nch), 1)
    qpos = qpos0 + i * tq + (row & (tq - 1))
    cmask = (col * CMP_STRIDE + (CMP_BLOCK - 1) <= qpos) & (col < n_cmp)
    s = jnp.where(cmask, s, NEG)
    m = jnp.max(s, axis=1, keepdims=True)
    e = jnp.where(cmask, jnp.exp2(s - m), 0.0)
    l = jnp.sum(e, axis=1, keepdims=True)
    p = (e * (1.0 / jnp.where(l > 0.0, l, 1.0))).astype(BF16)
    o_ref[...] = _unstack_heads(_dot(p, vc), tq)
    imp_all = _dot(p, msel_ref[...])
    blocks_on_rows = tq % LANE == 0
    nrow = -(-n_sel // 8) * 8
    shape = (nrow, tq) if blocks_on_rows else (tq, nsp)
    baxis, qaxis = (0, 1) if blocks_on_rows else (1, 0)
    blk = lax.broadcasted_iota(jnp.int32, shape, baxis)
    qp = qpos0 + i * tq + lax.broadcasted_iota(jnp.int32, shape, qaxis)
    cur = lax.shift_right_logical(qp, SEL_BLOCK.bit_length() - 1)
    valid = (blk <= cur) & (blk < n_sel)
    forced = (blk == 0) | (blk == cur) | (blk == cur - 1)
    for g in range(NSA_G):
        imp = imp_all[(g * NSA_J) * tq:(g * NSA_J + 1) * tq]
        for j in range(1, NSA_J):
            imp = imp + imp_all[(g * NSA_J + j) * tq:(g * NSA_J + j + 1) * tq]
        if blocks_on_rows:
            imp = imp.T[:nrow]
        score = jnp.where(valid, imp + FORCE_BONUS * forced.astype(F32), -1.0)
        score = jnp.where(blk < n_sel, score, -2.0)
        rank = jnp.zeros(shape, F32)
        for mth in range(n_sel):
            cm = score[mth:mth + 1, :] if blocks_on_rows else score[:, mth:mth + 1]
            beats = (cm > score) | ((cm == score) & (blk > mth))
            rank = rank + beats.astype(F32)
        chosen = ((rank < float(min(SEL_TOPK, n_sel))) & valid).astype(F32)
        if blocks_on_rows:
            chosen = jnp.concatenate([chosen, jnp.zeros((nsp - nrow, tq), F32)], axis=0).T
        sel_ref[:, g * nsp:(g + 1) * nsp] = chosen


def cmp_attend(q, kcvc, msel, *, tq, n_cmp, n_sel, qpos0):
    b, sq, _ = q.shape
    nch = kcvc.shape[1]
    nsp = msel.shape[1]
    return pl.pallas_call(
        functools.partial(_cmp_body, tq=tq, nch=nch, n_cmp=n_cmp, n_sel=n_sel, nsp=nsp, qpos0=qpos0),
        name="cmp_attend",
        grid=(b, sq // tq),
        in_specs=[pl.BlockSpec((None, tq, 512), lambda bi, i: (bi, i, 0)),
                  pl.BlockSpec((None, nch, 256), lambda bi, i: (bi, 0, 0)),
                  pl.BlockSpec((nch, nsp), lambda bi, i: (0, 0))],
        out_specs=[pl.BlockSpec((None, tq, 512), lambda bi, i: (bi, i, 0)),
                   pl.BlockSpec((None, tq, 2 * nsp), lambda bi, i: (bi, i, 0))],
        out_shape=[jax.ShapeDtypeStruct((b, sq, 512), F32), jax.ShapeDtypeStruct((b, sq, 2 * nsp), F32)],
        compiler_params=_cp(("parallel", "parallel")),
    )(q, kcvc, msel)


ROW_BLOCK = 128
KEY_SUB = 256


def _flash_update(qs, k, v, bias, m, l, acc, kv_t=False):
    s = _dot(qs, k) if kv_t else _dot_nt(qs, k)
    if bias is not None:
        s = s + bias
    m_new, l_new, alpha, pb = _softmax_part(s, m, l)
    acc_new = alpha * acc + (_dot_nt(pb, v) if kv_t else _dot(pb, v))
    return m_new, l_new, acc_new


def _softmax_part(s, m, l):
    tiles = [s[:, c:c + LANE] for c in range(0, s.shape[1], LANE)]
    m_new = jnp.maximum(m, jnp.max(functools.reduce(jnp.maximum, tiles), axis=1, keepdims=True))
    alpha = jnp.exp2(m - m_new)
    ps = [jnp.exp2(t - m_new) for t in tiles]
    l_new = alpha * l + jnp.sum(functools.reduce(jnp.add, ps), axis=1, keepdims=True)
    return m_new, l_new, alpha, jnp.concatenate(ps, axis=1).astype(BF16)


def _flash_rows(qs_ref, k, v, bias_of, m_ref, l_ref, acc_ref, nrows, tq, kv_t=False):
    if tq < ROW_BLOCK:
        parts = [bias_of(h, 0, tq) for h in range(nrows // tq)]
        bias = None if parts[0] is None else jnp.concatenate(parts, axis=0)
        blocks = [(pl.ds(0, nrows), bias)]
    else:
        blocks = [(pl.ds(r0, ROW_BLOCK), bias_of(r0 // tq, r0 % tq, ROW_BLOCK)) for r0 in range(0, nrows, ROW_BLOCK)]
    for sl, bias in blocks:
        m, l, acc = _flash_update(qs_ref[sl], k, v, bias, m_ref[sl], l_ref[sl], acc_ref[sl], kv_t)
        m_ref[sl] = m
        l_ref[sl] = l
        acc_ref[sl] = acc


def _sel_ok(sel, e, nsp):
    return [_dot(sel[:, g * nsp:(g + 1) * nsp], e) > 0.5 for g in range(NSA_G)]


def _nsa_flash_body(*refs, mode, tq, tk, nk, qpos0, kpos0, kt_fn, nsp, free_step):
    if mode == "sel":
        q_ref, k_ref, v_ref, sel_ref, e_ref, o_ref, qs_ref, m_ref, l_ref, acc_ref = refs
    else:
        q_ref, k_ref, v_ref, o_ref, qs_ref, m_ref, l_ref, acc_ref = refs
    i = pl.program_id(1)
    j = pl.program_id(2)
    r = 8 * tq

    @pl.when(j == 0)
    def _init():
        qs_ref[...] = _stack_heads(q_ref[...], tq).astype(BF16)
        m_ref[...] = jnp.full((r, LANE), NEG, F32)
        l_ref[...] = jnp.zeros((r, LANE), F32)
        acc_ref[...] = jnp.zeros((r, LANE), F32)

    kt, valid = kt_fn(i, j)

    ks = min(tk, KEY_SUB)

    def step(c0, masked):
        k = k_ref[c0:c0 + ks, :].astype(BF16)
        v = v_ref[c0:c0 + ks, :].astype(BF16)
        if not masked:
            _flash_rows(qs_ref, k, v, lambda h, t0, n: None, m_ref, l_ref, acc_ref, r, tq)
            return
        qpos = qpos0 + i * tq + lax.broadcasted_iota(jnp.int32, (tq, ks), 0)
        kpos = kpos0 + kt * tk + c0 + lax.broadcasted_iota(jnp.int32, (tq, ks), 1)
        ok = kpos <= qpos
        if mode == "win":
            ok = ok & (kpos > qpos - WINDOW)
            bias = [jnp.where(ok, 0.0, NEG)] * NSA_G
        else:
            oks = _sel_ok(sel_ref[...].astype(BF16), e_ref[:, c0:c0 + ks], nsp)
            bias = [jnp.where(ok & okg, 0.0, NEG) for okg in oks]
        _flash_rows(qs_ref, k, v, lambda h, t0, n: bias[h // NSA_J][t0:t0 + n], m_ref, l_ref, acc_ref, r, tq)

    for c0 in range(0, tk, ks):
        live = valid & (kpos0 + kt * tk + c0 <= qpos0 + (i + 1) * tq - 1)
        if free_step is None:
            pl.when(live)(functools.partial(step, c0, True))
        else:
            pl.when(live & (j != free_step))(functools.partial(step, c0, True))
            pl.when(live & (j == free_step))(functools.partial(step, c0, False))

    @pl.when(j == nk - 1)
    def _fin():
        l = l_ref[...]
        o = acc_ref[...] * (1.0 / jnp.where(l > 0.0, l, 1.0))
        o_ref[...] = _unstack_heads(o, tq)


def nsa_flash(q, kv, kcol, vcol, *, mode, tq, tk, nk, qpos0, kpos0, kt_fn, sel=None, emat=None, free_step=None):
    b, sq, _ = q.shape
    nsp = 0 if sel is None else sel.shape[2] // 2

    def kmap(col):
        return lambda bi, i, j: (bi, kt_fn(i, j)[0], col)

    in_specs = [pl.BlockSpec((None, tq, 512), lambda bi, i, j: (bi, i, 0)),
                pl.BlockSpec((None, tk, LANE), kmap(kcol)),
                pl.BlockSpec((None, tk, LANE), kmap(vcol))]
    args = [q, kv, kv]
    if mode == "sel":
        in_specs += [pl.BlockSpec((None, tq, 2 * nsp), lambda bi, i, j: (bi, i, 0)),
                     pl.BlockSpec((nsp, tk), lambda bi, i, j: (0, kt_fn(i, j)[0]))]
        args += [sel, emat]
    r = 8 * tq
    return pl.pallas_call(
        functools.partial(_nsa_flash_body, mode=mode, tq=tq, tk=tk, nk=nk, qpos0=qpos0, kpos0=kpos0,
                          kt_fn=kt_fn, nsp=nsp, free_step=free_step),
        name="nsa_" + mode, grid=(b, sq // tq, nk),
        in_specs=in_specs,
        out_specs=pl.BlockSpec((None, tq, 512), lambda bi, i, j: (bi, i, 0)),
        out_shape=jax.ShapeDtypeStruct((b, sq, 512), F32),
        scratch_shapes=[pltpu.VMEM((r, LANE), BF16), pltpu.VMEM((r, LANE), F32), pltpu.VMEM((r, LANE), F32),
                        pltpu.VMEM((r, LANE), F32)],
        compiler_params=_cp(("parallel", "parallel", "arbitrary")),
    )(*args)


def _diff_init(q_ref, qs_ref, m_ref, l_ref, acc_ref, tq):
    r = 4 * tq
    npair = DIFF_HEADS // 2
    lane = lax.broadcasted_iota(jnp.int32, (tq, LANE), 1)
    for hp in range(npair):
        q = q_ref[:, hp * LANE:(hp + 1) * LANE]
        parts = []
        for h in range(2):
            for mth in range(2):
                lo = h * 2 * DIFF_QK + mth * DIFF_QK
                parts.append(jnp.where((lane >= lo) & (lane < lo + DIFF_QK), q, 0.0))
        qs_ref[hp] = jnp.concatenate(parts, axis=0).astype(BF16)
    m_ref[...] = jnp.full((npair, r, LANE), NEG, F32)
    l_ref[...] = jnp.zeros((npair, r, LANE), F32)
    acc_ref[...] = jnp.zeros((npair, r, LANE), F32)


def _diff_step(k_of, v_of, bias, qs_ref, m_ref, l_ref, acc_ref, tq, kv_t=False):
    bias_of = lambda h, t0, n: None if bias is None else bias[t0:t0 + n]
    for hp in range(DIFF_HEADS // 2):
        _flash_rows(qs_ref.at[hp], k_of(hp), v_of(hp), bias_of, m_ref.at[hp], l_ref.at[hp], acc_ref.at[hp],
                    4 * tq, tq, kv_t)


def _diff_fin(lam_ref, g_ref, o_ref, l_ref, acc_ref, tq, lam_init):
    lp = lam_ref[...]
    lam = (jnp.exp(jnp.sum(lp[0:1] * lp[1:2], axis=1, keepdims=True))
           - jnp.exp(jnp.sum(lp[2:3] * lp[3:4], axis=1, keepdims=True)) + lam_init)
    lane = lax.broadcasted_iota(jnp.int32, (tq, LANE), 1)
    lo_half = lane < 2 * DIFF_QK
    for hp in range(DIFF_HEADS // 2):
        l = l_ref[hp]
        a = acc_ref[hp] * (1.0 / jnp.where(l > 0.0, l, 1.0))
        o0 = a[0:tq] - lam * a[tq:2 * tq]
        o1 = a[2 * tq:3 * tq] - lam * a[3 * tq:4 * tq]
        o = jnp.where(lo_half, o0, o1)
        sq = o * o
        ms0 = jnp.sum(jnp.where(lo_half, sq, 0.0), axis=1, keepdims=True)
        ms1 = jnp.sum(jnp.where(lo_half, 0.0, sq), axis=1, keepdims=True)
        ms = jnp.where(lo_half, ms0, ms1) * (1.0 / (2 * DIFF_QK))
        y = o * lax.rsqrt(ms + EPS) * g_ref[...]
        o_ref[:, hp * LANE:(hp + 1) * LANE] = y * (1.0 - lam_init)


def _diff_body(lam_ref, q_ref, k_ref, v_ref, g_ref, o_ref, qs_ref, m_ref, l_ref, acc_ref, *,
               tq, tk, nk, qpos0, lam_init, kt_fn):
    i = pl.program_id(1)
    j = pl.program_id(2)
    r = 4 * tq

    @pl.when(j == 0)
    def _init():
        _diff_init(q_ref, qs_ref, m_ref, l_ref, acc_ref, tq)

    kt, valid = kt_fn(i, j)
    ks = min(tk, KEY_SUB)

    def step(c0, masked):
        k_of = lambda hp: k_ref[c0:c0 + ks, hp * LANE:(hp + 1) * LANE].astype(BF16)
        v_of = lambda hp: v_ref[c0:c0 + ks, hp * LANE:(hp + 1) * LANE].astype(BF16)
        bias = None
        if masked:
            qpos = qpos0 + i * tq + lax.broadcasted_iota(jnp.int32, (tq, ks), 0)
            kpos = kt * tk + c0 + lax.broadcasted_iota(jnp.int32, (tq, ks), 1)
            bias = jnp.where(kpos <= qpos, 0.0, NEG)
        _diff_step(k_of, v_of, bias, qs_ref, m_ref, l_ref, acc_ref, tq)

    for c0 in range(0, tk, ks):
        k_first = kt * tk + c0
        live = valid & (k_first <= qpos0 + (i + 1) * tq - 1)
        below = k_first + ks - 1 <= qpos0 + i * tq
        pl.when(live & below)(functools.partial(step, c0, False))
        pl.when(live & jnp.logical_not(below))(functools.partial(step, c0, True))

    @pl.when(j == nk - 1)
    def _fin():
        _diff_fin(lam_ref, g_ref, o_ref, l_ref, acc_ref, tq, lam_init)


def diff_attend(lam_par, q, kv, gnorm, *, tq, tk, nk, qpos0, lam_init, kt_fn):
    b, sq, _ = q.shape
    r = 4 * tq
    npair = DIFF_HEADS // 2
    return pl.pallas_call(
        functools.partial(_diff_body, tq=tq, tk=tk, nk=nk, qpos0=qpos0, lam_init=lam_init, kt_fn=kt_fn),
        name="diff_attend",
        grid=(b, sq // tq, nk),
        in_specs=[pl.BlockSpec((4, DIFF_QK), lambda bi, i, j: (0, 0)),
                  pl.BlockSpec((None, tq, 512), lambda bi, i, j: (bi, i, 0)),
                  pl.BlockSpec((None, tk, 512), lambda bi, i, j: (bi, kt_fn(i, j)[0], 0)),
                  pl.BlockSpec((None, tk, 512), lambda bi, i, j: (bi, kt_fn(i, j)[0], 1)),
                  pl.BlockSpec((1, LANE), lambda bi, i, j: (0, 0))],
        out_specs=pl.BlockSpec((None, tq, 512), lambda bi, i, j: (bi, i, 0)),
        out_shape=jax.ShapeDtypeStruct((b, sq, 512), F32),
        scratch_shapes=[pltpu.VMEM((npair, r, LANE), BF16), pltpu.VMEM((npair, r, LANE), F32),
                        pltpu.VMEM((npair, r, LANE), F32), pltpu.VMEM((npair, r, LANE), F32)],
        compiler_params=_cp(("parallel", "parallel", "arbitrary")),
    )(lam_par, q, kv, kv, gnorm)


def _pool_body(prev_ref, z_ref, pw_ref, sc_ref, o_ref, x_ref, *, sq, start):
    x_ref[pl.ds(0, 16), :] = prev_ref[...]
    x_ref[pl.ds(16, sq), :] = z_ref[...]
    pos = start + lax.broadcasted_iota(jnp.int32, (sq, LANE), 0)
    for g, w in enumerate(POOL_WINDOWS):
        c0 = g * LANE
        x = x_ref[pl.ds(16, sq), c0:c0 + LANE]
        tot = x
        for back in range(1, w):
            tot = tot + x_ref[pl.ds(16 - back, sq), c0:c0 + LANE]
        cnt = jnp.minimum(w, pos + 1).astype(F32)
        mix = tot / cnt - x
        y = _dot(mix.astype(BF16), pw_ref[g])
        o_ref[:, c0:c0 + LANE] = y * sc_ref[:, c0:c0 + LANE]


def pool_mix(prev16, z, pw, scale, start):
    b, sq, _ = z.shape
    return pl.pallas_call(
        functools.partial(_pool_body, sq=sq, start=start), name="pool_mix", grid=(b,),
        in_specs=[pl.BlockSpec((None, 16, 512), lambda i: (i, 0, 0)),
                  pl.BlockSpec((None, sq, 512), lambda i: (i, 0, 0)),
                  pl.BlockSpec((4, LANE, LANE), lambda i: (0, 0, 0)),
                  pl.BlockSpec((1, 512), lambda i: (0, 0))],
        out_specs=pl.BlockSpec((None, sq, 512), lambda i: (i, 0, 0)),
        out_shape=jax.ShapeDtypeStruct((b, sq, 512), F32),
        scratch_shapes=[pltpu.VMEM((sq + 16, 512), F32)],
        compiler_params=_cp(("parallel",)),
    )(prev16, z, pw.astype(BF16), scale.reshape(1, 512))


def _gmlp_body(z_ref, ng_ref, nb_ref, ws_ref, bs_ref, o_ref, v_ref, *, tg):
    z = _gelu(z_ref[...])
    u = z[:, :GM_W]
    vr = z[:, GM_W:]
    xc = vr - jnp.mean(vr, axis=-1, keepdims=True)
    v = xc * lax.rsqrt(jnp.mean(xc * xc, axis=-1, keepdims=True) + EPS) * ng_ref[...] + nb_ref[...]
    v_ref[...] = v
    vb = v.astype(BF16)
    for c in range(tg // GM_CHUNK):
        r0 = c * GM_CHUNK
        for g in range(4):
            c0 = g * LANE
            mixed = _dot(ws_ref[g], vb[r0:r0 + GM_CHUNK, c0:c0 + LANE]) + bs_ref[:, c0:c0 + LANE]
            o_ref[r0:r0 + GM_CHUNK, c0:c0 + LANE] = u[r0:r0 + GM_CHUNK, c0:c0 + LANE] * mixed


def gmlp_mix(z, ng, nb, ws_tril, bs_exp):
    b, s, _ = z.shape
    tg = _tile(s, 512)
    return pl.pallas_call(
        functools.partial(_gmlp_body, tg=tg), name="gmlp_mix", grid=(b, s // tg),
        in_specs=[pl.BlockSpec((None, tg, 1024), lambda bi, i: (bi, i, 0)),
                  pl.BlockSpec((1, 512), lambda bi, i: (0, 0)),
                  pl.BlockSpec((1, 512), lambda bi, i: (0, 0)),
                  pl.BlockSpec((4, GM_CHUNK, GM_CHUNK), lambda bi, i: (0, 0, 0)),
                  pl.BlockSpec((GM_CHUNK, 512), lambda bi, i: (0, 0))],
        out_specs=[pl.BlockSpec((None, tg, 512), lambda bi, i: (bi, i, 0)),
                   pl.BlockSpec((None, tg, 512), lambda bi, i: (bi, i, 0))],
        out_shape=[jax.ShapeDtypeStruct((b, s, 512), F32), jax.ShapeDtypeStruct((b, s, 512), F32)],
        compiler_params=_cp(("parallel", "parallel")),
    )(z, ng.reshape(1, 512), nb.reshape(1, 512), ws_tril, bs_exp)


def _finish_body(x_ref, hn_ref, g3_ref, ocmp_ref, osel_ref, owin_ref, opool_ref, ogm_ref, odiff_ref,
                 wgate_ref, bgate_ref, wbr_ref, wo_ref, o_ref):
    hn = hn_ref[...]
    g3 = g3_ref[...]
    onsa = g3[:, 0:512] * ocmp_ref[...] + g3[:, 512:1024] * osel_ref[...] + g3[:, 1024:1536] * owin_ref[...]
    branches = (onsa, opool_ref[...], ogm_ref[...], odiff_ref[...])
    acc = jnp.zeros(o_ref.shape, F32)
    for n in range(N_BRANCH):
        c0 = n * D_MODEL
        gate = jax.nn.sigmoid(_dot(hn, wgate_ref[:, c0:c0 + D_MODEL]) + bgate_ref[:, c0:c0 + D_MODEL])
        acc = acc + gate * _dot(branches[n].astype(BF16), wbr_ref[n])
    o_ref[...] = x_ref[...] + _dot(acc.astype(BF16), wo_ref[...])


def finish_mixer(x, hn, g3, ocmp, osel, owin, opool, ogm, odiff, wgate, bgate, wbr, wo):
    t = x.shape[0]
    tm = _tile(t, 256)
    row = lambda w: pl.BlockSpec((tm, w), lambda i: (i, 0))
    const = lambda shape: pl.BlockSpec(shape, lambda i: (0,) * len(shape), pipeline_mode=pl.Buffered(1))
    return pl.pallas_call(
        _finish_body, name="finish_mixer", grid=(t // tm,),
        in_specs=[row(1024), row(1024), row(1536), row(512), row(512), row(512), row(512), row(512), row(512),
                  const((D_MODEL, N_BRANCH * D_MODEL)), const((1, N_BRANCH * D_MODEL)),
                  const((N_BRANCH, BRANCH_W, D_MODEL)), const((D_MODEL, D_MODEL))],
        out_specs=row(1024),
        out_shape=jax.ShapeDtypeStruct((t, D_MODEL), F32),
        compiler_params=_cp(("parallel",)),
    )(x, hn, g3, ocmp, osel, owin, opool, ogm, odiff, wgate, bgate, wbr, wo)


def _ffn_body(x_ref, g_ref, wg_ref, wu_ref, wd_ref, o_ref, hn_ref, acc_ref, *, nf):
    j = pl.program_id(1)

    @pl.when(j == 0)
    def _init():
        hn_ref[...] = _rms(x_ref[...], g_ref[...]).astype(BF16)
        acc_ref[...] = jnp.zeros(acc_ref.shape, F32)

    h = hn_ref[...]
    a = _dot(h, wg_ref[...])
    act = a * jax.nn.sigmoid(a) * _dot(h, wu_ref[...])
    acc_ref[...] += _dot(act.astype(BF16), wd_ref[...])

    @pl.when(j == nf - 1)
    def _fin():
        o_ref[...] = x_ref[...] + acc_ref[...]


def ffn_swiglu(x, g, wg, wu, wd):
    t, d = x.shape
    f = wg.shape[1]
    tm = _tile(t, 512)
    tf = 1408 if f % 1408 == 0 else _tile(f, 512)
    nf = f // tf
    return pl.pallas_call(
        functools.partial(_ffn_body, nf=nf), name="ffn_swiglu", grid=(t // tm, nf),
        in_specs=[pl.BlockSpec((tm, d), lambda i, j: (i, 0)), pl.BlockSpec((1, d), lambda i, j: (0, 0)),
                  pl.BlockSpec((d, tf), lambda i, j: (0, j)), pl.BlockSpec((d, tf), lambda i, j: (0, j)),
                  pl.BlockSpec((tf, d), lambda i, j: (j, 0))],
        out_specs=pl.BlockSpec((tm, d), lambda i, j: (i, 0)),
        out_shape=jax.ShapeDtypeStruct((t, d), F32),
        scratch_shapes=[pltpu.VMEM((tm, d), BF16), pltpu.VMEM((tm, d), F32)],
        compiler_params=_cp(("parallel", "arbitrary")),
    )(x, g.reshape(1, d), wg.astype(BF16), wu.astype(BF16), wd.astype(BF16))


MOE_SUB = 256


def _moe_body(x_ref, g_ref, r_ref, rb_ref, u_ref, wg_ref, wu_ref, wd_ref, o_ref,
              hn_ref, comb_ref, post_ref, posr_ref, cnt_ref, xe_ref, ye_ref, acc_ref, *, nf):
    e = pl.program_id(1)
    f = pl.program_id(2)
    tm = x_ref.shape[0]
    lane = lax.broadcasted_iota(jnp.int32, (tm, LANE), 1)

    @pl.when((e == 0) & (f == 0))
    def _init():
        hn = _rms(x_ref[...], g_ref[...]).astype(BF16)
        hn_ref[...] = hn
        lg = _dot(hn, r_ref[...]) + rb_ref[...]
        m1 = jnp.max(lg, axis=1, keepdims=True)
        i1 = jnp.min(jnp.where(lg == m1, lane, LANE), axis=1, keepdims=True)
        lg2 = jnp.where(lane == i1, -3e38, lg)
        m2 = jnp.max(lg2, axis=1, keepdims=True)
        i2 = jnp.min(jnp.where(lg2 == m2, lane, LANE), axis=1, keepdims=True)
        e2 = jnp.exp(m2 - m1)
        w1 = 1.0 / (1.0 + e2)
        comb = jnp.where(lane == i1, w1, 0.0) + jnp.where(lane == i2, e2 * w1, 0.0)
        comb_ref[...] = comb
        ind_t = (comb.T > 0.0).astype(F32)
        pos_t = _dot(ind_t.astype(BF16), u_ref[...])
        post_ref[...] = jnp.where(ind_t > 0.0, pos_t, -1.0)
        posr_ref[...] = post_ref[...].T
        cnt = jnp.sum(ind_t, axis=1, keepdims=True)
        for ex in range(N_EXPERTS):
            cnt_ref[ex] = cnt[ex, 0].astype(jnp.int32)
        acc_ref[...] = jnp.zeros(acc_ref.shape, F32)

    cnt = cnt_ref[e]
    rem = cnt % MOE_SUB
    has_tail = (rem > 0) & (rem <= MOE_SUB // 2)
    n_full = cnt // MOE_SUB + (rem > MOE_SUB // 2).astype(jnp.int32)

    def each_subtile(fn):
        lax.fori_loop(0, n_full, lambda u, c: (fn(u, MOE_SUB), c)[1], 0)
        pl.when(has_tail)(lambda: fn(n_full, MOE_SUB // 2))

    @pl.when(f == 0)
    def _gather():
        slot_t = post_ref[pl.ds(e, 1), :]

        def body(u, rows):
            base = lax.broadcasted_iota(jnp.int32, (rows, tm), 0).astype(F32) + (u * MOE_SUB).astype(F32)
            onehot = jnp.where(slot_t == base, 1.0, 0.0).astype(BF16)
            xe_ref[u, pl.ds(0, rows)] = _dot(onehot, hn_ref[...]).astype(BF16)
            ye_ref[u, pl.ds(0, rows)] = jnp.zeros((rows, x_ref.shape[1]), F32)

        each_subtile(body)

    def expert(u, rows):
        xe = xe_ref[u, pl.ds(0, rows)]
        a = _dot(xe, wg_ref[...])
        act = a * jax.nn.sigmoid(a) * _dot(xe, wu_ref[...])
        ye_ref[u, pl.ds(0, rows)] += _dot(act.astype(BF16), wd_ref[...])

    each_subtile(expert)

    @pl.when(f == nf - 1)
    def _scatter():
        slot_r = jnp.sum(jnp.where(lane == e, posr_ref[...], 0.0), axis=1, keepdims=True)
        w_r = jnp.sum(jnp.where(lane == e, comb_ref[...], 0.0), axis=1, keepdims=True)

        def body(u, rows):
            base = lax.broadcasted_iota(jnp.int32, (tm, rows), 1).astype(F32) + (u * MOE_SUB).astype(F32)
            onehot = jnp.where(slot_r == base, 1.0, 0.0).astype(BF16)
            y = ye_ref[u, pl.ds(0, rows)]
            y_hi = y.astype(BF16)
            y_lo = (y - y_hi.astype(F32)).astype(BF16)
            acc_ref[...] += w_r * (_dot(onehot, y_hi) + _dot(onehot, y_lo))

        each_subtile(body)

    @pl.when((e == N_EXPERTS - 1) & (f == nf - 1))
    def _fin():
        o_ref[...] = x_ref[...] + acc_ref[...]


def moe_swiglu(x, g, router, router_b, wg, wu, wd):
    t, d = x.shape
    f = wg.shape[2]
    tm = _tile(t, 1024)
    tf = 896 if f % 896 == 0 else _tile(f, 512)
    nf = f // tf
    nsubmax = -(-tm // MOE_SUB)
    rpad = jnp.zeros((d, LANE), F32).at[:, :N_EXPERTS].set(router).astype(BF16)
    rbpad = jnp.full((1, LANE), NEG, F32).at[0, :N_EXPERTS].set(router_b)
    before = jnp.asarray(np.triu(np.ones((tm, tm), np.float32), 1), BF16)
    const = lambda shape: pl.BlockSpec(shape, lambda i, e, j: (0,) * len(shape))
    return pl.pallas_call(
        functools.partial(_moe_body, nf=nf), name="moe_swiglu", grid=(t // tm, N_EXPERTS, nf),
        in_specs=[pl.BlockSpec((tm, d), lambda i, e, j: (i, 0)), const((1, d)), const((d, LANE)), const((1, LANE)),
                  const((tm, tm)),
                  pl.BlockSpec((None, d, tf), lambda i, e, j: (e, 0, j)),
                  pl.BlockSpec((None, d, tf), lambda i, e, j: (e, 0, j)),
                  pl.BlockSpec((None, tf, d), lambda i, e, j: (e, j, 0))],
        out_specs=pl.BlockSpec((tm, d), lambda i, e, j: (i, 0)),
        out_shape=jax.ShapeDtypeStruct((t, d), F32),
        scratch_shapes=[pltpu.VMEM((tm, d), BF16), pltpu.VMEM((tm, LANE), F32), pltpu.VMEM((LANE, tm), F32),
                        pltpu.VMEM((tm, LANE), F32), pltpu.SMEM((N_EXPERTS,), jnp.int32),
                        pltpu.VMEM((nsubmax, MOE_SUB, d), BF16), pltpu.VMEM((nsubmax, MOE_SUB, d), F32),
                        pltpu.VMEM((tm, d), F32)],
        compiler_params=_cp(("parallel", "arbitrary", "arbitrary")),
    )(x, g.reshape(1, d), rpad, rbpad, before, wg.astype(BF16), wu.astype(BF16), wd.astype(BF16))


def _ple_body(x_ref, p_ref, g_ref, wg_ref, wp_ref, fg_ref, o_ref, *, final):
    x = x_ref[...]
    hn = _rms(x, g_ref[...]).astype(BF16)
    gate = jax.nn.sigmoid(_dot(hn, wg_ref[...]))
    y = x + gate * _dot(p_ref[...].astype(BF16), wp_ref[...])
    if final:
        y = _rms(y, fg_ref[...])
    o_ref[...] = y


def ple(x, p, g, wg, wp, fg, final):
    t, d = x.shape
    pd = p.shape[1]
    tm = _tile(t, 512)
    return pl.pallas_call(
        functools.partial(_ple_body, final=final), name="ple", grid=(t // tm,),
        in_specs=[pl.BlockSpec((tm, d), lambda i: (i, 0)), pl.BlockSpec((tm, pd), lambda i: (i, 0)),
                  pl.BlockSpec((1, d), lambda i: (0, 0)), pl.BlockSpec((d, d), lambda i: (0, 0)),
                  pl.BlockSpec((pd, d), lambda i: (0, 0)), pl.BlockSpec((1, d), lambda i: (0, 0))],
        out_specs=pl.BlockSpec((tm, d), lambda i: (i, 0)),
        out_shape=jax.ShapeDtypeStruct((t, d), F32),
        compiler_params=_cp(("parallel",)),
    )(x, p, g.reshape(1, d), wg.astype(BF16), wp.astype(BF16), fg.reshape(1, d))


def _page_copies(pt_ref, cache_ref, bufs, sem, slot, b, c, *, layer, pc, cols):
    out = []
    for p in range(pc):
        pg = pt_ref[b, c * pc + p]
        for buf, (f0, w) in zip(bufs, cols):
            out.append(pltpu.make_async_copy(cache_ref.at[layer, pg, pl.ds(f0, w), :], buf.at[slot, p],
                                             sem.at[slot]))
    return out


def _paged_pipeline(pt_ref, cache_ref, bufs, sem, *, nb, nc, **kw):
    b = pl.program_id(0)
    c = pl.program_id(1) if nc > 1 else 0
    step = b * nc + c
    slot = step % 2

    @pl.when(step == 0)
    def _first():
        for d in _page_copies(pt_ref, cache_ref, bufs, sem, 0, 0, 0, **kw):
            d.start()

    @pl.when(step + 1 < nb * nc)
    def _prefetch():
        nxt = step + 1
        for d in _page_copies(pt_ref, cache_ref, bufs, sem, 1 - slot, nxt // nc, nxt % nc, **kw):
            d.start()

    for d in _page_copies(pt_ref, cache_ref, bufs, sem, slot, b, c, **kw):
        d.wait()
    return slot


def _compress_paged_body(pt_ref, cache_ref, wc_ref, bias_ref, w2_ref, o_ref, bufk, bufv, sem, rowk, rowv, *,
                         layer, npg, nb, nch):
    slot = _paged_pipeline(pt_ref, cache_ref, (bufk, bufv), sem, nb=nb, nc=1, layer=layer, pc=npg,
                           cols=((0, LANE), (LANE, LANE)))

    def to_rows(p, carry):
        rows = pl.ds(pl.multiple_of(p * PAGE, PAGE), PAGE)
        rowk[rows, :] = bufk[slot, p].T
        rowv[rows, :] = bufv[slot, p].T
        return carry

    lax.fori_loop(0, npg, to_rows, 0, unroll=4)
    _compress_core(lambda s: rowk[pl.ds(s, nch, stride=CMP_STRIDE), :],
                   lambda s: rowv[pl.ds(s, nch, stride=CMP_STRIDE), :], wc_ref, bias_ref, w2_ref, o_ref, nch)


def compress_paged(page_table, cache, layer, wc, bias, w2):
    b, npg = page_table.shape
    past = npg * PAGE
    nch = past // CMP_STRIDE
    const = lambda shape: pl.BlockSpec(shape, lambda i, pt: (0,) * len(shape))
    return pl.pallas_call(
        functools.partial(_compress_paged_body, layer=layer, npg=npg, nb=b, nch=nch), name="compress_paged",
        grid_spec=pltpu.PrefetchScalarGridSpec(
            num_scalar_prefetch=1, grid=(b,),
            in_specs=[pl.BlockSpec(memory_space=pl.ANY), const((CMP_STRIDE * 256, 512)), const((1, 256)),
                      const((256, 256))],
            out_specs=pl.BlockSpec((None, nch, 256), lambda i, pt: (i, 0, 0)),
            scratch_shapes=[pltpu.VMEM((2, npg, LANE, PAGE), F32), pltpu.VMEM((2, npg, LANE, PAGE), F32),
                            pltpu.SemaphoreType.DMA((2,)),
                            pltpu.VMEM((past, LANE), F32), pltpu.VMEM((past, LANE), F32)]),
        out_shape=jax.ShapeDtypeStruct((b, nch, 256), BF16),
        compiler_params=_cp(("arbitrary",)),
    )(page_table, cache, wc, bias, w2)


def _new_tile(new, c0, tq):
    return jnp.concatenate([new[:, c0:c0 + LANE], jnp.zeros((LANE - tq, LANE), F32)], axis=0).astype(BF16)


def _sel_paged_body(pt_ref, cache_ref, q_ref, new_ref, sel_ref, e_ref, o_ref, bufk, bufv, sem, *,
                    layer, npg, nb, tq, tk, nsp):
    slot = _paged_pipeline(pt_ref, cache_ref, (bufk, bufv), sem, nb=nb, nc=1, layer=layer, pc=npg,
                           cols=((2 * LANE, LANE), (3 * LANE, LANE)))
    past = npg * PAGE
    ppc = tk // PAGE
    r = 8 * tq
    qs = _stack_heads(q_ref[...], tq).astype(BF16)
    sel = sel_ref[...].astype(BF16)
    m = jnp.full((r, LANE), NEG, F32)
    l = jnp.zeros((r, LANE), F32)
    acc = jnp.zeros((r, LANE), F32)

    def stacked_bias(oks):
        return jnp.concatenate([jnp.where(ok, 0.0, NEG) for ok in oks for _ in range(NSA_J)], axis=0)

    for c in range(npg // ppc):
        kt = jnp.concatenate([bufk[slot, c * ppc + u] for u in range(ppc)], axis=1).astype(BF16)
        vt = jnp.concatenate([bufv[slot, c * ppc + u] for u in range(ppc)], axis=1).astype(BF16)
        bias = stacked_bias(_sel_ok(sel, e_ref[:, c * tk:(c + 1) * tk], nsp))
        m, l, acc = _flash_update(qs, kt, vt, bias, m, l, acc, kv_t=True)
    new = new_ref[...]
    causal = (lax.broadcasted_iota(jnp.int32, (tq, LANE), 1) <= lax.broadcasted_iota(jnp.int32, (tq, LANE), 0))
    bias = stacked_bias([ok & causal for ok in _sel_ok(sel, e_ref[:, past:past + LANE], nsp)])
    m, l, acc = _flash_update(qs, _new_tile(new, 2 * LANE, tq), _new_tile(new, 3 * LANE, tq), bias, m, l, acc)
    o_ref[...] = _unstack_heads(acc * (1.0 / jnp.where(l > 0.0, l, 1.0)), tq)


def sel_paged(page_table, cache, layer, q, new_rows, sel, emat):
    b, npg = page_table.shape
    past = npg * PAGE
    tq = q.shape[1]
    nsp = sel.shape[2] // 2
    return pl.pallas_call(
        functools.partial(_sel_paged_body, layer=layer, npg=npg, nb=b, tq=tq, tk=_tile(past, 512), nsp=nsp),
        name="sel_paged",
        grid_spec=pltpu.PrefetchScalarGridSpec(
            num_scalar_prefetch=1, grid=(b,),
            in_specs=[pl.BlockSpec(memory_space=pl.ANY),
                      pl.BlockSpec((None, tq, 512), lambda i, pt: (i, 0, 0)),
                      pl.BlockSpec((None, tq, 512), lambda i, pt: (i, 0, 0)),
                      pl.BlockSpec((None, tq, 2 * nsp), lambda i, pt: (i, 0, 0)),
                      pl.BlockSpec((nsp, past + LANE), lambda i, pt: (0, 0))],
            out_specs=pl.BlockSpec((None, tq, 512), lambda i, pt: (i, 0, 0)),
            scratch_shapes=[pltpu.VMEM((2, npg, LANE, PAGE), F32), pltpu.VMEM((2, npg, LANE, PAGE), F32),
                            pltpu.SemaphoreType.DMA((2,))]),
        out_shape=jax.ShapeDtypeStruct((b, tq, 512), F32),
        compiler_params=_cp(("arbitrary",)),
    )(page_table, cache, q, new_rows, sel, emat)


def _diff_paged_body(pt_ref, cache_ref, lam_ref, q_ref, new_ref, g_ref, o_ref, bufk, bufv, sem,
                     qs_ref, m_ref, l_ref, acc_ref, acc2_ref, *, layer, nb, nc, pc, tq, lam_init):
    c = pl.program_id(1)
    slot = _paged_pipeline(pt_ref, cache_ref, (bufk, bufv), sem, nb=nb, nc=nc, layer=layer, pc=pc,
                           cols=((0, 512), (512, 512)))
    ppc = _tile(pc, 4)
    r = 4 * tq

    @pl.when(c == 0)
    def _init():
        _diff_init(q_ref, qs_ref, m_ref, l_ref, acc_ref, tq)
        acc2_ref[...] = jnp.zeros(acc2_ref.shape, F32)

    def pages_t(buf, f0, nf, p0):
        return jnp.concatenate([buf[slot, p0 + u, f0:f0 + nf, :] for u in range(ppc)], axis=1).astype(BF16)

    for p0 in range(0, pc, ppc):
        for d in range(DIFF_HEADS // 4):
            probs, alphas = [], []
            for hp in (2 * d, 2 * d + 1):
                s = _dot(qs_ref[hp], pages_t(bufk, hp * LANE, LANE, p0))
                m_new, l_new, alpha, pb = _softmax_part(s, m_ref[hp], l_ref[hp])
                m_ref[hp] = m_new
                l_ref[hp] = l_new
                probs.append(pb)
                alphas.append(jnp.concatenate([alpha, alpha], axis=1))
            pv = _dot_nt(jnp.concatenate(probs, axis=0), pages_t(bufv, d * 2 * LANE, 2 * LANE, p0))
            acc2_ref[d] = jnp.concatenate(alphas, axis=0) * acc2_ref[d] + pv

    @pl.when(c == nc - 1)
    def _fin():
        for hp in range(DIFF_HEADS // 2):
            j = hp % 2
            acc_ref[hp] = acc2_ref[hp // 2][j * r:(j + 1) * r, j * LANE:(j + 1) * LANE]
        new = new_ref[...]
        causal = (lax.broadcasted_iota(jnp.int32, (tq, LANE), 1) <= lax.broadcasted_iota(jnp.int32, (tq, LANE), 0))
        _diff_step(lambda hp: _new_tile(new, hp * LANE, tq), lambda hp: _new_tile(new, 512 + hp * LANE, tq),
                   jnp.where(causal, 0.0, NEG), qs_ref, m_ref, l_ref, acc_ref, tq)
        _diff_fin(lam_ref, g_ref, o_ref, l_ref, acc_ref, tq, lam_init)


def diff_paged(page_table, cache, layer, lam_par, q, new_rows, gnorm, lam_init):
    b, npg = page_table.shape
    tq = q.shape[1]
    pc = _tile(npg, 16)
    nc = npg // pc
    r = 4 * tq
    npair = DIFF_HEADS // 2
    return pl.pallas_call(
        functools.partial(_diff_paged_body, layer=layer, nb=b, nc=nc, pc=pc, tq=tq, lam_init=lam_init),
        name="diff_paged",
        grid_spec=pltpu.PrefetchScalarGridSpec(
            num_scalar_prefetch=1, grid=(b, nc),
            in_specs=[pl.BlockSpec(memory_space=pl.ANY),
                      pl.BlockSpec((4, DIFF_QK), lambda i, c, pt: (0, 0)),
                      pl.BlockSpec((None, tq, 512), lambda i, c, pt: (i, 0, 0)),
                      pl.BlockSpec((None, tq, 1024), lambda i, c, pt: (i, 0, 0)),
                      pl.BlockSpec((1, LANE), lambda i, c, pt: (0, 0))],
            out_specs=pl.BlockSpec((None, tq, 512), lambda i, c, pt: (i, 0, 0)),
            scratch_shapes=[pltpu.VMEM((2, pc, 512, PAGE), F32), pltpu.VMEM((2, pc, 512, PAGE), F32),
                            pltpu.SemaphoreType.DMA((2,)),
                            pltpu.VMEM((npair, r, LANE), BF16), pltpu.VMEM((npair, r, LANE), F32),
                            pltpu.VMEM((npair, r, LANE), F32), pltpu.VMEM((npair, r, LANE), F32),
                            pltpu.VMEM((npair // 2, 2 * r, 2 * LANE), F32)]),
        out_shape=jax.ShapeDtypeStruct((b, tq, 512), F32),
        compiler_params=_cp(("arbitrary", "arbitrary")),
    )(page_table, cache, lam_par, q, new_rows, gnorm)


def _q_perm():
    idx = np.zeros(512, np.int32)
    for j in range(NSA_J):
        for g in range(NSA_G):
            for d in range(DH):
                idx[j * LANE + g * DH + d] = (g * NSA_J + j) * DH + d
    return idx


def _gate3_perm():
    idx = np.zeros(3 * 512, np.int32)
    for c in range(3):
        for j in range(NSA_J):
            for g in range(NSA_G):
                idx[c * 512 + j * LANE + g * DH:c * 512 + j * LANE + (g + 1) * DH] = (g * NSA_J + j) * 3 + c
    return idx


def _cmp_to_sel(n_cmp, n_sel, nch, nsp):
    r = SEL_BLOCK // CMP_STRIDE
    k = np.arange(n_cmp)[:, None] - r * np.arange(n_sel)[None, :]
    m = sum(((k + n >= 0) & (k + n < r)).astype(np.float32) for n in range(CMP_BLOCK // CMP_STRIDE))
    out = np.zeros((nch, nsp), np.float32)
    out[:n_cmp, :n_sel] = m
    return out


def _block_expand(nsp, lk):
    return (np.arange(lk)[None, :] // SEL_BLOCK == np.arange(nsp)[:, None]).astype(np.float32)


def _cmp_weights(pe, w1, w2):
    eye = jnp.eye(2, dtype=F32)
    w6 = w1.reshape(2, NSA_G, 2, CMP_STRIDE, DH, DH)
    wc = jnp.zeros((CMP_STRIDE, 256, 512), BF16)
    for kv in range(2):
        for g in range(NSA_G):
            for half in range(2):
                r0, c0 = kv * LANE + g * DH, half * 256 + kv * LANE + g * DH
                wc = wc.at[:, r0:r0 + DH, c0:c0 + DH].set(w6[kv, g, half].astype(BF16))
    wc = wc.reshape(CMP_STRIDE * 256, 512)
    w2b = jnp.einsum("kghd,kK,gG->kghKGd", w2, eye, eye).reshape(256, 256).astype(BF16)
    pe_rows = jnp.transpose(pe, (0, 2, 1, 3)).reshape(4, CMP_BLOCK * DH)
    xb = jnp.einsum("rc,rR->rRc", pe_rows, jnp.eye(4, dtype=F32)).reshape(4, 4 * CMP_BLOCK * DH)
    xb = jnp.zeros((16, 4 * CMP_BLOCK * DH), F32).at[:4].set(xb)
    wb = jnp.zeros((4 * CMP_BLOCK * DH, LANE), F32).at[:, :DH].set(w1.reshape(4 * CMP_BLOCK * DH, DH))
    bias = mm(xb, wb)[:4, :DH].reshape(1, 256)
    return wc, bias, w2b


def _mixer(x, lw, lam_init, *, b, sq, sample=None):
    t = b * sq
    w_in = lw["w_in"]
    qperm = _q_perm()
    segs = ((512, DH ** -0.5 * LOG2E, None), (512, None, None), (256, None, None), (1536, None, "sigmoid"),
            (512, None, None), (1024, None, None), (512, DIFF_QK ** -0.5 * LOG2E, None), (1024, None, None))
    w_all = jnp.concatenate([w_in[:, OFF_Q:OFF_KV][:, qperm], w_in[:, OFF_KV:OFF_GATE],
                             w_in[:, OFF_GATE:OFF_POOL][:, _gate3_perm()], w_in[:, OFF_POOL:OFF_END]],
                            axis=1).astype(BF16)
    hn, q, rows4, winkv, g3, zpool, zgm, dq, dkv = in_proj(x, lw["norm_mix_g"], w_all, segs)
    q, rows4, winkv = q.reshape(b, sq, 512), rows4.reshape(b, sq, 512), winkv.reshape(b, sq, 256)
    zpool, zgm = zpool.reshape(b, sq, 512), zgm.reshape(b, sq, 1024)
    dq, dkv = dq.reshape(b, sq, 512), dkv.reshape(b, sq, 1024)

    wc, cbias, w2b = _cmp_weights(lw["cmp_pe"], lw["cmp_w1"], lw["cmp_w2"])
    lam_par = jnp.stack([lw["diff_lq1"], lw["diff_lk1"], lw["diff_lq2"], lw["diff_lk2"]])
    gnorm = jnp.tile(lw["diff_norm_g"], 2).reshape(1, LANE)

    if sample is None:
        past, l_tot = 0, sq
        nch = sq // CMP_STRIDE
        tq = _tile(sq, 256)
        tk_sel = _tile(sq, 512)
        nk_sel = sq // tk_sel
        tw = _tile(sq, 256)
        tq_w = tw
        qk = tq_w // tw
        nband = WINDOW // tw + qk
        win_all, kpos0_w, nk_w = winkv, 0, nband
        win_kt = lambda i, j: (jnp.maximum(i * qk - WINDOW // tw + j, 0), i * qk - WINDOW // tw + j >= 0)
        free_w = 1 if (qk == 1 and nband == 3) else None
        tq_d = _tile(sq, 256)
        tk_d = _tile(sq, 512)
        nk_d = sq // tk_d
        prev16 = jnp.zeros((b, 16, 512), F32)
        pool_state = zpool[:, sq - POOL_MEM:]
        zgm_in = zgm
    else:
        layer = sample["layer"]
        pt = sample["page_table"]
        past = pt.shape[1] * PAGE
        l_tot = past + sq
        nch = past // CMP_STRIDE
        tq = sq
        wbuf = sample["win"].shape[1]
        wrows = -(-(wbuf + sq) // KEY_SUB) * KEY_SUB
        tw = wrows
        win_all = jnp.concatenate([sample["win"], winkv, jnp.zeros((b, wrows - wbuf - sq, 256), F32)], axis=1)
        kpos0_w, nk_w = past - wbuf, 1
        win_kt = lambda i, j: (j, j >= 0)
        tq_w = sq
        free_w = None
        prev16 = jnp.concatenate([jnp.zeros((b, 1, 512), F32), sample["pool"]], axis=1)
        pool_state = jnp.concatenate([sample["pool"], zpool], axis=1)[:, -POOL_MEM:]
        zgm_in = jnp.concatenate([zgm, jnp.zeros((b, GM_CHUNK - sq, 1024), F32)], axis=1)

    n_cmp = l_tot // CMP_STRIDE - 1
    n_sel = -(-l_tot // SEL_BLOCK)
    nsp = -(-n_sel // LANE) * LANE
    msel = jnp.asarray(_cmp_to_sel(n_cmp, n_sel, nch, nsp), BF16)
    if sample is None:
        kcvc = compress(rows4, wc, cbias, w2b, nch)
        ocmp, selmask = cmp_attend(q, kcvc, msel, tq=tq, n_cmp=n_cmp, n_sel=n_sel, qpos0=0)
        emat = jnp.asarray(_block_expand(nsp, sq), BF16)

        def sel_kt(i, j):
            last = ((i + 1) * tq - 1) // tk_sel
            return jnp.minimum(j, last), j <= last

        osel = nsa_flash(q, rows4, 2, 3, mode="sel", tq=tq, tk=tk_sel, nk=nk_sel, qpos0=0, kpos0=0,
                         kt_fn=sel_kt, sel=selmask, emat=emat)
    else:
        kcvc = compress_paged(pt, sample["cache_nsa"], layer, wc, cbias, w2b)
        ocmp, selmask = cmp_attend(q, kcvc, msel, tq=tq, n_cmp=n_cmp, n_sel=n_sel, qpos0=past)
        emat = jnp.asarray(_block_expand(nsp, past + LANE), BF16)
        osel = sel_paged(pt, sample["cache_nsa"], layer, q, rows4, selmask, emat)
    owin = nsa_flash(q, win_all, 0, 1, mode="win", tq=tq_w, tk=tw, nk=nk_w, qpos0=past, kpos0=kpos0_w,
                     kt_fn=win_kt, free_step=free_w)

    opool = pool_mix(prev16, zpool, lw["pool_w"], lw["pool_scale"], past)
    ws_tril = jnp.tril(lw["gm_ws"]).astype(BF16)
    bs_exp = jnp.repeat(lw["gm_bs"].T, LANE, axis=1)
    ogm, gm_v = gmlp_mix(zgm_in, lw["gm_ng"], lw["gm_nb"], ws_tril, bs_exp)

    if sample is None:
        def diff_kt(i, j):
            last = ((i + 1) * tq_d - 1) // tk_d
            return jnp.minimum(j, last), j <= last

        odiff = diff_attend(lam_par, dq, dkv, gnorm, tq=tq_d, tk=tk_d, nk=nk_d, qpos0=0,
                            lam_init=lam_init, kt_fn=diff_kt)
    else:
        odiff = diff_paged(pt, sample["cache_diff"], layer, lam_par, dq, dkv, gnorm, lam_init)
        ogm = ogm[:, :sq]
        gm_v = gm_v[:, :sq]
    wbr = lw["w_branch"].at[0].set(lw["w_branch"][0][qperm]).astype(BF16)
    out = finish_mixer(x, hn, g3, ocmp.reshape(t, 512), osel.reshape(t, 512), owin.reshape(t, 512),
                       opool.reshape(t, 512), ogm.reshape(t, 512), odiff.reshape(t, 512),
                       lw["w_gate"].astype(BF16), lw["b_gate"].reshape(1, -1), wbr, lw["w_o"].astype(BF16))
    states = dict(rows4=rows4, winkv=winkv, win_all=win_all, pool_state=pool_state, gm_v=gm_v, dkv=dkv)
    return out, states


def kernel(x_prompt, x_sample, cache_nsa, cache_diff, state_nsa_win, state_pool, page_table, p_prompt, p_sample, norm_mix_g, w_in, nsa_cmp_pe, nsa_cmp_w1, nsa_cmp_w2, pool_w, pool_scale, gm_norm_g, gm_norm_b, gm_ws, gm_bs, diff_lq1, diff_lk1, diff_lq2, diff_lk2, diff_norm_g, w_branch, w_gate, b_gate, w_o, norm_ffn_g, ffn_w_gate, ffn_w_up, ffn_w_down, moe_router, moe_router_b, moe_w_gate, moe_w_up, moe_w_down, ple_norm_g, ple_w_gate, ple_w_proj, final_norm_g):
    bp, sp, d = x_prompt.shape
    bs, ss, _ = x_sample.shape
    depth = w_in.shape[0]
    n_phys = cache_nsa.shape[1]
    wbuf = state_nsa_win.shape[2]
    cache_nsa2 = jnp.transpose(cache_nsa, (0, 1, 3, 4, 5, 2)).reshape(depth, n_phys, 512, PAGE)
    cache_diff2 = jnp.transpose(cache_diff, (0, 1, 3, 4, 5, 2)).reshape(depth, n_phys, 1024, PAGE)
    xp = x_prompt.reshape(bp * sp, d)
    xs = x_sample.reshape(bs * ss, d)
    outs = {k: [] for k in ("nsa_p", "nsa_s", "win_p", "win_s", "pool_p", "pool_s", "gmv_s", "diff_p", "diff_s")}
    for l in range(depth):
        lw = dict(norm_mix_g=norm_mix_g[l], w_in=w_in[l], cmp_pe=nsa_cmp_pe[l], cmp_w1=nsa_cmp_w1[l],
                  cmp_w2=nsa_cmp_w2[l], pool_w=pool_w[l], pool_scale=pool_scale[l], gm_ng=gm_norm_g[l],
                  gm_nb=gm_norm_b[l], gm_ws=gm_ws[l], gm_bs=gm_bs[l], diff_lq1=diff_lq1[l], diff_lk1=diff_lk1[l],
                  diff_lq2=diff_lq2[l], diff_lk2=diff_lk2[l], diff_norm_g=diff_norm_g[l], w_branch=w_branch[l],
                  w_gate=w_gate[l], b_gate=b_gate[l], w_o=w_o[l])
        lam_init = 0.8 - 0.6 * math.exp(-0.3 * l)
        xp, st_p = _mixer(xp, lw, lam_init, b=bp, sq=sp)
        sample = dict(cache_nsa=cache_nsa2, cache_diff=cache_diff2, win=state_nsa_win[l].reshape(bs, wbuf, 256),
                      pool=state_pool[l], page_table=page_table, layer=l)
        xs, st_s = _mixer(xs, lw, lam_init, b=bs, sq=ss, sample=sample)
        outs["nsa_p"].append(st_p["rows4"].reshape(bp, sp, 4, NSA_G, DH))
        outs["diff_p"].append(st_p["dkv"].reshape(bp, sp, 2, DIFF_HEADS, 2 * DIFF_QK))
        outs["nsa_s"].append(st_s["rows4"].reshape(bs, ss, 4, NSA_G, DH))
        wkeep = min(WINDOW, sp)
        outs["win_p"].append(st_p["winkv"][:, sp - wkeep:].reshape(bp, wkeep, 2, NSA_G, DH))
        outs["win_s"].append(st_s["win_all"][:, ss:ss + wbuf].reshape(bs, wbuf, 2, NSA_G, DH))
        outs["pool_p"].append(st_p["pool_state"])
        outs["pool_s"].append(st_s["pool_state"])
        outs["gmv_s"].append(st_s["gm_v"])
        outs["diff_s"].append(st_s["dkv"].reshape(bs, ss, 2, DIFF_HEADS, 2 * DIFF_QK))
        i = l // 2
        if l % 2 == 0:
            xp = ffn_swiglu(xp, norm_ffn_g[l], ffn_w_gate[i], ffn_w_up[i], ffn_w_down[i])
            xs = ffn_swiglu(xs, norm_ffn_g[l], ffn_w_gate[i], ffn_w_up[i], ffn_w_down[i])
        else:
            xp = moe_swiglu(xp, norm_ffn_g[l], moe_router[i], moe_router_b[i], moe_w_gate[i], moe_w_up[i], moe_w_down[i])
            xs = moe_swiglu(xs, norm_ffn_g[l], moe_router[i], moe_router_b[i], moe_w_gate[i], moe_w_up[i], moe_w_down[i])
        final = l == depth - 1
        xp = ple(xp, p_prompt[l].reshape(bp * sp, -1), ple_norm_g[l], ple_w_gate[l], ple_w_proj[l], final_norm_g, final)
        xs = ple(xs, p_sample[l].reshape(bs * ss, -1), ple_norm_g[l], ple_w_gate[l], ple_w_proj[l], final_norm_g, final)
    st = lambda k: jnp.stack(outs[k])
    return (xp.reshape(bp, sp, d), xs.reshape(bs, ss, d), st("nsa_p"), st("nsa_s"), st("win_p"), st("win_s"),
            st("pool_p"), st("pool_s"), st("gmv_s"), st("diff_p"), st("diff_s"))
```

```python
import functools
import math

import numpy as np
import jax
import jax.numpy as jnp
from jax import lax
from jax.experimental import pallas as pl
from jax.experimental.pallas import tpu as pltpu

F32 = jnp.float32
BF16 = jnp.bfloat16

D_MODEL = 1024
PAGE = 128
NSA_G = 2
NSA_J = 4
DH = 64
CMP_BLOCK = 32
CMP_STRIDE = 16
SEL_BLOCK = 64
SEL_TOPK = 16
FORCE_BONUS = 100.0
WINDOW = 512
POOL_WINDOWS = (2, 4, 8, 16)
POOL_MEM = 15
GM_W = 512
GM_CHUNK = 128
DIFF_HEADS = 8
DIFF_QK = 32
N_BRANCH = 4
BRANCH_W = 512
N_EXPERTS = 8
EPS = 1e-6
NEG = -1e30
LOG2E = 1.4426950408889634
LANE = 128
VMEM_LIMIT = 56 * 1024 * 1024

OFF_Q, OFF_KV, OFF_GATE, OFF_POOL, OFF_GM, OFF_DQ, OFF_DKV, OFF_END = 0, 512, 1280, 1304, 1816, 2840, 3352, 4376


def _tile(n, pref):
    t = min(n, pref)
    while n % t:
        t //= 2
    return t


def _cp(sem, vmem=VMEM_LIMIT):
    return pltpu.CompilerParams(dimension_semantics=sem, vmem_limit_bytes=vmem)


def _gelu(x):
    return 0.5 * x * (1.0 + jnp.tanh(0.7978845608028654 * (x + 0.044715 * (x * x * x))))


def _rms(x, g):
    return x * lax.rsqrt(jnp.mean(x * x, axis=-1, keepdims=True) + EPS) * g


def _dot(a, b):
    return jnp.dot(a, b, preferred_element_type=F32)


def _dot_nt(a, b):
    return lax.dot_general(a, b, (((1,), (1,)), ((), ())), preferred_element_type=F32)


def _mm_body(x_ref, w_ref, o_ref):
    o_ref[...] = _dot(x_ref[...].astype(BF16), w_ref[...])


def mm(x, w):
    t, k = x.shape
    n = w.shape[1]
    tm = _tile(t, 512)
    tn = _tile(n, 512)
    return pl.pallas_call(
        _mm_body, name="mm", grid=(t // tm, n // tn),
        in_specs=[pl.BlockSpec((tm, k), lambda i, j: (i, 0)), pl.BlockSpec((k, tn), lambda i, j: (0, j))],
        out_specs=pl.BlockSpec((tm, tn), lambda i, j: (i, j)),
        out_shape=jax.ShapeDtypeStruct((t, n), F32),
        compiler_params=_cp(("parallel", "parallel")),
    )(x, w.astype(BF16))


def _in_proj_body(x_ref, g_ref, w_ref, hn_ref, *o_refs, segs):
    hn = _rms(x_ref[...], g_ref[...]).astype(BF16)
    hn_ref[...] = hn
    c0 = 0
    for (n, scale, act), o_ref in zip(segs, o_refs):
        y = _dot(hn, w_ref[:, c0:c0 + n])
        if scale is not None:
            y = y * scale
        if act == "sigmoid":
            y = jax.nn.sigmoid(y)
        o_ref[...] = y
        c0 += n


def in_proj(x, g, w_all, segs):
    t, d = x.shape
    tm = _tile(t, 256)
    ntot = w_all.shape[1]
    return pl.pallas_call(
        functools.partial(_in_proj_body, segs=segs), name="in_proj", grid=(t // tm,),
        in_specs=[pl.BlockSpec((tm, d), lambda i: (i, 0)), pl.BlockSpec((1, d), lambda i: (0, 0)),
                  pl.BlockSpec((d, ntot), lambda i: (0, 0), pipeline_mode=pl.Buffered(1))],
        out_specs=[pl.BlockSpec((tm, d), lambda i: (i, 0))] + [pl.BlockSpec((tm, n), lambda i: (i, 0)) for n, _, _ in segs],
        out_shape=[jax.ShapeDtypeStruct((t, d), BF16)] + [jax.ShapeDtypeStruct((t, n), F32) for n, _, _ in segs],
        compiler_params=_cp(("parallel",)),
    )(x, g.reshape(1, d), w_all)


def _compress_body(xk_ref, xv_ref, wc_ref, bias_ref, w2_ref, o_ref, *, nch):
    _compress_core(lambda s: xk_ref[pl.ds(s, nch, stride=CMP_STRIDE), :],
                   lambda s: xv_ref[pl.ds(s, nch, stride=CMP_STRIDE), :], wc_ref, bias_ref, w2_ref, o_ref, nch)


def _compress_core(load_k, load_v, wc_ref, bias_ref, w2_ref, o_ref, nch):
    xs = jnp.concatenate([part(s) for s in range(CMP_STRIDE) for part in (load_k, load_v)], axis=1).astype(BF16)
    acc = _dot(xs, wc_ref[...])
    hi_next = pltpu.roll(acc[:, 256:], nch - 1, 0)
    hid = _gelu(acc[:, :256] + hi_next + bias_ref[...])
    o_ref[...] = _dot(hid.astype(BF16), w2_ref[...]).astype(o_ref.dtype)


def compress(rows, wc, bias, w2, nch):
    b = rows.shape[0]
    return pl.pallas_call(
        functools.partial(_compress_body, nch=nch), name="compress", grid=(b,),
        in_specs=[pl.BlockSpec((None, nch * CMP_STRIDE, LANE), lambda i: (i, 0, 0)),
                  pl.BlockSpec((None, nch * CMP_STRIDE, LANE), lambda i: (i, 0, 1)),
                  pl.BlockSpec((CMP_STRIDE * 256, 512), lambda i: (0, 0)),
                  pl.BlockSpec((1, 256), lambda i: (0, 0)),
                  pl.BlockSpec((256, 256), lambda i: (0, 0))],
        out_specs=pl.BlockSpec((None, nch, 256), lambda i: (i, 0, 0)),
        out_shape=jax.ShapeDtypeStruct((b, nch, 256), BF16),
        compiler_params=_cp(("parallel",)),
    )(rows, rows, wc, bias, w2)


def _stack_heads(q, tq):
    lane = lax.broadcasted_iota(jnp.int32, (tq, LANE), 1)
    parts = []
    for g in range(NSA_G):
        keep = (lane < DH) if g == 0 else (lane >= DH)
        for j in range(NSA_J):
            parts.append(jnp.where(keep, q[:, j * LANE:(j + 1) * LANE], 0.0))
    return jnp.concatenate(parts, axis=0)


def _unstack_heads(o, tq):
    lane = lax.broadcasted_iota(jnp.int32, (tq, LANE), 1)
    outs = []
    for j in range(NSA_J):
        outs.append(jnp.where(lane < DH, o[j * tq:(j + 1) * tq], o[(NSA_J + j) * tq:(NSA_J + j + 1) * tq]))
    return jnp.concatenate(outs, axis=1)


def _cmp_body(q_ref, kv_ref, msel_ref, o_ref, sel_ref, *, tq, nch, n_cmp, n_sel, nsp, qpos0):
    i = pl.program_id(1)
    r = 8 * tq
    qs = _stack_heads(q_ref[...], tq).astype(BF16)
    kc = kv_ref[:, 0:LANE]
    vc = kv_ref[:, LANE:2 * LANE]
    s = _dot_nt(qs, kc)
    row = lax.broadcasted_iota(jnp.int32, (r, nch), 0)
    col = lax.broadcasted_iota(jnp.int32, (r, nch), 1)
    qpos = qpos0 + i * tq + (row & (tq - 1))
    cmask = (col * CMP_STRIDE + (CMP_BLOCK - 1) <= qpos) & (col < n_cmp)
    s = jnp.where(cmask, s, NEG)
    m = jnp.max(s, axis=1, keepdims=True)
    e = jnp.where(cmask, jnp.exp2(s - m), 0.0)
    l = jnp.sum(e, axis=1, keepdims=True)
    p = (e * (1.0 / jnp.where(l > 0.0, l, 1.0))).astype(BF16)
    o_ref[...] = _unstack_heads(_dot(p, vc), tq)
    imp_all = _dot(p, msel_ref[...])
    blocks_on_rows = tq % LANE == 0
    nrow = -(-n_sel // 8) * 8
    shape = (nrow, tq) if blocks_on_rows else (tq, nsp)
    baxis, qaxis = (0, 1) if blocks_on_rows else (1, 0)
    blk = lax.broadcasted_iota(jnp.int32, shape, baxis)
    qp = qpos0 + i * tq + lax.broadcasted_iota(jnp.int32, shape, qaxis)
    cur = lax.shift_right_logical(qp, SEL_BLOCK.bit_length() - 1)
    valid = (blk <= cur) & (blk < n_sel)
    forced = (blk == 0) | (blk == cur) | (blk == cur - 1)
    for g in range(NSA_G):
        imp = imp_all[(g * NSA_J) * tq:(g * NSA_J + 1) * tq]
        for j in range(1, NSA_J):
            imp = imp + imp_all[(g * NSA_J + j) * tq:(g * NSA_J + j + 1) * tq]
        if blocks_on_rows:
            imp = imp.T[:nrow]
        score = jnp.where(valid, imp + FORCE_BONUS * forced.astype(F32), -1.0)
        score = jnp.where(blk < n_sel, score, -2.0)
        rank = jnp.zeros(shape, F32)
        for mth in range(n_sel):
            cm = score[mth:mth + 1, :] if blocks_on_rows else score[:, mth:mth + 1]
            beats = (cm > score) | ((cm == score) & (blk > mth))
            rank = rank + beats.astype(F32)
        chosen = ((rank < float(min(SEL_TOPK, n_sel))) & valid).astype(F32)
        if blocks_on_rows:
            chosen = jnp.concatenate([chosen, jnp.zeros((nsp - nrow, tq), F32)], axis=0).T
        sel_ref[:, g * nsp:(g + 1) * nsp] = chosen


def cmp_attend(q, kcvc, msel, *, tq, n_cmp, n_sel, qpos0):
    b, sq, _ = q.shape
    nch = kcvc.shape[1]
    nsp = msel.shape[1]
    return pl.pallas_call(
        functools.partial(_cmp_body, tq=tq, nch=nch, n_cmp=n_cmp, n_sel=n_sel, nsp=nsp, qpos0=qpos0),
        name="cmp_attend",
        grid=(b, sq // tq),
        in_specs=[pl.BlockSpec((None, tq, 512), lambda bi, i: (bi, i, 0)),
                  pl.BlockSpec((None, nch, 256), lambda bi, i: (bi, 0, 0)),
                  pl.BlockSpec((nch, nsp), lambda bi, i: (0, 0))],
        out_specs=[pl.BlockSpec((None, tq, 512), lambda bi, i: (bi, i, 0)),
                   pl.BlockSpec((None, tq, 2 * nsp), lambda bi, i: (bi, i, 0))],
        out_shape=[jax.ShapeDtypeStruct((b, sq, 512), F32), jax.ShapeDtypeStruct((b, sq, 2 * nsp), F32)],
        compiler_params=_cp(("parallel", "parallel")),
    )(q, kcvc, msel)


ROW_BLOCK = 128
KEY_SUB = 256


def _flash_update(qs, k, v, bias, m, l, acc, kv_t=False):
    s = _dot(qs, k) if kv_t else _dot_nt(qs, k)
    if bias is not None:
        s = s + bias
    m_new, l_new, alpha, pb = _softmax_part(s, m, l)
    acc_new = alpha * acc + (_dot_nt(pb, v) if kv_t else _dot(pb, v))
    return m_new, l_new, acc_new


def _softmax_part(s, m, l):
    tiles = [s[:, c:c + LANE] for c in range(0, s.shape[1], LANE)]
    m_new = jnp.maximum(m, jnp.max(functools.reduce(jnp.maximum, tiles), axis=1, keepdims=True))
    alpha = jnp.exp2(m - m_new)
    ps = [jnp.exp2(t - m_new) for t in tiles]
    l_new = alpha * l + jnp.sum(functools.reduce(jnp.add, ps), axis=1, keepdims=True)
    return m_new, l_new, alpha, jnp.concatenate(ps, axis=1).astype(BF16)


def _flash_rows(qs_ref, k, v, bias_of, m_ref, l_ref, acc_ref, nrows, tq, kv_t=False):
    if tq < ROW_BLOCK:
        parts = [bias_of(h, 0, tq) for h in range(nrows // tq)]
        bias = None if parts[0] is None else jnp.concatenate(parts, axis=0)
        blocks = [(pl.ds(0, nrows), bias)]
    else:
        blocks = [(pl.ds(r0, ROW_BLOCK), bias_of(r0 // tq, r0 % tq, ROW_BLOCK)) for r0 in range(0, nrows, ROW_BLOCK)]
    for sl, bias in blocks:
        m, l, acc = _flash_update(qs_ref[sl], k, v, bias, m_ref[sl], l_ref[sl], acc_ref[sl], kv_t)
        m_ref[sl] = m
        l_ref[sl] = l
        acc_ref[sl] = acc


def _sel_ok(sel, e, nsp):
    return [_dot(sel[:, g * nsp:(g + 1) * nsp], e) > 0.5 for g in range(NSA_G)]


def _nsa_flash_body(*refs, mode, tq, tk, nk, qpos0, kpos0, kt_fn, nsp, free_step):
    if mode == "sel":
        q_ref, k_ref, v_ref, sel_ref, e_ref, o_ref, qs_ref, m_ref, l_ref, acc_ref = refs
    else:
        q_ref, k_ref, v_ref, o_ref, qs_ref, m_ref, l_ref, acc_ref = refs
    i = pl.program_id(1)
    j = pl.program_id(2)
    r = 8 * tq

    @pl.when(j == 0)
    def _init():
        qs_ref[...] = _stack_heads(q_ref[...], tq).astype(BF16)
        m_ref[...] = jnp.full((r, LANE), NEG, F32)
        l_ref[...] = jnp.zeros((r, LANE), F32)
        acc_ref[...] = jnp.zeros((r, LANE), F32)

    kt, valid = kt_fn(i, j)

    ks = min(tk, KEY_SUB)

    def step(c0, masked):
        k = k_ref[c0:c0 + ks, :].astype(BF16)
        v = v_ref[c0:c0 + ks, :].astype(BF16)
        if not masked:
            _flash_rows(qs_ref, k, v, lambda h, t0, n: None, m_ref, l_ref, acc_ref, r, tq)
            return
        qpos = qpos0 + i * tq + lax.broadcasted_iota(jnp.int32, (tq, ks), 0)
        kpos = kpos0 + kt * tk + c0 + lax.broadcasted_iota(jnp.int32, (tq, ks), 1)
        ok = kpos <= qpos
        if mode == "win":
            ok = ok & (kpos > qpos - WINDOW)
            bias = [jnp.where(ok, 0.0, NEG)] * NSA_G
        else:
            oks = _sel_ok(sel_ref[...].astype(BF16), e_ref[:, c0:c0 + ks], nsp)
            bias = [jnp.where(ok & okg, 0.0, NEG) for okg in oks]
        _flash_rows(qs_ref, k, v, lambda h, t0, n: bias[h // NSA_J][t0:t0 + n], m_ref, l_ref, acc_ref, r, tq)

    for c0 in range(0, tk, ks):
        live = valid & (kpos0 + kt * tk + c0 <= qpos0 + (i + 1) * tq - 1)
        if free_step is None:
            pl.when(live)(functools.partial(step, c0, True))
        else:
            pl.when(live & (j != free_step))(functools.partial(step, c0, True))
            pl.when(live & (j == free_step))(functools.partial(step, c0, False))

    @pl.when(j == nk - 1)
    def _fin():
        l = l_ref[...]
        o = acc_ref[...] * (1.0 / jnp.where(l > 0.0, l, 1.0))
        o_ref[...] = _unstack_heads(o, tq)


def nsa_flash(q, kv, kcol, vcol, *, mode, tq, tk, nk, qpos0, kpos0, kt_fn, sel=None, emat=None, free_step=None):
    b, sq, _ = q.shape
    nsp = 0 if sel is None else sel.shape[2] // 2

    def kmap(col):
        return lambda bi, i, j: (bi, kt_fn(i, j)[0], col)

    in_specs = [pl.BlockSpec((None, tq, 512), lambda bi, i, j: (bi, i, 0)),
                pl.BlockSpec((None, tk, LANE), kmap(kcol)),
                pl.BlockSpec((None, tk, LANE), kmap(vcol))]
    args = [q, kv, kv]
    if mode == "sel":
        in_specs += [pl.BlockSpec((None, tq, 2 * nsp), lambda bi, i, j: (bi, i, 0)),
                     pl.BlockSpec((nsp, tk), lambda bi, i, j: (0, kt_fn(i, j)[0]))]
        args += [sel, emat]
    r = 8 * tq
    return pl.pallas_call(
        functools.partial(_nsa_flash_body, mode=mode, tq=tq, tk=tk, nk=nk, qpos0=qpos0, kpos0=kpos0,
                          kt_fn=kt_fn, nsp=nsp, free_step=free_step),
        name="nsa_" + mode, grid=(b, sq // tq, nk),
        in_specs=in_specs,
        out_specs=pl.BlockSpec((None, tq, 512), lambda bi, i, j: (bi, i, 0)),
        out_shape=jax.ShapeDtypeStruct((b, sq, 512), F32),
        scratch_shapes=[pltpu.VMEM((r, LANE), BF16), pltpu.VMEM((r, LANE), F32), pltpu.VMEM((r, LANE), F32),
                        pltpu.VMEM((r, LANE), F32)],
        compiler_params=_cp(("parallel", "parallel", "arbitrary")),
    )(*args)


def _diff_init(q_ref, qs_ref, m_ref, l_ref, acc_ref, tq):
    r = 4 * tq
    npair = DIFF_HEADS // 2
    lane = lax.broadcasted_iota(jnp.int32, (tq, LANE), 1)
    for hp in range(npair):
        q = q_ref[:, hp * LANE:(hp + 1) * LANE]
        parts = []
        for h in range(2):
            for mth in range(2):
                lo = h * 2 * DIFF_QK + mth * DIFF_QK
                parts.append(jnp.where((lane >= lo) & (lane < lo + DIFF_QK), q, 0.0))
        qs_ref[hp] = jnp.concatenate(parts, axis=0).astype(BF16)
    m_ref[...] = jnp.full((npair, r, LANE), NEG, F32)
    l_ref[...] = jnp.zeros((npair, r, LANE), F32)
    acc_ref[...] = jnp.zeros((npair, r, LANE), F32)


def _diff_step(k_of, v_of, bias, qs_ref, m_ref, l_ref, acc_ref, tq, kv_t=False):
    bias_of = lambda h, t0, n: None if bias is None else bias[t0:t0 + n]
    for hp in range(DIFF_HEADS // 2):
        _flash_rows(qs_ref.at[hp], k_of(hp), v_of(hp), bias_of, m_ref.at[hp], l_ref.at[hp], acc_ref.at[hp],
                    4 * tq, tq, kv_t)


def _diff_fin(lam_ref, g_ref, o_ref, l_ref, acc_ref, tq, lam_init):
    lp = lam_ref[...]
    lam = (jnp.exp(jnp.sum(lp[0:1] * lp[1:2], axis=1, keepdims=True))
           - jnp.exp(jnp.sum(lp[2:3] * lp[3:4], axis=1, keepdims=True)) + lam_init)
    lane = lax.broadcasted_iota(jnp.int32, (tq, LANE), 1)
    lo_half = lane < 2 * DIFF_QK
    for hp in range(DIFF_HEADS // 2):
        l = l_ref[hp]
        a = acc_ref[hp] * (1.0 / jnp.where(l > 0.0, l, 1.0))
        o0 = a[0:tq] - lam * a[tq:2 * tq]
        o1 = a[2 * tq:3 * tq] - lam * a[3 * tq:4 * tq]
        o = jnp.where(lo_half, o0, o1)
        sq = o * o
        ms0 = jnp.sum(jnp.where(lo_half, sq, 0.0), axis=1, keepdims=True)
        ms1 = jnp.sum(jnp.where(lo_half, 0.0, sq), axis=1, keepdims=True)
        ms = jnp.where(lo_half, ms0, ms1) * (1.0 / (2 * DIFF_QK))
        y = o * lax.rsqrt(ms + EPS) * g_ref[...]
        o_ref[:, hp * LANE:(hp + 1) * LANE] = y * (1.0 - lam_init)


def _diff_body(lam_ref, q_ref, k_ref, v_ref, g_ref, o_ref, qs_ref, m_ref, l_ref, acc_ref, *,
               tq, tk, nk, qpos0, lam_init, kt_fn):
    i = pl.program_id(1)
    j = pl.program_id(2)

    @pl.when(j == 0)
    def _init():
        _diff_init(q_ref, qs_ref, m_ref, l_ref, acc_ref, tq)

    kt, valid = kt_fn(i, j)
    ks = min(tk, KEY_SUB)

    def step(c0, masked):
        k_of = lambda hp: k_ref[c0:c0 + ks, hp * LANE:(hp + 1) * LANE].astype(BF16)
        v_of = lambda hp: v_ref[c0:c0 + ks, hp * LANE:(hp + 1) * LANE].astype(BF16)
        bias = None
        if masked:
            qpos = qpos0 + i * tq + lax.broadcasted_iota(jnp.int32, (tq, ks), 0)
            kpos = kt * tk + c0 + lax.broadcasted_iota(jnp.int32, (tq, ks), 1)
            bias = jnp.where(kpos <= qpos, 0.0, NEG)
        _diff_step(k_of, v_of, bias, qs_ref, m_ref, l_ref, acc_ref, tq)

    for c0 in range(0, tk, ks):
        k_first = kt * tk + c0
        live = valid & (k_first <= qpos0 + (i + 1) * tq - 1)
        below = k_first + ks - 1 <= qpos0 + i * tq
        pl.when(live & below)(functools.partial(step, c0, False))
        pl.when(live & jnp.logical_not(below))(functools.partial(step, c0, True))

    @pl.when(j == nk - 1)
    def _fin():
        _diff_fin(lam_ref, g_ref, o_ref, l_ref, acc_ref, tq, lam_init)


def diff_attend(lam_par, q, kv, gnorm, *, tq, tk, nk, qpos0, lam_init, kt_fn):
    b, sq, _ = q.shape
    r = 4 * tq
    npair = DIFF_HEADS // 2
    return pl.pallas_call(
        functools.partial(_diff_body, tq=tq, tk=tk, nk=nk, qpos0=qpos0, lam_init=lam_init, kt_fn=kt_fn),
        name="diff_attend",
        grid=(b, sq // tq, nk),
        in_specs=[pl.BlockSpec((4, DIFF_QK), lambda bi, i, j: (0, 0)),
                  pl.BlockSpec((None, tq, 512), lambda bi, i, j: (bi, i, 0)),
                  pl.BlockSpec((None, tk, 512), lambda bi, i, j: (bi, kt_fn(i, j)[0], 0)),
                  pl.BlockSpec((None, tk, 512), lambda bi, i, j: (bi, kt_fn(i, j)[0], 1)),
                  pl.BlockSpec((1, LANE), lambda bi, i, j: (0, 0))],
        out_specs=pl.BlockSpec((None, tq, 512), lambda bi, i, j: (bi, i, 0)),
        out_shape=jax.ShapeDtypeStruct((b, sq, 512), F32),
        scratch_shapes=[pltpu.VMEM((npair, r, LANE), BF16), pltpu.VMEM((npair, r, LANE), F32),
                        pltpu.VMEM((npair, r, LANE), F32), pltpu.VMEM((npair, r, LANE), F32)],
        compiler_params=_cp(("parallel", "parallel", "arbitrary")),
    )(lam_par, q, kv, kv, gnorm)


def _pool_body(prev_ref, z_ref, pw_ref, sc_ref, o_ref, x_ref, *, sq, start):
    x_ref[pl.ds(0, 16), :] = prev_ref[...]
    x_ref[pl.ds(16, sq), :] = z_ref[...]
    pos = start + lax.broadcasted_iota(jnp.int32, (sq, LANE), 0)
    for g, w in enumerate(POOL_WINDOWS):
        c0 = g * LANE
        x = x_ref[pl.ds(16, sq), c0:c0 + LANE]
        tot = x
        for back in range(1, w):
            tot = tot + x_ref[pl.ds(16 - back, sq), c0:c0 + LANE]
        cnt = jnp.minimum(w, pos + 1).astype(F32)
        mix = tot / cnt - x
        y = _dot(mix.astype(BF16), pw_ref[g])
        o_ref[:, c0:c0 + LANE] = y * sc_ref[:, c0:c0 + LANE]


def pool_mix(prev16, z, pw, scale, start):
    b, sq, _ = z.shape
    return pl.pallas_call(
        functools.partial(_pool_body, sq=sq, start=start), name="pool_mix", grid=(b,),
        in_specs=[pl.BlockSpec((None, 16, 512), lambda i: (i, 0, 0)),
                  pl.BlockSpec((None, sq, 512), lambda i: (i, 0, 0)),
                  pl.BlockSpec((4, LANE, LANE), lambda i: (0, 0, 0)),
                  pl.BlockSpec((1, 512), lambda i: (0, 0))],
        out_specs=pl.BlockSpec((None, sq, 512), lambda i: (i, 0, 0)),
        out_shape=jax.ShapeDtypeStruct((b, sq, 512), F32),
        scratch_shapes=[pltpu.VMEM((sq + 16, 512), F32)],
        compiler_params=_cp(("parallel",)),
    )(prev16, z, pw.astype(BF16), scale.reshape(1, 512))


def _gmlp_body(z_ref, ng_ref, nb_ref, ws_ref, bs_ref, o_ref, v_ref, *, tg):
    z = _gelu(z_ref[...])
    u = z[:, :GM_W]
    vr = z[:, GM_W:]
    xc = vr - jnp.mean(vr, axis=-1, keepdims=True)
    v = xc * lax.rsqrt(jnp.mean(xc * xc, axis=-1, keepdims=True) + EPS) * ng_ref[...] + nb_ref[...]
    v_ref[...] = v
    vb = v.astype(BF16)
    for c in range(tg // GM_CHUNK):
        r0 = c * GM_CHUNK
        for g in range(4):
            c0 = g * LANE
            mixed = _dot(ws_ref[g], vb[r0:r0 + GM_CHUNK, c0:c0 + LANE]) + bs_ref[:, c0:c0 + LANE]
            o_ref[r0:r0 + GM_CHUNK, c0:c0 + LANE] = u[r0:r0 + GM_CHUNK, c0:c0 + LANE] * mixed


def gmlp_mix(z, ng, nb, ws_tril, bs_exp):
    b, s, _ = z.shape
    tg = _tile(s, 512)
    return pl.pallas_call(
        functools.partial(_gmlp_body, tg=tg), name="gmlp_mix", grid=(b, s // tg),
        in_specs=[pl.BlockSpec((None, tg, 1024), lambda bi, i: (bi, i, 0)),
                  pl.BlockSpec((1, 512), lambda bi, i: (0, 0)),
                  pl.BlockSpec((1, 512), lambda bi, i: (0, 0)),
                  pl.BlockSpec((4, GM_CHUNK, GM_CHUNK), lambda bi, i: (0, 0, 0)),
                  pl.BlockSpec((GM_CHUNK, 512), lambda bi, i: (0, 0))],
        out_specs=[pl.BlockSpec((None, tg, 512), lambda bi, i: (bi, i, 0)),
                   pl.BlockSpec((None, tg, 512), lambda bi, i: (bi, i, 0))],
        out_shape=[jax.ShapeDtypeStruct((b, s, 512), F32), jax.ShapeDtypeStruct((b, s, 512), F32)],
        compiler_params=_cp(("parallel", "parallel")),
    )(z, ng.reshape(1, 512), nb.reshape(1, 512), ws_tril, bs_exp)


def _finish_body(x_ref, hn_ref, g3_ref, ocmp_ref, osel_ref, owin_ref, opool_ref, ogm_ref, odiff_ref,
                 wgate_ref, bgate_ref, wbr_ref, wo_ref, o_ref):
    hn = hn_ref[...]
    g3 = g3_ref[...]
    onsa = g3[:, 0:512] * ocmp_ref[...] + g3[:, 512:1024] * osel_ref[...] + g3[:, 1024:1536] * owin_ref[...]
    branches = (onsa, opool_ref[...], ogm_ref[...], odiff_ref[...])
    acc = jnp.zeros(o_ref.shape, F32)
    for n in range(N_BRANCH):
        c0 = n * D_MODEL
        gate = jax.nn.sigmoid(_dot(hn, wgate_ref[:, c0:c0 + D_MODEL]) + bgate_ref[:, c0:c0 + D_MODEL])
        acc = acc + gate * _dot(branches[n].astype(BF16), wbr_ref[n])
    o_ref[...] = x_ref[...] + _dot(acc.astype(BF16), wo_ref[...])


def finish_mixer(x, hn, g3, ocmp, osel, owin, opool, ogm, odiff, wgate, bgate, wbr, wo):
    t = x.shape[0]
    tm = _tile(t, 256)
    row = lambda w: pl.BlockSpec((tm, w), lambda i: (i, 0))
    const = lambda shape: pl.BlockSpec(shape, lambda i: (0,) * len(shape), pipeline_mode=pl.Buffered(1))
    return pl.pallas_call(
        _finish_body, name="finish_mixer", grid=(t // tm,),
        in_specs=[row(1024), row(1024), row(1536), row(512), row(512), row(512), row(512), row(512), row(512),
                  const((D_MODEL, N_BRANCH * D_MODEL)), const((1, N_BRANCH * D_MODEL)),
                  const((N_BRANCH, BRANCH_W, D_MODEL)), const((D_MODEL, D_MODEL))],
        out_specs=row(1024),
        out_shape=jax.ShapeDtypeStruct((t, D_MODEL), F32),
        compiler_params=_cp(("parallel",)),
    )(x, hn, g3, ocmp, osel, owin, opool, ogm, odiff, wgate, bgate, wbr, wo)


def _ffn_body(x_ref, g_ref, wg_ref, wu_ref, wd_ref, o_ref, hn_ref, acc_ref, *, nf):
    j = pl.program_id(1)

    @pl.when(j == 0)
    def _init():
        hn_ref[...] = _rms(x_ref[...], g_ref[...]).astype(BF16)
        acc_ref[...] = jnp.zeros(acc_ref.shape, F32)

    h = hn_ref[...]
    a = _dot(h, wg_ref[...])
    act = a * jax.nn.sigmoid(a) * _dot(h, wu_ref[...])
    acc_ref[...] += _dot(act.astype(BF16), wd_ref[...])

    @pl.when(j == nf - 1)
    def _fin():
        o_ref[...] = x_ref[...] + acc_ref[...]


def ffn_swiglu(x, g, wg, wu, wd):
    t, d = x.shape
    f = wg.shape[1]
    tm = _tile(t, 512)
    tf = 1408 if f % 1408 == 0 else _tile(f, 512)
    nf = f // tf
    return pl.pallas_call(
        functools.partial(_ffn_body, nf=nf), name="ffn_swiglu", grid=(t // tm, nf),
        in_specs=[pl.BlockSpec((tm, d), lambda i, j: (i, 0)), pl.BlockSpec((1, d), lambda i, j: (0, 0)),
                  pl.BlockSpec((d, tf), lambda i, j: (0, j)), pl.BlockSpec((d, tf), lambda i, j: (0, j)),
                  pl.BlockSpec((tf, d), lambda i, j: (j, 0))],
        out_specs=pl.BlockSpec((tm, d), lambda i, j: (i, 0)),
        out_shape=jax.ShapeDtypeStruct((t, d), F32),
        scratch_shapes=[pltpu.VMEM((tm, d), BF16), pltpu.VMEM((tm, d), F32)],
        compiler_params=_cp(("parallel", "arbitrary")),
    )(x, g.reshape(1, d), wg.astype(BF16), wu.astype(BF16), wd.astype(BF16))


MOE_SUB = 256


def _moe_body(x_ref, g_ref, r_ref, rb_ref, u_ref, wg_ref, wu_ref, wd_ref, o_ref,
              hn_ref, comb_ref, post_ref, posr_ref, cnt_ref, xe_ref, ye_ref, acc_ref, *, nf):
    e = pl.program_id(1)
    f = pl.program_id(2)
    tm = x_ref.shape[0]
    lane = lax.broadcasted_iota(jnp.int32, (tm, LANE), 1)

    @pl.when((e == 0) & (f == 0))
    def _init():
        hn = _rms(x_ref[...], g_ref[...]).astype(BF16)
        hn_ref[...] = hn
        lg = _dot(hn, r_ref[...]) + rb_ref[...]
        m1 = jnp.max(lg, axis=1, keepdims=True)
        i1 = jnp.min(jnp.where(lg == m1, lane, LANE), axis=1, keepdims=True)
        lg2 = jnp.where(lane == i1, -3e38, lg)
        m2 = jnp.max(lg2, axis=1, keepdims=True)
        i2 = jnp.min(jnp.where(lg2 == m2, lane, LANE), axis=1, keepdims=True)
        e2 = jnp.exp(m2 - m1)
        w1 = 1.0 / (1.0 + e2)
        comb = jnp.where(lane == i1, w1, 0.0) + jnp.where(lane == i2, e2 * w1, 0.0)
        comb_ref[...] = comb
        ind_t = (comb.T > 0.0).astype(F32)
        pos_t = _dot(ind_t.astype(BF16), u_ref[...])
        post_ref[...] = jnp.where(ind_t > 0.0, pos_t, -1.0)
        posr_ref[...] = post_ref[...].T
        cnt = jnp.sum(ind_t, axis=1, keepdims=True)
        for ex in range(N_EXPERTS):
            cnt_ref[ex] = cnt[ex, 0].astype(jnp.int32)
        acc_ref[...] = jnp.zeros(acc_ref.shape, F32)

    cnt = cnt_ref[e]
    rem = cnt % MOE_SUB
    has_tail = (rem > 0) & (rem <= MOE_SUB // 2)
    n_full = cnt // MOE_SUB + (rem > MOE_SUB // 2).astype(jnp.int32)

    def each_subtile(fn):
        lax.fori_loop(0, n_full, lambda u, c: (fn(u, MOE_SUB), c)[1], 0)
        pl.when(has_tail)(lambda: fn(n_full, MOE_SUB // 2))

    @pl.when(f == 0)
    def _gather():
        slot_t = post_ref[pl.ds(e, 1), :]

        def body(u, rows):
            base = lax.broadcasted_iota(jnp.int32, (rows, tm), 0).astype(F32) + (u * MOE_SUB).astype(F32)
            onehot = jnp.where(slot_t == base, 1.0, 0.0).astype(BF16)
            xe_ref[u, pl.ds(0, rows)] = _dot(onehot, hn_ref[...]).astype(BF16)
            ye_ref[u, pl.ds(0, rows)] = jnp.zeros((rows, x_ref.shape[1]), F32)

        each_subtile(body)

    def expert(u, rows):
        xe = xe_ref[u, pl.ds(0, rows)]
        a = _dot(xe, wg_ref[...])
        act = a * jax.nn.sigmoid(a) * _dot(xe, wu_ref[...])
        ye_ref[u, pl.ds(0, rows)] += _dot(act.astype(BF16), wd_ref[...])

    each_subtile(expert)

    @pl.when(f == nf - 1)
    def _scatter():
        slot_r = jnp.sum(jnp.where(lane == e, posr_ref[...], 0.0), axis=1, keepdims=True)
        w_r = jnp.sum(jnp.where(lane == e, comb_ref[...], 0.0), axis=1, keepdims=True)

        def body(u, rows):
            base = lax.broadcasted_iota(jnp.int32, (tm, rows), 1).astype(F32) + (u * MOE_SUB).astype(F32)
            onehot = jnp.where(slot_r == base, 1.0, 0.0).astype(BF16)
            y = ye_ref[u, pl.ds(0, rows)]
            y_hi = y.astype(BF16)
            y_lo = (y - y_hi.astype(F32)).astype(BF16)
            acc_ref[...] += w_r * (_dot(onehot, y_hi) + _dot(onehot, y_lo))

        each_subtile(body)

    @pl.when((e == N_EXPERTS - 1) & (f == nf - 1))
    def _fin():
        o_ref[...] = x_ref[...] + acc_ref[...]


def moe_swiglu(x, g, router, router_b, wg, wu, wd):
    t, d = x.shape
    f = wg.shape[2]
    tm = _tile(t, 1024)
    tf = 896 if f % 896 == 0 else _tile(f, 512)
    nf = f // tf
    nsubmax = -(-tm // MOE_SUB)
    rpad = jnp.zeros((d, LANE), F32).at[:, :N_EXPERTS].set(router).astype(BF16)
    rbpad = jnp.full((1, LANE), NEG, F32).at[0, :N_EXPERTS].set(router_b)
    before = jnp.asarray(np.triu(np.ones((tm, tm), np.float32), 1), BF16)
    const = lambda shape: pl.BlockSpec(shape, lambda i, e, j: (0,) * len(shape))
    return pl.pallas_call(
        functools.partial(_moe_body, nf=nf), name="moe_swiglu", grid=(t // tm, N_EXPERTS, nf),
        in_specs=[pl.BlockSpec((tm, d), lambda i, e, j: (i, 0)), const((1, d)), const((d, LANE)), const((1, LANE)),
                  const((tm, tm)),
                  pl.BlockSpec((None, d, tf), lambda i, e, j: (e, 0, j)),
                  pl.BlockSpec((None, d, tf), lambda i, e, j: (e, 0, j)),
                  pl.BlockSpec((None, tf, d), lambda i, e, j: (e, j, 0))],
        out_specs=pl.BlockSpec((tm, d), lambda i, e, j: (i, 0)),
        out_shape=jax.ShapeDtypeStruct((t, d), F32),
        scratch_shapes=[pltpu.VMEM((tm, d), BF16), pltpu.VMEM((tm, LANE), F32), pltpu.VMEM((LANE, tm), F32),
                        pltpu.VMEM((tm, LANE), F32), pltpu.SMEM((N_EXPERTS,), jnp.int32),
                        pltpu.VMEM((nsubmax, MOE_SUB, d), BF16), pltpu.VMEM((nsubmax, MOE_SUB, d), F32),
                        pltpu.VMEM((tm, d), F32)],
        compiler_params=_cp(("parallel", "arbitrary", "arbitrary")),
    )(x, g.reshape(1, d), rpad, rbpad, before, wg.astype(BF16), wu.astype(BF16), wd.astype(BF16))


def _ple_body(x_ref, p_ref, g_ref, wg_ref, wp_ref, fg_ref, o_ref, *, final):
    x = x_ref[...]
    hn = _rms(x, g_ref[...]).astype(BF16)
    gate = jax.nn.sigmoid(_dot(hn, wg_ref[...]))
    y = x + gate * _dot(p_ref[...].astype(BF16), wp_ref[...])
    if final:
        y = _rms(y, fg_ref[...])
    o_ref[...] = y


def ple(x, p, g, wg, wp, fg, final):
    t, d = x.shape
    pd = p.shape[1]
    tm = _tile(t, 512)
    return pl.pallas_call(
        functools.partial(_ple_body, final=final), name="ple", grid=(t // tm,),
        in_specs=[pl.BlockSpec((tm, d), lambda i: (i, 0)), pl.BlockSpec((tm, pd), lambda i: (i, 0)),
                  pl.BlockSpec((1, d), lambda i: (0, 0)), pl.BlockSpec((d, d), lambda i: (0, 0)),
                  pl.BlockSpec((pd, d), lambda i: (0, 0)), pl.BlockSpec((1, d), lambda i: (0, 0))],
        out_specs=pl.BlockSpec((tm, d), lambda i: (i, 0)),
        out_shape=jax.ShapeDtypeStruct((t, d), F32),
        compiler_params=_cp(("parallel",)),
    )(x, p, g.reshape(1, d), wg.astype(BF16), wp.astype(BF16), fg.reshape(1, d))


def _page_copies(pt_ref, cache_ref, bufs, sem, slot, b, c, *, layer, pc, cols):
    out = []
    for p in range(pc):
        pg = pt_ref[b, c * pc + p]
        for buf, (f0, w) in zip(bufs, cols):
            out.append(pltpu.make_async_copy(cache_ref.at[layer, pg, pl.ds(f0, w), :], buf.at[slot, p],
                                             sem.at[slot]))
    return out


def _paged_pipeline(pt_ref, cache_ref, bufs, sem, *, nb, nc, **kw):
    b = pl.program_id(0)
    c = pl.program_id(1) if nc > 1 else 0
    step = b * nc + c
    slot = step % 2

    @pl.when(step == 0)
    def _first():
        for d in _page_copies(pt_ref, cache_ref, bufs, sem, 0, 0, 0, **kw):
            d.start()

    @pl.when(step + 1 < nb * nc)
    def _prefetch():
        nxt = step + 1
        for d in _page_copies(pt_ref, cache_ref, bufs, sem, 1 - slot, nxt // nc, nxt % nc, **kw):
            d.start()

    for d in _page_copies(pt_ref, cache_ref, bufs, sem, slot, b, c, **kw):
        d.wait()
    return slot


def _compress_paged_body(pt_ref, cache_ref, wc_ref, bias_ref, w2_ref, o_ref, bufk, bufv, sem, rowk, rowv, *,
                         layer, npg, nb, nch):
    slot = _paged_pipeline(pt_ref, cache_ref, (bufk, bufv), sem, nb=nb, nc=1, layer=layer, pc=npg,
                           cols=((0, LANE), (LANE, LANE)))

    def to_rows(p, carry):
        rows = pl.ds(pl.multiple_of(p * PAGE, PAGE), PAGE)
        rowk[rows, :] = bufk[slot, p].T
        rowv[rows, :] = bufv[slot, p].T
        return carry

    lax.fori_loop(0, npg, to_rows, 0, unroll=4)
    _compress_core(lambda s: rowk[pl.ds(s, nch, stride=CMP_STRIDE), :],
                   lambda s: rowv[pl.ds(s, nch, stride=CMP_STRIDE), :], wc_ref, bias_ref, w2_ref, o_ref, nch)


def compress_paged(page_table, cache, layer, wc, bias, w2):
    b, npg = page_table.shape
    past = npg * PAGE
    nch = past // CMP_STRIDE
    const = lambda shape: pl.BlockSpec(shape, lambda i, pt: (0,) * len(shape))
    return pl.pallas_call(
        functools.partial(_compress_paged_body, layer=layer, npg=npg, nb=b, nch=nch), name="compress_paged",
        grid_spec=pltpu.PrefetchScalarGridSpec(
            num_scalar_prefetch=1, grid=(b,),
            in_specs=[pl.BlockSpec(memory_space=pl.ANY), const((CMP_STRIDE * 256, 512)), const((1, 256)),
                      const((256, 256))],
            out_specs=pl.BlockSpec((None, nch, 256), lambda i, pt: (i, 0, 0)),
            scratch_shapes=[pltpu.VMEM((2, npg, LANE, PAGE), F32), pltpu.VMEM((2, npg, LANE, PAGE), F32),
                            pltpu.SemaphoreType.DMA((2,)),
                            pltpu.VMEM((past, LANE), F32), pltpu.VMEM((past, LANE), F32)]),
        out_shape=jax.ShapeDtypeStruct((b, nch, 256), BF16),
        compiler_params=_cp(("arbitrary",)),
    )(page_table, cache, wc, bias, w2)


def _new_tile(new, c0, tq):
    return jnp.concatenate([new[:, c0:c0 + LANE], jnp.zeros((LANE - tq, LANE), F32)], axis=0).astype(BF16)


def _sel_paged_body(pt_ref, cache_ref, q_ref, new_ref, sel_ref, e_ref, o_ref, bufk, bufv, sem, *,
                    layer, npg, nb, tq, tk, nsp):
    slot = _paged_pipeline(pt_ref, cache_ref, (bufk, bufv), sem, nb=nb, nc=1, layer=layer, pc=npg,
                           cols=((2 * LANE, LANE), (3 * LANE, LANE)))
    past = npg * PAGE
    ppc = tk // PAGE
    r = 8 * tq
    qs = _stack_heads(q_ref[...], tq).astype(BF16)
    sel = sel_ref[...].astype(BF16)
    m = jnp.full((r, LANE), NEG, F32)
    l = jnp.zeros((r, LANE), F32)
    acc = jnp.zeros((r, LANE), F32)

    def stacked_bias(oks):
        return jnp.concatenate([jnp.where(ok, 0.0, NEG) for ok in oks for _ in range(NSA_J)], axis=0)

    for c in range(npg // ppc):
        kt = jnp.concatenate([bufk[slot, c * ppc + u] for u in range(ppc)], axis=1).astype(BF16)
        vt = jnp.concatenate([bufv[slot, c * ppc + u] for u in range(ppc)], axis=1).astype(BF16)
        bias = stacked_bias(_sel_ok(sel, e_ref[:, c * tk:(c + 1) * tk], nsp))
        m, l, acc = _flash_update(qs, kt, vt, bias, m, l, acc, kv_t=True)
    new = new_ref[...]
    causal = (lax.broadcasted_iota(jnp.int32, (tq, LANE), 1) <= lax.broadcasted_iota(jnp.int32, (tq, LANE), 0))
    bias = stacked_bias([ok & causal for ok in _sel_ok(sel, e_ref[:, past:past + LANE], nsp)])
    m, l, acc = _flash_update(qs, _new_tile(new, 2 * LANE, tq), _new_tile(new, 3 * LANE, tq), bias, m, l, acc)
    o_ref[...] = _unstack_heads(acc * (1.0 / jnp.where(l > 0.0, l, 1.0)), tq)


def sel_paged(page_table, cache, layer, q, new_rows, sel, emat):
    b, npg = page_table.shape
    past = npg * PAGE
    tq = q.shape[1]
    nsp = sel.shape[2] // 2
    return pl.pallas_call(
        functools.partial(_sel_paged_body, layer=layer, npg=npg, nb=b, tq=tq, tk=_tile(past, 512), nsp=nsp),
        name="sel_paged",
        grid_spec=pltpu.PrefetchScalarGridSpec(
            num_scalar_prefetch=1, grid=(b,),
            in_specs=[pl.BlockSpec(memory_space=pl.ANY),
                      pl.BlockSpec((None, tq, 512), lambda i, pt: (i, 0, 0)),
                      pl.BlockSpec((None, tq, 512), lambda i, pt: (i, 0, 0)),
                      pl.BlockSpec((None, tq, 2 * nsp), lambda i, pt: (i, 0, 0)),
                      pl.BlockSpec((nsp, past + LANE), lambda i, pt: (0, 0))],
            out_specs=pl.BlockSpec((None, tq, 512), lambda i, pt: (i, 0, 0)),
            scratch_shapes=[pltpu.VMEM((2, npg, LANE, PAGE), F32), pltpu.VMEM((2, npg, LANE, PAGE), F32),
                            pltpu.SemaphoreType.DMA((2,))]),
        out_shape=jax.ShapeDtypeStruct((b, tq, 512), F32),
        compiler_params=_cp(("arbitrary",)),
    )(page_table, cache, q, new_rows, sel, emat)


def _diff_paged_body(pt_ref, cache_ref, lam_ref, q_ref, new_ref, g_ref, o_ref, bufk, bufv, sem,
                     qs_ref, m_ref, l_ref, acc_ref, acc2_ref, *, layer, nb, nc, pc, tq, lam_init):
    c = pl.program_id(1)
    slot = _paged_pipeline(pt_ref, cache_ref, (bufk, bufv), sem, nb=nb, nc=nc, layer=layer, pc=pc,
                           cols=((0, 512), (512, 512)))
    ppc = _tile(pc, 4)
    r = 4 * tq

    @pl.when(c == 0)
    def _init():
        _diff_init(q_ref, qs_ref, m_ref, l_ref, acc_ref, tq)
        acc2_ref[...] = jnp.zeros(acc2_ref.shape, F32)

    def pages_t(buf, f0, nf, p0):
        return jnp.concatenate([buf[slot, p0 + u, f0:f0 + nf, :] for u in range(ppc)], axis=1).astype(BF16)

    for p0 in range(0, pc, ppc):
        for d in range(DIFF_HEADS // 4):
            probs, alphas = [], []
            for hp in (2 * d, 2 * d + 1):
                s = _dot(qs_ref[hp], pages_t(bufk, hp * LANE, LANE, p0))
                m_new, l_new, alpha, pb = _softmax_part(s, m_ref[hp], l_ref[hp])
                m_ref[hp] = m_new
                l_ref[hp] = l_new
                probs.append(pb)
                alphas.append(jnp.concatenate([alpha, alpha], axis=1))
            pv = _dot_nt(jnp.concatenate(probs, axis=0), pages_t(bufv, d * 2 * LANE, 2 * LANE, p0))
            acc2_ref[d] = jnp.concatenate(alphas, axis=0) * acc2_ref[d] + pv

    @pl.when(c == nc - 1)
    def _fin():
        for hp in range(DIFF_HEADS // 2):
            j = hp % 2
            acc_ref[hp] = acc2_ref[hp // 2][j * r:(j + 1) * r, j * LANE:(j + 1) * LANE]
        new = new_ref[...]
        causal = (lax.broadcasted_iota(jnp.int32, (tq, LANE), 1) <= lax.broadcasted_iota(jnp.int32, (tq, LANE), 0))
        _diff_step(lambda hp: _new_tile(new, hp * LANE, tq), lambda hp: _new_tile(new, 512 + hp * LANE, tq),
                   jnp.where(causal, 0.0, NEG), qs_ref, m_ref, l_ref, acc_ref, tq)
        _diff_fin(lam_ref, g_ref, o_ref, l_ref, acc_ref, tq, lam_init)


def diff_paged(page_table, cache, layer, lam_par, q, new_rows, gnorm, lam_init):
    b, npg = page_table.shape
    tq = q.shape[1]
    pc = _tile(npg, 16)
    nc = npg // pc
    r = 4 * tq
    npair = DIFF_HEADS // 2
    return pl.pallas_call(
        functools.partial(_diff_paged_body, layer=layer, nb=b, nc=nc, pc=pc, tq=tq, lam_init=lam_init),
        name="diff_paged",
        grid_spec=pltpu.PrefetchScalarGridSpec(
            num_scalar_prefetch=1, grid=(b, nc),
            in_specs=[pl.BlockSpec(memory_space=pl.ANY),
                      pl.BlockSpec((4, DIFF_QK), lambda i, c, pt: (0, 0)),
                      pl.BlockSpec((None, tq, 512), lambda i, c, pt: (i, 0, 0)),
                      pl.BlockSpec((None, tq, 1024), lambda i, c, pt: (i, 0, 0)),
                      pl.BlockSpec((1, LANE), lambda i, c, pt: (0, 0))],
            out_specs=pl.BlockSpec((None, tq, 512), lambda i, c, pt: (i, 0, 0)),
            scratch_shapes=[pltpu.VMEM((2, pc, 512, PAGE), F32), pltpu.VMEM((2, pc, 512, PAGE), F32),
                            pltpu.SemaphoreType.DMA((2,)),
                            pltpu.VMEM((npair, r, LANE), BF16), pltpu.VMEM((npair, r, LANE), F32),
                            pltpu.VMEM((npair, r, LANE), F32), pltpu.VMEM((npair, r, LANE), F32),
                            pltpu.VMEM((npair // 2, 2 * r, 2 * LANE), F32)]),
        out_shape=jax.ShapeDtypeStruct((b, tq, 512), F32),
        compiler_params=_cp(("arbitrary", "arbitrary")),
    )(page_table, cache, lam_par, q, new_rows, gnorm)


def _q_perm():
    idx = np.zeros(512, np.int32)
    for j in range(NSA_J):
        for g in range(NSA_G):
            for d in range(DH):
                idx[j * LANE + g * DH + d] = (g * NSA_J + j) * DH + d
    return idx


def _gate3_perm():
    idx = np.zeros(3 * 512, np.int32)
    for c in range(3):
        for j in range(NSA_J):
            for g in range(NSA_G):
                idx[c * 512 + j * LANE + g * DH:c * 512 + j * LANE + (g + 1) * DH] = (g * NSA_J + j) * 3 + c
    return idx


def _cmp_to_sel(n_cmp, n_sel, nch, nsp):
    r = SEL_BLOCK // CMP_STRIDE
    k = np.arange(n_cmp)[:, None] - r * np.arange(n_sel)[None, :]
    m = sum(((k + n >= 0) & (k + n < r)).astype(np.float32) for n in range(CMP_BLOCK // CMP_STRIDE))
    out = np.zeros((nch, nsp), np.float32)
    out[:n_cmp, :n_sel] = m
    return out


def _block_expand(nsp, lk):
    return (np.arange(lk)[None, :] // SEL_BLOCK == np.arange(nsp)[:, None]).astype(np.float32)


def _cmp_weights(pe, w1, w2):
    eye = jnp.eye(2, dtype=F32)
    w6 = w1.reshape(2, NSA_G, 2, CMP_STRIDE, DH, DH)
    wc = jnp.zeros((CMP_STRIDE, 256, 512), BF16)
    for kv in range(2):
        for g in range(NSA_G):
            for half in range(2):
                r0, c0 = kv * LANE + g * DH, half * 256 + kv * LANE + g * DH
                wc = wc.at[:, r0:r0 + DH, c0:c0 + DH].set(w6[kv, g, half].astype(BF16))
    wc = wc.reshape(CMP_STRIDE * 256, 512)
    w2b = jnp.einsum("kghd,kK,gG->kghKGd", w2, eye, eye).reshape(256, 256).astype(BF16)
    pe_rows = jnp.transpose(pe, (0, 2, 1, 3)).reshape(4, CMP_BLOCK * DH)
    xb = jnp.einsum("rc,rR->rRc", pe_rows, jnp.eye(4, dtype=F32)).reshape(4, 4 * CMP_BLOCK * DH)
    xb = jnp.zeros((16, 4 * CMP_BLOCK * DH), F32).at[:4].set(xb)
    wb = jnp.zeros((4 * CMP_BLOCK * DH, LANE), F32).at[:, :DH].set(w1.reshape(4 * CMP_BLOCK * DH, DH))
    bias = mm(xb, wb)[:4, :DH].reshape(1, 256)
    return wc, bias, w2b


def _mixer(x, lw, lam_init, *, b, sq, sample=None):
    t = b * sq
    w_in = lw["w_in"]
    qperm = _q_perm()
    segs = ((512, DH ** -0.5 * LOG2E, None), (512, None, None), (256, None, None), (1536, None, "sigmoid"),
            (512, None, None), (1024, None, None), (512, DIFF_QK ** -0.5 * LOG2E, None), (1024, None, None))
    w_all = jnp.concatenate([w_in[:, OFF_Q:OFF_KV][:, qperm], w_in[:, OFF_KV:OFF_GATE],
                             w_in[:, OFF_GATE:OFF_POOL][:, _gate3_perm()], w_in[:, OFF_POOL:OFF_END]],
                            axis=1).astype(BF16)
    hn, q, rows4, winkv, g3, zpool, zgm, dq, dkv = in_proj(x, lw["norm_mix_g"], w_all, segs)
    q, rows4, winkv = q.reshape(b, sq, 512), rows4.reshape(b, sq, 512), winkv.reshape(b, sq, 256)
    zpool, zgm = zpool.reshape(b, sq, 512), zgm.reshape(b, sq, 1024)
    dq, dkv = dq.reshape(b, sq, 512), dkv.reshape(b, sq, 1024)

    wc, cbias, w2b = _cmp_weights(lw["cmp_pe"], lw["cmp_w1"], lw["cmp_w2"])
    lam_par = jnp.stack([lw["diff_lq1"], lw["diff_lk1"], lw["diff_lq2"], lw["diff_lk2"]])
    gnorm = jnp.tile(lw["diff_norm_g"], 2).reshape(1, LANE)

    if sample is None:
        past, l_tot = 0, sq
        nch = sq // CMP_STRIDE
        tq = _tile(sq, 256)
        tk_sel = _tile(sq, 1024)
        nk_sel = sq // tk_sel
        tw = _tile(sq, 256)
        tq_w = tw
        nband = WINDOW // tw + 1
        win_all, kpos0_w, nk_w = winkv, 0, nband
        win_kt = lambda i, j: (jnp.maximum(i - (nband - 1) + j, 0), i - (nband - 1) + j >= 0)
        free_w = 1 if nband == 3 else None
        tq_d = _tile(sq, 256)
        tk_d = _tile(sq, 1024)
        nk_d = sq // tk_d
        prev16 = jnp.zeros((b, 16, 512), F32)
        pool_state = zpool[:, sq - POOL_MEM:]
        zgm_in = zgm
    else:
        layer = sample["layer"]
        pt = sample["page_table"]
        past = pt.shape[1] * PAGE
        l_tot = past + sq
        nch = past // CMP_STRIDE
        tq = sq
        wbuf = sample["win"].shape[1]
        wrows = -(-(wbuf + sq) // KEY_SUB) * KEY_SUB
        tw = wrows
        win_all = jnp.concatenate([sample["win"], winkv, jnp.zeros((b, wrows - wbuf - sq, 256), F32)], axis=1)
        kpos0_w, nk_w = past - wbuf, 1
        win_kt = lambda i, j: (j, j >= 0)
        tq_w = sq
        free_w = None
        prev16 = jnp.concatenate([jnp.zeros((b, 1, 512), F32), sample["pool"]], axis=1)
        pool_state = jnp.concatenate([sample["pool"], zpool], axis=1)[:, -POOL_MEM:]
        zgm_in = jnp.concatenate([zgm, jnp.zeros((b, GM_CHUNK - sq, 1024), F32)], axis=1)

    n_cmp = l_tot // CMP_STRIDE - 1
    n_sel = -(-l_tot // SEL_BLOCK)
    nsp = -(-n_sel // LANE) * LANE
    msel = jnp.asarray(_cmp_to_sel(n_cmp, n_sel, nch, nsp), BF16)
    if sample is None:
        kcvc = compress(rows4, wc, cbias, w2b, nch)
        ocmp, selmask = cmp_attend(q, kcvc, msel, tq=tq, n_cmp=n_cmp, n_sel=n_sel, qpos0=0)
        emat = jnp.asarray(_block_expand(nsp, sq), BF16)

        def sel_kt(i, j):
            last = ((i + 1) * tq - 1) // tk_sel
            return jnp.minimum(j, last), j <= last

        osel = nsa_flash(q, rows4, 2, 3, mode="sel", tq=tq, tk=tk_sel, nk=nk_sel, qpos0=0, kpos0=0,
                         kt_fn=sel_kt, sel=selmask, emat=emat)
    else:
        kcvc = compress_paged(pt, sample["cache_nsa"], layer, wc, cbias, w2b)
        ocmp, selmask = cmp_attend(q, kcvc, msel, tq=tq, n_cmp=n_cmp, n_sel=n_sel, qpos0=past)
        emat = jnp.asarray(_block_expand(nsp, past + LANE), BF16)
        osel = sel_paged(pt, sample["cache_nsa"], layer, q, rows4, selmask, emat)
    owin = nsa_flash(q, win_all, 0, 1, mode="win", tq=tq_w, tk=tw, nk=nk_w, qpos0=past, kpos0=kpos0_w,
                     kt_fn=win_kt, free_step=free_w)

    opool = pool_mix(prev16, zpool, lw["pool_w"], lw["pool_scale"], past)
    ws_tril = jnp.tril(lw["gm_ws"]).astype(BF16)
    bs_exp = jnp.repeat(lw["gm_bs"].T, LANE, axis=1)
    ogm, gm_v = gmlp_mix(zgm_in, lw["gm_ng"], lw["gm_nb"], ws_tril, bs_exp)

    if sample is None:
        def diff_kt(i, j):
            last = ((i + 1) * tq_d - 1) // tk_d
            return jnp.minimum(j, last), j <= last

        odiff = diff_attend(lam_par, dq, dkv, gnorm, tq=tq_d, tk=tk_d, nk=nk_d, qpos0=0,
                            lam_init=lam_init, kt_fn=diff_kt)
    else:
        odiff = diff_paged(pt, sample["cache_diff"], layer, lam_par, dq, dkv, gnorm, lam_init)
        ogm = ogm[:, :sq]
        gm_v = gm_v[:, :sq]
    wbr = lw["w_branch"].at[0].set(lw["w_branch"][0][qperm]).astype(BF16)
    out = finish_mixer(x, hn, g3, ocmp.reshape(t, 512), osel.reshape(t, 512), owin.reshape(t, 512),
                       opool.reshape(t, 512), ogm.reshape(t, 512), odiff.reshape(t, 512),
                       lw["w_gate"].astype(BF16), lw["b_gate"].reshape(1, -1), wbr, lw["w_o"].astype(BF16))
    states = dict(rows4=rows4, winkv=winkv, win_all=win_all, pool_state=pool_state, gm_v=gm_v, dkv=dkv)
    return out, states


def kernel(x_prompt, x_sample, cache_nsa, cache_diff, state_nsa_win, state_pool, page_table, p_prompt, p_sample, norm_mix_g, w_in, nsa_cmp_pe, nsa_cmp_w1, nsa_cmp_w2, pool_w, pool_scale, gm_norm_g, gm_norm_b, gm_ws, gm_bs, diff_lq1, diff_lk1, diff_lq2, diff_lk2, diff_norm_g, w_branch, w_gate, b_gate, w_o, norm_ffn_g, ffn_w_gate, ffn_w_up, ffn_w_down, moe_router, moe_router_b, moe_w_gate, moe_w_up, moe_w_down, ple_norm_g, ple_w_gate, ple_w_proj, final_norm_g):
    bp, sp, d = x_prompt.shape
    bs, ss, _ = x_sample.shape
    depth = w_in.shape[0]
    n_phys = cache_nsa.shape[1]
    wbuf = state_nsa_win.shape[2]
    cache_nsa2 = jnp.transpose(cache_nsa, (0, 1, 3, 4, 5, 2)).reshape(depth, n_phys, 512, PAGE)
    cache_diff2 = jnp.transpose(cache_diff, (0, 1, 3, 4, 5, 2)).reshape(depth, n_phys, 1024, PAGE)
    xp = x_prompt.reshape(bp * sp, d)
    xs = x_sample.reshape(bs * ss, d)
    outs = {k: [] for k in ("nsa_p", "nsa_s", "win_p", "win_s", "pool_p", "pool_s", "gmv_s", "diff_p", "diff_s")}
    for l in range(depth):
        lw = dict(norm_mix_g=norm_mix_g[l], w_in=w_in[l], cmp_pe=nsa_cmp_pe[l], cmp_w1=nsa_cmp_w1[l],
                  cmp_w2=nsa_cmp_w2[l], pool_w=pool_w[l], pool_scale=pool_scale[l], gm_ng=gm_norm_g[l],
                  gm_nb=gm_norm_b[l], gm_ws=gm_ws[l], gm_bs=gm_bs[l], diff_lq1=diff_lq1[l], diff_lk1=diff_lk1[l],
                  diff_lq2=diff_lq2[l], diff_lk2=diff_lk2[l], diff_norm_g=diff_norm_g[l], w_branch=w_branch[l],
                  w_gate=w_gate[l], b_gate=b_gate[l], w_o=w_o[l])
        lam_init = 0.8 - 0.6 * math.exp(-0.3 * l)
        xp, st_p = _mixer(xp, lw, lam_init, b=bp, sq=sp)
        sample = dict(cache_nsa=cache_nsa2, cache_diff=cache_diff2, win=state_nsa_win[l].reshape(bs, wbuf, 256),
                      pool=state_pool[l], page_table=page_table, layer=l)
        xs, st_s = _mixer(xs, lw, lam_init, b=bs, sq=ss, sample=sample)
        outs["nsa_p"].append(st_p["rows4"].reshape(bp, sp, 4, NSA_G, DH))
        outs["diff_p"].append(st_p["dkv"].reshape(bp, sp, 2, DIFF_HEADS, 2 * DIFF_QK))
        outs["nsa_s"].append(st_s["rows4"].reshape(bs, ss, 4, NSA_G, DH))
        wkeep = min(WINDOW, sp)
        outs["win_p"].append(st_p["winkv"][:, sp - wkeep:].reshape(bp, wkeep, 2, NSA_G, DH))
        outs["win_s"].append(st_s["win_all"][:, ss:ss + wbuf].reshape(bs, wbuf, 2, NSA_G, DH))
        outs["pool_p"].append(st_p["pool_state"])
        outs["pool_s"].append(st_s["pool_state"])
        outs["gmv_s"].append(st_s["gm_v"])
        outs["diff_s"].append(st_s["dkv"].reshape(bs, ss, 2, DIFF_HEADS, 2 * DIFF_QK))
        i = l // 2
        if l % 2 == 0:
            xp = ffn_swiglu(xp, norm_ffn_g[l], ffn_w_gate[i], ffn_w_up[i], ffn_w_down[i])
            xs = ffn_swiglu(xs, norm_ffn_g[l], ffn_w_gate[i], ffn_w_up[i], ffn_w_down[i])
        else:
            xp = moe_swiglu(xp, norm_ffn_g[l], moe_router[i], moe_router_b[i], moe_w_gate[i], moe_w_up[i], moe_w_down[i])
            xs = moe_swiglu(xs, norm_ffn_g[l], moe_router[i], moe_router_b[i], moe_w_gate[i], moe_w_up[i], moe_w_down[i])
        final = l == depth - 1
        xp = ple(xp, p_prompt[l].reshape(bp * sp, -1), ple_norm_g[l], ple_w_gate[l], ple_w_proj[l], final_norm_g, final)
        xs = ple(xs, p_sample[l].reshape(bs * ss, -1), ple_norm_g[l], ple_w_gate[l], ple_w_proj[l], final_norm_g, final)
    st = lambda k: jnp.stack(outs[k])
    return (xp.reshape(bp, sp, d), xs.reshape(bs, ss, d), st("nsa_p"), st("nsa_s"), st("win_p"), st("win_s"),
            st("pool_p"), st("pool_s"), st("gmv_s"), st("diff_p"), st("diff_s"))
```

```python
import functools
import math

import numpy as np
import jax
import jax.numpy as jnp
from jax import lax
from jax.experimental import pallas as pl
from jax.experimental.pallas import tpu as pltpu

F32 = jnp.float32
BF16 = jnp.bfloat16

D_MODEL = 1024
PAGE = 128
NSA_G = 2
NSA_J = 4
DH = 64
CMP_BLOCK = 32
CMP_STRIDE = 16
SEL_BLOCK = 64
SEL_TOPK = 16
FORCE_BONUS = 100.0
WINDOW = 512
POOL_WINDOWS = (2, 4, 8, 16)
POOL_MEM = 15
GM_W = 512
GM_CHUNK = 128
DIFF_HEADS = 8
DIFF_QK = 32
N_BRANCH = 4
BRANCH_W = 512
N_EXPERTS = 8
EPS = 1e-6
NEG = -1e30
LOG2E = 1.4426950408889634
LANE = 128
VMEM_LIMIT = 56 * 1024 * 1024

OFF_Q, OFF_KV, OFF_GATE, OFF_POOL, OFF_GM, OFF_DQ, OFF_DKV, OFF_END = 0, 512, 1280, 1304, 1816, 2840, 3352, 4376


def _tile(n, pref):
    t = min(n, pref)
    while n % t:
        t //= 2
    return t


def _cp(sem, vmem=VMEM_LIMIT):
    return pltpu.CompilerParams(dimension_semantics=sem, vmem_limit_bytes=vmem)


def _gelu(x):
    return 0.5 * x * (1.0 + jnp.tanh(0.7978845608028654 * (x + 0.044715 * (x * x * x))))


def _rms(x, g):
    return x * lax.rsqrt(jnp.mean(x * x, axis=-1, keepdims=True) + EPS) * g


def _dot(a, b):
    return jnp.dot(a, b, preferred_element_type=F32)


def _dot_nt(a, b):
    return lax.dot_general(a, b, (((1,), (1,)), ((), ())), preferred_element_type=F32)


def _mm_body(x_ref, w_ref, o_ref):
    o_ref[...] = _dot(x_ref[...].astype(BF16), w_ref[...])


def mm(x, w):
    t, k = x.shape
    n = w.shape[1]
    tm = _tile(t, 512)
    tn = _tile(n, 512)
    return pl.pallas_call(
        _mm_body, name="mm", grid=(t // tm, n // tn),
        in_specs=[pl.BlockSpec((tm, k), lambda i, j: (i, 0)), pl.BlockSpec((k, tn), lambda i, j: (0, j))],
        out_specs=pl.BlockSpec((tm, tn), lambda i, j: (i, j)),
        out_shape=jax.ShapeDtypeStruct((t, n), F32),
        compiler_params=_cp(("parallel", "parallel")),
    )(x, w.astype(BF16))


def _in_proj_body(x_ref, g_ref, w_ref, hn_ref, *o_refs, segs):
    hn = _rms(x_ref[...], g_ref[...]).astype(BF16)
    hn_ref[...] = hn
    c0 = 0
    for (n, scale, act), o_ref in zip(segs, o_refs):
        y = _dot(hn, w_ref[:, c0:c0 + n])
        if scale is not None:
            y = y * scale
        if act == "sigmoid":
            y = jax.nn.sigmoid(y)
        o_ref[...] = y
        c0 += n


def in_proj(x, g, w_all, segs):
    t, d = x.shape
    tm = _tile(t, 256)
    ntot = w_all.shape[1]
    return pl.pallas_call(
        functools.partial(_in_proj_body, segs=segs), name="in_proj", grid=(t // tm,),
        in_specs=[pl.BlockSpec((tm, d), lambda i: (i, 0)), pl.BlockSpec((1, d), lambda i: (0, 0)),
                  pl.BlockSpec((d, ntot), lambda i: (0, 0), pipeline_mode=pl.Buffered(1))],
        out_specs=[pl.BlockSpec((tm, d), lambda i: (i, 0))] + [pl.BlockSpec((tm, n), lambda i: (i, 0)) for n, _, _ in segs],
        out_shape=[jax.ShapeDtypeStruct((t, d), BF16)] + [jax.ShapeDtypeStruct((t, n), F32) for n, _, _ in segs],
        compiler_params=_cp(("parallel",)),
    )(x, g.reshape(1, d), w_all)


def _compress_body(xk_ref, xv_ref, wc_ref, bias_ref, w2_ref, o_ref, *, nch):
    _compress_core(lambda s: xk_ref[pl.ds(s, nch, stride=CMP_STRIDE), :],
                   lambda s: xv_ref[pl.ds(s, nch, stride=CMP_STRIDE), :], wc_ref, bias_ref, w2_ref, o_ref, nch)


def _compress_core(load_k, load_v, wc_ref, bias_ref, w2_ref, o_ref, nch):
    xs = jnp.concatenate([part(s) for s in range(CMP_STRIDE) for part in (load_k, load_v)], axis=1).astype(BF16)
    acc = _dot(xs, wc_ref[...])
    hi_next = pltpu.roll(acc[:, 256:], nch - 1, 0)
    hid = _gelu(acc[:, :256] + hi_next + bias_ref[...])
    o_ref[...] = _dot(hid.astype(BF16), w2_ref[...]).astype(o_ref.dtype)


def compress(rows, wc, bias, w2, nch):
    b = rows.shape[0]
    return pl.pallas_call(
        functools.partial(_compress_body, nch=nch), name="compress", grid=(b,),
        in_specs=[pl.BlockSpec((None, nch * CMP_STRIDE, LANE), lambda i: (i, 0, 0)),
                  pl.BlockSpec((None, nch * CMP_STRIDE, LANE), lambda i: (i, 0, 1)),
                  pl.BlockSpec((CMP_STRIDE * 256, 512), lambda i: (0, 0)),
                  pl.BlockSpec((1, 256), lambda i: (0, 0)),
                  pl.BlockSpec((256, 256), lambda i: (0, 0))],
        out_specs=pl.BlockSpec((None, nch, 256), lambda i: (i, 0, 0)),
        out_shape=jax.ShapeDtypeStruct((b, nch, 256), BF16),
        compiler_params=_cp(("parallel",)),
    )(rows, rows, wc, bias, w2)


def _stack_heads(q, tq):
    lane = lax.broadcasted_iota(jnp.int32, (tq, LANE), 1)
    parts = []
    for g in range(NSA_G):
        keep = (lane < DH) if g == 0 else (lane >= DH)
        for j in range(NSA_J):
            parts.append(jnp.where(keep, q[:, j * LANE:(j + 1) * LANE], 0.0))
    return jnp.concatenate(parts, axis=0)


def _unstack_heads(o, tq):
    lane = lax.broadcasted_iota(jnp.int32, (tq, LANE), 1)
    outs = []
    for j in range(NSA_J):
        outs.append(jnp.where(lane < DH, o[j * tq:(j + 1) * tq], o[(NSA_J + j) * tq:(NSA_J + j + 1) * tq]))
    return jnp.concatenate(outs, axis=1)


def _cmp_body(q_ref, kv_ref, msel_ref, o_ref, sel_ref, *, tq, nch, n_cmp, n_sel, nsp, qpos0):
    i = pl.program_id(1)
    r = 8 * tq
    qs = _stack_heads(q_ref[...], tq).astype(BF16)
    kc = kv_ref[:, 0:LANE]
    vc = kv_ref[:, LANE:2 * LANE]
    s = _dot_nt(qs, kc)
    row = lax.broadcasted_iota(jnp.int32, (r, nch), 0)
    col = lax.broadcasted_iota(jnp.int32, (r, nch), 1)
    qpos = qpos0 + i * tq + (row & (tq - 1))
    cmask = (col * CMP_STRIDE + (CMP_BLOCK - 1) <= qpos) & (col < n_cmp)
    s = jnp.where(cmask, s, NEG)
    m = jnp.max(s, axis=1, keepdims=True)
    e = jnp.where(cmask, jnp.exp2(s - m), 0.0)
    l = jnp.sum(e, axis=1, keepdims=True)
    p = (e * (1.0 / jnp.where(l > 0.0, l, 1.0))).astype(BF16)
    o_ref[...] = _unstack_heads(_dot(p, vc), tq)
    imp_all = _dot(p, msel_ref[...])
    blocks_on_rows = tq % LANE == 0
    nrow = -(-n_sel // 8) * 8
    shape = (nrow, tq) if blocks_on_rows else (tq, nsp)
    baxis, qaxis = (0, 1) if blocks_on_rows else (1, 0)
    blk = lax.broadcasted_iota(jnp.int32, shape, baxis)
    qp = qpos0 + i * tq + lax.broadcasted_iota(jnp.int32, shape, qaxis)
    cur = lax.shift_right_logical(qp, SEL_BLOCK.bit_length() - 1)
    valid = (blk <= cur) & (blk < n_sel)
    forced = (blk == 0) | (blk == cur) | (blk == cur - 1)
    for g in range(NSA_G):
        imp = imp_all[(g * NSA_J) * tq:(g * NSA_J + 1) * tq]
        for j in range(1, NSA_J):
            imp = imp + imp_all[(g * NSA_J + j) * tq:(g * NSA_J + j + 1) * tq]
        if blocks_on_rows:
            imp = imp.T[:nrow]
        score = jnp.where(valid, imp + FORCE_BONUS * forced.astype(F32), -1.0)
        score = jnp.where(blk < n_sel, score, -2.0)
        rank = jnp.zeros(shape, F32)
        for mth in range(n_sel):
            cm = score[mth:mth + 1, :] if blocks_on_rows else score[:, mth:mth + 1]
            beats = (cm > score) | ((cm == score) & (blk > mth))
            rank = rank + beats.astype(F32)
        chosen = ((rank < float(min(SEL_TOPK, n_sel))) & valid).astype(F32)
        if blocks_on_rows:
            chosen = jnp.concatenate([chosen, jnp.zeros((nsp - nrow, tq), F32)], axis=0).T
        sel_ref[:, g * nsp:(g + 1) * nsp] = chosen


def cmp_attend(q, kcvc, msel, *, tq, n_cmp, n_sel, qpos0):
    b, sq, _ = q.shape
    nch = kcvc.shape[1]
    nsp = msel.shape[1]
    return pl.pallas_call(
        functools.partial(_cmp_body, tq=tq, nch=nch, n_cmp=n_cmp, n_sel=n_sel, nsp=nsp, qpos0=qpos0),
        name="cmp_attend",
        grid=(b, sq // tq),
        in_specs=[pl.BlockSpec((None, tq, 512), lambda bi, i: (bi, i, 0)),
                  pl.BlockSpec((None, nch, 256), lambda bi, i: (bi, 0, 0)),
                  pl.BlockSpec((nch, nsp), lambda bi, i: (0, 0))],
        out_specs=[pl.BlockSpec((None, tq, 512), lambda bi, i: (bi, i, 0)),
                   pl.BlockSpec((None, tq, 2 * nsp), lambda bi, i: (bi, i, 0))],
        out_shape=[jax.ShapeDtypeStruct((b, sq, 512), F32), jax.ShapeDtypeStruct((b, sq, 2 * nsp), F32)],
        compiler_params=_cp(("parallel", "parallel")),
    )(q, kcvc, msel)


ROW_BLOCK = 128
KEY_SUB = 256


def _flash_update(qs, k, v, bias, m, l, acc, kv_t=False):
    s = _dot(qs, k) if kv_t else _dot_nt(qs, k)
    if bias is not None:
        s = s + bias
    m_new, l_new, alpha, pb = _softmax_part(s, m, l)
    acc_new = alpha * acc + (_dot_nt(pb, v) if kv_t else _dot(pb, v))
    return m_new, l_new, acc_new


def _softmax_part(s, m, l):
    tiles = [s[:, c:c + LANE] for c in range(0, s.shape[1], LANE)]
    m_new = jnp.maximum(m, jnp.max(functools.reduce(jnp.maximum, tiles), axis=1, keepdims=True))
    alpha = jnp.exp2(m - m_new)
    ps = [jnp.exp2(t - m_new) for t in tiles]
    l_new = alpha * l + jnp.sum(functools.reduce(jnp.add, ps), axis=1, keepdims=True)
    return m_new, l_new, alpha, jnp.concatenate(ps, axis=1).astype(BF16)


def _flash_rows(qs_ref, k, v, bias_of, m_ref, l_ref, acc_ref, nrows, tq, kv_t=False):
    if tq < ROW_BLOCK:
        parts = [bias_of(h, 0, tq) for h in range(nrows // tq)]
        bias = None if parts[0] is None else jnp.concatenate(parts, axis=0)
        blocks = [(pl.ds(0, nrows), bias)]
    else:
        blocks = [(pl.ds(r0, ROW_BLOCK), bias_of(r0 // tq, r0 % tq, ROW_BLOCK)) for r0 in range(0, nrows, ROW_BLOCK)]
    for sl, bias in blocks:
        m, l, acc = _flash_update(qs_ref[sl], k, v, bias, m_ref[sl], l_ref[sl], acc_ref[sl], kv_t)
        m_ref[sl] = m
        l_ref[sl] = l
        acc_ref[sl] = acc


def _sel_ok(sel, e, nsp):
    return [_dot(sel[:, g * nsp:(g + 1) * nsp], e) > 0.5 for g in range(NSA_G)]


def _nsa_flash_body(*refs, mode, tq, tk, nk, qpos0, kpos0, kt_fn, nsp):
    if mode == "sel":
        q_ref, k_ref, v_ref, sel_ref, e_ref, o_ref, qs_ref, m_ref, l_ref, acc_ref = refs
    else:
        q_ref, k_ref, v_ref, o_ref, qs_ref, m_ref, l_ref, acc_ref = refs
    i = pl.program_id(1)
    j = pl.program_id(2)
    r = 8 * tq

    @pl.when(j == 0)
    def _init():
        qs_ref[...] = _stack_heads(q_ref[...], tq).astype(BF16)
        m_ref[...] = jnp.full((r, LANE), NEG, F32)
        l_ref[...] = jnp.zeros((r, LANE), F32)
        acc_ref[...] = jnp.zeros((r, LANE), F32)

    kt, valid = kt_fn(i, j)

    ks = min(tk, KEY_SUB)

    def step(c0, masked):
        k = k_ref[c0:c0 + ks, :].astype(BF16)
        v = v_ref[c0:c0 + ks, :].astype(BF16)
        if not masked:
            _flash_rows(qs_ref, k, v, lambda h, t0, n: None, m_ref, l_ref, acc_ref, r, tq)
            return
        qpos = qpos0 + i * tq + lax.broadcasted_iota(jnp.int32, (tq, ks), 0)
        kpos = kpos0 + kt * tk + c0 + lax.broadcasted_iota(jnp.int32, (tq, ks), 1)
        ok = kpos <= qpos
        if mode == "win":
            ok = ok & (kpos > qpos - WINDOW)
            bias = [jnp.where(ok, 0.0, NEG)] * NSA_G
        else:
            oks = _sel_ok(sel_ref[...].astype(BF16), e_ref[:, c0:c0 + ks], nsp)
            bias = [jnp.where(ok & okg, 0.0, NEG) for okg in oks]
        _flash_rows(qs_ref, k, v, lambda h, t0, n: bias[h // NSA_J][t0:t0 + n], m_ref, l_ref, acc_ref, r, tq)

    q_first = qpos0 + i * tq
    q_last = q_first + tq - 1
    for c0 in range(0, tk, ks):
        k_first = kpos0 + kt * tk + c0
        k_last = k_first + ks - 1
        live = valid & (k_first <= q_last)
        if mode == "sel":
            pl.when(live)(functools.partial(step, c0, True))
        else:
            live = live & (k_last > q_first - WINDOW)
            free = (k_last <= q_first) & (k_first > q_last - WINDOW)
            pl.when(live & jnp.logical_not(free))(functools.partial(step, c0, True))
            pl.when(live & free)(functools.partial(step, c0, False))

    @pl.when(j == nk - 1)
    def _fin():
        l = l_ref[...]
        o = acc_ref[...] * (1.0 / jnp.where(l > 0.0, l, 1.0))
        o_ref[...] = _unstack_heads(o, tq)


def nsa_flash(q, kv, kcol, vcol, *, mode, tq, tk, nk, qpos0, kpos0, kt_fn, sel=None, emat=None):
    b, sq, _ = q.shape
    nsp = 0 if sel is None else sel.shape[2] // 2

    def kmap(col):
        return lambda bi, i, j: (bi, kt_fn(i, j)[0], col)

    in_specs = [pl.BlockSpec((None, tq, 512), lambda bi, i, j: (bi, i, 0)),
                pl.BlockSpec((None, tk, LANE), kmap(kcol)),
                pl.BlockSpec((None, tk, LANE), kmap(vcol))]
    args = [q, kv, kv]
    if mode == "sel":
        in_specs += [pl.BlockSpec((None, tq, 2 * nsp), lambda bi, i, j: (bi, i, 0)),
                     pl.BlockSpec((nsp, tk), lambda bi, i, j: (0, kt_fn(i, j)[0]))]
        args += [sel, emat]
    r = 8 * tq
    return pl.pallas_call(
        functools.partial(_nsa_flash_body, mode=mode, tq=tq, tk=tk, nk=nk, qpos0=qpos0, kpos0=kpos0,
                          kt_fn=kt_fn, nsp=nsp),
        name="nsa_" + mode, grid=(b, sq // tq, nk),
        in_specs=in_specs,
        out_specs=pl.BlockSpec((None, tq, 512), lambda bi, i, j: (bi, i, 0)),
        out_shape=jax.ShapeDtypeStruct((b, sq, 512), F32),
        scratch_shapes=[pltpu.VMEM((r, LANE), BF16), pltpu.VMEM((r, LANE), F32), pltpu.VMEM((r, LANE), F32),
                        pltpu.VMEM((r, LANE), F32)],
        compiler_params=_cp(("parallel", "parallel", "arbitrary")),
    )(*args)


def _diff_init(q_ref, qs_ref, m_ref, l_ref, acc_ref, tq):
    r = 4 * tq
    npair = DIFF_HEADS // 2
    lane = lax.broadcasted_iota(jnp.int32, (tq, LANE), 1)
    for hp in range(npair):
        q = q_ref[:, hp * LANE:(hp + 1) * LANE]
        parts = []
        for h in range(2):
            for mth in range(2):
                lo = h * 2 * DIFF_QK + mth * DIFF_QK
                parts.append(jnp.where((lane >= lo) & (lane < lo + DIFF_QK), q, 0.0))
        qs_ref[hp] = jnp.concatenate(parts, axis=0).astype(BF16)
    m_ref[...] = jnp.full((npair, r, LANE), NEG, F32)
    l_ref[...] = jnp.zeros((npair, r, LANE), F32)
    acc_ref[...] = jnp.zeros((npair, r, LANE), F32)


def _diff_step(k_of, v_of, bias, qs_ref, m_ref, l_ref, acc_ref, tq, kv_t=False):
    bias_of = lambda h, t0, n: None if bias is None else bias[t0:t0 + n]
    for hp in range(DIFF_HEADS // 2):
        _flash_rows(qs_ref.at[hp], k_of(hp), v_of(hp), bias_of, m_ref.at[hp], l_ref.at[hp], acc_ref.at[hp],
                    4 * tq, tq, kv_t)


def _diff_fin(lam_ref, g_ref, o_ref, l_ref, acc_ref, tq, lam_init):
    lp = lam_ref[...]
    lam = (jnp.exp(jnp.sum(lp[0:1] * lp[1:2], axis=1, keepdims=True))
           - jnp.exp(jnp.sum(lp[2:3] * lp[3:4], axis=1, keepdims=True)) + lam_init)
    lane = lax.broadcasted_iota(jnp.int32, (tq, LANE), 1)
    lo_half = lane < 2 * DIFF_QK
    for hp in range(DIFF_HEADS // 2):
        l = l_ref[hp]
        a = acc_ref[hp] * (1.0 / jnp.where(l > 0.0, l, 1.0))
        o0 = a[0:tq] - lam * a[tq:2 * tq]
        o1 = a[2 * tq:3 * tq] - lam * a[3 * tq:4 * tq]
        o = jnp.where(lo_half, o0, o1)
        sq = o * o
        ms0 = jnp.sum(jnp.where(lo_half, sq, 0.0), axis=1, keepdims=True)
        ms1 = jnp.sum(jnp.where(lo_half, 0.0, sq), axis=1, keepdims=True)
        ms = jnp.where(lo_half, ms0, ms1) * (1.0 / (2 * DIFF_QK))
        y = o * lax.rsqrt(ms + EPS) * g_ref[...]
        o_ref[:, hp * LANE:(hp + 1) * LANE] = y * (1.0 - lam_init)


def _diff_body(lam_ref, q_ref, k_ref, v_ref, g_ref, o_ref, qs_ref, m_ref, l_ref, acc_ref, *,
               tq, tk, nk, qpos0, lam_init, kt_fn):
    i = pl.program_id(1)
    j = pl.program_id(2)

    @pl.when(j == 0)
    def _init():
        _diff_init(q_ref, qs_ref, m_ref, l_ref, acc_ref, tq)

    kt, valid = kt_fn(i, j)
    ks = min(tk, KEY_SUB)

    def step(c0, masked):
        k_of = lambda hp: k_ref[c0:c0 + ks, hp * LANE:(hp + 1) * LANE].astype(BF16)
        v_of = lambda hp: v_ref[c0:c0 + ks, hp * LANE:(hp + 1) * LANE].astype(BF16)
        bias = None
        if masked:
            qpos = qpos0 + i * tq + lax.broadcasted_iota(jnp.int32, (tq, ks), 0)
            kpos = kt * tk + c0 + lax.broadcasted_iota(jnp.int32, (tq, ks), 1)
            bias = jnp.where(kpos <= qpos, 0.0, NEG)
        _diff_step(k_of, v_of, bias, qs_ref, m_ref, l_ref, acc_ref, tq)

    for c0 in range(0, tk, ks):
        k_first = kt * tk + c0
        live = valid & (k_first <= qpos0 + (i + 1) * tq - 1)
        below = k_first + ks - 1 <= qpos0 + i * tq
        pl.when(live & below)(functools.partial(step, c0, False))
        pl.when(live & jnp.logical_not(below))(functools.partial(step, c0, True))

    @pl.when(j == nk - 1)
    def _fin():
        _diff_fin(lam_ref, g_ref, o_ref, l_ref, acc_ref, tq, lam_init)


def diff_attend(lam_par, q, kv, gnorm, *, tq, tk, nk, qpos0, lam_init, kt_fn):
    b, sq, _ = q.shape
    r = 4 * tq
    npair = DIFF_HEADS // 2
    return pl.pallas_call(
        functools.partial(_diff_body, tq=tq, tk=tk, nk=nk, qpos0=qpos0, lam_init=lam_init, kt_fn=kt_fn),
        name="diff_attend",
        grid=(b, sq // tq, nk),
        in_specs=[pl.BlockSpec((4, DIFF_QK), lambda bi, i, j: (0, 0)),
                  pl.BlockSpec((None, tq, 512), lambda bi, i, j: (bi, i, 0)),
                  pl.BlockSpec((None, tk, 512), lambda bi, i, j: (bi, kt_fn(i, j)[0], 0)),
                  pl.BlockSpec((None, tk, 512), lambda bi, i, j: (bi, kt_fn(i, j)[0], 1)),
                  pl.BlockSpec((1, LANE), lambda bi, i, j: (0, 0))],
        out_specs=pl.BlockSpec((None, tq, 512), lambda bi, i, j: (bi, i, 0)),
        out_shape=jax.ShapeDtypeStruct((b, sq, 512), F32),
        scratch_shapes=[pltpu.VMEM((npair, r, LANE), BF16), pltpu.VMEM((npair, r, LANE), F32),
                        pltpu.VMEM((npair, r, LANE), F32), pltpu.VMEM((npair, r, LANE), F32)],
        compiler_params=_cp(("parallel", "parallel", "arbitrary")),
    )(lam_par, q, kv, kv, gnorm)


def _pool_body(prev_ref, z_ref, pw_ref, sc_ref, o_ref, x_ref, *, sq, start):
    x_ref[pl.ds(0, 16), :] = prev_ref[...]
    x_ref[pl.ds(16, sq), :] = z_ref[...]
    pos = start + lax.broadcasted_iota(jnp.int32, (sq, LANE), 0)
    for g, w in enumerate(POOL_WINDOWS):
        c0 = g * LANE
        x = x_ref[pl.ds(16, sq), c0:c0 + LANE]
        tot = x
        for back in range(1, w):
            tot = tot + x_ref[pl.ds(16 - back, sq), c0:c0 + LANE]
        cnt = jnp.minimum(w, pos + 1).astype(F32)
        mix = tot / cnt - x
        y = _dot(mix.astype(BF16), pw_ref[g])
        o_ref[:, c0:c0 + LANE] = y * sc_ref[:, c0:c0 + LANE]


def pool_mix(prev16, z, pw, scale, start):
    b, sq, _ = z.shape
    return pl.pallas_call(
        functools.partial(_pool_body, sq=sq, start=start), name="pool_mix", grid=(b,),
        in_specs=[pl.BlockSpec((None, 16, 512), lambda i: (i, 0, 0)),
                  pl.BlockSpec((None, sq, 512), lambda i: (i, 0, 0)),
                  pl.BlockSpec((4, LANE, LANE), lambda i: (0, 0, 0)),
                  pl.BlockSpec((1, 512), lambda i: (0, 0))],
        out_specs=pl.BlockSpec((None, sq, 512), lambda i: (i, 0, 0)),
        out_shape=jax.ShapeDtypeStruct((b, sq, 512), F32),
        scratch_shapes=[pltpu.VMEM((sq + 16, 512), F32)],
        compiler_params=_cp(("parallel",)),
    )(prev16, z, pw.astype(BF16), scale.reshape(1, 512))


def _gmlp_body(z_ref, ng_ref, nb_ref, ws_ref, bs_ref, o_ref, v_ref, *, tg):
    z = _gelu(z_ref[...])
    u = z[:, :GM_W]
    vr = z[:, GM_W:]
    xc = vr - jnp.mean(vr, axis=-1, keepdims=True)
    v = xc * lax.rsqrt(jnp.mean(xc * xc, axis=-1, keepdims=True) + EPS) * ng_ref[...] + nb_ref[...]
    v_ref[...] = v
    vb = v.astype(BF16)
    for c in range(tg // GM_CHUNK):
        r0 = c * GM_CHUNK
        for g in range(4):
            c0 = g * LANE
            mixed = _dot(ws_ref[g], vb[r0:r0 + GM_CHUNK, c0:c0 + LANE]) + bs_ref[:, c0:c0 + LANE]
            o_ref[r0:r0 + GM_CHUNK, c0:c0 + LANE] = u[r0:r0 + GM_CHUNK, c0:c0 + LANE] * mixed


def gmlp_mix(z, ng, nb, ws_tril, bs_exp):
    b, s, _ = z.shape
    tg = _tile(s, 512)
    return pl.pallas_call(
        functools.partial(_gmlp_body, tg=tg), name="gmlp_mix", grid=(b, s // tg),
        in_specs=[pl.BlockSpec((None, tg, 1024), lambda bi, i: (bi, i, 0)),
                  pl.BlockSpec((1, 512), lambda bi, i: (0, 0)),
                  pl.BlockSpec((1, 512), lambda bi, i: (0, 0)),
                  pl.BlockSpec((4, GM_CHUNK, GM_CHUNK), lambda bi, i: (0, 0, 0)),
                  pl.BlockSpec((GM_CHUNK, 512), lambda bi, i: (0, 0))],
        out_specs=[pl.BlockSpec((None, tg, 512), lambda bi, i: (bi, i, 0)),
                   pl.BlockSpec((None, tg, 512), lambda bi, i: (bi, i, 0))],
        out_shape=[jax.ShapeDtypeStruct((b, s, 512), F32), jax.ShapeDtypeStruct((b, s, 512), F32)],
        compiler_params=_cp(("parallel", "parallel")),
    )(z, ng.reshape(1, 512), nb.reshape(1, 512), ws_tril, bs_exp)


def _finish_body(x_ref, hn_ref, g3_ref, ocmp_ref, osel_ref, owin_ref, opool_ref, ogm_ref, odiff_ref,
                 wgate_ref, bgate_ref, wbr_ref, wo_ref, o_ref):
    hn = hn_ref[...]
    g3 = g3_ref[...]
    onsa = g3[:, 0:512] * ocmp_ref[...] + g3[:, 512:1024] * osel_ref[...] + g3[:, 1024:1536] * owin_ref[...]
    branches = (onsa, opool_ref[...], ogm_ref[...], odiff_ref[...])
    acc = jnp.zeros(o_ref.shape, F32)
    for n in range(N_BRANCH):
        c0 = n * D_MODEL
        gate = jax.nn.sigmoid(_dot(hn, wgate_ref[:, c0:c0 + D_MODEL]) + bgate_ref[:, c0:c0 + D_MODEL])
        acc = acc + gate * _dot(branches[n].astype(BF16), wbr_ref[n])
    o_ref[...] = x_ref[...] + _dot(acc.astype(BF16), wo_ref[...])


def finish_mixer(x, hn, g3, ocmp, osel, owin, opool, ogm, odiff, wgate, bgate, wbr, wo):
    t = x.shape[0]
    tm = _tile(t, 256)
    row = lambda w: pl.BlockSpec((tm, w), lambda i: (i, 0))
    const = lambda shape: pl.BlockSpec(shape, lambda i: (0,) * len(shape), pipeline_mode=pl.Buffered(1))
    return pl.pallas_call(
        _finish_body, name="finish_mixer", grid=(t // tm,),
        in_specs=[row(1024), row(1024), row(1536), row(512), row(512), row(512), row(512), row(512), row(512),
                  const((D_MODEL, N_BRANCH * D_MODEL)), const((1, N_BRANCH * D_MODEL)),
                  const((N_BRANCH, BRANCH_W, D_MODEL)), const((D_MODEL, D_MODEL))],
        out_specs=row(1024),
        out_shape=jax.ShapeDtypeStruct((t, D_MODEL), F32),
        compiler_params=_cp(("parallel",)),
    )(x, hn, g3, ocmp, osel, owin, opool, ogm, odiff, wgate, bgate, wbr, wo)


def _ffn_body(x_ref, g_ref, wg_ref, wu_ref, wd_ref, o_ref, hn_ref, acc_ref, *, nf):
    j = pl.program_id(1)

    @pl.when(j == 0)
    def _init():
        hn_ref[...] = _rms(x_ref[...], g_ref[...]).astype(BF16)
        acc_ref[...] = jnp.zeros(acc_ref.shape, F32)

    h = hn_ref[...]
    a = _dot(h, wg_ref[...])
    act = a * jax.nn.sigmoid(a) * _dot(h, wu_ref[...])
    acc_ref[...] += _dot(act.astype(BF16), wd_ref[...])

    @pl.when(j == nf - 1)
    def _fin():
        o_ref[...] = x_ref[...] + acc_ref[...]


def ffn_swiglu(x, g, wg, wu, wd):
    t, d = x.shape
    f = wg.shape[1]
    tm = _tile(t, 512)
    tf = 1408 if f % 1408 == 0 else _tile(f, 512)
    nf = f // tf
    return pl.pallas_call(
        functools.partial(_ffn_body, nf=nf), name="ffn_swiglu", grid=(t // tm, nf),
        in_specs=[pl.BlockSpec((tm, d), lambda i, j: (i, 0)), pl.BlockSpec((1, d), lambda i, j: (0, 0)),
                  pl.BlockSpec((d, tf), lambda i, j: (0, j)), pl.BlockSpec((d, tf), lambda i, j: (0, j)),
                  pl.BlockSpec((tf, d), lambda i, j: (j, 0))],
        out_specs=pl.BlockSpec((tm, d), lambda i, j: (i, 0)),
        out_shape=jax.ShapeDtypeStruct((t, d), F32),
        scratch_shapes=[pltpu.VMEM((tm, d), BF16), pltpu.VMEM((tm, d), F32)],
        compiler_params=_cp(("parallel", "arbitrary")),
    )(x, g.reshape(1, d), wg.astype(BF16), wu.astype(BF16), wd.astype(BF16))


MOE_SUB = 256


def _moe_body(x_ref, g_ref, r_ref, rb_ref, u_ref, wg_ref, wu_ref, wd_ref, o_ref,
              hn_ref, comb_ref, post_ref, posr_ref, cnt_ref, xe_ref, ye_ref, acc_ref, *, nf):
    e = pl.program_id(1)
    f = pl.program_id(2)
    tm = x_ref.shape[0]
    lane = lax.broadcasted_iota(jnp.int32, (tm, LANE), 1)

    @pl.when((e == 0) & (f == 0))
    def _init():
        hn = _rms(x_ref[...], g_ref[...]).astype(BF16)
        hn_ref[...] = hn
        lg = _dot(hn, r_ref[...]) + rb_ref[...]
        m1 = jnp.max(lg, axis=1, keepdims=True)
        i1 = jnp.min(jnp.where(lg == m1, lane, LANE), axis=1, keepdims=True)
        lg2 = jnp.where(lane == i1, -3e38, lg)
        m2 = jnp.max(lg2, axis=1, keepdims=True)
        i2 = jnp.min(jnp.where(lg2 == m2, lane, LANE), axis=1, keepdims=True)
        e2 = jnp.exp(m2 - m1)
        w1 = 1.0 / (1.0 + e2)
        comb = jnp.where(lane == i1, w1, 0.0) + jnp.where(lane == i2, e2 * w1, 0.0)
        comb_ref[...] = comb
        ind_t = (comb.T > 0.0).astype(F32)
        pos_t = _dot(ind_t.astype(BF16), u_ref[...])
        post_ref[...] = jnp.where(ind_t > 0.0, pos_t, -1.0)
        posr_ref[...] = post_ref[...].T
        cnt = jnp.sum(ind_t, axis=1, keepdims=True)
        for ex in range(N_EXPERTS):
            cnt_ref[ex] = cnt[ex, 0].astype(jnp.int32)
        acc_ref[...] = jnp.zeros(acc_ref.shape, F32)

    cnt = cnt_ref[e]
    rem = cnt % MOE_SUB
    has_tail = (rem > 0) & (rem <= MOE_SUB // 2)
    n_full = cnt // MOE_SUB + (rem > MOE_SUB // 2).astype(jnp.int32)

    def each_subtile(fn):
        lax.fori_loop(0, n_full, lambda u, c: (fn(u, MOE_SUB), c)[1], 0)
        pl.when(has_tail)(lambda: fn(n_full, MOE_SUB // 2))

    @pl.when(f == 0)
    def _gather():
        slot_t = post_ref[pl.ds(e, 1), :]

        def body(u, rows):
            base = lax.broadcasted_iota(jnp.int32, (rows, tm), 0).astype(F32) + (u * MOE_SUB).astype(F32)
            onehot = jnp.where(slot_t == base, 1.0, 0.0).astype(BF16)
            xe_ref[u, pl.ds(0, rows)] = _dot(onehot, hn_ref[...]).astype(BF16)
            ye_ref[u, pl.ds(0, rows)] = jnp.zeros((rows, x_ref.shape[1]), F32)

        each_subtile(body)

    def expert(u, rows):
        xe = xe_ref[u, pl.ds(0, rows)]
        a = _dot(xe, wg_ref[...])
        act = a * jax.nn.sigmoid(a) * _dot(xe, wu_ref[...])
        ye_ref[u, pl.ds(0, rows)] += _dot(act.astype(BF16), wd_ref[...])

    each_subtile(expert)

    @pl.when(f == nf - 1)
    def _scatter():
        slot_r = jnp.sum(jnp.where(lane == e, posr_ref[...], 0.0), axis=1, keepdims=True)
        w_r = jnp.sum(jnp.where(lane == e, comb_ref[...], 0.0), axis=1, keepdims=True)

        def body(u, rows):
            base = lax.broadcasted_iota(jnp.int32, (tm, rows), 1).astype(F32) + (u * MOE_SUB).astype(F32)
            onehot = jnp.where(slot_r == base, 1.0, 0.0).astype(BF16)
            y = ye_ref[u, pl.ds(0, rows)]
            y_hi = y.astype(BF16)
            y_lo = (y - y_hi.astype(F32)).astype(BF16)
            acc_ref[...] += w_r * (_dot(onehot, y_hi) + _dot(onehot, y_lo))

        each_subtile(body)

    @pl.when((e == N_EXPERTS - 1) & (f == nf - 1))
    def _fin():
        o_ref[...] = x_ref[...] + acc_ref[...]


def moe_swiglu(x, g, router, router_b, wg, wu, wd):
    t, d = x.shape
    f = wg.shape[2]
    tm = _tile(t, 1024)
    tf = 896 if f % 896 == 0 else _tile(f, 512)
    nf = f // tf
    nsubmax = -(-tm // MOE_SUB)
    rpad = jnp.zeros((d, LANE), F32).at[:, :N_EXPERTS].set(router).astype(BF16)
    rbpad = jnp.full((1, LANE), NEG, F32).at[0, :N_EXPERTS].set(router_b)
    before = jnp.asarray(np.triu(np.ones((tm, tm), np.float32), 1), BF16)
    const = lambda shape: pl.BlockSpec(shape, lambda i, e, j: (0,) * len(shape))
    return pl.pallas_call(
        functools.partial(_moe_body, nf=nf), name="moe_swiglu", grid=(t // tm, N_EXPERTS, nf),
        in_specs=[pl.BlockSpec((tm, d), lambda i, e, j: (i, 0)), const((1, d)), const((d, LANE)), const((1, LANE)),
                  const((tm, tm)),
                  pl.BlockSpec((None, d, tf), lambda i, e, j: (e, 0, j)),
                  pl.BlockSpec((None, d, tf), lambda i, e, j: (e, 0, j)),
                  pl.BlockSpec((None, tf, d), lambda i, e, j: (e, j, 0))],
        out_specs=pl.BlockSpec((tm, d), lambda i, e, j: (i, 0)),
        out_shape=jax.ShapeDtypeStruct((t, d), F32),
        scratch_shapes=[pltpu.VMEM((tm, d), BF16), pltpu.VMEM((tm, LANE), F32), pltpu.VMEM((LANE, tm), F32),
                        pltpu.VMEM((tm, LANE), F32), pltpu.SMEM((N_EXPERTS,), jnp.int32),
                        pltpu.VMEM((nsubmax, MOE_SUB, d), BF16), pltpu.VMEM((nsubmax, MOE_SUB, d), F32),
                        pltpu.VMEM((tm, d), F32)],
        compiler_params=_cp(("parallel", "arbitrary", "arbitrary")),
    )(x, g.reshape(1, d), rpad, rbpad, before, wg.astype(BF16), wu.astype(BF16), wd.astype(BF16))


def _ple_body(x_ref, p_ref, g_ref, wg_ref, wp_ref, fg_ref, o_ref, *, final):
    x = x_ref[...]
    hn = _rms(x, g_ref[...]).astype(BF16)
    gate = jax.nn.sigmoid(_dot(hn, wg_ref[...]))
    y = x + gate * _dot(p_ref[...].astype(BF16), wp_ref[...])
    if final:
        y = _rms(y, fg_ref[...])
    o_ref[...] = y


def ple(x, p, g, wg, wp, fg, final):
    t, d = x.shape
    pd = p.shape[1]
    tm = _tile(t, 512)
    return pl.pallas_call(
        functools.partial(_ple_body, final=final), name="ple", grid=(t // tm,),
        in_specs=[pl.BlockSpec((tm, d), lambda i: (i, 0)), pl.BlockSpec((tm, pd), lambda i: (i, 0)),
                  pl.BlockSpec((1, d), lambda i: (0, 0)), pl.BlockSpec((d, d), lambda i: (0, 0)),
                  pl.BlockSpec((pd, d), lambda i: (0, 0)), pl.BlockSpec((1, d), lambda i: (0, 0))],
        out_specs=pl.BlockSpec((tm, d), lambda i: (i, 0)),
        out_shape=jax.ShapeDtypeStruct((t, d), F32),
        compiler_params=_cp(("parallel",)),
    )(x, p, g.reshape(1, d), wg.astype(BF16), wp.astype(BF16), fg.reshape(1, d))


def _page_copies(pt_ref, cache_ref, bufs, sem, slot, b, c, *, layer, pc, cols):
    out = []
    for p in range(pc):
        pg = pt_ref[b, c * pc + p]
        for buf, (f0, w) in zip(bufs, cols):
            out.append(pltpu.make_async_copy(cache_ref.at[layer, pg, pl.ds(f0, w), :], buf.at[slot, p],
                                             sem.at[slot]))
    return out


def _paged_pipeline(pt_ref, cache_ref, bufs, sem, *, nb, nc, **kw):
    b = pl.program_id(0)
    c = pl.program_id(1) if nc > 1 else 0
    step = b * nc + c
    slot = step % 2

    @pl.when(step == 0)
    def _first():
        for d in _page_copies(pt_ref, cache_ref, bufs, sem, 0, 0, 0, **kw):
            d.start()

    @pl.when(step + 1 < nb * nc)
    def _prefetch():
        nxt = step + 1
        for d in _page_copies(pt_ref, cache_ref, bufs, sem, 1 - slot, nxt // nc, nxt % nc, **kw):
            d.start()

    for d in _page_copies(pt_ref, cache_ref, bufs, sem, slot, b, c, **kw):
        d.wait()
    return slot


def _compress_paged_body(pt_ref, cache_ref, wc_ref, bias_ref, w2_ref, o_ref, bufk, bufv, sem, rowk, rowv, *,
                         layer, npg, nb, nch):
    slot = _paged_pipeline(pt_ref, cache_ref, (bufk, bufv), sem, nb=nb, nc=1, layer=layer, pc=npg,
                           cols=((0, LANE), (LANE, LANE)))

    def to_rows(p, carry):
        rows = pl.ds(pl.multiple_of(p * PAGE, PAGE), PAGE)
        rowk[rows, :] = bufk[slot, p].T
        rowv[rows, :] = bufv[slot, p].T
        return carry

    lax.fori_loop(0, npg, to_rows, 0, unroll=4)
    _compress_core(lambda s: rowk[pl.ds(s, nch, stride=CMP_STRIDE), :],
                   lambda s: rowv[pl.ds(s, nch, stride=CMP_STRIDE), :], wc_ref, bias_ref, w2_ref, o_ref, nch)


def compress_paged(page_table, cache, layer, wc, bias, w2):
    b, npg = page_table.shape
    past = npg * PAGE
    nch = past // CMP_STRIDE
    const = lambda shape: pl.BlockSpec(shape, lambda i, pt: (0,) * len(shape))
    return pl.pallas_call(
        functools.partial(_compress_paged_body, layer=layer, npg=npg, nb=b, nch=nch), name="compress_paged",
        grid_spec=pltpu.PrefetchScalarGridSpec(
            num_scalar_prefetch=1, grid=(b,),
            in_specs=[pl.BlockSpec(memory_space=pl.ANY), const((CMP_STRIDE * 256, 512)), const((1, 256)),
                      const((256, 256))],
            out_specs=pl.BlockSpec((None, nch, 256), lambda i, pt: (i, 0, 0)),
            scratch_shapes=[pltpu.VMEM((2, npg, LANE, PAGE), F32), pltpu.VMEM((2, npg, LANE, PAGE), F32),
                            pltpu.SemaphoreType.DMA((2,)),
                            pltpu.VMEM((past, LANE), F32), pltpu.VMEM((past, LANE), F32)]),
        out_shape=jax.ShapeDtypeStruct((b, nch, 256), BF16),
        compiler_params=_cp(("arbitrary",)),
    )(page_table, cache, wc, bias, w2)


def _new_tile(new, c0, tq):
    return jnp.concatenate([new[:, c0:c0 + LANE], jnp.zeros((LANE - tq, LANE), F32)], axis=0).astype(BF16)


def _sel_paged_body(pt_ref, cache_ref, q_ref, new_ref, sel_ref, e_ref, o_ref, bufk, bufv, sem, *,
                    layer, npg, nb, tq, tk, nsp):
    slot = _paged_pipeline(pt_ref, cache_ref, (bufk, bufv), sem, nb=nb, nc=1, layer=layer, pc=npg,
                           cols=((2 * LANE, LANE), (3 * LANE, LANE)))
    past = npg * PAGE
    ppc = tk // PAGE
    r = 8 * tq
    qs = _stack_heads(q_ref[...], tq).astype(BF16)
    sel = sel_ref[...].astype(BF16)
    m = jnp.full((r, LANE), NEG, F32)
    l = jnp.zeros((r, LANE), F32)
    acc = jnp.zeros((r, LANE), F32)

    def stacked_bias(oks):
        return jnp.concatenate([jnp.where(ok, 0.0, NEG) for ok in oks for _ in range(NSA_J)], axis=0)

    for c in range(npg // ppc):
        kt = jnp.concatenate([bufk[slot, c * ppc + u] for u in range(ppc)], axis=1).astype(BF16)
        vt = jnp.concatenate([bufv[slot, c * ppc + u] for u in range(ppc)], axis=1).astype(BF16)
        bias = stacked_bias(_sel_ok(sel, e_ref[:, c * tk:(c + 1) * tk], nsp))
        m, l, acc = _flash_update(qs, kt, vt, bias, m, l, acc, kv_t=True)
    new = new_ref[...]
    causal = (lax.broadcasted_iota(jnp.int32, (tq, LANE), 1) <= lax.broadcasted_iota(jnp.int32, (tq, LANE), 0))
    bias = stacked_bias([ok & causal for ok in _sel_ok(sel, e_ref[:, past:past + LANE], nsp)])
    m, l, acc = _flash_update(qs, _new_tile(new, 2 * LANE, tq), _new_tile(new, 3 * LANE, tq), bias, m, l, acc)
    o_ref[...] = _unstack_heads(acc * (1.0 / jnp.where(l > 0.0, l, 1.0)), tq)


def sel_paged(page_table, cache, layer, q, new_rows, sel, emat):
    b, npg = page_table.shape
    past = npg * PAGE
    tq = q.shape[1]
    nsp = sel.shape[2] // 2
    return pl.pallas_call(
        functools.partial(_sel_paged_body, layer=layer, npg=npg, nb=b, tq=tq, tk=_tile(past, 512), nsp=nsp),
        name="sel_paged",
        grid_spec=pltpu.PrefetchScalarGridSpec(
            num_scalar_prefetch=1, grid=(b,),
            in_specs=[pl.BlockSpec(memory_space=pl.ANY),
                      pl.BlockSpec((None, tq, 512), lambda i, pt: (i, 0, 0)),
                      pl.BlockSpec((None, tq, 512), lambda i, pt: (i, 0, 0)),
                      pl.BlockSpec((None, tq, 2 * nsp), lambda i, pt: (i, 0, 0)),
                      pl.BlockSpec((nsp, past + LANE), lambda i, pt: (0, 0))],
            out_specs=pl.BlockSpec((None, tq, 512), lambda i, pt: (i, 0, 0)),
            scratch_shapes=[pltpu.VMEM((2, npg, LANE, PAGE), F32), pltpu.VMEM((2, npg, LANE, PAGE), F32),
                            pltpu.SemaphoreType.DMA((2,))]),
        out_shape=jax.ShapeDtypeStruct((b, tq, 512), F32),
        compiler_params=_cp(("arbitrary",)),
    )(page_table, cache, q, new_rows, sel, emat)


def _diff_paged_body(pt_ref, cache_ref, lam_ref, q_ref, new_ref, g_ref, o_ref, bufk, bufv, sem,
                     qs_ref, m_ref, l_ref, acc_ref, acc2_ref, *, layer, nb, nc, pc, tq, lam_init):
    c = pl.program_id(1)
    slot = _paged_pipeline(pt_ref, cache_ref, (bufk, bufv), sem, nb=nb, nc=nc, layer=layer, pc=pc,
                           cols=((0, 512), (512, 512)))
    ppc = _tile(pc, 4)
    r = 4 * tq

    @pl.when(c == 0)
    def _init():
        _diff_init(q_ref, qs_ref, m_ref, l_ref, acc_ref, tq)
        acc2_ref[...] = jnp.zeros(acc2_ref.shape, F32)

    def pages_t(buf, f0, nf, p0):
        return jnp.concatenate([buf[slot, p0 + u, f0:f0 + nf, :] for u in range(ppc)], axis=1).astype(BF16)

    for p0 in range(0, pc, ppc):
        for d in range(DIFF_HEADS // 4):
            probs, alphas = [], []
            for hp in (2 * d, 2 * d + 1):
                s = _dot(qs_ref[hp], pages_t(bufk, hp * LANE, LANE, p0))
                m_new, l_new, alpha, pb = _softmax_part(s, m_ref[hp], l_ref[hp])
                m_ref[hp] = m_new
                l_ref[hp] = l_new
                probs.append(pb)
                alphas.append(jnp.concatenate([alpha, alpha], axis=1))
            pv = _dot_nt(jnp.concatenate(probs, axis=0), pages_t(bufv, d * 2 * LANE, 2 * LANE, p0))
            acc2_ref[d] = jnp.concatenate(alphas, axis=0) * acc2_ref[d] + pv

    @pl.when(c == nc - 1)
    def _fin():
        for hp in range(DIFF_HEADS // 2):
            j = hp % 2
            acc_ref[hp] = acc2_ref[hp // 2][j * r:(j + 1) * r, j * LANE:(j + 1) * LANE]
        new = new_ref[...]
        causal = (lax.broadcasted_iota(jnp.int32, (tq, LANE), 1) <= lax.broadcasted_iota(jnp.int32, (tq, LANE), 0))
        _diff_step(lambda hp: _new_tile(new, hp * LANE, tq), lambda hp: _new_tile(new, 512 + hp * LANE, tq),
                   jnp.where(causal, 0.0, NEG), qs_ref, m_ref, l_ref, acc_ref, tq)
        _diff_fin(lam_ref, g_ref, o_ref, l_ref, acc_ref, tq, lam_init)


def diff_paged(page_table, cache, layer, lam_par, q, new_rows, gnorm, lam_init):
    b, npg = page_table.shape
    tq = q.shape[1]
    pc = _tile(npg, 16)
    nc = npg // pc
    r = 4 * tq
    npair = DIFF_HEADS // 2
    return pl.pallas_call(
        functools.partial(_diff_paged_body, layer=layer, nb=b, nc=nc, pc=pc, tq=tq, lam_init=lam_init),
        name="diff_paged",
        grid_spec=pltpu.PrefetchScalarGridSpec(
            num_scalar_prefetch=1, grid=(b, nc),
            in_specs=[pl.BlockSpec(memory_space=pl.ANY),
                      pl.BlockSpec((4, DIFF_QK), lambda i, c, pt: (0, 0)),
                      pl.BlockSpec((None, tq, 512), lambda i, c, pt: (i, 0, 0)),
                      pl.BlockSpec((None, tq, 1024), lambda i, c, pt: (i, 0, 0)),
                      pl.BlockSpec((1, LANE), lambda i, c, pt: (0, 0))],
            out_specs=pl.BlockSpec((None, tq, 512), lambda i, c, pt: (i, 0, 0)),
            scratch_shapes=[pltpu.VMEM((2, pc, 512, PAGE), F32), pltpu.VMEM((2, pc, 512, PAGE), F32),
                            pltpu.SemaphoreType.DMA((2,)),
                            pltpu.VMEM((npair, r, LANE), BF16), pltpu.VMEM((npair, r, LANE), F32),
                            pltpu.VMEM((npair, r, LANE), F32), pltpu.VMEM((npair, r, LANE), F32),
                            pltpu.VMEM((npair // 2, 2 * r, 2 * LANE), F32)]),
        out_shape=jax.ShapeDtypeStruct((b, tq, 512), F32),
        compiler_params=_cp(("arbitrary", "arbitrary")),
    )(page_table, cache, lam_par, q, new_rows, gnorm)


def _q_perm():
    idx = np.zeros(512, np.int32)
    for j in range(NSA_J):
        for g in range(NSA_G):
            for d in range(DH):
                idx[j * LANE + g * DH + d] = (g * NSA_J + j) * DH + d
    return idx


def _gate3_perm():
    idx = np.zeros(3 * 512, np.int32)
    for c in range(3):
        for j in range(NSA_J):
            for g in range(NSA_G):
                idx[c * 512 + j * LANE + g * DH:c * 512 + j * LANE + (g + 1) * DH] = (g * NSA_J + j) * 3 + c
    return idx


def _cmp_to_sel(n_cmp, n_sel, nch, nsp):
    r = SEL_BLOCK // CMP_STRIDE
    k = np.arange(n_cmp)[:, None] - r * np.arange(n_sel)[None, :]
    m = sum(((k + n >= 0) & (k + n < r)).astype(np.float32) for n in range(CMP_BLOCK // CMP_STRIDE))
    out = np.zeros((nch, nsp), np.float32)
    out[:n_cmp, :n_sel] = m
    return out


def _block_expand(nsp, lk):
    return (np.arange(lk)[None, :] // SEL_BLOCK == np.arange(nsp)[:, None]).astype(np.float32)


def _cmp_weights(pe, w1, w2):
    eye = jnp.eye(2, dtype=F32)
    w6 = w1.reshape(2, NSA_G, 2, CMP_STRIDE, DH, DH)
    wc = jnp.zeros((CMP_STRIDE, 256, 512), BF16)
    for kv in range(2):
        for g in range(NSA_G):
            for half in range(2):
                r0, c0 = kv * LANE + g * DH, half * 256 + kv * LANE + g * DH
                wc = wc.at[:, r0:r0 + DH, c0:c0 + DH].set(w6[kv, g, half].astype(BF16))
    wc = wc.reshape(CMP_STRIDE * 256, 512)
    w2b = jnp.einsum("kghd,kK,gG->kghKGd", w2, eye, eye).reshape(256, 256).astype(BF16)
    pe_rows = jnp.transpose(pe, (0, 2, 1, 3)).reshape(4, CMP_BLOCK * DH)
    xb = jnp.einsum("rc,rR->rRc", pe_rows, jnp.eye(4, dtype=F32)).reshape(4, 4 * CMP_BLOCK * DH)
    xb = jnp.zeros((16, 4 * CMP_BLOCK * DH), F32).at[:4].set(xb)
    wb = jnp.zeros((4 * CMP_BLOCK * DH, LANE), F32).at[:, :DH].set(w1.reshape(4 * CMP_BLOCK * DH, DH))
    bias = mm(xb, wb)[:4, :DH].reshape(1, 256)
    return wc, bias, w2b


def _mixer(x, lw, lam_init, *, b, sq, sample=None):
    t = b * sq
    w_in = lw["w_in"]
    qperm = _q_perm()
    segs = ((512, DH ** -0.5 * LOG2E, None), (512, None, None), (256, None, None), (1536, None, "sigmoid"),
            (512, None, None), (1024, None, None), (512, DIFF_QK ** -0.5 * LOG2E, None), (1024, None, None))
    w_all = jnp.concatenate([w_in[:, OFF_Q:OFF_KV][:, qperm], w_in[:, OFF_KV:OFF_GATE],
                             w_in[:, OFF_GATE:OFF_POOL][:, _gate3_perm()], w_in[:, OFF_POOL:OFF_END]],
                            axis=1).astype(BF16)
    hn, q, rows4, winkv, g3, zpool, zgm, dq, dkv = in_proj(x, lw["norm_mix_g"], w_all, segs)
    q, rows4, winkv = q.reshape(b, sq, 512), rows4.reshape(b, sq, 512), winkv.reshape(b, sq, 256)
    zpool, zgm = zpool.reshape(b, sq, 512), zgm.reshape(b, sq, 1024)
    dq, dkv = dq.reshape(b, sq, 512), dkv.reshape(b, sq, 1024)

    wc, cbias, w2b = _cmp_weights(lw["cmp_pe"], lw["cmp_w1"], lw["cmp_w2"])
    lam_par = jnp.stack([lw["diff_lq1"], lw["diff_lk1"], lw["diff_lq2"], lw["diff_lk2"]])
    gnorm = jnp.tile(lw["diff_norm_g"], 2).reshape(1, LANE)

    if sample is None:
        past, l_tot = 0, sq
        nch = sq // CMP_STRIDE
        tq = _tile(sq, 256)
        tk_sel = _tile(sq, 2048)
        nk_sel = sq // tk_sel
        tq_w = _tile(sq, 256)
        tw = _tile(sq, 2048)
        win_all, kpos0_w, nk_w = winkv, 0, sq // tw
        win_kt = lambda i, j: (j, j >= 0)
        tq_d = _tile(sq, 256)
        tk_d = _tile(sq, 2048)
        nk_d = sq // tk_d
        prev16 = jnp.zeros((b, 16, 512), F32)
        pool_state = zpool[:, sq - POOL_MEM:]
        zgm_in = zgm
    else:
        layer = sample["layer"]
        pt = sample["page_table"]
        past = pt.shape[1] * PAGE
        l_tot = past + sq
        nch = past // CMP_STRIDE
        tq = sq
        wbuf = sample["win"].shape[1]
        wrows = -(-(wbuf + sq) // KEY_SUB) * KEY_SUB
        tw = wrows
        win_all = jnp.concatenate([sample["win"], winkv, jnp.zeros((b, wrows - wbuf - sq, 256), F32)], axis=1)
        kpos0_w, nk_w = past - wbuf, 1
        win_kt = lambda i, j: (j, j >= 0)
        tq_w = sq
        prev16 = jnp.concatenate([jnp.zeros((b, 1, 512), F32), sample["pool"]], axis=1)
        pool_state = jnp.concatenate([sample["pool"], zpool], axis=1)[:, -POOL_MEM:]
        zgm_in = jnp.concatenate([zgm, jnp.zeros((b, GM_CHUNK - sq, 1024), F32)], axis=1)

    n_cmp = l_tot // CMP_STRIDE - 1
    n_sel = -(-l_tot // SEL_BLOCK)
    nsp = -(-n_sel // LANE) * LANE
    msel = jnp.asarray(_cmp_to_sel(n_cmp, n_sel, nch, nsp), BF16)
    if sample is None:
        kcvc = compress(rows4, wc, cbias, w2b, nch)
        ocmp, selmask = cmp_attend(q, kcvc, msel, tq=tq, n_cmp=n_cmp, n_sel=n_sel, qpos0=0)
        emat = jnp.asarray(_block_expand(nsp, sq), BF16)

        def sel_kt(i, j):
            last = ((i + 1) * tq - 1) // tk_sel
            return jnp.minimum(j, last), j <= last

        osel = nsa_flash(q, rows4, 2, 3, mode="sel", tq=tq, tk=tk_sel, nk=nk_sel, qpos0=0, kpos0=0,
                         kt_fn=sel_kt, sel=selmask, emat=emat)
    else:
        kcvc = compress_paged(pt, sample["cache_nsa"], layer, wc, cbias, w2b)
        ocmp, selmask = cmp_attend(q, kcvc, msel, tq=tq, n_cmp=n_cmp, n_sel=n_sel, qpos0=past)
        emat = jnp.asarray(_block_expand(nsp, past + LANE), BF16)
        osel = sel_paged(pt, sample["cache_nsa"], layer, q, rows4, selmask, emat)
    owin = nsa_flash(q, win_all, 0, 1, mode="win", tq=tq_w, tk=tw, nk=nk_w, qpos0=past, kpos0=kpos0_w,
                     kt_fn=win_kt)

    opool = pool_mix(prev16, zpool, lw["pool_w"], lw["pool_scale"], past)
    ws_tril = jnp.tril(lw["gm_ws"]).astype(BF16)
    bs_exp = jnp.repeat(lw["gm_bs"].T, LANE, axis=1)
    ogm, gm_v = gmlp_mix(zgm_in, lw["gm_ng"], lw["gm_nb"], ws_tril, bs_exp)

    if sample is None:
        def diff_kt(i, j):
            last = ((i + 1) * tq_d - 1) // tk_d
            return jnp.minimum(j, last), j <= last

        odiff = diff_attend(lam_par, dq, dkv, gnorm, tq=tq_d, tk=tk_d, nk=nk_d, qpos0=0,
                            lam_init=lam_init, kt_fn=diff_kt)
    else:
        odiff = diff_paged(pt, sample["cache_diff"], layer, lam_par, dq, dkv, gnorm, lam_init)
        ogm = ogm[:, :sq]
        gm_v = gm_v[:, :sq]
    wbr = lw["w_branch"].at[0].set(lw["w_branch"][0][qperm]).astype(BF16)
    out = finish_mixer(x, hn, g3, ocmp.reshape(t, 512), osel.reshape(t, 512), owin.reshape(t, 512),
                       opool.reshape(t, 512), ogm.reshape(t, 512), odiff.reshape(t, 512),
                       lw["w_gate"].astype(BF16), lw["b_gate"].reshape(1, -1), wbr, lw["w_o"].astype(BF16))
    states = dict(rows4=rows4, winkv=winkv, win_all=win_all, pool_state=pool_state, gm_v=gm_v, dkv=dkv)
    return out, states


def kernel(x_prompt, x_sample, cache_nsa, cache_diff, state_nsa_win, state_pool, page_table, p_prompt, p_sample, norm_mix_g, w_in, nsa_cmp_pe, nsa_cmp_w1, nsa_cmp_w2, pool_w, pool_scale, gm_norm_g, gm_norm_b, gm_ws, gm_bs, diff_lq1, diff_lk1, diff_lq2, diff_lk2, diff_norm_g, w_branch, w_gate, b_gate, w_o, norm_ffn_g, ffn_w_gate, ffn_w_up, ffn_w_down, moe_router, moe_router_b, moe_w_gate, moe_w_up, moe_w_down, ple_norm_g, ple_w_gate, ple_w_proj, final_norm_g):
    bp, sp, d = x_prompt.shape
    bs, ss, _ = x_sample.shape
    depth = w_in.shape[0]
    n_phys = cache_nsa.shape[1]
    wbuf = state_nsa_win.shape[2]
    cache_nsa2 = jnp.transpose(cache_nsa, (0, 1, 3, 4, 5, 2)).reshape(depth, n_phys, 512, PAGE)
    cache_diff2 = jnp.transpose(cache_diff, (0, 1, 3, 4, 5, 2)).reshape(depth, n_phys, 1024, PAGE)
    xp = x_prompt.reshape(bp * sp, d)
    xs = x_sample.reshape(bs * ss, d)
    outs = {k: [] for k in ("nsa_p", "nsa_s", "win_p", "win_s", "pool_p", "pool_s", "gmv_s", "diff_p", "diff_s")}
    for l in range(depth):
        lw = dict(norm_mix_g=norm_mix_g[l], w_in=w_in[l], cmp_pe=nsa_cmp_pe[l], cmp_w1=nsa_cmp_w1[l],
                  cmp_w2=nsa_cmp_w2[l], pool_w=pool_w[l], pool_scale=pool_scale[l], gm_ng=gm_norm_g[l],
                  gm_nb=gm_norm_b[l], gm_ws=gm_ws[l], gm_bs=gm_bs[l], diff_lq1=diff_lq1[l], diff_lk1=diff_lk1[l],
                  diff_lq2=diff_lq2[l], diff_lk2=diff_lk2[l], diff_norm_g=diff_norm_g[l], w_branch=w_branch[l],
                  w_gate=w_gate[l], b_gate=b_gate[l], w_o=w_o[l])
        lam_init = 0.8 - 0.6 * math.exp(-0.3 * l)
        xp, st_p = _mixer(xp, lw, lam_init, b=bp, sq=sp)
        sample = dict(cache_nsa=cache_nsa2, cache_diff=cache_diff2, win=state_nsa_win[l].reshape(bs, wbuf, 256),
                      pool=state_pool[l], page_table=page_table, layer=l)
        xs, st_s = _mixer(xs, lw, lam_init, b=bs, sq=ss, sample=sample)
        outs["nsa_p"].append(st_p["rows4"].reshape(bp, sp, 4, NSA_G, DH))
        outs["diff_p"].append(st_p["dkv"].reshape(bp, sp, 2, DIFF_HEADS, 2 * DIFF_QK))
        outs["nsa_s"].append(st_s["rows4"].reshape(bs, ss, 4, NSA_G, DH))
        wkeep = min(WINDOW, sp)
        outs["win_p"].append(st_p["winkv"][:, sp - wkeep:].reshape(bp, wkeep, 2, NSA_G, DH))
        outs["win_s"].append(st_s["win_all"][:, ss:ss + wbuf].reshape(bs, wbuf, 2, NSA_G, DH))
        outs["pool_p"].append(st_p["pool_state"])
        outs["pool_s"].append(st_s["pool_state"])
        outs["gmv_s"].append(st_s["gm_v"])
        outs["diff_s"].append(st_s["dkv"].reshape(bs, ss, 2, DIFF_HEADS, 2 * DIFF_QK))
        i = l // 2
        if l % 2 == 0:
            xp = ffn_swiglu(xp, norm_ffn_g[l], ffn_w_gate[i], ffn_w_up[i], ffn_w_down[i])
            xs = ffn_swiglu(xs, norm_ffn_g[l], ffn_w_gate[i], ffn_w_up[i], ffn_w_down[i])
        else:
            xp = moe_swiglu(xp, norm_ffn_g[l], moe_router[i], moe_router_b[i], moe_w_gate[i], moe_w_up[i], moe_w_down[i])
            xs = moe_swiglu(xs, norm_ffn_g[l], moe_router[i], moe_router_b[i], moe_w_gate[i], moe_w_up[i], moe_w_down[i])
        final = l == depth - 1
        xp = ple(xp, p_prompt[l].reshape(bp * sp, -1), ple_norm_g[l], ple_w_gate[l], ple_w_proj[l], final_norm_g, final)
        xs = ple(xs, p_sample[l].reshape(bs * ss, -1), ple_norm_g[l], ple_w_gate[l], ple_w_proj[l], final_norm_g, final)
    st = lambda k: jnp.stack(outs[k])
    return (xp.reshape(bp, sp, d), xs.reshape(bs, ss, d), st("nsa_p"), st("nsa_s"), st("win_p"), st("win_s"),
            st("pool_p"), st("pool_s"), st("gmv_s"), st("diff_p"), st("diff_s"))
```

```python
import functools
import math

import numpy as np
import jax
import jax.numpy as jnp
from jax import lax
from jax.experimental import pallas as pl
from jax.experimental.pallas import tpu as pltpu

F32 = jnp.float32
BF16 = jnp.bfloat16

D_MODEL = 1024
PAGE = 128
NSA_G = 2
NSA_J = 4
DH = 64
CMP_BLOCK = 32
CMP_STRIDE = 16
SEL_BLOCK = 64
SEL_TOPK = 16
FORCE_BONUS = 100.0
WINDOW = 512
POOL_WINDOWS = (2, 4, 8, 16)
POOL_MEM = 15
GM_W = 512
GM_CHUNK = 128
DIFF_HEADS = 8
DIFF_QK = 32
N_BRANCH = 4
BRANCH_W = 512
N_EXPERTS = 8
EPS = 1e-6
NEG = -1e30
LOG2E = 1.4426950408889634
LANE = 128
VMEM_LIMIT = 56 * 1024 * 1024

OFF_Q, OFF_KV, OFF_GATE, OFF_POOL, OFF_GM, OFF_DQ, OFF_DKV, OFF_END = 0, 512, 1280, 1304, 1816, 2840, 3352, 4376


def _tile(n, pref):
    t = min(n, pref)
    while n % t:
        t //= 2
    return t


def _cp(sem, vmem=VMEM_LIMIT):
    return pltpu.CompilerParams(dimension_semantics=sem, vmem_limit_bytes=vmem)


def _gelu(x):
    return 0.5 * x * (1.0 + jnp.tanh(0.7978845608028654 * (x + 0.044715 * (x * x * x))))


def _rms(x, g):
    return x * lax.rsqrt(jnp.mean(x * x, axis=-1, keepdims=True) + EPS) * g


def _dot(a, b):
    return jnp.dot(a, b, preferred_element_type=F32)


def _dot_nt(a, b):
    return lax.dot_general(a, b, (((1,), (1,)), ((), ())), preferred_element_type=F32)


def _mm_body(x_ref, w_ref, o_ref):
    o_ref[...] = _dot(x_ref[...].astype(BF16), w_ref[...])


def mm(x, w):
    t, k = x.shape
    n = w.shape[1]
    tm = _tile(t, 512)
    tn = _tile(n, 512)
    return pl.pallas_call(
        _mm_body, name="mm", grid=(t // tm, n // tn),
        in_specs=[pl.BlockSpec((tm, k), lambda i, j: (i, 0)), pl.BlockSpec((k, tn), lambda i, j: (0, j))],
        out_specs=pl.BlockSpec((tm, tn), lambda i, j: (i, j)),
        out_shape=jax.ShapeDtypeStruct((t, n), F32),
        compiler_params=_cp(("parallel", "parallel")),
    )(x, w.astype(BF16))


def _in_proj_body(x_ref, g_ref, w_ref, hn_ref, *o_refs, segs):
    hn = _rms(x_ref[...], g_ref[...]).astype(BF16)
    hn_ref[...] = hn
    c0 = 0
    for (n, scale, act), o_ref in zip(segs, o_refs):
        y = _dot(hn, w_ref[:, c0:c0 + n])
        if scale is not None:
            y = y * scale
        if act == "sigmoid":
            y = jax.nn.sigmoid(y)
        o_ref[...] = y
        c0 += n


def in_proj(x, g, w_all, segs):
    t, d = x.shape
    tm = _tile(t, 256)
    ntot = w_all.shape[1]
    return pl.pallas_call(
        functools.partial(_in_proj_body, segs=segs), name="in_proj", grid=(t // tm,),
        in_specs=[pl.BlockSpec((tm, d), lambda i: (i, 0)), pl.BlockSpec((1, d), lambda i: (0, 0)),
                  pl.BlockSpec((d, ntot), lambda i: (0, 0), pipeline_mode=pl.Buffered(1))],
        out_specs=[pl.BlockSpec((tm, d), lambda i: (i, 0))] + [pl.BlockSpec((tm, n), lambda i: (i, 0)) for n, _, _ in segs],
        out_shape=[jax.ShapeDtypeStruct((t, d), BF16)] + [jax.ShapeDtypeStruct((t, n), F32) for n, _, _ in segs],
        compiler_params=_cp(("parallel",)),
    )(x, g.reshape(1, d), w_all)


def _compress_body(xk_ref, xv_ref, wc_ref, bias_ref, w2_ref, o_ref, *, nch):
    _compress_core(lambda s: xk_ref[pl.ds(s, nch, stride=CMP_STRIDE), :],
                   lambda s: xv_ref[pl.ds(s, nch, stride=CMP_STRIDE), :], wc_ref, bias_ref, w2_ref, o_ref, nch)


def _compress_core(load_k, load_v, wc_ref, bias_ref, w2_ref, o_ref, nch):
    xs = jnp.concatenate([part(s) for s in range(CMP_STRIDE) for part in (load_k, load_v)], axis=1).astype(BF16)
    acc = _dot(xs, wc_ref[...])
    hi_next = pltpu.roll(acc[:, 256:], nch - 1, 0)
    hid = _gelu(acc[:, :256] + hi_next + bias_ref[...])
    o_ref[...] = _dot(hid.astype(BF16), w2_ref[...]).astype(o_ref.dtype)


def compress(rows, wc, bias, w2, nch):
    b = rows.shape[0]
    return pl.pallas_call(
        functools.partial(_compress_body, nch=nch), name="compress", grid=(b,),
        in_specs=[pl.BlockSpec((None, nch * CMP_STRIDE, LANE), lambda i: (i, 0, 0)),
                  pl.BlockSpec((None, nch * CMP_STRIDE, LANE), lambda i: (i, 0, 1)),
                  pl.BlockSpec((CMP_STRIDE * 256, 512), lambda i: (0, 0)),
                  pl.BlockSpec((1, 256), lambda i: (0, 0)),
                  pl.BlockSpec((256, 256), lambda i: (0, 0))],
        out_specs=pl.BlockSpec((None, nch, 256), lambda i: (i, 0, 0)),
        out_shape=jax.ShapeDtypeStruct((b, nch, 256), BF16),
        compiler_params=_cp(("parallel",)),
    )(rows, rows, wc, bias, w2)


def _stack_heads(q, tq):
    lane = lax.broadcasted_iota(jnp.int32, (tq, LANE), 1)
    parts = []
    for g in range(NSA_G):
        keep = (lane < DH) if g == 0 else (lane >= DH)
        for j in range(NSA_J):
            parts.append(jnp.where(keep, q[:, j * LANE:(j + 1) * LANE], 0.0))
    return jnp.concatenate(parts, axis=0)


def _unstack_heads(o, tq):
    lane = lax.broadcasted_iota(jnp.int32, (tq, LANE), 1)
    outs = []
    for j in range(NSA_J):
        outs.append(jnp.where(lane < DH, o[j * tq:(j + 1) * tq], o[(NSA_J + j) * tq:(NSA_J + j + 1) * tq]))
    return jnp.concatenate(outs, axis=1)


def _cmp_body(q_ref, kv_ref, msel_ref, o_ref, sel_ref, *, tq, nch, n_cmp, n_sel, nsp, qpos0):
    i = pl.program_id(1)
    r = 8 * tq
    qs = _stack_heads(q_ref[...], tq).astype(BF16)
    kc = kv_ref[:, 0:LANE]
    vc = kv_ref[:, LANE:2 * LANE]
    s = _dot_nt(qs, kc)
    row = lax.broadcasted_iota(jnp.int32, (r, nch), 0)
    col = lax.broadcasted_iota(jnp.int32, (r, nch), 1)
    qpos = qpos0 + i * tq + (row & (tq - 1))
    cmask = (col * CMP_STRIDE + (CMP_BLOCK - 1) <= qpos) & (col < n_cmp)
    s = jnp.where(cmask, s, NEG)
    m = jnp.max(s, axis=1, keepdims=True)
    e = jnp.where(cmask, jnp.exp2(s - m), 0.0)
    l = jnp.sum(e, axis=1, keepdims=True)
    p = (e * (1.0 / jnp.where(l > 0.0, l, 1.0))).astype(BF16)
    o_ref[...] = _unstack_heads(_dot(p, vc), tq)
    imp_all = _dot(p, msel_ref[...])
    blocks_on_rows = tq % LANE == 0
    nrow = -(-n_sel // 8) * 8
    shape = (nrow, tq) if blocks_on_rows else (tq, nsp)
    baxis, qaxis = (0, 1) if blocks_on_rows else (1, 0)
    blk = lax.broadcasted_iota(jnp.int32, shape, baxis)
    qp = qpos0 + i * tq + lax.broadcasted_iota(jnp.int32, shape, qaxis)
    cur = lax.shift_right_logical(qp, SEL_BLOCK.bit_length() - 1)
    valid = (blk <= cur) & (blk < n_sel)
    forced = (blk == 0) | (blk == cur) | (blk == cur - 1)
    for g in range(NSA_G):
        imp = imp_all[(g * NSA_J) * tq:(g * NSA_J + 1) * tq]
        for j in range(1, NSA_J):
            imp = imp + imp_all[(g * NSA_J + j) * tq:(g * NSA_J + j + 1) * tq]
        if blocks_on_rows:
            imp = imp.T[:nrow]
        score = jnp.where(valid, imp + FORCE_BONUS * forced.astype(F32), -1.0)
        score = jnp.where(blk < n_sel, score, -2.0)
        rank = jnp.zeros(shape, F32)
        for mth in range(n_sel):
            cm = score[mth:mth + 1, :] if blocks_on_rows else score[:, mth:mth + 1]
            beats = (cm > score) | ((cm == score) & (blk > mth))
            rank = rank + beats.astype(F32)
        chosen = ((rank < float(min(SEL_TOPK, n_sel))) & valid).astype(F32)
        if blocks_on_rows:
            chosen = jnp.concatenate([chosen, jnp.zeros((nsp - nrow, tq), F32)], axis=0).T
        sel_ref[:, g * nsp:(g + 1) * nsp] = chosen


def cmp_attend(q, kcvc, msel, *, tq, n_cmp, n_sel, qpos0):
    b, sq, _ = q.shape
    nch = kcvc.shape[1]
    nsp = msel.shape[1]
    return pl.pallas_call(
        functools.partial(_cmp_body, tq=tq, nch=nch, n_cmp=n_cmp, n_sel=n_sel, nsp=nsp, qpos0=qpos0),
        name="cmp_attend",
        grid=(b, sq // tq),
        in_specs=[pl.BlockSpec((None, tq, 512), lambda bi, i: (bi, i, 0)),
                  pl.BlockSpec((None, nch, 256), lambda bi, i: (bi, 0, 0)),
                  pl.BlockSpec((nch, nsp), lambda bi, i: (0, 0))],
        out_specs=[pl.BlockSpec((None, tq, 512), lambda bi, i: (bi, i, 0)),
                   pl.BlockSpec((None, tq, 2 * nsp), lambda bi, i: (bi, i, 0))],
        out_shape=[jax.ShapeDtypeStruct((b, sq, 512), F32), jax.ShapeDtypeStruct((b, sq, 2 * nsp), F32)],
        compiler_params=_cp(("parallel", "parallel")),
    )(q, kcvc, msel)


ROW_BLOCK = 128
KEY_SUB = 256


def _flash_update(qs, k, v, bias, m, l, acc, kv_t=False):
    s = _dot(qs, k) if kv_t else _dot_nt(qs, k)
    if bias is not None:
        s = s + bias
    m_new, l_new, alpha, pb = _softmax_part(s, m, l)
    acc_new = alpha * acc + (_dot_nt(pb, v) if kv_t else _dot(pb, v))
    return m_new, l_new, acc_new


def _softmax_part(s, m, l):
    tiles = [s[:, c:c + LANE] for c in range(0, s.shape[1], LANE)]
    m_new = jnp.maximum(m, jnp.max(functools.reduce(jnp.maximum, tiles), axis=1, keepdims=True))
    alpha = jnp.exp2(m - m_new)
    ps = [jnp.exp2(t - m_new) for t in tiles]
    l_new = alpha * l + jnp.sum(functools.reduce(jnp.add, ps), axis=1, keepdims=True)
    return m_new, l_new, alpha, jnp.concatenate(ps, axis=1).astype(BF16)


def _flash_rows(qs_ref, k, v, bias_of, m_ref, l_ref, acc_ref, nrows, tq, kv_t=False):
    if tq < ROW_BLOCK:
        parts = [bias_of(h, 0, tq) for h in range(nrows // tq)]
        bias = None if parts[0] is None else jnp.concatenate(parts, axis=0)
        blocks = [(pl.ds(0, nrows), bias)]
    else:
        blocks = [(pl.ds(r0, ROW_BLOCK), bias_of(r0 // tq, r0 % tq, ROW_BLOCK)) for r0 in range(0, nrows, ROW_BLOCK)]
    for sl, bias in blocks:
        m, l, acc = _flash_update(qs_ref[sl], k, v, bias, m_ref[sl], l_ref[sl], acc_ref[sl], kv_t)
        m_ref[sl] = m
        l_ref[sl] = l
        acc_ref[sl] = acc


def _sel_ok(sel, e, nsp):
    return [_dot(sel[:, g * nsp:(g + 1) * nsp], e) > 0.5 for g in range(NSA_G)]


def _nsa_flash_body(*refs, mode, tq, tk, nk, qpos0, kpos0, kt_fn, nsp):
    if mode == "sel":
        q_ref, k_ref, v_ref, sel_ref, e_ref, o_ref, qs_ref, m_ref, l_ref, acc_ref = refs
    else:
        q_ref, k_ref, v_ref, o_ref, qs_ref, m_ref, l_ref, acc_ref = refs
    i = pl.program_id(1)
    j = pl.program_id(2)
    r = 8 * tq

    @pl.when(j == 0)
    def _init():
        qs_ref[...] = _stack_heads(q_ref[...], tq).astype(BF16)
        m_ref[...] = jnp.full((r, LANE), NEG, F32)
        l_ref[...] = jnp.zeros((r, LANE), F32)
        acc_ref[...] = jnp.zeros((r, LANE), F32)

    kt, valid = kt_fn(i, j)

    ks = min(tk, KEY_SUB)

    def step(c0, masked):
        k = k_ref[c0:c0 + ks, :].astype(BF16)
        v = v_ref[c0:c0 + ks, :].astype(BF16)
        if not masked:
            _flash_rows(qs_ref, k, v, lambda h, t0, n: None, m_ref, l_ref, acc_ref, r, tq)
            return
        qpos = qpos0 + i * tq + lax.broadcasted_iota(jnp.int32, (tq, ks), 0)
        kpos = kpos0 + kt * tk + c0 + lax.broadcasted_iota(jnp.int32, (tq, ks), 1)
        ok = kpos <= qpos
        if mode == "win":
            ok = ok & (kpos > qpos - WINDOW)
            bias = [jnp.where(ok, 0.0, NEG)] * NSA_G
        else:
            oks = _sel_ok(sel_ref[...].astype(BF16), e_ref[:, c0:c0 + ks], nsp)
            bias = [jnp.where(ok & okg, 0.0, NEG) for okg in oks]
        _flash_rows(qs_ref, k, v, lambda h, t0, n: bias[h // NSA_J][t0:t0 + n], m_ref, l_ref, acc_ref, r, tq)

    q_first = qpos0 + i * tq
    q_last = q_first + tq - 1
    for c0 in range(0, tk, ks):
        k_first = kpos0 + kt * tk + c0
        k_last = k_first + ks - 1
        live = valid & (k_first <= q_last)
        if mode == "sel":
            pl.when(live)(functools.partial(step, c0, True))
        else:
            live = live & (k_last > q_first - WINDOW)
            free = (k_last <= q_first) & (k_first > q_last - WINDOW)
            pl.when(live & jnp.logical_not(free))(functools.partial(step, c0, True))
            pl.when(live & free)(functools.partial(step, c0, False))

    @pl.when(j == nk - 1)
    def _fin():
        l = l_ref[...]
        o = acc_ref[...] * (1.0 / jnp.where(l > 0.0, l, 1.0))
        o_ref[...] = _unstack_heads(o, tq)


def nsa_flash(q, kv, kcol, vcol, *, mode, tq, tk, nk, qpos0, kpos0, kt_fn, sel=None, emat=None):
    b, sq, _ = q.shape
    nsp = 0 if sel is None else sel.shape[2] // 2

    def kmap(col):
        return lambda bi, i, j: (bi, kt_fn(i, j)[0], col)

    in_specs = [pl.BlockSpec((None, tq, 512), lambda bi, i, j: (bi, i, 0)),
                pl.BlockSpec((None, tk, LANE), kmap(kcol)),
                pl.BlockSpec((None, tk, LANE), kmap(vcol))]
    args = [q, kv, kv]
    if mode == "sel":
        in_specs += [pl.BlockSpec((None, tq, 2 * nsp), lambda bi, i, j: (bi, i, 0)),
                     pl.BlockSpec((nsp, tk), lambda bi, i, j: (0, kt_fn(i, j)[0]))]
        args += [sel, emat]
    r = 8 * tq
    return pl.pallas_call(
        functools.partial(_nsa_flash_body, mode=mode, tq=tq, tk=tk, nk=nk, qpos0=qpos0, kpos0=kpos0,
                          kt_fn=kt_fn, nsp=nsp),
        name="nsa_" + mode, grid=(b, sq // tq, nk),
        in_specs=in_specs,
        out_specs=pl.BlockSpec((None, tq, 512), lambda bi, i, j: (bi, i, 0)),
        out_shape=jax.ShapeDtypeStruct((b, sq, 512), F32),
        scratch_shapes=[pltpu.VMEM((r, LANE), BF16), pltpu.VMEM((r, LANE), F32), pltpu.VMEM((r, LANE), F32),
                        pltpu.VMEM((r, LANE), F32)],
        compiler_params=_cp(("parallel", "parallel", "arbitrary")),
    )(*args)


def _diff_init(q_ref, qs_ref, m_ref, l_ref, acc_ref, tq):
    r = 4 * tq
    npair = DIFF_HEADS // 2
    lane = lax.broadcasted_iota(jnp.int32, (tq, LANE), 1)
    for hp in range(npair):
        q = q_ref[:, hp * LANE:(hp + 1) * LANE]
        parts = []
        for h in range(2):
            for mth in range(2):
                lo = h * 2 * DIFF_QK + mth * DIFF_QK
                parts.append(jnp.where((lane >= lo) & (lane < lo + DIFF_QK), q, 0.0))
        qs_ref[hp] = jnp.concatenate(parts, axis=0).astype(BF16)
    m_ref[...] = jnp.full((npair, r, LANE), NEG, F32)
    l_ref[...] = jnp.zeros((npair, r, LANE), F32)
    acc_ref[...] = jnp.zeros((npair, r, LANE), F32)


def _diff_step(k_of, v_of, bias, qs_ref, m_ref, l_ref, acc_ref, tq, kv_t=False):
    bias_of = lambda h, t0, n: None if bias is None else bias[t0:t0 + n]
    for hp in range(DIFF_HEADS // 2):
        _flash_rows(qs_ref.at[hp], k_of(hp), v_of(hp), bias_of, m_ref.at[hp], l_ref.at[hp], acc_ref.at[hp],
                    4 * tq, tq, kv_t)


def _diff_fin(lam_ref, g_ref, o_ref, l_ref, acc_ref, tq, lam_init):
    lp = lam_ref[...]
    lam = (jnp.exp(jnp.sum(lp[0:1] * lp[1:2], axis=1, keepdims=True))
           - jnp.exp(jnp.sum(lp[2:3] * lp[3:4], axis=1, keepdims=True)) + lam_init)
    lane = lax.broadcasted_iota(jnp.int32, (tq, LANE), 1)
    lo_half = lane < 2 * DIFF_QK
    for hp in range(DIFF_HEADS // 2):
        l = l_ref[hp]
        a = acc_ref[hp] * (1.0 / jnp.where(l > 0.0, l, 1.0))
        o0 = a[0:tq] - lam * a[tq:2 * tq]
        o1 = a[2 * tq:3 * tq] - lam * a[3 * tq:4 * tq]
        o = jnp.where(lo_half, o0, o1)
        sq = o * o
        ms0 = jnp.sum(jnp.where(lo_half, sq, 0.0), axis=1, keepdims=True)
        ms1 = jnp.sum(jnp.where(lo_half, 0.0, sq), axis=1, keepdims=True)
        ms = jnp.where(lo_half, ms0, ms1) * (1.0 / (2 * DIFF_QK))
        y = o * lax.rsqrt(ms + EPS) * g_ref[...]
        o_ref[:, hp * LANE:(hp + 1) * LANE] = y * (1.0 - lam_init)


def _diff_body(lam_ref, q_ref, k_ref, v_ref, g_ref, o_ref, qs_ref, m_ref, l_ref, acc_ref, *,
               tq, tk, nk, qpos0, lam_init, kt_fn):
    i = pl.program_id(1)
    j = pl.program_id(2)

    @pl.when(j == 0)
    def _init():
        _diff_init(q_ref, qs_ref, m_ref, l_ref, acc_ref, tq)

    kt, valid = kt_fn(i, j)
    ks = min(tk, KEY_SUB)

    def step(c0, masked):
        k_of = lambda hp: k_ref[c0:c0 + ks, hp * LANE:(hp + 1) * LANE].astype(BF16)
        v_of = lambda hp: v_ref[c0:c0 + ks, hp * LANE:(hp + 1) * LANE].astype(BF16)
        bias = None
        if masked:
            qpos = qpos0 + i * tq + lax.broadcasted_iota(jnp.int32, (tq, ks), 0)
            kpos = kt * tk + c0 + lax.broadcasted_iota(jnp.int32, (tq, ks), 1)
            bias = jnp.where(kpos <= qpos, 0.0, NEG)
        _diff_step(k_of, v_of, bias, qs_ref, m_ref, l_ref, acc_ref, tq)

    for c0 in range(0, tk, ks):
        k_first = kt * tk + c0
        live = valid & (k_first <= qpos0 + (i + 1) * tq - 1)
        below = k_first + ks - 1 <= qpos0 + i * tq
        pl.when(live & below)(functools.partial(step, c0, False))
        pl.when(live & jnp.logical_not(below))(functools.partial(step, c0, True))

    @pl.when(j == nk - 1)
    def _fin():
        _diff_fin(lam_ref, g_ref, o_ref, l_ref, acc_ref, tq, lam_init)


def diff_attend(lam_par, q, kv, gnorm, *, tq, tk, nk, qpos0, lam_init, kt_fn):
    b, sq, _ = q.shape
    r = 4 * tq
    npair = DIFF_HEADS // 2
    return pl.pallas_call(
        functools.partial(_diff_body, tq=tq, tk=tk, nk=nk, qpos0=qpos0, lam_init=lam_init, kt_fn=kt_fn),
        name="diff_attend",
        grid=(b, sq // tq, nk),
        in_specs=[pl.BlockSpec((4, DIFF_QK), lambda bi, i, j: (0, 0)),
                  pl.BlockSpec((None, tq, 512), lambda bi, i, j: (bi, i, 0)),
                  pl.BlockSpec((None, tk, 512), lambda bi, i, j: (bi, kt_fn(i, j)[0], 0)),
                  pl.BlockSpec((None, tk, 512), lambda bi, i, j: (bi, kt_fn(i, j)[0], 1)),
                  pl.BlockSpec((1, LANE), lambda bi, i, j: (0, 0))],
        out_specs=pl.BlockSpec((None, tq, 512), lambda bi, i, j: (bi, i, 0)),
        out_shape=jax.ShapeDtypeStruct((b, sq, 512), F32),
        scratch_shapes=[pltpu.VMEM((npair, r, LANE), BF16), pltpu.VMEM((npair, r, LANE), F32),
                        pltpu.VMEM((npair, r, LANE), F32), pltpu.VMEM((npair, r, LANE), F32)],
        compiler_params=_cp(("parallel", "parallel", "arbitrary")),
    )(lam_par, q, kv, kv, gnorm)


def _pool_body(prev_ref, z_ref, pw_ref, sc_ref, o_ref, x_ref, *, sq, start):
    x_ref[pl.ds(0, 16), :] = prev_ref[...]
    x_ref[pl.ds(16, sq), :] = z_ref[...]
    pos = start + lax.broadcasted_iota(jnp.int32, (sq, LANE), 0)
    for g, w in enumerate(POOL_WINDOWS):
        c0 = g * LANE
        x = x_ref[pl.ds(16, sq), c0:c0 + LANE]
        tot = x
        for back in range(1, w):
            tot = tot + x_ref[pl.ds(16 - back, sq), c0:c0 + LANE]
        cnt = jnp.minimum(w, pos + 1).astype(F32)
        mix = tot / cnt - x
        y = _dot(mix.astype(BF16), pw_ref[g])
        o_ref[:, c0:c0 + LANE] = y * sc_ref[:, c0:c0 + LANE]


def pool_mix(prev16, z, pw, scale, start):
    b, sq, _ = z.shape
    return pl.pallas_call(
        functools.partial(_pool_body, sq=sq, start=start), name="pool_mix", grid=(b,),
        in_specs=[pl.BlockSpec((None, 16, 512), lambda i: (i, 0, 0)),
                  pl.BlockSpec((None, sq, 512), lambda i: (i, 0, 0)),
                  pl.BlockSpec((4, LANE, LANE), lambda i: (0, 0, 0)),
                  pl.BlockSpec((1, 512), lambda i: (0, 0))],
        out_specs=pl.BlockSpec((None, sq, 512), lambda i: (i, 0, 0)),
        out_shape=jax.ShapeDtypeStruct((b, sq, 512), F32),
        scratch_shapes=[pltpu.VMEM((sq + 16, 512), F32)],
        compiler_params=_cp(("parallel",)),
    )(prev16, z, pw.astype(BF16), scale.reshape(1, 512))


def _gmlp_body(z_ref, ng_ref, nb_ref, ws_ref, bs_ref, o_ref, v_ref, *, tg):
    z = _gelu(z_ref[...])
    u = z[:, :GM_W]
    vr = z[:, GM_W:]
    xc = vr - jnp.mean(vr, axis=-1, keepdims=True)
    v = xc * lax.rsqrt(jnp.mean(xc * xc, axis=-1, keepdims=True) + EPS) * ng_ref[...] + nb_ref[...]
    v_ref[...] = v
    vb = v.astype(BF16)
    for c in range(tg // GM_CHUNK):
        r0 = c * GM_CHUNK
        for g in range(4):
            c0 = g * LANE
            mixed = _dot(ws_ref[g], vb[r0:r0 + GM_CHUNK, c0:c0 + LANE]) + bs_ref[:, c0:c0 + LANE]
            o_ref[r0:r0 + GM_CHUNK, c0:c0 + LANE] = u[r0:r0 + GM_CHUNK, c0:c0 + LANE] * mixed


def gmlp_mix(z, ng, nb, ws_tril, bs_exp):
    b, s, _ = z.shape
    tg = _tile(s, 512)
    return pl.pallas_call(
        functools.partial(_gmlp_body, tg=tg), name="gmlp_mix", grid=(b, s // tg),
        in_specs=[pl.BlockSpec((None, tg, 1024), lambda bi, i: (bi, i, 0)),
                  pl.BlockSpec((1, 512), lambda bi, i: (0, 0)),
                  pl.BlockSpec((1, 512), lambda bi, i: (0, 0)),
                  pl.BlockSpec((4, GM_CHUNK, GM_CHUNK), lambda bi, i: (0, 0, 0)),
                  pl.BlockSpec((GM_CHUNK, 512), lambda bi, i: (0, 0))],
        out_specs=[pl.BlockSpec((None, tg, 512), lambda bi, i: (bi, i, 0)),
                   pl.BlockSpec((None, tg, 512), lambda bi, i: (bi, i, 0))],
        out_shape=[jax.ShapeDtypeStruct((b, s, 512), F32), jax.ShapeDtypeStruct((b, s, 512), F32)],
        compiler_params=_cp(("parallel", "parallel")),
    )(z, ng.reshape(1, 512), nb.reshape(1, 512), ws_tril, bs_exp)


def _finish_body(x_ref, hn_ref, g3_ref, ocmp_ref, osel_ref, owin_ref, opool_ref, ogm_ref, odiff_ref,
                 wgate_ref, bgate_ref, wbr_ref, wo_ref, o_ref):
    hn = hn_ref[...]
    g3 = g3_ref[...]
    onsa = g3[:, 0:512] * ocmp_ref[...] + g3[:, 512:1024] * osel_ref[...] + g3[:, 1024:1536] * owin_ref[...]
    branches = (onsa, opool_ref[...], ogm_ref[...], odiff_ref[...])
    acc = jnp.zeros(o_ref.shape, F32)
    for n in range(N_BRANCH):
        c0 = n * D_MODEL
        gate = jax.nn.sigmoid(_dot(hn, wgate_ref[:, c0:c0 + D_MODEL]) + bgate_ref[:, c0:c0 + D_MODEL])
        acc = acc + gate * _dot(branches[n].astype(BF16), wbr_ref[n])
    o_ref[...] = x_ref[...] + _dot(acc.astype(BF16), wo_ref[...])


def finish_mixer(x, hn, g3, ocmp, osel, owin, opool, ogm, odiff, wgate, bgate, wbr, wo):
    t = x.shape[0]
    tm = _tile(t, 256)
    row = lambda w: pl.BlockSpec((tm, w), lambda i: (i, 0))
    const = lambda shape: pl.BlockSpec(shape, lambda i: (0,) * len(shape), pipeline_mode=pl.Buffered(1))
    return pl.pallas_call(
        _finish_body, name="finish_mixer", grid=(t // tm,),
        in_specs=[row(1024), row(1024), row(1536), row(512), row(512), row(512), row(512), row(512), row(512),
                  const((D_MODEL, N_BRANCH * D_MODEL)), const((1, N_BRANCH * D_MODEL)),
                  const((N_BRANCH, BRANCH_W, D_MODEL)), const((D_MODEL, D_MODEL))],
        out_specs=row(1024),
        out_shape=jax.ShapeDtypeStruct((t, D_MODEL), F32),
        compiler_params=_cp(("parallel",)),
    )(x, hn, g3, ocmp, osel, owin, opool, ogm, odiff, wgate, bgate, wbr, wo)


def _ffn_body(x_ref, g_ref, wg_ref, wu_ref, wd_ref, o_ref, hn_ref, acc_ref, *, nf):
    j = pl.program_id(1)

    @pl.when(j == 0)
    def _init():
        hn_ref[...] = _rms(x_ref[...], g_ref[...]).astype(BF16)
        acc_ref[...] = jnp.zeros(acc_ref.shape, F32)

    h = hn_ref[...]
    a = _dot(h, wg_ref[...])
    act = a * jax.nn.sigmoid(a) * _dot(h, wu_ref[...])
    acc_ref[...] += _dot(act.astype(BF16), wd_ref[...])

    @pl.when(j == nf - 1)
    def _fin():
        o_ref[...] = x_ref[...] + acc_ref[...]


def ffn_swiglu(x, g, wg, wu, wd):
    t, d = x.shape
    f = wg.shape[1]
    tm = _tile(t, 512)
    tf = 1408 if f % 1408 == 0 else _tile(f, 512)
    nf = f // tf
    return pl.pallas_call(
        functools.partial(_ffn_body, nf=nf), name="ffn_swiglu", grid=(t // tm, nf),
        in_specs=[pl.BlockSpec((tm, d), lambda i, j: (i, 0)), pl.BlockSpec((1, d), lambda i, j: (0, 0)),
                  pl.BlockSpec((d, tf), lambda i, j: (0, j)), pl.BlockSpec((d, tf), lambda i, j: (0, j)),
                  pl.BlockSpec((tf, d), lambda i, j: (j, 0))],
        out_specs=pl.BlockSpec((tm, d), lambda i, j: (i, 0)),
        out_shape=jax.ShapeDtypeStruct((t, d), F32),
        scratch_shapes=[pltpu.VMEM((tm, d), BF16), pltpu.VMEM((tm, d), F32)],
        compiler_params=_cp(("parallel", "arbitrary")),
    )(x, g.reshape(1, d), wg.astype(BF16), wu.astype(BF16), wd.astype(BF16))


MOE_SUB = 256


def _moe_body(x_ref, g_ref, r_ref, rb_ref, u_ref, wg_ref, wu_ref, wd_ref, o_ref,
              hn_ref, comb_ref, post_ref, posr_ref, cnt_ref, xe_ref, ye_ref, acc_ref, *, nf):
    e = pl.program_id(1)
    f = pl.program_id(2)
    tm = x_ref.shape[0]
    lane = lax.broadcasted_iota(jnp.int32, (tm, LANE), 1)

    @pl.when((e == 0) & (f == 0))
    def _init():
        hn = _rms(x_ref[...], g_ref[...]).astype(BF16)
        hn_ref[...] = hn
        lg = _dot(hn, r_ref[...]) + rb_ref[...]
        m1 = jnp.max(lg, axis=1, keepdims=True)
        i1 = jnp.min(jnp.where(lg == m1, lane, LANE), axis=1, keepdims=True)
        lg2 = jnp.where(lane == i1, -3e38, lg)
        m2 = jnp.max(lg2, axis=1, keepdims=True)
        i2 = jnp.min(jnp.where(lg2 == m2, lane, LANE), axis=1, keepdims=True)
        e2 = jnp.exp(m2 - m1)
        w1 = 1.0 / (1.0 + e2)
        comb = jnp.where(lane == i1, w1, 0.0) + jnp.where(lane == i2, e2 * w1, 0.0)
        comb_ref[...] = comb
        ind_t = (comb.T > 0.0).astype(F32)
        pos_t = _dot(ind_t.astype(BF16), u_ref[...])
        post_ref[...] = jnp.where(ind_t > 0.0, pos_t, -1.0)
        posr_ref[...] = post_ref[...].T
        cnt = jnp.sum(ind_t, axis=1, keepdims=True)
        for ex in range(N_EXPERTS):
            cnt_ref[ex] = cnt[ex, 0].astype(jnp.int32)
        acc_ref[...] = jnp.zeros(acc_ref.shape, F32)

    cnt = cnt_ref[e]
    rem = cnt % MOE_SUB
    has_tail = (rem > 0) & (rem <= MOE_SUB // 2)
    n_full = cnt // MOE_SUB + (rem > MOE_SUB // 2).astype(jnp.int32)

    def each_subtile(fn):
        lax.fori_loop(0, n_full, lambda u, c: (fn(u, MOE_SUB), c)[1], 0)
        pl.when(has_tail)(lambda: fn(n_full, MOE_SUB // 2))

    @pl.when(f == 0)
    def _gather():
        slot_t = post_ref[pl.ds(e, 1), :]

        def body(u, rows):
            base = lax.broadcasted_iota(jnp.int32, (rows, tm), 0).astype(F32) + (u * MOE_SUB).astype(F32)
            onehot = jnp.where(slot_t == base, 1.0, 0.0).astype(BF16)
            xe_ref[u, pl.ds(0, rows)] = _dot(onehot, hn_ref[...]).astype(BF16)
            ye_ref[u, pl.ds(0, rows)] = jnp.zeros((rows, x_ref.shape[1]), F32)

        each_subtile(body)

    def expert(u, rows):
        xe = xe_ref[u, pl.ds(0, rows)]
        a = _dot(xe, wg_ref[...])
        act = a * jax.nn.sigmoid(a) * _dot(xe, wu_ref[...])
        ye_ref[u, pl.ds(0, rows)] += _dot(act.astype(BF16), wd_ref[...])

    each_subtile(expert)

    @pl.when(f == nf - 1)
    def _scatter():
        slot_r = jnp.sum(jnp.where(lane == e, posr_ref[...], 0.0), axis=1, keepdims=True)
        w_r = jnp.sum(jnp.where(lane == e, comb_ref[...], 0.0), axis=1, keepdims=True)

        def body(u, rows):
            base = lax.broadcasted_iota(jnp.int32, (tm, rows), 1).astype(F32) + (u * MOE_SUB).astype(F32)
            onehot = jnp.where(slot_r == base, 1.0, 0.0).astype(BF16)
            y = ye_ref[u, pl.ds(0, rows)]
            y_hi = y.astype(BF16)
            y_lo = (y - y_hi.astype(F32)).astype(BF16)
            acc_ref[...] += w_r * (_dot(onehot, y_hi) + _dot(onehot, y_lo))

        each_subtile(body)

    @pl.when((e == N_EXPERTS - 1) & (f == nf - 1))
    def _fin():
        o_ref[...] = x_ref[...] + acc_ref[...]


def moe_swiglu(x, g, router, router_b, wg, wu, wd):
    t, d = x.shape
    f = wg.shape[2]
    tm = _tile(t, 1024)
    tf = 896 if f % 896 == 0 else _tile(f, 512)
    nf = f // tf
    nsubmax = -(-tm // MOE_SUB)
    rpad = jnp.zeros((d, LANE), F32).at[:, :N_EXPERTS].set(router).astype(BF16)
    rbpad = jnp.full((1, LANE), NEG, F32).at[0, :N_EXPERTS].set(router_b)
    before = jnp.asarray(np.triu(np.ones((tm, tm), np.float32), 1), BF16)
    const = lambda shape: pl.BlockSpec(shape, lambda i, e, j: (0,) * len(shape))
    return pl.pallas_call(
        functools.partial(_moe_body, nf=nf), name="moe_swiglu", grid=(t // tm, N_EXPERTS, nf),
        in_specs=[pl.BlockSpec((tm, d), lambda i, e, j: (i, 0)), const((1, d)), const((d, LANE)), const((1, LANE)),
                  const((tm, tm)),
                  pl.BlockSpec((None, d, tf), lambda i, e, j: (e, 0, j)),
                  pl.BlockSpec((None, d, tf), lambda i, e, j: (e, 0, j)),
                  pl.BlockSpec((None, tf, d), lambda i, e, j: (e, j, 0))],
        out_specs=pl.BlockSpec((tm, d), lambda i, e, j: (i, 0)),
        out_shape=jax.ShapeDtypeStruct((t, d), F32),
        scratch_shapes=[pltpu.VMEM((tm, d), BF16), pltpu.VMEM((tm, LANE), F32), pltpu.VMEM((LANE, tm), F32),
                        pltpu.VMEM((tm, LANE), F32), pltpu.SMEM((N_EXPERTS,), jnp.int32),
                        pltpu.VMEM((nsubmax, MOE_SUB, d), BF16), pltpu.VMEM((nsubmax, MOE_SUB, d), F32),
                        pltpu.VMEM((tm, d), F32)],
        compiler_params=_cp(("parallel", "arbitrary", "arbitrary")),
    )(x, g.reshape(1, d), rpad, rbpad, before, wg.astype(BF16), wu.astype(BF16), wd.astype(BF16))


def _ple_body(x_ref, p_ref, g_ref, wg_ref, wp_ref, fg_ref, o_ref, *, final):
    x = x_ref[...]
    hn = _rms(x, g_ref[...]).astype(BF16)
    gate = jax.nn.sigmoid(_dot(hn, wg_ref[...]))
    y = x + gate * _dot(p_ref[...].astype(BF16), wp_ref[...])
    if final:
        y = _rms(y, fg_ref[...])
    o_ref[...] = y


def ple(x, p, g, wg, wp, fg, final):
    t, d = x.shape
    pd = p.shape[1]
    tm = _tile(t, 512)
    return pl.pallas_call(
        functools.partial(_ple_body, final=final), name="ple", grid=(t // tm,),
        in_specs=[pl.BlockSpec((tm, d), lambda i: (i, 0)), pl.BlockSpec((tm, pd), lambda i: (i, 0)),
                  pl.BlockSpec((1, d), lambda i: (0, 0)), pl.BlockSpec((d, d), lambda i: (0, 0)),
                  pl.BlockSpec((pd, d), lambda i: (0, 0)), pl.BlockSpec((1, d), lambda i: (0, 0))],
        out_specs=pl.BlockSpec((tm, d), lambda i: (i, 0)),
        out_shape=jax.ShapeDtypeStruct((t, d), F32),
        compiler_params=_cp(("parallel",)),
    )(x, p, g.reshape(1, d), wg.astype(BF16), wp.astype(BF16), fg.reshape(1, d))


def _page_copies(pt_ref, cache_ref, bufs, sem, slot, b, c, *, layer, pc, cols):
    out = []
    for p in range(pc):
        pg = pt_ref[b, c * pc + p]
        for buf, (f0, w) in zip(bufs, cols):
            out.append(pltpu.make_async_copy(cache_ref.at[layer, pg, pl.ds(f0, w), :], buf.at[slot, p],
                                             sem.at[slot]))
    return out


def _paged_pipeline(pt_ref, cache_ref, bufs, sem, *, nb, nc, **kw):
    b = pl.program_id(0)
    c = pl.program_id(1) if nc > 1 else 0
    step = b * nc + c
    slot = step % 2

    @pl.when(step == 0)
    def _first():
        for d in _page_copies(pt_ref, cache_ref, bufs, sem, 0, 0, 0, **kw):
            d.start()

    @pl.when(step + 1 < nb * nc)
    def _prefetch():
        nxt = step + 1
        for d in _page_copies(pt_ref, cache_ref, bufs, sem, 1 - slot, nxt // nc, nxt % nc, **kw):
            d.start()

    for d in _page_copies(pt_ref, cache_ref, bufs, sem, slot, b, c, **kw):
        d.wait()
    return slot


def _compress_paged_body(pt_ref, cache_ref, wc_ref, bias_ref, w2_ref, o_ref, bufk, bufv, sem, rowk, rowv, *,
                         layer, npg, nb, nch):
    slot = _paged_pipeline(pt_ref, cache_ref, (bufk, bufv), sem, nb=nb, nc=1, layer=layer, pc=npg,
                           cols=((0, LANE), (LANE, LANE)))

    def to_rows(p, carry):
        rows = pl.ds(pl.multiple_of(p * PAGE, PAGE), PAGE)
        rowk[rows, :] = bufk[slot, p].T
        rowv[rows, :] = bufv[slot, p].T
        return carry

    lax.fori_loop(0, npg, to_rows, 0, unroll=4)
    _compress_core(lambda s: rowk[pl.ds(s, nch, stride=CMP_STRIDE), :],
                   lambda s: rowv[pl.ds(s, nch, stride=CMP_STRIDE), :], wc_ref, bias_ref, w2_ref, o_ref, nch)


def compress_paged(page_table, cache, layer, wc, bias, w2):
    b, npg = page_table.shape
    past = npg * PAGE
    nch = past // CMP_STRIDE
    const = lambda shape: pl.BlockSpec(shape, lambda i, pt: (0,) * len(shape))
    return pl.pallas_call(
        functools.partial(_compress_paged_body, layer=layer, npg=npg, nb=b, nch=nch), name="compress_paged",
        grid_spec=pltpu.PrefetchScalarGridSpec(
            num_scalar_prefetch=1, grid=(b,),
            in_specs=[pl.BlockSpec(memory_space=pl.ANY), const((CMP_STRIDE * 256, 512)), const((1, 256)),
                      const((256, 256))],
            out_specs=pl.BlockSpec((None, nch, 256), lambda i, pt: (i, 0, 0)),
            scratch_shapes=[pltpu.VMEM((2, npg, LANE, PAGE), F32), pltpu.VMEM((2, npg, LANE, PAGE), F32),
                            pltpu.SemaphoreType.DMA((2,)),
                            pltpu.VMEM((past, LANE), F32), pltpu.VMEM((past, LANE), F32)]),
        out_shape=jax.ShapeDtypeStruct((b, nch, 256), BF16),
        compiler_params=_cp(("arbitrary",)),
    )(page_table, cache, wc, bias, w2)


def _new_tile(new, c0, tq):
    return jnp.concatenate([new[:, c0:c0 + LANE], jnp.zeros((LANE - tq, LANE), F32)], axis=0).astype(BF16)


def _sel_paged_body(pt_ref, cache_ref, q_ref, new_ref, sel_ref, e_ref, o_ref, bufk, bufv, sem, *,
                    layer, npg, nb, tq, tk, nsp):
    slot = _paged_pipeline(pt_ref, cache_ref, (bufk, bufv), sem, nb=nb, nc=1, layer=layer, pc=npg,
                           cols=((2 * LANE, LANE), (3 * LANE, LANE)))
    past = npg * PAGE
    ppc = tk // PAGE
    r = 8 * tq
    qs = _stack_heads(q_ref[...], tq).astype(BF16)
    sel = sel_ref[...].astype(BF16)
    m = jnp.full((r, LANE), NEG, F32)
    l = jnp.zeros((r, LANE), F32)
    acc = jnp.zeros((r, LANE), F32)

    def stacked_bias(oks):
        return jnp.concatenate([jnp.where(ok, 0.0, NEG) for ok in oks for _ in range(NSA_J)], axis=0)

    for c in range(npg // ppc):
        kt = jnp.concatenate([bufk[slot, c * ppc + u] for u in range(ppc)], axis=1).astype(BF16)
        vt = jnp.concatenate([bufv[slot, c * ppc + u] for u in range(ppc)], axis=1).astype(BF16)
        bias = stacked_bias(_sel_ok(sel, e_ref[:, c * tk:(c + 1) * tk], nsp))
        m, l, acc = _flash_update(qs, kt, vt, bias, m, l, acc, kv_t=True)
    new = new_ref[...]
    causal = (lax.broadcasted_iota(jnp.int32, (tq, LANE), 1) <= lax.broadcasted_iota(jnp.int32, (tq, LANE), 0))
    bias = stacked_bias([ok & causal for ok in _sel_ok(sel, e_ref[:, past:past + LANE], nsp)])
    m, l, acc = _flash_update(qs, _new_tile(new, 2 * LANE, tq), _new_tile(new, 3 * LANE, tq), bias, m, l, acc)
    o_ref[...] = _unstack_heads(acc * (1.0 / jnp.where(l > 0.0, l, 1.0)), tq)


def sel_paged(page_table, cache, layer, q, new_rows, sel, emat):
    b, npg = page_table.shape
    past = npg * PAGE
    tq = q.shape[1]
    nsp = sel.shape[2] // 2
    return pl.pallas_call(
        functools.partial(_sel_paged_body, layer=layer, npg=npg, nb=b, tq=tq, tk=_tile(past, 512), nsp=nsp),
        name="sel_paged",
        grid_spec=pltpu.PrefetchScalarGridSpec(
            num_scalar_prefetch=1, grid=(b,),
            in_specs=[pl.BlockSpec(memory_space=pl.ANY),
                      pl.BlockSpec((None, tq, 512), lambda i, pt: (i, 0, 0)),
                      pl.BlockSpec((None, tq, 512), lambda i, pt: (i, 0, 0)),
                      pl.BlockSpec((None, tq, 2 * nsp), lambda i, pt: (i, 0, 0)),
                      pl.BlockSpec((nsp, past + LANE), lambda i, pt: (0, 0))],
            out_specs=pl.BlockSpec((None, tq, 512), lambda i, pt: (i, 0, 0)),
            scratch_shapes=[pltpu.VMEM((2, npg, LANE, PAGE), F32), pltpu.VMEM((2, npg, LANE, PAGE), F32),
                            pltpu.SemaphoreType.DMA((2,))]),
        out_shape=jax.ShapeDtypeStruct((b, tq, 512), F32),
        compiler_params=_cp(("arbitrary",)),
    )(page_table, cache, q, new_rows, sel, emat)


def _diff_paged_body(pt_ref, cache_ref, lam_ref, q_ref, new_ref, g_ref, o_ref, bufk, bufv, sem,
                     qs_ref, m_ref, l_ref, acc_ref, acc2_ref, *, layer, nb, nc, pc, tq, lam_init):
    c = pl.program_id(1)
    slot = _paged_pipeline(pt_ref, cache_ref, (bufk, bufv), sem, nb=nb, nc=nc, layer=layer, pc=pc,
                           cols=((0, 512), (512, 512)))
    ppc = _tile(pc, 4)
    r = 4 * tq

    @pl.when(c == 0)
    def _init():
        _diff_init(q_ref, qs_ref, m_ref, l_ref, acc_ref, tq)
        acc2_ref[...] = jnp.zeros(acc2_ref.shape, F32)

    def pages_t(buf, f0, nf, p0):
        return jnp.concatenate([buf[slot, p0 + u, f0:f0 + nf, :] for u in range(ppc)], axis=1).astype(BF16)

    for p0 in range(0, pc, ppc):
        for d in range(DIFF_HEADS // 4):
            probs, alphas = [], []
            for hp in (2 * d, 2 * d + 1):
                s = _dot(qs_ref[hp], pages_t(bufk, hp * LANE, LANE, p0))
                m_new, l_new, alpha, pb = _softmax_part(s, m_ref[hp], l_ref[hp])
                m_ref[hp] = m_new
                l_ref[hp] = l_new
                probs.append(pb)
                alphas.append(jnp.concatenate([alpha, alpha], axis=1))
            pv = _dot_nt(jnp.concatenate(probs, axis=0), pages_t(bufv, d * 2 * LANE, 2 * LANE, p0))
            acc2_ref[d] = jnp.concatenate(alphas, axis=0) * acc2_ref[d] + pv

    @pl.when(c == nc - 1)
    def _fin():
        for hp in range(DIFF_HEADS // 2):
            j = hp % 2
            acc_ref[hp] = acc2_ref[hp // 2][j * r:(j + 1) * r, j * LANE:(j + 1) * LANE]
        new = new_ref[...]
        causal = (lax.broadcasted_iota(jnp.int32, (tq, LANE), 1) <= lax.broadcasted_iota(jnp.int32, (tq, LANE), 0))
        _diff_step(lambda hp: _new_tile(new, hp * LANE, tq), lambda hp: _new_tile(new, 512 + hp * LANE, tq),
                   jnp.where(causal, 0.0, NEG), qs_ref, m_ref, l_ref, acc_ref, tq)
        _diff_fin(lam_ref, g_ref, o_ref, l_ref, acc_ref, tq, lam_init)


def diff_paged(page_table, cache, layer, lam_par, q, new_rows, gnorm, lam_init):
    b, npg = page_table.shape
    tq = q.shape[1]
    pc = _tile(npg, 16)
    nc = npg // pc
    r = 4 * tq
    npair = DIFF_HEADS // 2
    return pl.pallas_call(
        functools.partial(_diff_paged_body, layer=layer, nb=b, nc=nc, pc=pc, tq=tq, lam_init=lam_init),
        name="diff_paged",
        grid_spec=pltpu.PrefetchScalarGridSpec(
            num_scalar_prefetch=1, grid=(b, nc),
            in_specs=[pl.BlockSpec(memory_space=pl.ANY),
                      pl.BlockSpec((4, DIFF_QK), lambda i, c, pt: (0, 0)),
                      pl.BlockSpec((None, tq, 512), lambda i, c, pt: (i, 0, 0)),
                      pl.BlockSpec((None, tq, 1024), lambda i, c, pt: (i, 0, 0)),
                      pl.BlockSpec((1, LANE), lambda i, c, pt: (0, 0))],
            out_specs=pl.BlockSpec((None, tq, 512), lambda i, c, pt: (i, 0, 0)),
            scratch_shapes=[pltpu.VMEM((2, pc, 512, PAGE), F32), pltpu.VMEM((2, pc, 512, PAGE), F32),
                            pltpu.SemaphoreType.DMA((2,)),
                            pltpu.VMEM((npair, r, LANE), BF16), pltpu.VMEM((npair, r, LANE), F32),
                            pltpu.VMEM((npair, r, LANE), F32), pltpu.VMEM((npair, r, LANE), F32),
                            pltpu.VMEM((npair // 2, 2 * r, 2 * LANE), F32)]),
        out_shape=jax.ShapeDtypeStruct((b, tq, 512), F32),
        compiler_params=_cp(("arbitrary", "arbitrary")),
    )(page_table, cache, lam_par, q, new_rows, gnorm)


def _stack_body(*refs):
    for k, r in enumerate(refs[:-1]):
        refs[-1][k] = r[...].T


def stack_rows(arrs, b, s):
    n = arrs[0].shape[1]
    ts = _tile(s, 512)
    return pl.pallas_call(
        _stack_body, name="stack_rows", grid=(b, s // ts),
        in_specs=[pl.BlockSpec((ts, n), lambda bi, i: (bi * (s // ts) + i, 0))] * len(arrs),
        out_specs=pl.BlockSpec((len(arrs), None, n, ts), lambda bi, i: (0, bi, 0, i)),
        out_shape=jax.ShapeDtypeStruct((len(arrs), b, n, s), arrs[0].dtype),
        compiler_params=_cp(("parallel", "parallel")),
    )(*arrs)


def _q_perm():
    idx = np.zeros(512, np.int32)
    for j in range(NSA_J):
        for g in range(NSA_G):
            for d in range(DH):
                idx[j * LANE + g * DH + d] = (g * NSA_J + j) * DH + d
    return idx


def _gate3_perm():
    idx = np.zeros(3 * 512, np.int32)
    for c in range(3):
        for j in range(NSA_J):
            for g in range(NSA_G):
                idx[c * 512 + j * LANE + g * DH:c * 512 + j * LANE + (g + 1) * DH] = (g * NSA_J + j) * 3 + c
    return idx


def _cmp_to_sel(n_cmp, n_sel, nch, nsp):
    r = SEL_BLOCK // CMP_STRIDE
    k = np.arange(n_cmp)[:, None] - r * np.arange(n_sel)[None, :]
    m = sum(((k + n >= 0) & (k + n < r)).astype(np.float32) for n in range(CMP_BLOCK // CMP_STRIDE))
    out = np.zeros((nch, nsp), np.float32)
    out[:n_cmp, :n_sel] = m
    return out


def _block_expand(nsp, lk):
    return (np.arange(lk)[None, :] // SEL_BLOCK == np.arange(nsp)[:, None]).astype(np.float32)


def _cmp_weights(pe, w1, w2):
    eye = jnp.eye(2, dtype=F32)
    w6 = w1.reshape(2, NSA_G, 2, CMP_STRIDE, DH, DH)
    wc = jnp.zeros((CMP_STRIDE, 256, 512), BF16)
    for kv in range(2):
        for g in range(NSA_G):
            for half in range(2):
                r0, c0 = kv * LANE + g * DH, half * 256 + kv * LANE + g * DH
                wc = wc.at[:, r0:r0 + DH, c0:c0 + DH].set(w6[kv, g, half].astype(BF16))
    wc = wc.reshape(CMP_STRIDE * 256, 512)
    w2b = jnp.einsum("kghd,kK,gG->kghKGd", w2, eye, eye).reshape(256, 256).astype(BF16)
    pe_rows = jnp.transpose(pe, (0, 2, 1, 3)).reshape(4, CMP_BLOCK * DH)
    xb = jnp.einsum("rc,rR->rRc", pe_rows, jnp.eye(4, dtype=F32)).reshape(4, 4 * CMP_BLOCK * DH)
    xb = jnp.zeros((16, 4 * CMP_BLOCK * DH), F32).at[:4].set(xb)
    wb = jnp.zeros((4 * CMP_BLOCK * DH, LANE), F32).at[:, :DH].set(w1.reshape(4 * CMP_BLOCK * DH, DH))
    bias = mm(xb, wb)[:4, :DH].reshape(1, 256)
    return wc, bias, w2b


def _mixer(x, lw, lam_init, *, b, sq, sample=None):
    t = b * sq
    w_in = lw["w_in"]
    qperm = _q_perm()
    segs = ((512, DH ** -0.5 * LOG2E, None), (512, None, None), (256, None, None), (1536, None, "sigmoid"),
            (512, None, None), (1024, None, None), (512, DIFF_QK ** -0.5 * LOG2E, None), (1024, None, None))
    w_all = jnp.concatenate([w_in[:, OFF_Q:OFF_KV][:, qperm], w_in[:, OFF_KV:OFF_GATE],
                             w_in[:, OFF_GATE:OFF_POOL][:, _gate3_perm()], w_in[:, OFF_POOL:OFF_END]],
                            axis=1).astype(BF16)
    hn, q, rows4, winkv, g3, zpool, zgm, dq, dkv = in_proj(x, lw["norm_mix_g"], w_all, segs)
    q, rows4, winkv = q.reshape(b, sq, 512), rows4.reshape(b, sq, 512), winkv.reshape(b, sq, 256)
    zpool, zgm = zpool.reshape(b, sq, 512), zgm.reshape(b, sq, 1024)
    dq, dkv = dq.reshape(b, sq, 512), dkv.reshape(b, sq, 1024)

    wc, cbias, w2b = _cmp_weights(lw["cmp_pe"], lw["cmp_w1"], lw["cmp_w2"])
    lam_par = jnp.stack([lw["diff_lq1"], lw["diff_lk1"], lw["diff_lq2"], lw["diff_lk2"]])
    gnorm = jnp.tile(lw["diff_norm_g"], 2).reshape(1, LANE)

    if sample is None:
        past, l_tot = 0, sq
        nch = sq // CMP_STRIDE
        tq = _tile(sq, 256)
        tk_sel = _tile(sq, 2048)
        nk_sel = sq // tk_sel
        tq_w = _tile(sq, 256)
        tw = _tile(sq, 2048)
        win_all, kpos0_w, nk_w = winkv, 0, sq // tw
        win_kt = lambda i, j: (j, j >= 0)
        tq_d = _tile(sq, 256)
        tk_d = _tile(sq, 2048)
        nk_d = sq // tk_d
        prev16 = jnp.zeros((b, 16, 512), F32)
        pool_state = zpool[:, sq - POOL_MEM:]
        zgm_in = zgm
    else:
        layer = sample["layer"]
        pt = sample["page_table"]
        past = pt.shape[1] * PAGE
        l_tot = past + sq
        nch = past // CMP_STRIDE
        tq = sq
        wbuf = sample["win"].shape[1]
        wrows = -(-(wbuf + sq) // KEY_SUB) * KEY_SUB
        tw = wrows
        win_all = jnp.concatenate([sample["win"], winkv, jnp.zeros((b, wrows - wbuf - sq, 256), F32)], axis=1)
        kpos0_w, nk_w = past - wbuf, 1
        win_kt = lambda i, j: (j, j >= 0)
        tq_w = sq
        prev16 = jnp.concatenate([jnp.zeros((b, 1, 512), F32), sample["pool"]], axis=1)
        pool_state = jnp.concatenate([sample["pool"], zpool], axis=1)[:, -POOL_MEM:]
        zgm_in = jnp.concatenate([zgm, jnp.zeros((b, GM_CHUNK - sq, 1024), F32)], axis=1)

    n_cmp = l_tot // CMP_STRIDE - 1
    n_sel = -(-l_tot // SEL_BLOCK)
    nsp = -(-n_sel // LANE) * LANE
    msel = jnp.asarray(_cmp_to_sel(n_cmp, n_sel, nch, nsp), BF16)
    if sample is None:
        kcvc = compress(rows4, wc, cbias, w2b, nch)
        ocmp, selmask = cmp_attend(q, kcvc, msel, tq=tq, n_cmp=n_cmp, n_sel=n_sel, qpos0=0)
        emat = jnp.asarray(_block_expand(nsp, sq), BF16)

        def sel_kt(i, j):
            last = ((i + 1) * tq - 1) // tk_sel
            return jnp.minimum(j, last), j <= last

        osel = nsa_flash(q, rows4, 2, 3, mode="sel", tq=tq, tk=tk_sel, nk=nk_sel, qpos0=0, kpos0=0,
                         kt_fn=sel_kt, sel=selmask, emat=emat)
    else:
        kcvc = compress_paged(pt, sample["cache_nsa"], layer, wc, cbias, w2b)
        ocmp, selmask = cmp_attend(q, kcvc, msel, tq=tq, n_cmp=n_cmp, n_sel=n_sel, qpos0=past)
        emat = jnp.asarray(_block_expand(nsp, past + LANE), BF16)
        osel = sel_paged(pt, sample["cache_nsa"], layer, q, rows4, selmask, emat)
    owin = nsa_flash(q, win_all, 0, 1, mode="win", tq=tq_w, tk=tw, nk=nk_w, qpos0=past, kpos0=kpos0_w,
                     kt_fn=win_kt)

    opool = pool_mix(prev16, zpool, lw["pool_w"], lw["pool_scale"], past)
    ws_tril = jnp.tril(lw["gm_ws"]).astype(BF16)
    bs_exp = jnp.repeat(lw["gm_bs"].T, LANE, axis=1)
    ogm, gm_v = gmlp_mix(zgm_in, lw["gm_ng"], lw["gm_nb"], ws_tril, bs_exp)

    if sample is None:
        def diff_kt(i, j):
            last = ((i + 1) * tq_d - 1) // tk_d
            return jnp.minimum(j, last), j <= last

        odiff = diff_attend(lam_par, dq, dkv, gnorm, tq=tq_d, tk=tk_d, nk=nk_d, qpos0=0,
                            lam_init=lam_init, kt_fn=diff_kt)
    else:
        odiff = diff_paged(pt, sample["cache_diff"], layer, lam_par, dq, dkv, gnorm, lam_init)
        ogm = ogm[:, :sq]
        gm_v = gm_v[:, :sq]
    wbr = lw["w_branch"].at[0].set(lw["w_branch"][0][qperm]).astype(BF16)
    out = finish_mixer(x, hn, g3, ocmp.reshape(t, 512), osel.reshape(t, 512), owin.reshape(t, 512),
                       opool.reshape(t, 512), ogm.reshape(t, 512), odiff.reshape(t, 512),
                       lw["w_gate"].astype(BF16), lw["b_gate"].reshape(1, -1), wbr, lw["w_o"].astype(BF16))
    states = dict(rows4=rows4, winkv=winkv, win_all=win_all, pool_state=pool_state, gm_v=gm_v, dkv=dkv)
    return out, states


def kernel(x_prompt, x_sample, cache_nsa, cache_diff, state_nsa_win, state_pool, page_table, p_prompt, p_sample, norm_mix_g, w_in, nsa_cmp_pe, nsa_cmp_w1, nsa_cmp_w2, pool_w, pool_scale, gm_norm_g, gm_norm_b, gm_ws, gm_bs, diff_lq1, diff_lk1, diff_lq2, diff_lk2, diff_norm_g, w_branch, w_gate, b_gate, w_o, norm_ffn_g, ffn_w_gate, ffn_w_up, ffn_w_down, moe_router, moe_router_b, moe_w_gate, moe_w_up, moe_w_down, ple_norm_g, ple_w_gate, ple_w_proj, final_norm_g):
    bp, sp, d = x_prompt.shape
    bs, ss, _ = x_sample.shape
    depth = w_in.shape[0]
    n_phys = cache_nsa.shape[1]
    wbuf = state_nsa_win.shape[2]
    cache_nsa2 = jnp.transpose(cache_nsa, (0, 1, 3, 4, 5, 2)).reshape(depth, n_phys, 512, PAGE)
    cache_diff2 = jnp.transpose(cache_diff, (0, 1, 3, 4, 5, 2)).reshape(depth, n_phys, 1024, PAGE)
    xp = x_prompt.reshape(bp * sp, d)
    xs = x_sample.reshape(bs * ss, d)
    outs = {k: [] for k in ("nsa_p", "nsa_s", "win_p", "win_s", "pool_p", "pool_s", "gmv_s", "diff_p", "diff_s")}
    for l in range(depth):
        lw = dict(norm_mix_g=norm_mix_g[l], w_in=w_in[l], cmp_pe=nsa_cmp_pe[l], cmp_w1=nsa_cmp_w1[l],
                  cmp_w2=nsa_cmp_w2[l], pool_w=pool_w[l], pool_scale=pool_scale[l], gm_ng=gm_norm_g[l],
                  gm_nb=gm_norm_b[l], gm_ws=gm_ws[l], gm_bs=gm_bs[l], diff_lq1=diff_lq1[l], diff_lk1=diff_lk1[l],
                  diff_lq2=diff_lq2[l], diff_lk2=diff_lk2[l], diff_norm_g=diff_norm_g[l], w_branch=w_branch[l],
                  w_gate=w_gate[l], b_gate=b_gate[l], w_o=w_o[l])
        lam_init = 0.8 - 0.6 * math.exp(-0.3 * l)
        xp, st_p = _mixer(xp, lw, lam_init, b=bp, sq=sp)
        sample = dict(cache_nsa=cache_nsa2, cache_diff=cache_diff2, win=state_nsa_win[l].reshape(bs, wbuf, 256),
                      pool=state_pool[l], page_table=page_table, layer=l)
        xs, st_s = _mixer(xs, lw, lam_init, b=bs, sq=ss, sample=sample)
        outs["nsa_p"].append(st_p["rows4"].reshape(bp * sp, 512))
        outs["diff_p"].append(st_p["dkv"].reshape(bp * sp, 1024))
        outs["nsa_s"].append(st_s["rows4"].reshape(bs, ss, 4, NSA_G, DH))
        wkeep = min(WINDOW, sp)
        outs["win_p"].append(st_p["winkv"][:, sp - wkeep:].reshape(bp, wkeep, 2, NSA_G, DH))
        outs["win_s"].append(st_s["win_all"][:, ss:ss + wbuf].reshape(bs, wbuf, 2, NSA_G, DH))
        outs["pool_p"].append(st_p["pool_state"])
        outs["pool_s"].append(st_s["pool_state"])
        outs["gmv_s"].append(st_s["gm_v"])
        outs["diff_s"].append(st_s["dkv"].reshape(bs, ss, 2, DIFF_HEADS, 2 * DIFF_QK))
        i = l // 2
        if l % 2 == 0:
            xp = ffn_swiglu(xp, norm_ffn_g[l], ffn_w_gate[i], ffn_w_up[i], ffn_w_down[i])
            xs = ffn_swiglu(xs, norm_ffn_g[l], ffn_w_gate[i], ffn_w_up[i], ffn_w_down[i])
        else:
            xp = moe_swiglu(xp, norm_ffn_g[l], moe_router[i], moe_router_b[i], moe_w_gate[i], moe_w_up[i], moe_w_down[i])
            xs = moe_swiglu(xs, norm_ffn_g[l], moe_router[i], moe_router_b[i], moe_w_gate[i], moe_w_up[i], moe_w_down[i])
        final = l == depth - 1
        xp = ple(xp, p_prompt[l].reshape(bp * sp, -1), ple_norm_g[l], ple_w_gate[l], ple_w_proj[l], final_norm_g, final)
        xs = ple(xs, p_sample[l].reshape(bs * ss, -1), ple_norm_g[l], ple_w_gate[l], ple_w_proj[l], final_norm_g, final)
    st = lambda k: jnp.stack(outs[k])
    nsa_p = jnp.transpose(stack_rows(outs["nsa_p"], bp, sp).reshape(depth, bp, 4, NSA_G, DH, sp), (0, 1, 5, 2, 3, 4))
    diff_p = jnp.transpose(stack_rows(outs["diff_p"], bp, sp).reshape(depth, bp, 2, DIFF_HEADS, 2 * DIFF_QK, sp),
                           (0, 1, 5, 2, 3, 4))
    return (xp.reshape(bp, sp, d), xs.reshape(bs, ss, d), nsa_p, st("nsa_s"), st("win_p"), st("win_s"),
            st("pool_p"), st("pool_s"), st("gmv_s"), diff_p, st("diff_s"))
```
